```python
import jax, jax.numpy as jnp
from jax import lax
import numpy as np

D_MODEL = 1024
BATCH = 8
SEQ = 4096
DEPTH = 4

N_META = 16
POOL_WIDTH = D_MODEL
POOL_GROUPS = 4
POOL_GROUP_DIM = POOL_WIDTH // POOL_GROUPS
POOL_WINDOWS = (2, 4, 8, 16)
SSM_EXPAND = 2
D_INNER = SSM_EXPAND * D_MODEL
SSM_HEAD_DIM = 64
SSM_HEADS = D_INNER // SSM_HEAD_DIM
SSM_GROUPS = 8
HEADS_PER_GROUP = SSM_HEADS // SSM_GROUPS
D_STATE = 128
CONV_WIDTH = 4
CHUNK = 128
D_XBC = D_INNER + 2 * SSM_GROUPS * D_STATE
D_FF = 4 * D_MODEL
N_BRANCHES = 2
EPS = 1e-5

OFF_POOL = 0
OFF_Z = OFF_POOL + POOL_WIDTH
OFF_XBC = OFF_Z + D_INNER
OFF_DT = OFF_XBC + D_XBC
OFF_GATE = OFF_DT + SSM_HEADS
IN_COLS = OFF_GATE + N_BRANCHES * D_MODEL

kernel_name = "hybrid_pool_ssd_gated_parallel"


def rmsnorm(x, w):
    xf = x.astype(jnp.float32)
    xf = xf * lax.rsqrt(jnp.mean(xf * xf, axis=-1, keepdims=True) + EPS)
    return xf.astype(x.dtype) * w


def pool_mixer(u, w_group, scale):
    bsz, seqlen, _ = u.shape
    ug = u.reshape(bsz, seqlen, POOL_GROUPS, POOL_GROUP_DIM)
    pos = jnp.arange(seqlen)[None, :, None]
    outs = []
    for g, win in enumerate(POOL_WINDOWS):
        xg = ug[:, :, g, :]
        cs = jnp.cumsum(xg.astype(jnp.float32), axis=1)
        shifted = jnp.pad(cs, ((0, 0), (win, 0), (0, 0)))[:, :seqlen]
        count = jnp.minimum(pos + 1, win).astype(jnp.float32)
        mean = (cs - shifted) / count
        outs.append(mean.astype(u.dtype) - xg)
    pooled = jnp.stack(outs, axis=2)
    y = jnp.einsum("blgc,gcd->blgd", pooled, w_group).reshape(bsz, seqlen, POOL_WIDTH)
    return y * scale


def causal_depthwise_conv(x, w, b):
    seqlen = x.shape[1]
    xp = jnp.pad(x, ((0, 0), (CONV_WIDTH - 1, 0), (0, 0)))
    y = b
    for k in range(CONV_WIDTH):
        y = y + xp[:, k:k + seqlen] * w[k]
    return y


def ssd_chunked(x, dt, a, b_mat, c_mat):
    bsz, seqlen = x.shape[0], x.shape[1]
    pad = (-N_META) % CHUNK
    lp = seqlen + pad
    nc = lp // CHUNK

    def fpad(t):
        return jnp.pad(t, [(0, 0), (pad, 0)] + [(0, 0)] * (t.ndim - 2))

    xf = fpad(x).astype(jnp.float32)
    dtf = fpad(dt).astype(jnp.float32)
    bf = fpad(b_mat).astype(jnp.float32).reshape(bsz, nc, CHUNK, SSM_GROUPS, D_STATE)
    cf = fpad(c_mat).astype(jnp.float32).reshape(bsz, nc, CHUNK, SSM_GROUPS, D_STATE)
    xdt = (xf * dtf[..., None]).reshape(bsz, nc, CHUNK, SSM_GROUPS, HEADS_PER_GROUP, SSM_HEAD_DIM)
    a_dt = (dtf * a.astype(jnp.float32)).reshape(bsz, nc, CHUNK, SSM_GROUPS, HEADS_PER_GROUP)
    a_cs = jnp.cumsum(a_dt, axis=2)

    mask = jnp.tril(jnp.ones((CHUNK, CHUNK), dtype=bool))[:, :, None, None]
    diff = a_cs[:, :, :, None] - a_cs[:, :, None, :]
    lmat = jnp.exp(jnp.where(mask, diff, -jnp.inf))
    cb = jnp.einsum("bclgn,bcsgn->bclsg", cf, bf)
    y_diag = jnp.einsum("bclsg,bclsgr,bcsgrp->bclgrp", cb, lmat, xdt)

    decay_states = jnp.exp(a_cs[:, :, -1:] - a_cs)
    states = jnp.einsum("bclgn,bclgr,bclgrp->bcgrpn", bf, decay_states, xdt)
    chunk_decay = jnp.exp(a_cs[:, :, -1])

    def step(h, inp):
        dec, st = inp
        return dec[..., None, None] * h + st, h

    h0 = jnp.zeros((bsz, SSM_GROUPS, HEADS_PER_GROUP, SSM_HEAD_DIM, D_STATE), jnp.float32)
    _, prev = lax.scan(step, h0, (jnp.moveaxis(chunk_decay, 1, 0), jnp.moveaxis(states, 1, 0)))
    prev = jnp.moveaxis(prev, 0, 1)

    y_off = jnp.einsum("bclgn,bcgrpn,bclgr->bclgrp", cf, prev, jnp.exp(a_cs))
    y = (y_diag + y_off).reshape(bsz, lp, SSM_HEADS, SSM_HEAD_DIM)[:, pad:]
    return y.astype(x.dtype)


def mamba2_branch(z, xbc, dt_raw, conv_w, conv_b, dt_bias, a_log, d_skip, norm_w):
    bsz, seqlen, _ = z.shape
    xbc = jax.nn.silu(causal_depthwise_conv(xbc, conv_w, conv_b))
    xs = xbc[..., :D_INNER].reshape(bsz, seqlen, SSM_HEADS, SSM_HEAD_DIM)
    b_mat = xbc[..., D_INNER:D_INNER + SSM_GROUPS * D_STATE].reshape(bsz, seqlen, SSM_GROUPS, D_STATE)
    c_mat = xbc[..., D_INNER + SSM_GROUPS * D_STATE:].reshape(bsz, seqlen, SSM_GROUPS, D_STATE)
    dt = jax.nn.softplus(dt_raw + dt_bias)
    a = -jnp.exp(a_log)
    y = ssd_chunked(xs, dt, a, b_mat, c_mat) + xs * d_skip[:, None]
    y = y.reshape(bsz, seqlen, D_INNER) * jax.nn.silu(z)
    yg = y.reshape(bsz, seqlen, SSM_GROUPS, D_INNER // SSM_GROUPS).astype(jnp.float32)
    yg = yg * lax.rsqrt(jnp.mean(yg * yg, axis=-1, keepdims=True) + EPS)
    return yg.reshape(bsz, seqlen, D_INNER).astype(z.dtype) * norm_w


def _fwd_setup_inputs(seed: int = 0) -> dict:
    key = jax.random.key(seed)
    ks = jax.random.split(key, 24)
    f32 = jnp.float32
    nrm = lambda k, shape, s: jax.random.normal(k, shape, f32) * s
    dt_init = jnp.exp(jax.random.uniform(ks[9], (DEPTH, SSM_HEADS), f32, np.log(1e-3), np.log(1e-1)))
    return {
        "x": nrm(ks[0], (BATCH, SEQ, D_MODEL), 1.0),
        "meta_tokens": nrm(ks[1], (N_META, D_MODEL), 1.0),
        "mix_norm_w": 1.0 + nrm(ks[2], (DEPTH, D_MODEL), 0.02),
        "w_in": nrm(ks[3], (DEPTH, D_MODEL, IN_COLS), D_MODEL ** -0.5),
        "b_gate": nrm(ks[4], (DEPTH, N_BRANCHES * D_MODEL), 0.02),
        "pool_w_group": nrm(ks[5], (DEPTH, POOL_GROUPS, POOL_GROUP_DIM, POOL_GROUP_DIM), POOL_GROUP_DIM ** -0.5),
        "pool_scale": 1.0 + nrm(ks[6], (DEPTH, POOL_WIDTH), 0.02),
        "w_pool_up": nrm(ks[7], (DEPTH, POOL_WIDTH, D_MODEL), POOL_WIDTH ** -0.5),
        "conv_w": nrm(ks[8], (DEPTH, CONV_WIDTH, D_XBC), CONV_WIDTH ** -0.5),
        "conv_b": nrm(ks[10], (DEPTH, D_XBC), 0.02),
        "dt_bias": dt_init + jnp.log(-jnp.expm1(-dt_init)),
        "a_log": jnp.log(jax.random.uniform(ks[11], (DEPTH, SSM_HEADS), f32, 1.0, 16.0)),
        "d_skip": 1.0 + nrm(ks[12], (DEPTH, SSM_HEADS), 0.02),
        "ssd_norm_w": 1.0 + nrm(ks[13], (DEPTH, D_INNER), 0.02),
        "w_ssd_out": nrm(ks[14], (DEPTH, D_INNER, D_MODEL), D_INNER ** -0.5),
        "w_o": nrm(ks[15], (DEPTH, D_MODEL, D_MODEL), D_MODEL ** -0.5),
        "mlp_norm_w": 1.0 + nrm(ks[16], (DEPTH, D_MODEL), 0.02),
        "w_ff1": nrm(ks[17], (DEPTH, D_MODEL, D_FF), D_MODEL ** -0.5),
        "w_ff2": nrm(ks[18], (DEPTH, D_FF, D_MODEL), 0.5 * D_FF ** -0.5),
        "final_norm_w": 1.0 + nrm(ks[19], (D_MODEL,), 0.02),
    }


def _fwd_reference(x, meta_tokens, mix_norm_w, w_in, b_gate, pool_w_group, pool_scale, w_pool_up,
              conv_w, conv_b, dt_bias, a_log, d_skip, ssd_norm_w, w_ssd_out, w_o,
              mlp_norm_w, w_ff1, w_ff2, final_norm_w):
    bsz = x.shape[0]
    meta = jnp.broadcast_to(meta_tokens[None].astype(x.dtype), (bsz, N_META, D_MODEL))
    h = jnp.concatenate([meta, x], axis=1)
    for i in range(DEPTH):
        u = rmsnorm(h, mix_norm_w[i])
        proj = u @ w_in[i]
        u_pool = proj[..., OFF_POOL:OFF_Z]
        z = proj[..., OFF_Z:OFF_XBC]
        xbc = proj[..., OFF_XBC:OFF_DT]
        dt_raw = proj[..., OFF_DT:OFF_GATE]
        gates = jax.nn.sigmoid(proj[..., OFF_GATE:] + b_gate[i])
        gate_pool = gates[..., :D_MODEL]
        gate_ssd = gates[..., D_MODEL:]

        y_pool = pool_mixer(u_pool, pool_w_group[i], pool_scale[i]) @ w_pool_up[i]
        y_ssd = mamba2_branch(z, xbc, dt_raw, conv_w[i], conv_b[i], dt_bias[i], a_log[i],
                              d_skip[i], ssd_norm_w[i]) @ w_ssd_out[i]
        h = h + (gate_pool * y_pool + gate_ssd * y_ssd) @ w_o[i]

        v = rmsnorm(h, mlp_norm_w[i])
        hid = jax.nn.relu(v @ w_ff1[i])
        h = h + (hid * hid) @ w_ff2[i]
    out = rmsnorm(h, final_norm_w)
    return out[:, N_META:]


import jax as _jax
import jax.numpy as _jnp

TWIN_FORMAT = 'train_step'
FWD_PARAMS = ['x', 'meta_tokens', 'mix_norm_w', 'w_in', 'b_gate', 'pool_w_group', 'pool_scale', 'w_pool_up', 'conv_w', 'conv_b', 'dt_bias', 'a_log', 'd_skip', 'ssd_norm_w', 'w_ssd_out', 'w_o', 'mlp_norm_w', 'w_ff1', 'w_ff2', 'final_norm_w']
TWIN_WEIGHTS = ['meta_tokens', 'mix_norm_w', 'w_in', 'b_gate', 'pool_w_group', 'pool_scale', 'w_pool_up', 'conv_w', 'conv_b', 'dt_bias', 'a_log', 'd_skip', 'ssd_norm_w', 'w_ssd_out', 'w_o', 'mlp_norm_w', 'w_ff1', 'w_ff2', 'final_norm_w']
TWIN_DIFF_INPUT = 'x'
TWIN_INPUTS = ['x', 'meta_tokens', 'mix_norm_w', 'w_in', 'b_gate', 'pool_w_group', 'pool_scale', 'w_pool_up', 'conv_w', 'conv_b', 'dt_bias', 'a_log', 'd_skip', 'ssd_norm_w', 'w_ssd_out', 'w_o', 'mlp_norm_w', 'w_ff1', 'w_ff2', 'final_norm_w', 'loss_target', 'm_meta_tokens', 'm_mix_norm_w', 'm_w_in', 'm_b_gate', 'm_pool_w_group', 'm_pool_scale', 'm_w_pool_up', 'm_conv_w', 'm_conv_b', 'm_dt_bias', 'm_a_log', 'm_d_skip', 'm_ssd_norm_w', 'm_w_ssd_out', 'm_w_o', 'm_mlp_norm_w', 'm_w_ff1', 'm_w_ff2', 'm_final_norm_w', 'v_meta_tokens', 'v_mix_norm_w', 'v_w_in', 'v_b_gate', 'v_pool_w_group', 'v_pool_scale', 'v_w_pool_up', 'v_conv_w', 'v_conv_b', 'v_dt_bias', 'v_a_log', 'v_d_skip', 'v_ssd_norm_w', 'v_w_ssd_out', 'v_w_o', 'v_mlp_norm_w', 'v_w_ff1', 'v_w_ff2', 'v_final_norm_w']
TWIN_OUTPUTS = ['loss', 'grad_x', 'grad_meta_tokens', 'grad_mix_norm_w', 'grad_w_in', 'grad_b_gate', 'grad_pool_w_group', 'grad_pool_scale', 'grad_w_pool_up', 'grad_conv_w', 'grad_conv_b', 'grad_dt_bias', 'grad_a_log', 'grad_d_skip', 'grad_ssd_norm_w', 'grad_w_ssd_out', 'grad_w_o', 'grad_mlp_norm_w', 'grad_w_ff1', 'grad_w_ff2', 'grad_final_norm_w', 'delta_meta_tokens', 'delta_mix_norm_w', 'delta_w_in', 'delta_b_gate', 'delta_pool_w_group', 'delta_pool_scale', 'delta_w_pool_up', 'delta_conv_w', 'delta_conv_b', 'delta_dt_bias', 'delta_a_log', 'delta_d_skip', 'delta_ssd_norm_w', 'delta_w_ssd_out', 'delta_w_o', 'delta_mlp_norm_w', 'delta_w_ff1', 'delta_w_ff2', 'delta_final_norm_w', 'new_m_meta_tokens', 'new_m_mix_norm_w', 'new_m_w_in', 'new_m_b_gate', 'new_m_pool_w_group', 'new_m_pool_scale', 'new_m_w_pool_up', 'new_m_conv_w', 'new_m_conv_b', 'new_m_dt_bias', 'new_m_a_log', 'new_m_d_skip', 'new_m_ssd_norm_w', 'new_m_w_ssd_out', 'new_m_w_o', 'new_m_mlp_norm_w', 'new_m_w_ff1', 'new_m_w_ff2', 'new_m_final_norm_w', 'new_v_meta_tokens', 'new_v_mix_norm_w', 'new_v_w_in', 'new_v_b_gate', 'new_v_pool_w_group', 'new_v_pool_scale', 'new_v_w_pool_up', 'new_v_conv_w', 'new_v_conv_b', 'new_v_dt_bias', 'new_v_a_log', 'new_v_d_skip', 'new_v_ssd_norm_w', 'new_v_w_ssd_out', 'new_v_w_o', 'new_v_mlp_norm_w', 'new_v_w_ff1', 'new_v_w_ff2', 'new_v_final_norm_w']
TWIN_LEAF_KINDS = {'loss': 'loss', 'grad_x': 'grad_x', 'grad_meta_tokens': 'grad_w', 'grad_mix_norm_w': 'grad_w', 'grad_w_in': 'grad_w', 'grad_b_gate': 'grad_w', 'grad_pool_w_group': 'grad_w', 'grad_pool_scale': 'grad_w', 'grad_w_pool_up': 'grad_w', 'grad_conv_w': 'grad_w', 'grad_conv_b': 'grad_w', 'grad_dt_bias': 'grad_w', 'grad_a_log': 'grad_w', 'grad_d_skip': 'grad_w', 'grad_ssd_norm_w': 'grad_w', 'grad_w_ssd_out': 'grad_w', 'grad_w_o': 'grad_w', 'grad_mlp_norm_w': 'grad_w', 'grad_w_ff1': 'grad_w', 'grad_w_ff2': 'grad_w', 'grad_final_norm_w': 'grad_w', 'delta_meta_tokens': 'delta_w', 'delta_mix_norm_w': 'delta_w', 'delta_w_in': 'delta_w', 'delta_b_gate': 'delta_w', 'delta_pool_w_group': 'delta_w', 'delta_pool_scale': 'delta_w', 'delta_w_pool_up': 'delta_w', 'delta_conv_w': 'delta_w', 'delta_conv_b': 'delta_w', 'delta_dt_bias': 'delta_w', 'delta_a_log': 'delta_w', 'delta_d_skip': 'delta_w', 'delta_ssd_norm_w': 'delta_w', 'delta_w_ssd_out': 'delta_w', 'delta_w_o': 'delta_w', 'delta_mlp_norm_w': 'delta_w', 'delta_w_ff1': 'delta_w', 'delta_w_ff2': 'delta_w', 'delta_final_norm_w': 'delta_w', 'new_m_meta_tokens': 'new_m', 'new_m_mix_norm_w': 'new_m', 'new_m_w_in': 'new_m', 'new_m_b_gate': 'new_m', 'new_m_pool_w_group': 'new_m', 'new_m_pool_scale': 'new_m', 'new_m_w_pool_up': 'new_m', 'new_m_conv_w': 'new_m', 'new_m_conv_b': 'new_m', 'new_m_dt_bias': 'new_m', 'new_m_a_log': 'new_m', 'new_m_d_skip': 'new_m', 'new_m_ssd_norm_w': 'new_m', 'new_m_w_ssd_out': 'new_m', 'new_m_w_o': 'new_m', 'new_m_mlp_norm_w': 'new_m', 'new_m_w_ff1': 'new_m', 'new_m_w_ff2': 'new_m', 'new_m_final_norm_w': 'new_m', 'new_v_meta_tokens': 'new_v', 'new_v_mix_norm_w': 'new_v', 'new_v_w_in': 'new_v', 'new_v_b_gate': 'new_v', 'new_v_pool_w_group': 'new_v', 'new_v_pool_scale': 'new_v', 'new_v_w_pool_up': 'new_v', 'new_v_conv_w': 'new_v', 'new_v_conv_b': 'new_v', 'new_v_dt_bias': 'new_v', 'new_v_a_log': 'new_v', 'new_v_d_skip': 'new_v', 'new_v_ssd_norm_w': 'new_v', 'new_v_w_ssd_out': 'new_v', 'new_v_w_o': 'new_v', 'new_v_mlp_norm_w': 'new_v', 'new_v_w_ff1': 'new_v', 'new_v_w_ff2': 'new_v', 'new_v_final_norm_w': 'new_v'}


def _forward(args):
    return _fwd_reference(*[args[k] for k in FWD_PARAMS])


def _output_shape():
    def fwd():
        inp = _fwd_setup_inputs(0)
        return _fwd_reference(*[inp[k] for k in FWD_PARAMS])
    out = _jax.eval_shape(fwd)
    return out.shape, out.dtype

N_MICROBATCH = 1
ADAM_LR = 0.001
ADAM_B1 = 0.9
ADAM_B2 = 0.999
ADAM_EPS = 1e-08
ADAM_WD = 0.01
ADAM_STEP = 10
PER_EXAMPLE_BATCH_AXIS = {'x': 0, 'loss_target': 0}
SHARED_INPUTS = []
_WEIGHT_DTYPES = {'meta_tokens': _jnp.float32, 'mix_norm_w': _jnp.float32, 'w_in': _jnp.float32, 'b_gate': _jnp.float32, 'pool_w_group': _jnp.float32, 'pool_scale': _jnp.float32, 'w_pool_up': _jnp.float32, 'conv_w': _jnp.float32, 'conv_b': _jnp.float32, 'dt_bias': _jnp.float32, 'a_log': _jnp.float32, 'd_skip': _jnp.float32, 'ssd_norm_w': _jnp.float32, 'w_ssd_out': _jnp.float32, 'w_o': _jnp.float32, 'mlp_norm_w': _jnp.float32, 'w_ff1': _jnp.float32, 'w_ff2': _jnp.float32, 'final_norm_w': _jnp.float32}
MOMENT_SCALE = {'meta_tokens': 4.994316e-03, 'mix_norm_w': 1.347907e-01, 'w_in': 4.429123e-02, 'b_gate': 2.742746e-02, 'pool_w_group': 6.356403e-02, 'pool_scale': 6.436538e-02, 'w_pool_up': 6.343731e-02, 'conv_w': 3.951717e-02, 'conv_b': 5.580086e-02, 'dt_bias': 9.069667e-02, 'a_log': 2.041540e-01, 'd_skip': 2.781416e-01, 'ssd_norm_w': 5.405997e-02, 'w_ssd_out': 7.456209e-02, 'w_o': 9.781191e-02, 'mlp_norm_w': 8.805733e-02, 'w_ff1': 4.295434e-02, 'w_ff2': 1.619359e-01, 'final_norm_w': 3.216809e+01}


def _to_microbatches(a, axis):
    t = _jnp.moveaxis(a, axis, 0)
    t = t.reshape((N_MICROBATCH, t.shape[0] // N_MICROBATCH) + t.shape[1:])
    return _jnp.moveaxis(t, 1, axis + 1)


def setup_inputs(seed: int = 0) -> dict:
    inp = _fwd_setup_inputs(seed)
    key = _jax.random.fold_in(_jax.random.key(seed), 7919)
    shape, _ = _output_shape()
    out = dict(inp)
    out["loss_target"] = _jax.random.normal(_jax.random.fold_in(key, 0), shape, _jnp.float32)
    for i, name in enumerate(TWIN_WEIGHTS):
        w = inp[name].astype(_jnp.float32)
        if MOMENT_SCALE is None:
            s = _jnp.sqrt(_jnp.mean(_jnp.square(w)) + 1e-30)
        else:
            s = MOMENT_SCALE[name]
        km, kv = _jax.random.split(_jax.random.fold_in(key, i + 1))
        out[name] = w
        out["m_" + name] = s * _jax.random.normal(km, w.shape, _jnp.float32)
        out["v_" + name] = (s * s) * _jax.random.uniform(kv, w.shape, _jnp.float32, 0.5, 1.5)
    if N_MICROBATCH > 1:
        for name, axis in PER_EXAMPLE_BATCH_AXIS.items():
            out[name] = _to_microbatches(out[name], axis)
    return {'x': out['x'], 'meta_tokens': out['meta_tokens'], 'mix_norm_w': out['mix_norm_w'], 'w_in': out['w_in'], 'b_gate': out['b_gate'], 'pool_w_group': out['pool_w_group'], 'pool_scale': out['pool_scale'], 'w_pool_up': out['w_pool_up'], 'conv_w': out['conv_w'], 'conv_b': out['conv_b'], 'dt_bias': out['dt_bias'], 'a_log': out['a_log'], 'd_skip': out['d_skip'], 'ssd_norm_w': out['ssd_norm_w'], 'w_ssd_out': out['w_ssd_out'], 'w_o': out['w_o'], 'mlp_norm_w': out['mlp_norm_w'], 'w_ff1': out['w_ff1'], 'w_ff2': out['w_ff2'], 'final_norm_w': out['final_norm_w'], 'loss_target': out['loss_target'], 'm_meta_tokens': out['m_meta_tokens'], 'm_mix_norm_w': out['m_mix_norm_w'], 'm_w_in': out['m_w_in'], 'm_b_gate': out['m_b_gate'], 'm_pool_w_group': out['m_pool_w_group'], 'm_pool_scale': out['m_pool_scale'], 'm_w_pool_up': out['m_w_pool_up'], 'm_conv_w': out['m_conv_w'], 'm_conv_b': out['m_conv_b'], 'm_dt_bias': out['m_dt_bias'], 'm_a_log': out['m_a_log'], 'm_d_skip': out['m_d_skip'], 'm_ssd_norm_w': out['m_ssd_norm_w'], 'm_w_ssd_out': out['m_w_ssd_out'], 'm_w_o': out['m_w_o'], 'm_mlp_norm_w': out['m_mlp_norm_w'], 'm_w_ff1': out['m_w_ff1'], 'm_w_ff2': out['m_w_ff2'], 'm_final_norm_w': out['m_final_norm_w'], 'v_meta_tokens': out['v_meta_tokens'], 'v_mix_norm_w': out['v_mix_norm_w'], 'v_w_in': out['v_w_in'], 'v_b_gate': out['v_b_gate'], 'v_pool_w_group': out['v_pool_w_group'], 'v_pool_scale': out['v_pool_scale'], 'v_w_pool_up': out['v_w_pool_up'], 'v_conv_w': out['v_conv_w'], 'v_conv_b': out['v_conv_b'], 'v_dt_bias': out['v_dt_bias'], 'v_a_log': out['v_a_log'], 'v_d_skip': out['v_d_skip'], 'v_ssd_norm_w': out['v_ssd_norm_w'], 'v_w_ssd_out': out['v_w_ssd_out'], 'v_w_o': out['v_w_o'], 'v_mlp_norm_w': out['v_mlp_norm_w'], 'v_w_ff1': out['v_w_ff1'], 'v_w_ff2': out['v_w_ff2'], 'v_final_norm_w': out['v_final_norm_w']}


def _loss(weights, diff, rest, loss_target):
    with _jax.named_scope("forward"):
        args = {**rest, TWIN_DIFF_INPUT: diff, **{k: w.astype(_WEIGHT_DTYPES[k]) for k, w in weights.items()}}
        y = _forward(args)
    with _jax.named_scope("loss_head"):
        err = _jnp.square(y.astype(_jnp.float32) - loss_target)
        return 0.5 * _jnp.sum(_jnp.mean(err, axis=-1)) if err.ndim else 0.5 * err


def _adamw(w, g, m, v):
    m = ADAM_B1 * m + (1.0 - ADAM_B1) * g
    v = ADAM_B2 * v + (1.0 - ADAM_B2) * _jnp.square(g)
    m_hat = m / (1.0 - ADAM_B1 ** ADAM_STEP)
    v_hat = v / (1.0 - ADAM_B2 ** ADAM_STEP)
    delta = -ADAM_LR * (m_hat / (_jnp.sqrt(v_hat) + ADAM_EPS) + ADAM_WD * w)
    return delta, m, v


def reference(x, meta_tokens, mix_norm_w, w_in, b_gate, pool_w_group, pool_scale, w_pool_up, conv_w, conv_b, dt_bias, a_log, d_skip, ssd_norm_w, w_ssd_out, w_o, mlp_norm_w, w_ff1, w_ff2, final_norm_w, loss_target, m_meta_tokens, m_mix_norm_w, m_w_in, m_b_gate, m_pool_w_group, m_pool_scale, m_w_pool_up, m_conv_w, m_conv_b, m_dt_bias, m_a_log, m_d_skip, m_ssd_norm_w, m_w_ssd_out, m_w_o, m_mlp_norm_w, m_w_ff1, m_w_ff2, m_final_norm_w, v_meta_tokens, v_mix_norm_w, v_w_in, v_b_gate, v_pool_w_group, v_pool_scale, v_w_pool_up, v_conv_w, v_conv_b, v_dt_bias, v_a_log, v_d_skip, v_ssd_norm_w, v_w_ssd_out, v_w_o, v_mlp_norm_w, v_w_ff1, v_w_ff2, v_final_norm_w):
    given = dict(x=x, meta_tokens=meta_tokens, mix_norm_w=mix_norm_w, w_in=w_in, b_gate=b_gate, pool_w_group=pool_w_group, pool_scale=pool_scale, w_pool_up=w_pool_up, conv_w=conv_w, conv_b=conv_b, dt_bias=dt_bias, a_log=a_log, d_skip=d_skip, ssd_norm_w=ssd_norm_w, w_ssd_out=w_ssd_out, w_o=w_o, mlp_norm_w=mlp_norm_w, w_ff1=w_ff1, w_ff2=w_ff2, final_norm_w=final_norm_w, loss_target=loss_target, m_meta_tokens=m_meta_tokens, m_mix_norm_w=m_mix_norm_w, m_w_in=m_w_in, m_b_gate=m_b_gate, m_pool_w_group=m_pool_w_group, m_pool_scale=m_pool_scale, m_w_pool_up=m_w_pool_up, m_conv_w=m_conv_w, m_conv_b=m_conv_b, m_dt_bias=m_dt_bias, m_a_log=m_a_log, m_d_skip=m_d_skip, m_ssd_norm_w=m_ssd_norm_w, m_w_ssd_out=m_w_ssd_out, m_w_o=m_w_o, m_mlp_norm_w=m_mlp_norm_w, m_w_ff1=m_w_ff1, m_w_ff2=m_w_ff2, m_final_norm_w=m_final_norm_w, v_meta_tokens=v_meta_tokens, v_mix_norm_w=v_mix_norm_w, v_w_in=v_w_in, v_b_gate=v_b_gate, v_pool_w_group=v_pool_w_group, v_pool_scale=v_pool_scale, v_w_pool_up=v_w_pool_up, v_conv_w=v_conv_w, v_conv_b=v_conv_b, v_dt_bias=v_dt_bias, v_a_log=v_a_log, v_d_skip=v_d_skip, v_ssd_norm_w=v_ssd_norm_w, v_w_ssd_out=v_w_ssd_out, v_w_o=v_w_o, v_mlp_norm_w=v_mlp_norm_w, v_w_ff1=v_w_ff1, v_w_ff2=v_w_ff2, v_final_norm_w=v_final_norm_w)
    weights = {n: given[n] for n in TWIN_WEIGHTS}
    shared = {n: given[n] for n in SHARED_INPUTS}
    per_example = {n: given[n] for n in ['x']}
    grad_fn = _jax.value_and_grad(_loss, argnums=(0, 1))

    def one_microbatch(ex, loss_target):
        ex = dict(ex)
        diff = ex.pop(TWIN_DIFF_INPUT)
        return grad_fn(weights, diff, {**shared, **ex}, loss_target)

    if N_MICROBATCH == 1:
        loss, (grad_w, grad_x) = one_microbatch(per_example, given["loss_target"])
    else:
        def body(carry, xs):
            loss_sum, grad_sum = carry
            l_k, (gw_k, gx_k) = one_microbatch(xs[0], xs[1])
            with _jax.named_scope("update"):
                return (loss_sum + l_k, _jax.tree.map(_jnp.add, grad_sum, gw_k)), gx_k

        init = (_jnp.zeros((), _jnp.float32), _jax.tree.map(_jnp.zeros_like, weights))
        (loss, grad_w), grad_x = _jax.lax.scan(body, init, (per_example, given["loss_target"]))
    with _jax.named_scope("update"):
        delta_w, new_m, new_v = {}, {}, {}
        for n in TWIN_WEIGHTS:
            delta_w[n], new_m[n], new_v[n] = _adamw(weights[n], grad_w[n], given["m_" + n], given["v_" + n])
    return (loss, grad_x, *[grad_w[n] for n in TWIN_WEIGHTS], *[delta_w[n] for n in TWIN_WEIGHTS],
            *[new_m[n] for n in TWIN_WEIGHTS], *[new_v[n] for n in TWIN_WEIGHTS])
```

```python
import functools

import jax
import jax.numpy as jnp
from jax import lax
from jax.experimental import pallas as pl
from jax.experimental.pallas import tpu as pltpu

F32 = jnp.float32
BF16 = jnp.bfloat16

EPS = 1e-5
D_STATE = 128
CHUNK = 128
LANES = 128
POOL_WINDOWS = (2, 4, 8, 16)
POOL_HALO = 16
CONV_WIDTH = 4
CONV_HALO = 8
N_DEV = 8
ADAM_LR = 0.001
ADAM_B1 = 0.9
ADAM_B2 = 0.999
ADAM_EPS = 1e-08
ADAM_WD = 0.01
ADAM_STEP = 10
VMEM_LIMIT = 52 * 1024 * 1024
MESH = pl.DeviceIdType.MESH
ANY = pl.BlockSpec(memory_space=pl.ANY)


def _cp(*sem):
    return pltpu.CompilerParams(dimension_semantics=sem, vmem_limit_bytes=VMEM_LIMIT)


def _tile(n, target, mult):
    best = None
    for t in range(mult, min(n, target) + 1, mult):
        if n % t == 0:
            best = t
    return best if best is not None else n


def _sigmoid(x):
    return jax.nn.sigmoid(x)


def _iota(shape, dim):
    return lax.broadcasted_iota(jnp.int32, shape, dim)


_DIMS = {"nn": (((1,), (0,)), ((), ())), "nt": (((1,), (1,)), ((), ())), "tn": (((0,), (0,)), ((), ()))}


def _dot(a, b, mode="nn"):
    return lax.dot_general(a.astype(BF16), b.astype(BF16), _DIMS[mode], preferred_element_type=F32)


def _mm(a, b, mode, name, *, out_dtype=F32, res=None, epi=None, aux=None):
    if mode == "nn":
        (M, K), (_, N) = a.shape, b.shape
    elif mode == "nt":
        (M, K), (N, _) = a.shape, b.shape
    else:
        (K, M), (_, N) = a.shape, b.shape
    if mode == "tn":
        tm, tn, tk = _tile(M, 512, 128), _tile(N, 1024, 128), _tile(K, 1056, 16)
    else:
        tm, tn, tk = _tile(M, 1056, 16), _tile(N, 512, 128), _tile(K, 4096, 128)
    nk = K // tk
    a_spec = pl.BlockSpec((tk, tm), lambda i, j, k: (k, i)) if mode == "tn" else pl.BlockSpec((tm, tk), lambda i, j, k: (i, k))
    b_spec = pl.BlockSpec((tn, tk), lambda i, j, k: (j, k)) if mode == "nt" else pl.BlockSpec((tk, tn), lambda i, j, k: (k, j))
    o_spec = pl.BlockSpec((tm, tn), lambda i, j, k: (i, j))
    extra = [t for t in (res, aux) if t is not None]

    def body(*refs):
        a_ref, b_ref = refs[0], refs[1]
        x_ref = refs[2] if extra else None
        outs = refs[2 + len(extra):]
        p = _dot(a_ref[...], b_ref[...], mode)

        def finish(r):
            if res is not None:
                outs[0][...] = (x_ref[...] + r).astype(out_dtype)
            elif epi == "relu2":
                outs[0][...] = r
                hid = jnp.maximum(r, 0.0)
                outs[1][...] = (hid * hid).astype(BF16)
            elif epi == "drelu2":
                outs[0][...] = (r * (2.0 * jnp.maximum(x_ref[...], 0.0))).astype(BF16)
            else:
                outs[0][...] = r.astype(out_dtype)

        if nk == 1:
            finish(p)
        else:
            acc = outs[-1]
            k = pl.program_id(2)

            @pl.when(k == 0)
            def _():
                acc[...] = p

            @pl.when(k > 0)
            def _():
                acc[...] += p

            @pl.when(k == nk - 1)
            def _():
                finish(acc[...])

    if epi == "relu2":
        out_shape = (jax.ShapeDtypeStruct((M, N), F32), jax.ShapeDtypeStruct((M, N), BF16))
        out_specs = (o_spec, o_spec)
    elif epi == "drelu2":
        out_shape, out_specs = jax.ShapeDtypeStruct((M, N), BF16), o_spec
    else:
        out_shape, out_specs = jax.ShapeDtypeStruct((M, N), out_dtype), o_spec
    return pl.pallas_call(
        body, name=name, grid=(M // tm, N // tn, nk),
        in_specs=[a_spec, b_spec] + [o_spec] * len(extra),
        out_specs=out_specs, out_shape=out_shape,
        scratch_shapes=[pltpu.VMEM((tm, tn), F32)] if nk > 1 else [],
        compiler_params=_cp("parallel", "parallel", "arbitrary"),
    )(a, b, *extra)


def _rms_fwd(h, w, name):
    T, D = h.shape
    tr = _tile(T, 1056, 16)

    def body(h_ref, w_ref, o_ref):
        x = h_ref[...]
        xn = x * lax.rsqrt(jnp.mean(x * x, axis=-1, keepdims=True) + EPS)
        o_ref[...] = (xn * w_ref[...]).astype(BF16)

    return pl.pallas_call(
        body, name=name, grid=(T // tr,),
        in_specs=[pl.BlockSpec((tr, D), lambda i: (i, 0)), pl.BlockSpec((1, D), lambda i: (0, 0))],
        out_specs=pl.BlockSpec((tr, D), lambda i: (i, 0)), out_shape=jax.ShapeDtypeStruct((T, D), BF16),
        compiler_params=_cp("parallel"),
    )(h, w.reshape(1, D))


def _rms_bwd(h, w, dy, dres, name):
    T, D = h.shape
    tr = _tile(T, 528, 8)

    def body(h_ref, w_ref, dy_ref, dres_ref, dh_ref, dw_ref):
        x = h_ref[...]
        rstd = lax.rsqrt(jnp.mean(x * x, axis=-1, keepdims=True) + EPS)
        xn = x * rstd
        dy = dy_ref[...]
        dxn = dy * w_ref[...]
        dh_ref[...] = dres_ref[...] + rstd * (dxn - xn * jnp.mean(dxn * xn, axis=-1, keepdims=True))
        dw = jnp.sum(dy * xn, axis=0, keepdims=True)

        @pl.when(pl.program_id(0) == 0)
        def _():
            dw_ref[...] = dw

        @pl.when(pl.program_id(0) > 0)
        def _():
            dw_ref[...] += dw

    row = pl.BlockSpec((tr, D), lambda i: (i, 0))
    vec = pl.BlockSpec((1, D), lambda i: (0, 0))
    return pl.pallas_call(
        body, name=name, grid=(T // tr,),
        in_specs=[row, vec, row, row], out_specs=(row, vec),
        out_shape=(jax.ShapeDtypeStruct((T, D), F32), jax.ShapeDtypeStruct((1, D), F32)),
        compiler_params=_cp("arbitrary"),
    )(h, w.reshape(1, D), dy, dres)


def _final_loss(h, w, target, first_row, name):
    T, D = h.shape
    tr = CHUNK
    assert first_row == tr

    def body(h_ref, w_ref, t_ref, loss_ref, dh_ref, dw_ref):
        i = pl.program_id(0)
        x = h_ref[...]
        rstd = lax.rsqrt(jnp.mean(x * x, axis=-1, keepdims=True) + EPS)
        xn = x * rstd
        w = w_ref[...]
        live = i > 0
        err = jnp.where(live, xn * w - t_ref[...], 0.0)
        part = 0.5 * jnp.sum(jnp.mean(err * err, axis=-1, keepdims=True), axis=0, keepdims=True)
        dout = err * (1.0 / D)
        dxn = dout * w
        dh_ref[...] = rstd * (dxn - xn * jnp.mean(dxn * xn, axis=-1, keepdims=True))
        dw = jnp.sum(dout * xn, axis=0, keepdims=True)

        @pl.when(i == 0)
        def _():
            loss_ref[...] = part
            dw_ref[...] = dw

        @pl.when(i > 0)
        def _():
            loss_ref[...] += part
            dw_ref[...] += dw

    row = pl.BlockSpec((tr, D), lambda i: (i, 0))
    vec = pl.BlockSpec((1, D), lambda i: (0, 0))
    return pl.pallas_call(
        body, name=name, grid=(T // tr,),
        in_specs=[row, vec, pl.BlockSpec((tr, D), lambda i: (jnp.maximum(i - 1, 0), 0))],
        out_specs=(pl.BlockSpec((1, 1), lambda i: (0, 0)), row, vec),
        out_shape=(jax.ShapeDtypeStruct((1, 1), F32), jax.ShapeDtypeStruct((T, D), F32), jax.ShapeDtypeStruct((1, D), F32)),
        compiler_params=_cp("arbitrary"),
    )(h, w.reshape(1, D), target)


def _gate_fwd(pg, bg, y_pool, y_ssd, name):
    T, D = y_pool.shape
    tr = _tile(T, 528, 16)

    def body(gp_ref, gs_ref, bp_ref, bs_ref, yp_ref, ys_ref, o_ref):
        gp = _sigmoid(gp_ref[...] + bp_ref[...])
        gs = _sigmoid(gs_ref[...] + bs_ref[...])
        o_ref[...] = (gp * yp_ref[...] + gs * ys_ref[...]).astype(BF16)

    row = pl.BlockSpec((tr, D), lambda i: (i, 0))
    row1 = pl.BlockSpec((tr, D), lambda i: (i, 1))
    vec = pl.BlockSpec((1, D), lambda i: (0, 0))
    vec1 = pl.BlockSpec((1, D), lambda i: (0, 1))
    b2 = bg.reshape(1, 2 * D)
    return pl.pallas_call(
        body, name=name, grid=(T // tr,),
        in_specs=[row, row1, vec, vec1, row, row], out_specs=row,
        out_shape=jax.ShapeDtypeStruct((T, D), BF16), compiler_params=_cp("parallel"),
    )(pg, pg, b2, b2, y_pool, y_ssd)


def _gate_bwd(pg, bg, y_pool, y_ssd, dmix, name):
    T, D = y_pool.shape
    tr = _tile(T, 528, 16)

    def body(gp_ref, gs_ref, bp_ref, bs_ref, yp_ref, ys_ref, dm_ref, dg_ref, dyp_ref, dys_ref, db_ref):
        gp = _sigmoid(gp_ref[...] + bp_ref[...])
        gs = _sigmoid(gs_ref[...] + bs_ref[...])
        dm = dm_ref[...]
        dyp_ref[...] = (dm * gp).astype(BF16)
        dys_ref[...] = (dm * gs).astype(BF16)
        dgp = dm * yp_ref[...] * gp * (1.0 - gp)
        dgs = dm * ys_ref[...] * gs * (1.0 - gs)
        dg_ref[:, :D] = dgp.astype(BF16)
        dg_ref[:, D:] = dgs.astype(BF16)
        db = jnp.concatenate([jnp.sum(dgp, axis=0, keepdims=True), jnp.sum(dgs, axis=0, keepdims=True)], axis=1)

        @pl.when(pl.program_id(0) == 0)
        def _():
            db_ref[...] = db

        @pl.when(pl.program_id(0) > 0)
        def _():
            db_ref[...] += db

    row = pl.BlockSpec((tr, D), lambda i: (i, 0))
    row1 = pl.BlockSpec((tr, D), lambda i: (i, 1))
    wide = pl.BlockSpec((tr, 2 * D), lambda i: (i, 0))
    vec = pl.BlockSpec((1, D), lambda i: (0, 0))
    vec1 = pl.BlockSpec((1, D), lambda i: (0, 1))
    vec2 = pl.BlockSpec((1, 2 * D), lambda i: (0, 0))
    b2 = bg.reshape(1, 2 * D)
    return pl.pallas_call(
        body, name=name, grid=(T // tr,),
        in_specs=[row, row1, vec, vec1, row, row, row], out_specs=(wide, row, row, vec2),
        out_shape=(jax.ShapeDtypeStruct((T, 2 * D), BF16), jax.ShapeDtypeStruct((T, D), BF16),
                   jax.ShapeDtypeStruct((T, D), BF16), jax.ShapeDtypeStruct((1, 2 * D), F32)),
        compiler_params=_cp("arbitrary"),
    )(pg, pg, b2, b2, y_pool, y_ssd, dmix)


def _pool_count(c, pad, window):
    pos = c * CHUNK + _iota((CHUNK, 1), 0) - pad
    return jnp.clip(pos + 1, 1, window).astype(F32)


def _by_group(g, vals):
    out = vals[-1]
    for k in range(len(vals) - 2, -1, -1):
        out = jnp.where(g == k, vals[k], out)
    return out


def _by_row_block(rows, vals, block):
    out = vals[0]
    for r in range(1, len(vals)):
        out = jnp.where(rows >= r * block, vals[r], out)
    return out


def _pool_fwd(u, wg, scale, pad, name):
    T, C = u.shape
    G, Cg, _ = wg.shape
    nc = T // CHUNK

    def body(u_ref, wg_ref, s_ref, p_ref, y_ref):
        g = pl.program_id(0)
        window = _by_group(g, POOL_WINDOWS)

        def chunk(c, carry):
            r0 = pl.multiple_of(c * CHUNK, CHUNK)
            h0 = pl.multiple_of(jnp.maximum(r0 - POOL_HALO, 0), 8)
            halo = jnp.where(c > 0, u_ref[pl.ds(h0, POOL_HALO), :], 0.0)
            xc = u_ref[pl.ds(r0, CHUNK), :]
            s = jnp.concatenate([halo, xc], axis=0)
            sums = []
            k = 1
            while k < POOL_WINDOWS[-1]:
                s = s + pltpu.roll(s, k, 0)
                k *= 2
                if k in POOL_WINDOWS:
                    sums.append(s[POOL_HALO:])
            wsum = _by_group(g, sums)
            pooled = wsum / _pool_count(c, pad, window) - xc
            pb = pooled.astype(BF16)
            p_ref[pl.ds(r0, CHUNK), :] = pb
            y_ref[pl.ds(r0, CHUNK), :] = (_dot(pb, wg_ref[0]) * s_ref[...]).astype(BF16)
            return carry

        lax.fori_loop(0, nc, chunk, 0)

    col = pl.BlockSpec((T, Cg), lambda g: (0, g))
    return pl.pallas_call(
        body, name=name, grid=(G,),
        in_specs=[col, pl.BlockSpec((1, Cg, Cg), lambda g: (g, 0, 0)), pl.BlockSpec((1, Cg), lambda g: (0, g))],
        out_specs=(col, col),
        out_shape=(jax.ShapeDtypeStruct((T, C), BF16), jax.ShapeDtypeStruct((T, C), BF16)),
        compiler_params=_cp("parallel"),
    )(u, wg, scale)


def _pool_bwd(pooled, wg, scale, dy, pad, name):
    T, C = dy.shape
    G, Cg, _ = wg.shape
    nc = T // CHUNK

    def body(p_ref, wg_ref, s_ref, dy_ref, du_ref, dwg_ref, ds_ref, halo_ref):
        g = pl.program_id(0)
        window = _by_group(g, POOL_WINDOWS)
        halo_ref[...] = jnp.zeros_like(halo_ref)
        dwg_ref[...] = jnp.zeros_like(dwg_ref)
        ds_ref[...] = jnp.zeros_like(ds_ref)

        def chunk(i, carry):
            c = nc - 1 - i
            r0 = pl.multiple_of(c * CHUNK, CHUNK)
            pb = p_ref[pl.ds(r0, CHUNK), :]
            dyc = dy_ref[pl.ds(r0, CHUNK), :]
            w = wg_ref[0]
            ypre = _dot(pb, w)
            ds_ref[...] += jnp.sum(dyc * ypre, axis=0, keepdims=True)
            dyp = (dyc * s_ref[...]).astype(BF16)
            dwg_ref[0] += _dot(pb, dyp, "tn")
            dpool = _dot(dyp, w, "nt")
            q = dpool / _pool_count(c, pad, window)
            s = jnp.concatenate([q, halo_ref[...]], axis=0)
            n = CHUNK + POOL_HALO
            sums = []
            k = 1
            while k < POOL_WINDOWS[-1]:
                s = s + pltpu.roll(s, n - k, 0)
                k *= 2
                if k in POOL_WINDOWS:
                    sums.append(s[:CHUNK])
            du = _by_group(g, sums) - dpool
            rows = r0 + _iota((CHUNK, 1), 0)
            du_ref[pl.ds(r0, CHUNK), :] = jnp.where(rows >= pad, du, 0.0).astype(BF16)
            halo_ref[...] = q[:POOL_HALO]
            return carry

        lax.fori_loop(0, nc, chunk, 0)

    col = pl.BlockSpec((T, Cg), lambda g: (0, g))
    return pl.pallas_call(
        body, name=name, grid=(G,),
        in_specs=[col, pl.BlockSpec((1, Cg, Cg), lambda g: (g, 0, 0)), pl.BlockSpec((1, Cg), lambda g: (0, g)), col],
        out_specs=(col, pl.BlockSpec((1, Cg, Cg), lambda g: (g, 0, 0)), pl.BlockSpec((1, Cg), lambda g: (0, g))),
        out_shape=(jax.ShapeDtypeStruct((T, C), BF16), jax.ShapeDtypeStruct((G, Cg, Cg), F32), jax.ShapeDtypeStruct((1, C), F32)),
        scratch_shapes=[pltpu.VMEM((POOL_HALO, Cg), F32)],
        compiler_params=_cp("parallel"),
    )(pooled, wg, scale, dy)


def _conv_pre(x_ref, w_ref, b_ref, c, r0):
    h0 = pl.multiple_of(jnp.maximum(r0 - CONV_HALO, 0), 8)
    halo = jnp.where(c > 0, x_ref[pl.ds(h0, CONV_HALO), :], 0.0)
    xe = jnp.concatenate([halo, x_ref[pl.ds(r0, CHUNK), :]], axis=0)
    y = jnp.broadcast_to(b_ref[...], (CHUNK, xe.shape[1]))
    for k in range(CONV_WIDTH):
        shift = CONV_WIDTH - 1 - k
        xs = xe if shift == 0 else pltpu.roll(xe, shift, 0)
        y = y + xs[CONV_HALO:] * w_ref[k:k + 1, :]
    return y, xe


def _conv_fwd(x, w, b, name):
    T = x.shape[0]
    C = w.shape[1]
    tc = _tile(C, 256, 128)
    nc = T // CHUNK

    def body(x_ref, w_ref, b_ref, o_ref):
        def chunk(c, carry):
            r0 = pl.multiple_of(c * CHUNK, CHUNK)
            y, _ = _conv_pre(x_ref, w_ref, b_ref, c, r0)
            o_ref[pl.ds(r0, CHUNK), :] = y * _sigmoid(y)
            return carry

        lax.fori_loop(0, nc, chunk, 0)

    col = pl.BlockSpec((T, tc), lambda j: (0, j))
    return pl.pallas_call(
        body, name=name, grid=(C // tc,),
        in_specs=[col, pl.BlockSpec((CONV_WIDTH, tc), lambda j: (0, j)), pl.BlockSpec((1, tc), lambda j: (0, j))],
        out_specs=col, out_shape=jax.ShapeDtypeStruct((T, C), F32), compiler_params=_cp("parallel"),
    )(x, w, b)


def _conv_bwd(x, w, b, dact, pad, name):
    T = x.shape[0]
    C = w.shape[1]
    tc = _tile(C, 256, 128)
    nc = T // CHUNK

    def body(x_ref, w_ref, b_ref, da_ref, dx_ref, dw_ref, db_ref, halo_ref):
        halo_ref[...] = jnp.zeros_like(halo_ref)
        dw_ref[...] = jnp.zeros_like(dw_ref)
        db_ref[...] = jnp.zeros_like(db_ref)

        def chunk(i, carry):
            c = nc - 1 - i
            r0 = pl.multiple_of(c * CHUNK, CHUNK)
            y, xe = _conv_pre(x_ref, w_ref, b_ref, c, r0)
            sg = _sigmoid(y)
            dpre = da_ref[pl.ds(r0, CHUNK), :] * (sg * (1.0 + y * (1.0 - sg)))
            db_ref[...] += jnp.sum(dpre, axis=0, keepdims=True)
            de = jnp.concatenate([dpre, halo_ref[...]], axis=0)
            n = CHUNK + CONV_HALO
            dx = jnp.zeros_like(dpre)
            for k in range(CONV_WIDTH):
                shift = CONV_WIDTH - 1 - k
                xs = xe if shift == 0 else pltpu.roll(xe, shift, 0)
                dw_ref[k:k + 1, :] += jnp.sum(dpre * xs[CONV_HALO:], axis=0, keepdims=True)
                ds = de if shift == 0 else pltpu.roll(de, n - shift, 0)
                dx = dx + ds[:CHUNK] * w_ref[k:k + 1, :]
            rows = r0 + _iota((CHUNK, 1), 0)
            dx_ref[pl.ds(r0, CHUNK), :] = jnp.where(rows >= pad, dx, 0.0).astype(BF16)
            halo_ref[...] = dpre[:CONV_HALO]
            return carry

        lax.fori_loop(0, nc, chunk, 0)

    col = pl.BlockSpec((T, tc), lambda j: (0, j))
    wspec = pl.BlockSpec((CONV_WIDTH, tc), lambda j: (0, j))
    bspec = pl.BlockSpec((1, tc), lambda j: (0, j))
    return pl.pallas_call(
        body, name=name, grid=(C // tc,),
        in_specs=[col, wspec, bspec, col], out_specs=(col, wspec, bspec),
        out_shape=(jax.ShapeDtypeStruct((T, C), BF16), jax.ShapeDtypeStruct((CONV_WIDTH, C), F32), jax.ShapeDtypeStruct((1, C), F32)),
        scratch_shapes=[pltpu.VMEM((CONV_HALO, tc), F32)],
        compiler_params=_cp("parallel"),
    )(x, w, b, dact)


def _cumsum_rows(x, reverse=False):
    n = x.shape[0]
    idx = _iota(x.shape, 0)
    k = 1
    while k < n:
        if reverse:
            x = x + jnp.where(idx < n - k, pltpu.roll(x, n - k, 0), 0.0)
        else:
            x = x + jnp.where(idx >= k, pltpu.roll(x, k, 0), 0.0)
        k *= 2
    return x


def _softplus(x):
    return jnp.maximum(x, 0.0) + jnp.log1p(jnp.exp(-jnp.abs(x)))


def _heads_to_lanes(cols, width):
    return jnp.concatenate([jnp.broadcast_to(c, (c.shape[0], width)) for c in cols], axis=1)


def _ssd_common(c, pad, n_heads, dtr_ref, dtb_ref, al_ref):
    rows = c * CHUNK + _iota((CHUNK, 1), 0)
    valid = rows >= pad
    live = jnp.logical_and(valid, _iota((1, LANES), 1) < n_heads)
    pre = dtr_ref[...] + dtb_ref[...]
    dt = jnp.where(live, _softplus(pre), 0.0)
    a = -jnp.exp(al_ref[...])
    cs = _cumsum_rows(dt * a)
    return valid, live, dt, a, cs, cs.T, _sigmoid(pre)


def _ssd_specs(T, DI, GN, cfirst):
    xcol = DI // GN

    def at(col):
        return lambda c: (cfirst(c), col)

    x = pl.BlockSpec((CHUNK, DI), at(0))
    b = pl.BlockSpec((CHUNK, GN), at(xcol))
    cm = pl.BlockSpec((CHUNK, GN), at(xcol + 1))
    dt = pl.BlockSpec((CHUNK, LANES), at(0))
    vec = pl.BlockSpec((1, LANES), lambda c: (0, 0))
    nw = pl.BlockSpec((1, DI), lambda c: (0, 0))
    return x, b, cm, dt, vec, nw


def _ssd_fwd(xbc, pdt, pz, dt_bias, a_log, d_skip, norm_w, pad, n_heads, name):
    T = xbc.shape[0]
    DI = pz.shape[1]
    P = DI // n_heads
    GN = (xbc.shape[1] - DI) // 2
    G = GN // D_STATE
    R = n_heads // G
    GW = R * P
    nc = T // CHUNK
    Q, N = CHUNK, D_STATE

    def body(x_ref, b_ref, c_ref, dtr_ref, z_ref, dtb_ref, al_ref, dsk_ref, nw_ref, y_ref, yn_ref, prev_ref, s_ref):
        c = pl.program_id(0)

        @pl.when(c == 0)
        def _():
            s_ref[...] = jnp.zeros_like(s_ref)

        valid, _, dt, _, cs, cst, _ = _ssd_common(c, pad, n_heads, dtr_ref, dtb_ref, al_ref)
        e = jnp.exp(cs)
        cs_last = cs[Q - 1:Q, :]
        dec = jnp.exp(cs_last - cs)
        e_last = jnp.exp(cs_last)
        tri = _iota((Q, Q), 0) >= _iota((Q, Q), 1)
        state_rows = _iota((GW, 1), 0)
        for g in range(G):
            gs = slice(g * GW, (g + 1) * GW)
            bg = jnp.where(valid, b_ref[:, g * N:(g + 1) * N], 0.0).astype(BF16)
            cg = jnp.where(valid, c_ref[:, g * N:(g + 1) * N], 0.0).astype(BF16)
            xg = jnp.where(valid, x_ref[:, gs], 0.0)
            sg = s_ref[gs, :]
            prev_ref[0, gs, :] = sg
            cb = _dot(cg, bg, "nt")
            csg = _dot(cg, sg, "nt")
            ys, xds = [], []
            for r in range(R):
                h = g * R + r
                xdt = xg[:, r * P:(r + 1) * P] * dt[:, h:h + 1]
                lmat = jnp.exp(jnp.where(tri, cs[:, h:h + 1] - cst[h:h + 1, :], -jnp.inf))
                ys.append(_dot(cb * lmat, xdt) + csg[:, r * P:(r + 1) * P] * e[:, h:h + 1])
                xds.append(xdt * dec[:, h:h + 1])
            yg = jnp.concatenate(ys, axis=1)
            y_ref[:, gs] = yg
            decay = _by_row_block(state_rows, [e_last[:, g * R + r:g * R + r + 1] for r in range(R)], P)
            s_ref[gs, :] = sg * decay + _dot(jnp.concatenate(xds, axis=1), bg, "tn")
            dsk = _heads_to_lanes([dsk_ref[:, g * R + r:g * R + r + 1] for r in range(R)], P)
            z = z_ref[:, gs]
            gz = (yg + xg * dsk) * (z * _sigmoid(z))
            rstd = lax.rsqrt(jnp.mean(gz * gz, axis=-1, keepdims=True) + EPS)
            yn_ref[:, gs] = ((gz * rstd) * nw_ref[:, gs]).astype(BF16)

    x_s, b_s, c_s, dt_s, vec, nw = _ssd_specs(T, DI, GN, lambda c: c)
    wide = pl.BlockSpec((Q, DI), lambda c: (c, 0))
    return pl.pallas_call(
        body, name=name, grid=(nc,),
        in_specs=[x_s, b_s, c_s, dt_s, wide, vec, vec, vec, nw],
        out_specs=(wide, wide, pl.BlockSpec((1, DI, N), lambda c: (c, 0, 0))),
        out_shape=(jax.ShapeDtypeStruct((T, DI), F32), jax.ShapeDtypeStruct((T, DI), BF16), jax.ShapeDtypeStruct((nc, DI, N), F32)),
        scratch_shapes=[pltpu.VMEM((DI, N), F32)],
        compiler_params=_cp("arbitrary"),
    )(xbc, xbc, xbc, pdt, pz, dt_bias, a_log, d_skip, norm_w)


def _ssd_bwd(xbc, pdt, pz, dt_bias, a_log, d_skip, norm_w, y, prev, dyn, pad, n_heads, name):
    T, W = xbc.shape
    DI = pz.shape[1]
    P = DI // n_heads
    GN = (W - DI) // 2
    G = GN // D_STATE
    R = n_heads // G
    GW = R * P
    nc = T // CHUNK
    Q, N = CHUNK, D_STATE

    def body(x_ref, b_ref, c_ref, dtr_ref, z_ref, dtb_ref, al_ref, dsk_ref, nw_ref, y_ref, prev_ref, next_ref, dyn_ref,
             dxbc_ref, dz_ref, ddt_ref, ddtb_ref, dal_ref, ddsk_ref, dnw_ref, ds_ref):
        i = pl.program_id(0)
        c = nc - 1 - i

        @pl.when(i == 0)
        def _():
            ds_ref[...] = jnp.zeros_like(ds_ref)
            ddtb_ref[...] = jnp.zeros_like(ddtb_ref)
            dal_ref[...] = jnp.zeros_like(dal_ref)
            ddsk_ref[...] = jnp.zeros_like(ddsk_ref)
            dnw_ref[...] = jnp.zeros_like(dnw_ref)

        valid, live, dt, a, cs, cst, sig_pre = _ssd_common(c, pad, n_heads, dtr_ref, dtb_ref, al_ref)
        e = jnp.exp(cs)
        cs_last = cs[Q - 1:Q, :]
        dec = jnp.exp(cs_last - cs)
        e_last = jnp.exp(cs_last)
        tri = _iota((Q, Q), 0) >= _iota((Q, Q), 1)
        state_rows = _iota((GW, 1), 0)
        lane = _iota((1, LANES), 1)
        head_rows = _iota((LANES, 1), 0)
        dcs = jnp.zeros((Q, LANES), F32)
        dcs_rows = jnp.zeros((Q, LANES), F32)
        dcs_cols = jnp.zeros((LANES, Q), F32)
        ddt = jnp.zeros((Q, LANES), F32)
        c_end = jnp.zeros((1, LANES), F32)
        ddsk = jnp.zeros((1, LANES), F32)
        for g in range(G):
            gs = slice(g * GW, (g + 1) * GW)
            bg = jnp.where(valid, b_ref[:, g * N:(g + 1) * N], 0.0).astype(BF16)
            cg = jnp.where(valid, c_ref[:, g * N:(g + 1) * N], 0.0).astype(BF16)
            xg = jnp.where(valid, x_ref[:, gs], 0.0)
            s_prev = prev_ref[0, gs, :]
            dsg = ds_ref[gs, :]
            end = dsg * next_ref[0, gs, :]
            yg = y_ref[:, gs]
            dsk = _heads_to_lanes([dsk_ref[:, g * R + r:g * R + r + 1] for r in range(R)], P)
            ytot = yg + xg * dsk
            z = z_ref[:, gs]
            sz = _sigmoid(z)
            silu = z * sz
            gz = ytot * silu
            rstd = lax.rsqrt(jnp.mean(gz * gz, axis=-1, keepdims=True) + EPS)
            gn = gz * rstd
            dyn_g = dyn_ref[:, gs]
            dnw_ref[:, gs] += jnp.sum(dyn_g * gn, axis=0, keepdims=True)
            dgn = dyn_g * nw_ref[:, gs]
            dgz = rstd * (dgn - gn * jnp.mean(dgn * gn, axis=-1, keepdims=True))
            dz_ref[:, gs] = (dgz * ytot * (sz * (1.0 + z * (1.0 - sz)))).astype(BF16)
            dy = dgz * silu
            dsk_rows = jnp.sum(dy * xg, axis=0, keepdims=True)
            cb = _dot(cg, bg, "nt")
            bds = _dot(bg, dsg, "nt")
            csg = _dot(cg, s_prev, "nt")
            dcb = jnp.zeros((Q, Q), F32)
            dxs, dyes, xds = [], [], []
            for r in range(R):
                h = g * R + r
                hs = slice(r * P, (r + 1) * P)
                dt_c = dt[:, h:h + 1]
                xr = xg[:, hs]
                xdt = xr * dt_c
                lmat = jnp.exp(jnp.where(tri, cs[:, h:h + 1] - cst[h:h + 1, :], -jnp.inf))
                m = cb * lmat
                dyr = dy[:, hs]
                dm = _dot(dyr, xdt, "nt")
                dcb = dcb + dm * lmat
                w_rc = dm * m
                dcs_rows = jnp.where(lane == h, jnp.sum(w_rc, axis=1, keepdims=True), dcs_rows)
                dcs_cols = jnp.where(head_rows == h, jnp.sum(w_rc, axis=0, keepdims=True), dcs_cols)
                bds_r = bds[:, hs] * dec[:, h:h + 1]
                dxdt = _dot(m, dyr, "tn") + bds_r
                p_off = (jnp.sum(dyr * csg[:, hs], axis=1, keepdims=True) * e[:, h:h + 1]
                         - dt_c * jnp.sum(xr * bds_r, axis=1, keepdims=True))
                dcs = jnp.where(lane == h, p_off, dcs)
                ddt = jnp.where(lane == h, jnp.sum(xr * dxdt, axis=1, keepdims=True), ddt)
                c_end = jnp.where(lane == h, jnp.sum(end[hs, :], keepdims=True).reshape(1, 1), c_end)
                ddsk = jnp.where(lane == h, jnp.sum(dsk_rows[:, hs], axis=1, keepdims=True), ddsk)
                dxs.append(dxdt * dt_c + dy[:, hs] * dsk[:, hs])
                dyes.append(dyr * e[:, h:h + 1])
                xds.append(xdt * dec[:, h:h + 1])
            dye = jnp.concatenate(dyes, axis=1)
            xd = jnp.concatenate(xds, axis=1)
            dcbb = dcb.astype(BF16)
            dc = _dot(dcbb, bg) + _dot(dye, s_prev)
            db = _dot(dcbb, cg, "tn") + _dot(xd, dsg)
            decay = _by_row_block(state_rows, [e_last[:, g * R + r:g * R + r + 1] for r in range(R)], P)
            ds_ref[gs, :] = dsg * decay + _dot(dye, cg, "tn")
            dxbc_ref[:, gs] = jnp.where(valid, jnp.concatenate(dxs, axis=1), 0.0)
            dxbc_ref[:, DI + g * N:DI + (g + 1) * N] = jnp.where(valid, db, 0.0)
            dxbc_ref[:, DI + GN + g * N:DI + GN + (g + 1) * N] = jnp.where(valid, dc, 0.0)
        da_cs = _cumsum_rows(dcs + (dcs_rows - dcs_cols.T), reverse=True) + c_end
        ddt_all = jnp.where(live, da_cs * a + ddt, 0.0)
        ddt_raw = ddt_all * sig_pre
        ddt_ref[...] = ddt_raw.astype(BF16)
        ddtb_ref[...] += jnp.sum(ddt_raw, axis=0, keepdims=True)
        dal_ref[...] += jnp.sum(da_cs * dt, axis=0, keepdims=True) * a
        ddsk_ref[...] += ddsk

    rev = lambda i: nc - 1 - i
    x_s, b_s, c_s, dt_s, vec, nw = _ssd_specs(T, DI, GN, rev)
    wide = pl.BlockSpec((Q, DI), lambda i: (rev(i), 0))
    st = pl.BlockSpec((1, DI, N), lambda i: (rev(i), 0, 0))
    st_next = pl.BlockSpec((1, DI, N), lambda i: (jnp.minimum(rev(i) + 1, nc - 1), 0, 0))
    return pl.pallas_call(
        body, name=name, grid=(nc,),
        in_specs=[x_s, b_s, c_s, dt_s, wide, vec, vec, vec, nw, wide, st, st_next, wide],
        out_specs=(pl.BlockSpec((Q, W), lambda i: (rev(i), 0)), wide, dt_s, vec, vec, vec, nw),
        out_shape=(jax.ShapeDtypeStruct((T, W), F32), jax.ShapeDtypeStruct((T, DI), BF16), jax.ShapeDtypeStruct((T, LANES), BF16),
                   jax.ShapeDtypeStruct((1, LANES), F32), jax.ShapeDtypeStruct((1, LANES), F32),
                   jax.ShapeDtypeStruct((1, LANES), F32), jax.ShapeDtypeStruct((1, DI), F32)),
        scratch_shapes=[pltpu.VMEM((DI, N), F32)],
        compiler_params=_cp("arbitrary"),
    )(xbc, xbc, xbc, pdt, pz, dt_bias, a_log, d_skip, norm_w, y, prev, prev, dyn)


def _adamw(w, g, m, v, name):
    rows, cols = w.shape
    tr = _tile(rows, 256, 8)

    def body(w_ref, g_ref, m_ref, v_ref, d_ref, nm_ref, nv_ref):
        g = g_ref[...]
        m = ADAM_B1 * m_ref[...] + (1.0 - ADAM_B1) * g
        v = ADAM_B2 * v_ref[...] + (1.0 - ADAM_B2) * (g * g)
        m_hat = m / (1.0 - ADAM_B1 ** ADAM_STEP)
        v_hat = v / (1.0 - ADAM_B2 ** ADAM_STEP)
        d_ref[...] = -ADAM_LR * (m_hat / (jnp.sqrt(v_hat) + ADAM_EPS) + ADAM_WD * w_ref[...])
        nm_ref[...] = m
        nv_ref[...] = v

    blk = pl.BlockSpec((tr, cols), lambda i: (i, 0))
    out = jax.ShapeDtypeStruct((rows, cols), F32)
    return pl.pallas_call(
        body, name=name, grid=(rows // tr,), in_specs=[blk] * 4, out_specs=(blk,) * 3, out_shape=(out,) * 3,
        compiler_params=_cp("parallel"),
    )(w, g, m, v)


def _final_sum(own, others, name):
    rows, cols = own.shape
    tr = _tile(rows, 512, 16)

    def body(o_ref, a_ref, b_ref, c_ref, out_ref):
        out_ref[...] = ((o_ref[...] + a_ref[0].astype(F32)) + b_ref[0].astype(F32)) + c_ref[0].astype(F32)

    blk = pl.BlockSpec((tr, cols), lambda i: (i, 0))
    return pl.pallas_call(
        body, name=name, grid=(rows // tr,),
        in_specs=[blk] + [pl.BlockSpec((1, tr, cols), functools.partial(lambda j, i: (j, i, 0), j)) for j in range(3)],
        out_specs=blk, out_shape=jax.ShapeDtypeStruct((rows, cols), F32), compiler_params=_cp("parallel"),
    )(own, others, others, others)


def _sum_stack(parts, name, out_dtype=F32, lead=None):
    rows, cols = parts[0].shape
    tr = _tile(rows, 512, 16)

    def body(*refs):
        acc = refs[0][...].astype(F32)
        for r in refs[1:-1]:
            acc = acc + r[...].astype(F32)
        refs[-1][...] = acc.astype(out_dtype)

    blk = pl.BlockSpec((tr, cols), lambda i: (i, 0))
    return pl.pallas_call(
        body, name=name, grid=(rows // tr,), in_specs=[blk] * len(parts), out_specs=blk,
        out_shape=jax.ShapeDtypeStruct((rows, cols), out_dtype), compiler_params=_cp("parallel"),
    )(*parts)


def _place():
    return lax.axis_index("x"), lax.axis_index("y"), lax.axis_index("c")


def _other_chips(x, y):
    return [(1 - x, y), (x, 1 - y), (1 - x, 1 - y)]


def _all_gather(shards, name):
    nb = len(shards)

    def body(*refs):
        ins, outs = refs[:nb], refs[nb:2 * nb]
        send_sems, recv_sems, local_sems = refs[2 * nb:]
        x, y, c = _place()
        me, sibling = (x, y, c), (x, y, 1 - c)
        chips = _other_chips(x, y)

        def copy(q, k, block, to, src=None):
            dst = outs[q].at[4 * block[0] + 2 * block[1] + block[2]]
            return pltpu.make_async_remote_copy(
                src_ref=dst if src is None else src, dst_ref=dst,
                send_sem=send_sems.at[7 * q + k], recv_sem=recv_sems.at[7 * q + k], device_id=to, device_id_type=MESH)

        started = []
        for q in range(nb):
            mine = pltpu.make_async_copy(ins[q], outs[q].at[4 * x + 2 * y + c], local_sems.at[q])
            mine.start()
            started.append(mine)
        first = []
        for q in range(nb):
            first.append(copy(q, 0, me, sibling, src=ins[q]))
            first += [copy(q, 1 + j, me, (*chip, c), src=ins[q]) for j, chip in enumerate(chips)]
        for cp in first:
            cp.start()
        passed = []
        for j, chip in enumerate(chips):
            for q in range(nb):
                copy(q, 1 + j, (*chip, c), me).wait_recv()
                fwd = copy(q, 4 + j, (*chip, c), sibling)
                fwd.start()
                passed.append(fwd)
        for q in range(nb):
            copy(q, 0, sibling, me).wait_recv()
            for j, chip in enumerate(chips):
                copy(q, 4 + j, (*chip, 1 - c), me).wait_recv()
        for cp in first + passed:
            cp.wait_send()
        for mine in started:
            mine.wait()

    return pl.pallas_call(
        body, name=name, in_specs=[ANY] * nb, out_specs=tuple([ANY] * nb),
        out_shape=tuple(jax.ShapeDtypeStruct((N_DEV,) + s.shape, s.dtype) for s in shards),
        scratch_shapes=[pltpu.SemaphoreType.DMA((7 * nb,)), pltpu.SemaphoreType.DMA((7 * nb,)), pltpu.SemaphoreType.DMA((nb,))],
    )(*shards)


def _swap_core_halves(gs, name):
    nb = len(gs)

    def body(*refs):
        g_refs, o_refs = refs[:nb], refs[nb:2 * nb]
        send_sems, recv_sems = refs[2 * nb:]
        x, y, c = _place()
        cps = []
        for q in range(nb):
            cp = pltpu.make_async_remote_copy(
                src_ref=g_refs[q].at[:, 1 - c], dst_ref=o_refs[q], send_sem=send_sems.at[q], recv_sem=recv_sems.at[q],
                device_id=(x, y, 1 - c), device_id_type=MESH)
            cp.start()
            cps.append(cp)
        for cp in cps:
            cp.wait()

    return pl.pallas_call(
        body, name=name, in_specs=[ANY] * nb, out_specs=tuple([ANY] * nb),
        out_shape=tuple(jax.ShapeDtypeStruct((4,) + g.shape[2:], g.dtype) for g in gs),
        scratch_shapes=[pltpu.SemaphoreType.DMA((nb,)), pltpu.SemaphoreType.DMA((nb,))],
    )(*gs)


def _exchange_chips(parts, name):
    nb = len(parts)

    def body(*refs):
        p_refs, o_refs = refs[:nb], refs[nb:2 * nb]
        send_sems, recv_sems = refs[2 * nb:]
        x, y, c = _place()
        cps = []
        for j, chip in enumerate(_other_chips(x, y)):
            for q in range(nb):
                cp = pltpu.make_async_remote_copy(
                    src_ref=p_refs[q].at[j], dst_ref=o_refs[q].at[j], send_sem=send_sems.at[3 * q + j],
                    recv_sem=recv_sems.at[3 * q + j], device_id=(*chip, c), device_id_type=MESH)
                cp.start()
                cps.append(cp)
        for cp in cps:
            cp.wait()

    return pl.pallas_call(
        body, name=name, in_specs=[ANY] * nb, out_specs=tuple([ANY] * nb),
        out_shape=tuple(jax.ShapeDtypeStruct(p.shape, p.dtype) for p in parts),
        scratch_shapes=[pltpu.SemaphoreType.DMA((3 * nb,)), pltpu.SemaphoreType.DMA((3 * nb,))],
    )(*parts)


def _chip_partials(g, got, place, name):
    _, _, rows, cols = g.shape
    tr = _tile(rows, 512, 16)

    def body_own(p_ref, g_ref, r_ref, o_ref):
        o_ref[...] = g_ref[0, 0] + r_ref[0]

    def body_parts(p_ref, g_ref, r_ref, o_ref):
        o_ref[0] = (g_ref[0, 0] + r_ref[0]).astype(BF16)

    own = pl.pallas_call(
        body_own, name=name + "_own",
        grid_spec=pltpu.PrefetchScalarGridSpec(
            num_scalar_prefetch=1, grid=(rows // tr,),
            in_specs=[pl.BlockSpec((1, 1, tr, cols), lambda i, p: (p[0], p[4], i, 0)),
                      pl.BlockSpec((1, tr, cols), lambda i, p: (p[0], i, 0))],
            out_specs=pl.BlockSpec((tr, cols), lambda i, p: (i, 0))),
        out_shape=jax.ShapeDtypeStruct((rows, cols), F32), compiler_params=_cp("parallel"),
    )(place, g, got)
    parts = pl.pallas_call(
        body_parts, name=name + "_parts",
        grid_spec=pltpu.PrefetchScalarGridSpec(
            num_scalar_prefetch=1, grid=(3, rows // tr),
            in_specs=[pl.BlockSpec((1, 1, tr, cols), lambda j, i, p: (p[1 + j], p[4], i, 0)),
                      pl.BlockSpec((1, tr, cols), lambda j, i, p: (p[1 + j], i, 0))],
            out_specs=pl.BlockSpec((1, tr, cols), lambda j, i, p: (j, i, 0))),
        out_shape=jax.ShapeDtypeStruct((3, rows, cols), BF16), compiler_params=_cp("parallel", "parallel"),
    )(place, g, got)
    return own, parts


class _Shard:
    def __init__(self, name, axis):
        self.name, self.axis = name, axis


BIG = [_Shard("w_in", 2), _Shard("pool_w_group", 2), _Shard("w_pool_up", 1), _Shard("w_ssd_out", 1),
       _Shard("w_o", 1), _Shard("w_ff1", 2), _Shard("w_ff2", 1)]
SMALL_SHARDED = [_Shard("meta_tokens", 1), _Shard("conv_w", 2)]
REPLICATED = ["mix_norm_w", "b_gate", "pool_scale", "conv_b", "dt_bias", "a_log", "d_skip", "ssd_norm_w",
              "mlp_norm_w", "final_norm_w"]
WEIGHTS = ["meta_tokens", "mix_norm_w", "w_in", "b_gate", "pool_w_group", "pool_scale", "w_pool_up", "conv_w", "conv_b",
           "dt_bias", "a_log", "d_skip", "ssd_norm_w", "w_ssd_out", "w_o", "mlp_norm_w", "w_ff1", "w_ff2", "final_norm_w"]


def _rows2(a):
    return a.reshape(-1, a.shape[-1])


def _unshard(stack, shard_shape, axis):
    t = stack.reshape((N_DEV,) + tuple(shard_shape))
    return jnp.concatenate([t[d] for d in range(N_DEV)], axis=axis)


def _reshard(full, axis):
    blocks = jnp.split(full, N_DEV, axis=axis)
    return jnp.stack([_rows2(b) for b in blocks]).reshape((4, 2) + _rows2(blocks[0]).shape)


def _lane_rows(a):
    n = a.size
    if n % LANES:
        return jnp.pad(a.reshape(-1), (0, (-n) % LANES)).reshape(-1, LANES)
    return a.reshape(-1, LANES)


def _unpack_small(buf, spans, shapes):
    out = []
    for (o, r), shp in zip(spans, shapes):
        n = 1
        for d in shp:
            n *= d
        t = buf[o:o + r]
        out.append(t.reshape(shp) if n == r * LANES else t.reshape(-1)[:n].reshape(shp))
    return out


def _pack_small(parts, mult):
    mats = [_lane_rows(p) for p in parts]
    spans, o = [], 0
    for t in mats:
        spans.append((o, t.shape[0]))
        o += t.shape[0]
    fill = (-o) % mult
    if fill:
        mats.append(jnp.zeros((fill, LANES), mats[0].dtype))
    return jnp.concatenate(mats, axis=0), spans


def _layer_fwd(h, lw, cfg, tag):
    pad, n_heads = cfg["pad"], cfg["n_heads"]
    u = _rms_fwd(h, lw["mix_norm_w"], f"rms_mix_{tag}")
    p_xbc = _mm(u, lw["w_xbc"], "nn", f"proj_xbc_{tag}")
    p_z = _mm(u, lw["w_z"], "nn", f"proj_z_{tag}")
    p_gate = _mm(u, lw["w_gate"], "nn", f"proj_gate_{tag}")
    p_pool = _mm(u, lw["w_pool"], "nn", f"proj_pool_{tag}")
    p_dt = _mm(u, lw["w_dt"], "nn", f"proj_dt_{tag}")
    pooled, y1 = _pool_fwd(p_pool, lw["pool_w_group"], lw["pool_scale"], pad, f"pool_fwd_{tag}")
    y_pool = _mm(y1, lw["w_pool_up"], "nn", f"pool_up_{tag}")
    xbc = _conv_fwd(p_xbc, lw["conv_w"], lw["conv_b"], f"conv_fwd_{tag}")
    y, yn, prev = _ssd_fwd(xbc, p_dt, p_z, lw["dt_bias"], lw["a_log"], lw["d_skip"], lw["ssd_norm_w"], pad, n_heads, f"ssd_fwd_{tag}")
    y_ssd = _mm(yn, lw["w_ssd_out"], "nn", f"ssd_out_{tag}")
    mix = _gate_fwd(p_gate, lw["b_gate"], y_pool, y_ssd, f"gate_fwd_{tag}")
    h_mid = _mm(mix, lw["w_o"], "nn", f"mix_out_{tag}", res=h)
    v = _rms_fwd(h_mid, lw["mlp_norm_w"], f"rms_mlp_{tag}")
    hid, act = _mm(v, lw["w_ff1"], "nn", f"ff1_{tag}", epi="relu2")
    h_out = _mm(act, lw["w_ff2"], "nn", f"ff2_{tag}", res=h_mid)
    saved = dict(h=h, u=u, p_xbc=p_xbc, p_z=p_z, p_gate=p_gate, p_dt=p_dt, pooled=pooled, y1=y1, y_pool=y_pool, xbc=xbc,
                 y=y, yn=yn, prev=prev, y_ssd=y_ssd, mix=mix, h_mid=h_mid, v=v, hid=hid, act=act)
    return h_out, saved


def _layer_bwd(dh, lw, s, cfg, tag):
    pad, n_heads = cfg["pad"], cfg["n_heads"]
    g = {}
    dhid = _mm(dh, lw["w_ff2"], "nt", f"d_act_{tag}", epi="drelu2", aux=s["hid"])
    g["w_ff2"] = _mm(s["act"], dh, "tn", f"dw_ff2_{tag}")
    dv = _mm(dhid, lw["w_ff1"], "nt", f"d_v_{tag}")
    g["w_ff1"] = _mm(s["v"], dhid, "tn", f"dw_ff1_{tag}")
    dh_mid, g["mlp_norm_w"] = _rms_bwd(s["h_mid"], lw["mlp_norm_w"], dv, dh, f"rms_mlp_bwd_{tag}")
    dmix = _mm(dh_mid, lw["w_o"], "nt", f"d_mix_{tag}")
    g["w_o"] = _mm(s["mix"], dh_mid, "tn", f"dw_o_{tag}")
    dgate, dy_pool, dy_ssd, g["b_gate"] = _gate_bwd(s["p_gate"], lw["b_gate"], s["y_pool"], s["y_ssd"], dmix, f"gate_bwd_{tag}")
    dy1 = _mm(dy_pool, lw["w_pool_up"], "nt", f"d_y1_{tag}")
    g["w_pool_up"] = _mm(s["y1"], dy_pool, "tn", f"dw_pool_up_{tag}")
    dpool, g["pool_w_group"], g["pool_scale"] = _pool_bwd(s["pooled"], lw["pool_w_group"], lw["pool_scale"], dy1, pad, f"pool_bwd_{tag}")
    dyn = _mm(dy_ssd, lw["w_ssd_out"], "nt", f"d_yn_{tag}")
    g["w_ssd_out"] = _mm(s["yn"], dy_ssd, "tn", f"dw_ssd_out_{tag}")
    dact, dz, ddt, g["dt_bias"], g["a_log"], g["d_skip"], g["ssd_norm_w"] = _ssd_bwd(
        s["xbc"], s["p_dt"], s["p_z"], lw["dt_bias"], lw["a_log"], lw["d_skip"], lw["ssd_norm_w"], s["y"], s["prev"], dyn,
        pad, n_heads, f"ssd_bwd_{tag}")
    dxbc, g["conv_w"], g["conv_b"] = _conv_bwd(s["p_xbc"], lw["conv_w"], lw["conv_b"], dact, pad, f"conv_bwd_{tag}")
    u = s["u"]
    du = _mm(dxbc, lw["w_xbc"], "nt", f"du_xbc_{tag}")
    du = _mm(dz, lw["w_z"], "nt", f"du_z_{tag}", res=du)
    du = _mm(dgate, lw["w_gate"], "nt", f"du_gate_{tag}", res=du)
    du = _mm(dpool, lw["w_pool"], "nt", f"du_pool_{tag}", res=du)
    du = _mm(ddt, lw["w_dt"], "nt", f"du_dt_{tag}", res=du)
    g["w_xbc"] = _mm(u, dxbc, "tn", f"dw_xbc_{tag}")
    g["w_z"] = _mm(u, dz, "tn", f"dw_z_{tag}")
    g["w_gate"] = _mm(u, dgate, "tn", f"dw_gate_{tag}")
    g["w_pool"] = _mm(u, dpool, "tn", f"dw_pool_{tag}")
    g["w_dt"] = _mm(u, ddt, "tn", f"dw_dt_{tag}")
    dh_in, g["mix_norm_w"] = _rms_bwd(s["h"], lw["mix_norm_w"], du, dh_mid, f"rms_mix_bwd_{tag}")
    return dh_in, g


def _pad_lanes(v):
    return jnp.pad(v, (0, LANES - v.shape[0])).reshape(1, LANES)


def _local_step(x2, target, meta_full, full, rep, cfg):
    depth, pad, n_meta, H = cfg["depth"], cfg["pad"], cfg["n_meta"], cfg["n_heads"]
    D = x2.shape[1]
    di = rep["ssd_norm_w"].shape[1]
    c_pool, c_z, c_xbc, c_dt = cfg["cols"]
    h = jnp.concatenate([jnp.zeros((pad, D), F32), meta_full, x2], axis=0)
    lws, saves = [], []
    for i in range(depth):
        w_in = full["w_in"][i]
        o = 0
        w_pool = w_in[:, o:o + c_pool]; o += c_pool
        w_z = w_in[:, o:o + c_z]; o += c_z
        w_xbc = w_in[:, o:o + c_xbc]; o += c_xbc
        w_dt = jnp.pad(w_in[:, o:o + c_dt], ((0, 0), (0, LANES - c_dt))); o += c_dt
        w_gate = w_in[:, o:]
        lw = dict(
            w_pool=w_pool, w_z=w_z, w_xbc=w_xbc, w_dt=w_dt, w_gate=w_gate,
            pool_w_group=full["pool_w_group"][i], w_pool_up=full["w_pool_up"][i], w_ssd_out=full["w_ssd_out"][i],
            w_o=full["w_o"][i], w_ff1=full["w_ff1"][i], w_ff2=full["w_ff2"][i], conv_w=full["conv_w"][i],
            mix_norm_w=rep["mix_norm_w"][i], b_gate=rep["b_gate"][i], pool_scale=rep["pool_scale"][i].reshape(1, -1),
            conv_b=rep["conv_b"][i].reshape(1, -1), dt_bias=_pad_lanes(rep["dt_bias"][i]), a_log=_pad_lanes(rep["a_log"][i]),
            d_skip=_pad_lanes(rep["d_skip"][i]), ssd_norm_w=rep["ssd_norm_w"][i].reshape(1, di), mlp_norm_w=rep["mlp_norm_w"][i])
        lws.append(lw)
        h, s = _layer_fwd(h, lw, cfg, f"l{i}")
        saves.append(s)
    loss, dh, g_final = _final_loss(h, rep["final_norm_w"], target, pad + n_meta, "final_loss")
    per_layer = []
    for i in range(depth - 1, -1, -1):
        dh, g = _layer_bwd(dh, lws[i], saves[i], cfg, f"l{i}")
        per_layer.append(g)
    per_layer.reverse()

    def stack(key, fn=lambda t: t):
        return jnp.stack([fn(g[key]) for g in per_layer])

    grads = dict(
        w_in=jnp.stack([jnp.concatenate([g["w_pool"], g["w_z"], g["w_xbc"], g["w_dt"][:, :c_dt], g["w_gate"]], axis=1) for g in per_layer]),
        pool_w_group=stack("pool_w_group"), w_pool_up=stack("w_pool_up"), w_ssd_out=stack("w_ssd_out"), w_o=stack("w_o"),
        w_ff1=stack("w_ff1"), w_ff2=stack("w_ff2"), conv_w=stack("conv_w"),
        mix_norm_w=stack("mix_norm_w", lambda t: t[0]), b_gate=stack("b_gate", lambda t: t[0]),
        pool_scale=stack("pool_scale", lambda t: t[0]), conv_b=stack("conv_b", lambda t: t[0]),
        dt_bias=stack("dt_bias", lambda t: t[0, :H]), a_log=stack("a_log", lambda t: t[0, :H]), d_skip=stack("d_skip", lambda t: t[0, :H]),
        ssd_norm_w=stack("ssd_norm_w", lambda t: t[0]), mlp_norm_w=stack("mlp_norm_w", lambda t: t[0]),
        final_norm_w=g_final[0], meta_tokens=dh[pad:pad + n_meta])
    return loss, dh[pad + n_meta:], grads


def kernel(x, meta_tokens, mix_norm_w, w_in, b_gate, pool_w_group, pool_scale, w_pool_up, conv_w, conv_b, dt_bias, a_log, d_skip, ssd_norm_w, w_ssd_out, w_o, mlp_norm_w, w_ff1, w_ff2, final_norm_w, loss_target, m_meta_tokens, m_mix_norm_w, m_w_in, m_b_gate, m_pool_w_group, m_pool_scale, m_w_pool_up, m_conv_w, m_conv_b, m_dt_bias, m_a_log, m_d_skip, m_ssd_norm_w, m_w_ssd_out, m_w_o, m_mlp_norm_w, m_w_ff1, m_w_ff2, m_final_norm_w, v_meta_tokens, v_mix_norm_w, v_w_in, v_b_gate, v_pool_w_group, v_pool_scale, v_w_pool_up, v_conv_w, v_conv_b, v_dt_bias, v_a_log, v_d_skip, v_ssd_norm_w, v_w_ssd_out, v_w_o, v_mlp_norm_w, v_w_ff1, v_w_ff2, v_final_norm_w):
    w = dict(meta_tokens=meta_tokens, mix_norm_w=mix_norm_w, w_in=w_in, b_gate=b_gate, pool_w_group=pool_w_group,
             pool_scale=pool_scale, w_pool_up=w_pool_up, conv_w=conv_w, conv_b=conv_b, dt_bias=dt_bias, a_log=a_log,
             d_skip=d_skip, ssd_norm_w=ssd_norm_w, w_ssd_out=w_ssd_out, w_o=w_o, mlp_norm_w=mlp_norm_w, w_ff1=w_ff1,
             w_ff2=w_ff2, final_norm_w=final_norm_w)
    m = dict(meta_tokens=m_meta_tokens, mix_norm_w=m_mix_norm_w, w_in=m_w_in, b_gate=m_b_gate, pool_w_group=m_pool_w_group,
             pool_scale=m_pool_scale, w_pool_up=m_w_pool_up, conv_w=m_conv_w, conv_b=m_conv_b, dt_bias=m_dt_bias, a_log=m_a_log,
             d_skip=m_d_skip, ssd_norm_w=m_ssd_norm_w, w_ssd_out=m_w_ssd_out, w_o=m_w_o, mlp_norm_w=m_mlp_norm_w, w_ff1=m_w_ff1,
             w_ff2=m_w_ff2, final_norm_w=m_final_norm_w)
    v = dict(meta_tokens=v_meta_tokens, mix_norm_w=v_mix_norm_w, w_in=v_w_in, b_gate=v_b_gate, pool_w_group=v_pool_w_group,
             pool_scale=v_pool_scale, w_pool_up=v_w_pool_up, conv_w=v_conv_w, conv_b=v_conv_b, dt_bias=v_dt_bias, a_log=v_a_log,
             d_skip=v_d_skip, ssd_norm_w=v_ssd_norm_w, w_ssd_out=v_w_ssd_out, w_o=v_w_o, mlp_norm_w=v_mlp_norm_w, w_ff1=v_w_ff1,
             w_ff2=v_w_ff2, final_norm_w=v_final_norm_w)

    _, seq, D = x.shape
    n_meta = meta_tokens.shape[0]
    depth = w_in.shape[0]
    n_heads = dt_bias.shape[1]
    d_inner = ssd_norm_w.shape[1]
    d_xbc = conv_b.shape[1]
    pool_width = pool_scale.shape[1]
    pad = (-n_meta) % CHUNK
    cfg = dict(depth=depth, pad=pad, n_meta=n_meta, n_heads=n_heads, cols=(pool_width, d_inner, d_xbc, n_heads))
    assert (pad + n_meta + seq) % CHUNK == 0 and pad + n_meta == CHUNK

    xi, yi, ci = _place()
    me = 4 * xi + 2 * yi + ci

    sharded = BIG + SMALL_SHARDED
    gathered = _all_gather([_rows2(w[s.name].astype(BF16)) for s in BIG] + [_rows2(w[s.name]) for s in SMALL_SHARDED], "gather_weights")
    full = {s.name: _unshard(t, w[s.name].shape, s.axis) for s, t in zip(sharded, gathered)}
    rep = {k: w[k] for k in REPLICATED}

    loss_part, dx, grads = _local_step(x[0], loss_target[0], full["meta_tokens"], full, rep, cfg)
    loss = lax.psum(loss_part[0, 0], ("x", "y", "c"))

    gstacks = [_reshard(grads[s.name], s.axis) for s in BIG]
    gots = _swap_core_halves(gstacks, "rs_sibling")
    chip = 2 * xi + yi
    place = jnp.stack([chip, chip ^ 2, chip ^ 1, chip ^ 3, ci]).astype(jnp.int32)
    sums = [_chip_partials(g4, got, place, f"rs_chip_sum_{s.name}") for s, g4, got in zip(BIG, gstacks, gots)]
    others = _exchange_chips([parts for _, parts in sums], "rs_chips")
    g_loc = {}
    for s, (own, _), oth in zip(BIG, sums, others):
        g_loc[s.name] = _final_sum(own, oth, f"rs_final_sum_{s.name}").reshape(w[s.name].shape)

    small_names = REPLICATED + [s.name for s in SMALL_SHARDED]
    sm_buf, sm_spans = _pack_small([grads[k] for k in small_names], 16)
    (sm_all,) = _all_gather([sm_buf], "gather_small_grads")
    sm_sum = _sum_stack([sm_all[d] for d in range(N_DEV)], "small_grads_sum")
    g_small = dict(zip(small_names, _unpack_small(sm_sum, sm_spans, [grads[k].shape for k in small_names])))
    for k in REPLICATED:
        g_loc[k] = g_small[k]
    for s in SMALL_SHARDED:
        blk = w[s.name].shape[s.axis]
        g_loc[s.name] = lax.dynamic_slice_in_dim(g_small[s.name], me * blk, blk, axis=s.axis)

    delta, new_m, new_v = {}, {}, {}
    for s in BIG:
        shp = w[s.name].shape
        d2, m2, v2 = _adamw(_rows2(w[s.name]), _rows2(g_loc[s.name]), _rows2(m[s.name]), _rows2(v[s.name]), f"adamw_{s.name}")
        delta[s.name], new_m[s.name], new_v[s.name] = d2.reshape(shp), m2.reshape(shp), v2.reshape(shp)
    loc_shapes = [w[k].shape for k in small_names]
    packed = [_pack_small([t[k] for k in small_names], 8) for t in (w, g_loc, m, v)]
    loc_spans = packed[0][1]
    outs = _adamw(*[p[0] for p in packed], "adamw_small")
    for res, buf in zip((delta, new_m, new_v), outs):
        res.update(zip(small_names, _unpack_small(buf, loc_spans, loc_shapes)))

    return (loss, dx[None], *[g_loc[k] for k in WEIGHTS], *[delta[k] for k in WEIGHTS],
            *[new_m[k] for k in WEIGHTS], *[new_v[k] for k in WEIGHTS])
```

```python
import functools

import jax
import jax.numpy as jnp
from jax import lax
from jax.experimental import pallas as pl
from jax.experimental.pallas import tpu as pltpu

F32 = jnp.float32
BF16 = jnp.bfloat16

EPS = 1e-5
D_STATE = 128
CHUNK = 128
LANES = 128
POOL_WINDOWS = (2, 4, 8, 16)
POOL_HALO = 16
CONV_WIDTH = 4
CONV_HALO = 8
N_DEV = 8
ADAM_LR = 0.001
ADAM_B1 = 0.9
ADAM_B2 = 0.999
ADAM_EPS = 1e-08
ADAM_WD = 0.01
ADAM_STEP = 10
VMEM_LIMIT = 52 * 1024 * 1024
MESH = pl.DeviceIdType.MESH
ANY = pl.BlockSpec(memory_space=pl.ANY)


def _cp(*sem):
    return pltpu.CompilerParams(dimension_semantics=sem, vmem_limit_bytes=VMEM_LIMIT)


def _tile(n, target, mult):
    best = None
    for t in range(mult, min(n, target) + 1, mult):
        if n % t == 0:
            best = t
    return best if best is not None else n


def _sigmoid(x):
    return jax.nn.sigmoid(x)


def _iota(shape, dim):
    return lax.broadcasted_iota(jnp.int32, shape, dim)


_DIMS = {"nn": (((1,), (0,)), ((), ())), "nt": (((1,), (1,)), ((), ())), "tn": (((0,), (0,)), ((), ()))}


def _dot(a, b, mode="nn"):
    return lax.dot_general(a.astype(BF16), b.astype(BF16), _DIMS[mode], preferred_element_type=F32)


def _mm(a, b, mode, name, *, out_dtype=F32, res=None, epi=None, aux=None):
    if mode == "nn":
        (M, K), (_, N) = a.shape, b.shape
    elif mode == "nt":
        (M, K), (N, _) = a.shape, b.shape
    else:
        (K, M), (_, N) = a.shape, b.shape
    if mode == "tn":
        tm, tn, tk = _tile(M, 512, 128), _tile(N, 1024, 128), _tile(K, 1056, 16)
    else:
        tm, tn, tk = _tile(M, 1056, 16), _tile(N, 512, 128), _tile(K, 4096, 128)
    nk = K // tk
    a_spec = pl.BlockSpec((tk, tm), lambda i, j, k: (k, i)) if mode == "tn" else pl.BlockSpec((tm, tk), lambda i, j, k: (i, k))
    b_spec = pl.BlockSpec((tn, tk), lambda i, j, k: (j, k)) if mode == "nt" else pl.BlockSpec((tk, tn), lambda i, j, k: (k, j))
    o_spec = pl.BlockSpec((tm, tn), lambda i, j, k: (i, j))
    extra = [t for t in (res, aux) if t is not None]

    def body(*refs):
        a_ref, b_ref = refs[0], refs[1]
        x_ref = refs[2] if extra else None
        outs = refs[2 + len(extra):]
        p = _dot(a_ref[...], b_ref[...], mode)

        def finish(r):
            if res is not None:
                outs[0][...] = (x_ref[...] + r).astype(out_dtype)
            elif epi == "relu2":
                outs[0][...] = r
                hid = jnp.maximum(r, 0.0)
                outs[1][...] = (hid * hid).astype(BF16)
            elif epi == "drelu2":
                outs[0][...] = (r * (2.0 * jnp.maximum(x_ref[...], 0.0))).astype(BF16)
            else:
                outs[0][...] = r.astype(out_dtype)

        if nk == 1:
            finish(p)
        else:
            acc = outs[-1]
            k = pl.program_id(2)

            @pl.when(k == 0)
            def _():
                acc[...] = p

            @pl.when(k > 0)
            def _():
                acc[...] += p

            @pl.when(k == nk - 1)
            def _():
                finish(acc[...])

    if epi == "relu2":
        out_shape = (jax.ShapeDtypeStruct((M, N), F32), jax.ShapeDtypeStruct((M, N), BF16))
        out_specs = (o_spec, o_spec)
    elif epi == "drelu2":
        out_shape, out_specs = jax.ShapeDtypeStruct((M, N), BF16), o_spec
    else:
        out_shape, out_specs = jax.ShapeDtypeStruct((M, N), out_dtype), o_spec
    return pl.pallas_call(
        body, name=name, grid=(M // tm, N // tn, nk),
        in_specs=[a_spec, b_spec] + [o_spec] * len(extra),
        out_specs=out_specs, out_shape=out_shape,
        scratch_shapes=[pltpu.VMEM((tm, tn), F32)] if nk > 1 else [],
        compiler_params=_cp("parallel", "parallel", "arbitrary"),
    )(a, b, *extra)


def _rms_fwd(h, w, name):
    T, D = h.shape
    tr = _tile(T, 1056, 16)

    def body(h_ref, w_ref, o_ref):
        x = h_ref[...]
        xn = x * lax.rsqrt(jnp.mean(x * x, axis=-1, keepdims=True) + EPS)
        o_ref[...] = (xn * w_ref[...]).astype(BF16)

    return pl.pallas_call(
        body, name=name, grid=(T // tr,),
        in_specs=[pl.BlockSpec((tr, D), lambda i: (i, 0)), pl.BlockSpec((1, D), lambda i: (0, 0))],
        out_specs=pl.BlockSpec((tr, D), lambda i: (i, 0)), out_shape=jax.ShapeDtypeStruct((T, D), BF16),
        compiler_params=_cp("parallel"),
    )(h, w.reshape(1, D))


def _rms_bwd(h, w, dy, dres, name):
    T, D = h.shape
    tr = _tile(T, 528, 8)

    def body(h_ref, w_ref, dy_ref, dres_ref, dh_ref, dw_ref):
        x = h_ref[...]
        rstd = lax.rsqrt(jnp.mean(x * x, axis=-1, keepdims=True) + EPS)
        xn = x * rstd
        dy = dy_ref[...]
        dxn = dy * w_ref[...]
        dh_ref[...] = dres_ref[...] + rstd * (dxn - xn * jnp.mean(dxn * xn, axis=-1, keepdims=True))
        dw = jnp.sum(dy * xn, axis=0, keepdims=True)

        @pl.when(pl.program_id(0) == 0)
        def _():
            dw_ref[...] = dw

        @pl.when(pl.program_id(0) > 0)
        def _():
            dw_ref[...] += dw

    row = pl.BlockSpec((tr, D), lambda i: (i, 0))
    vec = pl.BlockSpec((1, D), lambda i: (0, 0))
    return pl.pallas_call(
        body, name=name, grid=(T // tr,),
        in_specs=[row, vec, row, row], out_specs=(row, vec),
        out_shape=(jax.ShapeDtypeStruct((T, D), F32), jax.ShapeDtypeStruct((1, D), F32)),
        compiler_params=_cp("arbitrary"),
    )(h, w.reshape(1, D), dy, dres)


def _final_loss(h, w, target, first_row, name):
    T, D = h.shape
    tr = CHUNK
    assert first_row == tr

    def body(h_ref, w_ref, t_ref, loss_ref, dh_ref, dw_ref):
        i = pl.program_id(0)
        x = h_ref[...]
        rstd = lax.rsqrt(jnp.mean(x * x, axis=-1, keepdims=True) + EPS)
        xn = x * rstd
        w = w_ref[...]
        live = i > 0
        err = jnp.where(live, xn * w - t_ref[...], 0.0)
        part = 0.5 * jnp.sum(jnp.mean(err * err, axis=-1, keepdims=True), axis=0, keepdims=True)
        dout = err * (1.0 / D)
        dxn = dout * w
        dh_ref[...] = rstd * (dxn - xn * jnp.mean(dxn * xn, axis=-1, keepdims=True))
        dw = jnp.sum(dout * xn, axis=0, keepdims=True)

        @pl.when(i == 0)
        def _():
            loss_ref[...] = part
            dw_ref[...] = dw

        @pl.when(i > 0)
        def _():
            loss_ref[...] += part
            dw_ref[...] += dw

    row = pl.BlockSpec((tr, D), lambda i: (i, 0))
    vec = pl.BlockSpec((1, D), lambda i: (0, 0))
    return pl.pallas_call(
        body, name=name, grid=(T // tr,),
        in_specs=[row, vec, pl.BlockSpec((tr, D), lambda i: (jnp.maximum(i - 1, 0), 0))],
        out_specs=(pl.BlockSpec((1, 1), lambda i: (0, 0)), row, vec),
        out_shape=(jax.ShapeDtypeStruct((1, 1), F32), jax.ShapeDtypeStruct((T, D), F32), jax.ShapeDtypeStruct((1, D), F32)),
        compiler_params=_cp("arbitrary"),
    )(h, w.reshape(1, D), target)


def _gate_fwd(pg, bg, y_pool, y_ssd, name):
    T, D = y_pool.shape
    tr = _tile(T, 528, 16)

    def body(gp_ref, gs_ref, bp_ref, bs_ref, yp_ref, ys_ref, o_ref):
        gp = _sigmoid(gp_ref[...] + bp_ref[...])
        gs = _sigmoid(gs_ref[...] + bs_ref[...])
        o_ref[...] = (gp * yp_ref[...] + gs * ys_ref[...]).astype(BF16)

    row = pl.BlockSpec((tr, D), lambda i: (i, 0))
    row1 = pl.BlockSpec((tr, D), lambda i: (i, 1))
    vec = pl.BlockSpec((1, D), lambda i: (0, 0))
    vec1 = pl.BlockSpec((1, D), lambda i: (0, 1))
    b2 = bg.reshape(1, 2 * D)
    return pl.pallas_call(
        body, name=name, grid=(T // tr,),
        in_specs=[row, row1, vec, vec1, row, row], out_specs=row,
        out_shape=jax.ShapeDtypeStruct((T, D), BF16), compiler_params=_cp("parallel"),
    )(pg, pg, b2, b2, y_pool, y_ssd)


def _gate_bwd(pg, bg, y_pool, y_ssd, dmix, name):
    T, D = y_pool.shape
    tr = _tile(T, 528, 16)

    def body(gp_ref, gs_ref, bp_ref, bs_ref, yp_ref, ys_ref, dm_ref, dg_ref, dyp_ref, dys_ref, db_ref):
        gp = _sigmoid(gp_ref[...] + bp_ref[...])
        gs = _sigmoid(gs_ref[...] + bs_ref[...])
        dm = dm_ref[...]
        dyp_ref[...] = (dm * gp).astype(BF16)
        dys_ref[...] = (dm * gs).astype(BF16)
        dgp = dm * yp_ref[...] * gp * (1.0 - gp)
        dgs = dm * ys_ref[...] * gs * (1.0 - gs)
        dg_ref[:, :D] = dgp.astype(BF16)
        dg_ref[:, D:] = dgs.astype(BF16)
        db = jnp.concatenate([jnp.sum(dgp, axis=0, keepdims=True), jnp.sum(dgs, axis=0, keepdims=True)], axis=1)

        @pl.when(pl.program_id(0) == 0)
        def _():
            db_ref[...] = db

        @pl.when(pl.program_id(0) > 0)
        def _():
            db_ref[...] += db

    row = pl.BlockSpec((tr, D), lambda i: (i, 0))
    row1 = pl.BlockSpec((tr, D), lambda i: (i, 1))
    wide = pl.BlockSpec((tr, 2 * D), lambda i: (i, 0))
    vec = pl.BlockSpec((1, D), lambda i: (0, 0))
    vec1 = pl.BlockSpec((1, D), lambda i: (0, 1))
    vec2 = pl.BlockSpec((1, 2 * D), lambda i: (0, 0))
    b2 = bg.reshape(1, 2 * D)
    return pl.pallas_call(
        body, name=name, grid=(T // tr,),
        in_specs=[row, row1, vec, vec1, row, row, row], out_specs=(wide, row, row, vec2),
        out_shape=(jax.ShapeDtypeStruct((T, 2 * D), BF16), jax.ShapeDtypeStruct((T, D), BF16),
                   jax.ShapeDtypeStruct((T, D), BF16), jax.ShapeDtypeStruct((1, 2 * D), F32)),
        compiler_params=_cp("arbitrary"),
    )(pg, pg, b2, b2, y_pool, y_ssd, dmix)


def _pool_count(c, pad, window):
    pos = c * CHUNK + _iota((CHUNK, 1), 0) - pad
    return jnp.clip(pos + 1, 1, window).astype(F32)


def _by_group(g, vals):
    out = vals[-1]
    for k in range(len(vals) - 2, -1, -1):
        out = jnp.where(g == k, vals[k], out)
    return out


def _by_row_block(rows, vals, block):
    out = vals[0]
    for r in range(1, len(vals)):
        out = jnp.where(rows >= r * block, vals[r], out)
    return out


def _pool_fwd(u, wg, scale, pad, name):
    T, C = u.shape
    G, Cg, _ = wg.shape
    nc = T // CHUNK

    def body(u_ref, wg_ref, s_ref, p_ref, y_ref):
        g = pl.program_id(0)
        window = _by_group(g, POOL_WINDOWS)

        def chunk(c, carry):
            r0 = pl.multiple_of(c * CHUNK, CHUNK)
            h0 = pl.multiple_of(jnp.maximum(r0 - POOL_HALO, 0), 8)
            halo = jnp.where(c > 0, u_ref[pl.ds(h0, POOL_HALO), :], 0.0)
            xc = u_ref[pl.ds(r0, CHUNK), :]
            s = jnp.concatenate([halo, xc], axis=0)
            sums = []
            k = 1
            while k < POOL_WINDOWS[-1]:
                s = s + pltpu.roll(s, k, 0)
                k *= 2
                if k in POOL_WINDOWS:
                    sums.append(s[POOL_HALO:])
            wsum = _by_group(g, sums)
            pooled = wsum / _pool_count(c, pad, window) - xc
            pb = pooled.astype(BF16)
            p_ref[pl.ds(r0, CHUNK), :] = pb
            y_ref[pl.ds(r0, CHUNK), :] = (_dot(pb, wg_ref[0]) * s_ref[...]).astype(BF16)
            return carry

        lax.fori_loop(0, nc, chunk, 0)

    col = pl.BlockSpec((T, Cg), lambda g: (0, g))
    return pl.pallas_call(
        body, name=name, grid=(G,),
        in_specs=[col, pl.BlockSpec((1, Cg, Cg), lambda g: (g, 0, 0)), pl.BlockSpec((1, Cg), lambda g: (0, g))],
        out_specs=(col, col),
        out_shape=(jax.ShapeDtypeStruct((T, C), BF16), jax.ShapeDtypeStruct((T, C), BF16)),
        compiler_params=_cp("parallel"),
    )(u, wg, scale)


def _pool_bwd(pooled, wg, scale, dy, pad, name):
    T, C = dy.shape
    G, Cg, _ = wg.shape
    nc = T // CHUNK

    def body(p_ref, wg_ref, s_ref, dy_ref, du_ref, dwg_ref, ds_ref, halo_ref):
        g = pl.program_id(0)
        window = _by_group(g, POOL_WINDOWS)
        halo_ref[...] = jnp.zeros_like(halo_ref)
        dwg_ref[...] = jnp.zeros_like(dwg_ref)
        ds_ref[...] = jnp.zeros_like(ds_ref)

        def chunk(i, carry):
            c = nc - 1 - i
            r0 = pl.multiple_of(c * CHUNK, CHUNK)
            pb = p_ref[pl.ds(r0, CHUNK), :]
            dyc = dy_ref[pl.ds(r0, CHUNK), :]
            w = wg_ref[0]
            ypre = _dot(pb, w)
            ds_ref[...] += jnp.sum(dyc * ypre, axis=0, keepdims=True)
            dyp = (dyc * s_ref[...]).astype(BF16)
            dwg_ref[0] += _dot(pb, dyp, "tn")
            dpool = _dot(dyp, w, "nt")
            q = dpool / _pool_count(c, pad, window)
            s = jnp.concatenate([q, halo_ref[...]], axis=0)
            n = CHUNK + POOL_HALO
            sums = []
            k = 1
            while k < POOL_WINDOWS[-1]:
                s = s + pltpu.roll(s, n - k, 0)
                k *= 2
                if k in POOL_WINDOWS:
                    sums.append(s[:CHUNK])
            du = _by_group(g, sums) - dpool
            rows = r0 + _iota((CHUNK, 1), 0)
            du_ref[pl.ds(r0, CHUNK), :] = jnp.where(rows >= pad, du, 0.0).astype(BF16)
            halo_ref[...] = q[:POOL_HALO]
            return carry

        lax.fori_loop(0, nc, chunk, 0)

    col = pl.BlockSpec((T, Cg), lambda g: (0, g))
    return pl.pallas_call(
        body, name=name, grid=(G,),
        in_specs=[col, pl.BlockSpec((1, Cg, Cg), lambda g: (g, 0, 0)), pl.BlockSpec((1, Cg), lambda g: (0, g)), col],
        out_specs=(col, pl.BlockSpec((1, Cg, Cg), lambda g: (g, 0, 0)), pl.BlockSpec((1, Cg), lambda g: (0, g))),
        out_shape=(jax.ShapeDtypeStruct((T, C), BF16), jax.ShapeDtypeStruct((G, Cg, Cg), F32), jax.ShapeDtypeStruct((1, C), F32)),
        scratch_shapes=[pltpu.VMEM((POOL_HALO, Cg), F32)],
        compiler_params=_cp("parallel"),
    )(pooled, wg, scale, dy)


def _conv_pre(x_ref, w_ref, b_ref, c, r0):
    h0 = pl.multiple_of(jnp.maximum(r0 - CONV_HALO, 0), 8)
    halo = jnp.where(c > 0, x_ref[pl.ds(h0, CONV_HALO), :], 0.0)
    xe = jnp.concatenate([halo, x_ref[pl.ds(r0, CHUNK), :]], axis=0)
    y = jnp.broadcast_to(b_ref[...], (CHUNK, xe.shape[1]))
    for k in range(CONV_WIDTH):
        shift = CONV_WIDTH - 1 - k
        xs = xe if shift == 0 else pltpu.roll(xe, shift, 0)
        y = y + xs[CONV_HALO:] * w_ref[k:k + 1, :]
    return y, xe


def _conv_fwd(x, w, b, name):
    T = x.shape[0]
    C = w.shape[1]
    tc = _tile(C, 256, 128)
    nc = T // CHUNK

    def body(x_ref, w_ref, b_ref, o_ref):
        def chunk(c, carry):
            r0 = pl.multiple_of(c * CHUNK, CHUNK)
            y, _ = _conv_pre(x_ref, w_ref, b_ref, c, r0)
            o_ref[pl.ds(r0, CHUNK), :] = y * _sigmoid(y)
            return carry

        lax.fori_loop(0, nc, chunk, 0)

    col = pl.BlockSpec((T, tc), lambda j: (0, j))
    return pl.pallas_call(
        body, name=name, grid=(C // tc,),
        in_specs=[col, pl.BlockSpec((CONV_WIDTH, tc), lambda j: (0, j)), pl.BlockSpec((1, tc), lambda j: (0, j))],
        out_specs=col, out_shape=jax.ShapeDtypeStruct((T, C), F32), compiler_params=_cp("parallel"),
    )(x, w, b)


def _conv_bwd(x, w, b, dact, pad, name):
    T = x.shape[0]
    C = w.shape[1]
    tc = _tile(C, 256, 128)
    nc = T // CHUNK

    def body(x_ref, w_ref, b_ref, da_ref, dx_ref, dw_ref, db_ref, halo_ref):
        halo_ref[...] = jnp.zeros_like(halo_ref)
        dw_ref[...] = jnp.zeros_like(dw_ref)
        db_ref[...] = jnp.zeros_like(db_ref)

        def chunk(i, carry):
            c = nc - 1 - i
            r0 = pl.multiple_of(c * CHUNK, CHUNK)
            y, xe = _conv_pre(x_ref, w_ref, b_ref, c, r0)
            sg = _sigmoid(y)
            dpre = da_ref[pl.ds(r0, CHUNK), :] * (sg * (1.0 + y * (1.0 - sg)))
            db_ref[...] += jnp.sum(dpre, axis=0, keepdims=True)
            de = jnp.concatenate([dpre, halo_ref[...]], axis=0)
            n = CHUNK + CONV_HALO
            dx = jnp.zeros_like(dpre)
            for k in range(CONV_WIDTH):
                shift = CONV_WIDTH - 1 - k
                xs = xe if shift == 0 else pltpu.roll(xe, shift, 0)
                dw_ref[k:k + 1, :] += jnp.sum(dpre * xs[CONV_HALO:], axis=0, keepdims=True)
                ds = de if shift == 0 else pltpu.roll(de, n - shift, 0)
                dx = dx + ds[:CHUNK] * w_ref[k:k + 1, :]
            rows = r0 + _iota((CHUNK, 1), 0)
            dx_ref[pl.ds(r0, CHUNK), :] = jnp.where(rows >= pad, dx, 0.0).astype(BF16)
            halo_ref[...] = dpre[:CONV_HALO]
            return carry

        lax.fori_loop(0, nc, chunk, 0)

    col = pl.BlockSpec((T, tc), lambda j: (0, j))
    wspec = pl.BlockSpec((CONV_WIDTH, tc), lambda j: (0, j))
    bspec = pl.BlockSpec((1, tc), lambda j: (0, j))
    return pl.pallas_call(
        body, name=name, grid=(C // tc,),
        in_specs=[col, wspec, bspec, col], out_specs=(col, wspec, bspec),
        out_shape=(jax.ShapeDtypeStruct((T, C), BF16), jax.ShapeDtypeStruct((CONV_WIDTH, C), F32), jax.ShapeDtypeStruct((1, C), F32)),
        scratch_shapes=[pltpu.VMEM((CONV_HALO, tc), F32)],
        compiler_params=_cp("parallel"),
    )(x, w, b, dact)


def _cumsum_rows(x, reverse=False):
    n = x.shape[0]
    idx = _iota(x.shape, 0)
    k = 1
    while k < n:
        if reverse:
            x = x + jnp.where(idx < n - k, pltpu.roll(x, n - k, 0), 0.0)
        else:
            x = x + jnp.where(idx >= k, pltpu.roll(x, k, 0), 0.0)
        k *= 2
    return x


def _softplus(x):
    return jnp.maximum(x, 0.0) + jnp.log1p(jnp.exp(-jnp.abs(x)))


def _head_selector(n_heads, width):
    lane = jnp.arange(n_heads * width)[None, :] // width
    return (lane == jnp.arange(LANES)[:, None]).astype(BF16)


def _dot_exact(x, sel, parts, mode="nn"):
    acc = None
    for _ in range(parts):
        piece = x.astype(BF16)
        x = x - piece.astype(F32)
        t = lax.dot_general(piece, sel, _DIMS[mode], preferred_element_type=F32)
        acc = t if acc is None else acc + t
    return acc


def _ssd_decays(dt, cs, sel_p_ref, sel_q_ref):
    cs_b = _dot_exact(cs, sel_q_ref[...], 3)
    dt_x = _dot_exact(dt, sel_p_ref[...], 3)
    cs_x = _dot_exact(cs, sel_p_ref[...], 3)
    return cs_b, dt_x, jnp.exp(cs_x), jnp.exp(cs_x[CHUNK - 1:CHUNK, :] - cs_x)


def _ssd_common(c, pad, n_heads, dtr_ref, dtb_ref, al_ref):
    rows = c * CHUNK + _iota((CHUNK, 1), 0)
    valid = rows >= pad
    live = jnp.logical_and(valid, _iota((1, LANES), 1) < n_heads)
    pre = dtr_ref[...] + dtb_ref[...]
    dt = jnp.where(live, _softplus(pre), 0.0)
    a = -jnp.exp(al_ref[...])
    cs = _cumsum_rows(dt * a)
    return valid, live, dt, a, cs, cs.T, _sigmoid(pre)


def _ssd_specs(T, DI, GN, cfirst):
    xcol = DI // GN

    def at(col):
        return lambda c: (cfirst(c), col)

    x = pl.BlockSpec((CHUNK, DI), at(0))
    b = pl.BlockSpec((CHUNK, GN), at(xcol))
    cm = pl.BlockSpec((CHUNK, GN), at(xcol + 1))
    dt = pl.BlockSpec((CHUNK, LANES), at(0))
    vec = pl.BlockSpec((1, LANES), lambda c: (0, 0))
    nw = pl.BlockSpec((1, DI), lambda c: (0, 0))
    return x, b, cm, dt, vec, nw


def _ssd_fwd(xbc, pdt, pz, dt_bias, a_log, d_skip, norm_w, pad, n_heads, name):
    T = xbc.shape[0]
    DI = pz.shape[1]
    P = DI // n_heads
    GN = (xbc.shape[1] - DI) // 2
    G = GN // D_STATE
    R = n_heads // G
    GW = R * P
    nc = T // CHUNK
    Q, N = CHUNK, D_STATE

    def body(x_ref, b_ref, c_ref, dtr_ref, z_ref, dtb_ref, al_ref, dsk_ref, nw_ref, sel_p_ref, sel_q_ref,
             y_ref, yn_ref, prev_ref, s_ref):
        c = pl.program_id(0)

        @pl.when(c == 0)
        def _():
            s_ref[...] = jnp.zeros_like(s_ref)

        valid, _, dt, _, cs, cst, _ = _ssd_common(c, pad, n_heads, dtr_ref, dtb_ref, al_ref)
        cs_b, dt_x, e_x, dec_x = _ssd_decays(dt, cs, sel_p_ref, sel_q_ref)
        e_last = jnp.exp(cs[Q - 1:Q, :])
        tri = _iota((Q, Q), 0) >= _iota((Q, Q), 1)
        state_rows = _iota((GW, 1), 0)
        head_lane = _iota((1, GW), 1)
        for g in range(G):
            gs = slice(g * GW, (g + 1) * GW)
            bg = jnp.where(valid, b_ref[:, g * N:(g + 1) * N], 0.0).astype(BF16)
            cg = jnp.where(valid, c_ref[:, g * N:(g + 1) * N], 0.0).astype(BF16)
            xg = jnp.where(valid, x_ref[:, gs], 0.0)
            sg = s_ref[gs, :]
            prev_ref[0, gs, :] = sg
            cb = _dot(cg, bg, "nt")
            xdt = xg * dt_x[:, gs]
            yg = _dot(cg, sg, "nt") * e_x[:, gs]
            for r in range(R):
                h = g * R + r
                lmat = jnp.exp(jnp.where(tri, cs_b[:, h * Q:(h + 1) * Q] - cst[h:h + 1, :], -jnp.inf))
                in_head = jnp.logical_and(head_lane >= r * P, head_lane < (r + 1) * P)
                yg = yg + _dot(cb * lmat, jnp.where(in_head, xdt, 0.0))
            y_ref[:, gs] = yg
            decay = _by_row_block(state_rows, [e_last[:, g * R + r:g * R + r + 1] for r in range(R)], P)
            s_ref[gs, :] = sg * decay + _dot(xdt * dec_x[:, gs], bg, "tn")
            z = z_ref[:, gs]
            gz = (yg + xg * dsk_ref[:, gs]) * (z * _sigmoid(z))
            rstd = lax.rsqrt(jnp.mean(gz * gz, axis=-1, keepdims=True) + EPS)
            yn_ref[:, gs] = ((gz * rstd) * nw_ref[:, gs]).astype(BF16)

    x_s, b_s, c_s, dt_s, vec, nw = _ssd_specs(T, DI, GN, lambda c: c)
    wide = pl.BlockSpec((Q, DI), lambda c: (c, 0))
    sel_p = pl.BlockSpec((LANES, DI), lambda c: (0, 0))
    sel_q = pl.BlockSpec((LANES, n_heads * Q), lambda c: (0, 0))
    return pl.pallas_call(
        body, name=name, grid=(nc,),
        in_specs=[x_s, b_s, c_s, dt_s, wide, vec, vec, nw, nw, sel_p, sel_q],
        out_specs=(wide, wide, pl.BlockSpec((1, DI, N), lambda c: (c, 0, 0))),
        out_shape=(jax.ShapeDtypeStruct((T, DI), F32), jax.ShapeDtypeStruct((T, DI), BF16), jax.ShapeDtypeStruct((nc, DI, N), F32)),
        scratch_shapes=[pltpu.VMEM((DI, N), F32)],
        compiler_params=_cp("arbitrary"),
    )(xbc, xbc, xbc, pdt, pz, dt_bias, a_log, jnp.repeat(d_skip[:, :n_heads], P, axis=1), norm_w,
      _head_selector(n_heads, P), _head_selector(n_heads, Q))


def _ssd_bwd(xbc, pdt, pz, dt_bias, a_log, d_skip, norm_w, y, prev, dyn, pad, n_heads, name):
    T, W = xbc.shape
    DI = pz.shape[1]
    P = DI // n_heads
    GN = (W - DI) // 2
    G = GN // D_STATE
    R = n_heads // G
    GW = R * P
    nc = T // CHUNK
    Q, N = CHUNK, D_STATE

    def body(x_ref, b_ref, c_ref, dtr_ref, z_ref, dtb_ref, al_ref, dsk_ref, nw_ref, y_ref, prev_ref, next_ref, dyn_ref,
             sel_p_ref, sel_q_ref, sel_pt_ref,
             dxbc_ref, dz_ref, ddt_ref, ddtb_ref, dal_ref, ddsk_ref, dnw_ref, ds_ref):
        i = pl.program_id(0)
        c = nc - 1 - i

        @pl.when(i == 0)
        def _():
            ds_ref[...] = jnp.zeros_like(ds_ref)
            ddtb_ref[...] = jnp.zeros_like(ddtb_ref)
            dal_ref[...] = jnp.zeros_like(dal_ref)
            ddsk_ref[...] = jnp.zeros_like(ddsk_ref)
            dnw_ref[...] = jnp.zeros_like(dnw_ref)

        valid, live, dt, a, cs, cst, sig_pre = _ssd_common(c, pad, n_heads, dtr_ref, dtb_ref, al_ref)
        cs_b, dt_x, e_x, dec_x = _ssd_decays(dt, cs, sel_p_ref, sel_q_ref)
        e_last = jnp.exp(cs[Q - 1:Q, :])
        tri = _iota((Q, Q), 0) >= _iota((Q, Q), 1)
        tri_t = _iota((Q, Q), 0) <= _iota((Q, Q), 1)
        state_rows = _iota((GW, 1), 0)
        head_lane = _iota((1, GW), 1)
        lane = _iota((1, LANES), 1)
        head_rows = _iota((LANES, 1), 0)
        s_dy_cs = jnp.zeros((Q, LANES), F32)
        s_x_bds = jnp.zeros((Q, LANES), F32)
        s_x_dxdt = jnp.zeros((Q, LANES), F32)
        dcs_rows = jnp.zeros((Q, LANES), F32)
        dcs_cols = jnp.zeros((LANES, Q), F32)
        c_end = jnp.zeros((1, LANES), F32)
        dsk_rows = []
        for g in range(G):
            gs = slice(g * GW, (g + 1) * GW)
            bg = jnp.where(valid, b_ref[:, g * N:(g + 1) * N], 0.0).astype(BF16)
            cg = jnp.where(valid, c_ref[:, g * N:(g + 1) * N], 0.0).astype(BF16)
            xg = jnp.where(valid, x_ref[:, gs], 0.0)
            s_prev = prev_ref[0, gs, :]
            dsg = ds_ref[gs, :]
            end = dsg * next_ref[0, gs, :]
            yg = y_ref[:, gs]
            dsk = dsk_ref[:, gs]
            ytot = yg + xg * dsk
            z = z_ref[:, gs]
            sz = _sigmoid(z)
            silu = z * sz
            gz = ytot * silu
            rstd = lax.rsqrt(jnp.mean(gz * gz, axis=-1, keepdims=True) + EPS)
            gn = gz * rstd
            dyn_g = dyn_ref[:, gs]
            dnw_ref[:, gs] += jnp.sum(dyn_g * gn, axis=0, keepdims=True)
            dgn = dyn_g * nw_ref[:, gs]
            dgz = rstd * (dgn - gn * jnp.mean(dgn * gn, axis=-1, keepdims=True))
            dz_ref[:, gs] = (dgz * ytot * (sz * (1.0 + z * (1.0 - sz)))).astype(BF16)
            dy = dgz * silu
            dsk_rows.append(jnp.sum(dy * xg, axis=0, keepdims=True))
            cb = _dot(cg, bg, "nt")
            cb_t = _dot(bg, cg, "nt")
            bds = _dot(bg, dsg, "nt") * dec_x[:, gs]
            csg = _dot(cg, s_prev, "nt")
            xdt = xg * dt_x[:, gs]
            dxdt = bds
            dcb = jnp.zeros((Q, Q), F32)
            dcb_t = jnp.zeros((Q, Q), F32)
            for r in range(R):
                h = g * R + r
                cs_col = cs_b[:, h * Q:(h + 1) * Q]
                cs_row = cst[h:h + 1, :]
                lmat = jnp.exp(jnp.where(tri, cs_col - cs_row, -jnp.inf))
                lmat_t = jnp.exp(jnp.where(tri_t, cs_row - cs_col, -jnp.inf))
                in_head = jnp.logical_and(head_lane >= r * P, head_lane < (r + 1) * P)
                dyr = jnp.where(in_head, dy, 0.0)
                dm = _dot(dyr, xdt, "nt")
                dcb = dcb + dm * lmat
                dcb_t = dcb_t + _dot(xdt, dyr, "nt") * lmat_t
                w_rc = dm * (cb * lmat)
                dcs_rows = jnp.where(lane == h, jnp.sum(w_rc, axis=1, keepdims=True), dcs_rows)
                dcs_cols = jnp.where(head_rows == h, jnp.sum(w_rc, axis=0, keepdims=True), dcs_cols)
                dxdt = dxdt + _dot(cb_t * lmat_t, dyr)
            sel_t = sel_pt_ref[gs, :]
            s_dy_cs = s_dy_cs + _dot_exact(dy * csg, sel_t, 2)
            s_x_bds = s_x_bds + _dot_exact(xg * bds, sel_t, 2)
            s_x_dxdt = s_x_dxdt + _dot_exact(xg * dxdt, sel_t, 2)
            c_end = c_end + jnp.sum(jnp.sum(end, axis=1, keepdims=True) * sel_t.astype(F32), axis=0, keepdims=True)
            dye = dy * e_x[:, gs]
            dc = _dot(dcb, bg) + _dot(dye, s_prev)
            db = _dot(dcb_t, cg) + _dot(xdt * dec_x[:, gs], dsg)
            decay = _by_row_block(state_rows, [e_last[:, g * R + r:g * R + r + 1] for r in range(R)], P)
            ds_ref[gs, :] = dsg * decay + _dot(dye, cg, "tn")
            dxbc_ref[:, gs] = jnp.where(valid, dxdt * dt_x[:, gs] + dy * dsk, 0.0)
            dxbc_ref[:, DI + g * N:DI + (g + 1) * N] = jnp.where(valid, db, 0.0)
            dxbc_ref[:, DI + GN + g * N:DI + GN + (g + 1) * N] = jnp.where(valid, dc, 0.0)
        dcs = s_dy_cs * jnp.exp(cs) - dt * s_x_bds
        da_cs = _cumsum_rows(dcs + (dcs_rows - dcs_cols.T), reverse=True) + c_end
        ddt_all = jnp.where(live, da_cs * a + s_x_dxdt, 0.0)
        ddt_raw = ddt_all * sig_pre
        ddt_ref[...] = ddt_raw.astype(BF16)
        ddtb_ref[...] += jnp.sum(ddt_raw, axis=0, keepdims=True)
        dal_ref[...] += jnp.sum(da_cs * dt, axis=0, keepdims=True) * a
        dsk_all = jnp.broadcast_to(jnp.concatenate(dsk_rows, axis=1), (8, DI))
        ddsk_ref[...] += _dot_exact(dsk_all, sel_pt_ref[...], 3)[0:1]

    rev = lambda i: nc - 1 - i
    x_s, b_s, c_s, dt_s, vec, nw = _ssd_specs(T, DI, GN, rev)
    wide = pl.BlockSpec((Q, DI), lambda i: (rev(i), 0))
    st = pl.BlockSpec((1, DI, N), lambda i: (rev(i), 0, 0))
    st_next = pl.BlockSpec((1, DI, N), lambda i: (jnp.minimum(rev(i) + 1, nc - 1), 0, 0))
    sel_p = pl.BlockSpec((LANES, DI), lambda i: (0, 0))
    sel_q = pl.BlockSpec((LANES, n_heads * Q), lambda i: (0, 0))
    sel_pt = pl.BlockSpec((DI, LANES), lambda i: (0, 0))
    sel = _head_selector(n_heads, P)
    return pl.pallas_call(
        body, name=name, grid=(nc,),
        in_specs=[x_s, b_s, c_s, dt_s, wide, vec, vec, nw, nw, wide, st, st_next, wide, sel_p, sel_q, sel_pt],
        out_specs=(pl.BlockSpec((Q, W), lambda i: (rev(i), 0)), wide, dt_s, vec, vec, vec, nw),
        out_shape=(jax.ShapeDtypeStruct((T, W), F32), jax.ShapeDtypeStruct((T, DI), BF16), jax.ShapeDtypeStruct((T, LANES), BF16),
                   jax.ShapeDtypeStruct((1, LANES), F32), jax.ShapeDtypeStruct((1, LANES), F32),
                   jax.ShapeDtypeStruct((1, LANES), F32), jax.ShapeDtypeStruct((1, DI), F32)),
        scratch_shapes=[pltpu.VMEM((DI, N), F32)],
        compiler_params=_cp("arbitrary"),
    )(xbc, xbc, xbc, pdt, pz, dt_bias, a_log, jnp.repeat(d_skip[:, :n_heads], P, axis=1), norm_w, y, prev, prev, dyn,
      sel, _head_selector(n_heads, Q), sel.T)


def _adamw(w, g, m, v, name):
    rows, cols = w.shape
    tr = _tile(rows, 256, 8)

    def body(w_ref, g_ref, m_ref, v_ref, d_ref, nm_ref, nv_ref):
        g = g_ref[...]
        m = ADAM_B1 * m_ref[...] + (1.0 - ADAM_B1) * g
        v = ADAM_B2 * v_ref[...] + (1.0 - ADAM_B2) * (g * g)
        m_hat = m / (1.0 - ADAM_B1 ** ADAM_STEP)
        v_hat = v / (1.0 - ADAM_B2 ** ADAM_STEP)
        d_ref[...] = -ADAM_LR * (m_hat / (jnp.sqrt(v_hat) + ADAM_EPS) + ADAM_WD * w_ref[...])
        nm_ref[...] = m
        nv_ref[...] = v

    blk = pl.BlockSpec((tr, cols), lambda i: (i, 0))
    out = jax.ShapeDtypeStruct((rows, cols), F32)
    return pl.pallas_call(
        body, name=name, grid=(rows // tr,), in_specs=[blk] * 4, out_specs=(blk,) * 3, out_shape=(out,) * 3,
        compiler_params=_cp("parallel"),
    )(w, g, m, v)


def _final_sum(own, others, name):
    rows, cols = own.shape
    tr = _tile(rows, 512, 16)

    def body(o_ref, a_ref, b_ref, c_ref, out_ref):
        out_ref[...] = ((o_ref[...] + a_ref[0].astype(F32)) + b_ref[0].astype(F32)) + c_ref[0].astype(F32)

    blk = pl.BlockSpec((tr, cols), lambda i: (i, 0))
    return pl.pallas_call(
        body, name=name, grid=(rows // tr,),
        in_specs=[blk] + [pl.BlockSpec((1, tr, cols), functools.partial(lambda j, i: (j, i, 0), j)) for j in range(3)],
        out_specs=blk, out_shape=jax.ShapeDtypeStruct((rows, cols), F32), compiler_params=_cp("parallel"),
    )(own, others, others, others)


def _sum_stack(parts, name, out_dtype=F32):
    rows, cols = parts[0].shape
    tr = _tile(rows, 512, 16)

    def body(*refs):
        acc = refs[0][...].astype(F32)
        for r in refs[1:-1]:
            acc = acc + r[...].astype(F32)
        refs[-1][...] = acc.astype(out_dtype)

    blk = pl.BlockSpec((tr, cols), lambda i: (i, 0))
    return pl.pallas_call(
        body, name=name, grid=(rows // tr,), in_specs=[blk] * len(parts), out_specs=blk,
        out_shape=jax.ShapeDtypeStruct((rows, cols), out_dtype), compiler_params=_cp("parallel"),
    )(*parts)


def _place():
    return lax.axis_index("x"), lax.axis_index("y"), lax.axis_index("c")


def _other_chips(x, y):
    return [(1 - x, y), (x, 1 - y), (1 - x, 1 - y)]


def _all_gather(shards, name):
    nb = len(shards)

    def body(*refs):
        ins, outs = refs[:nb], refs[nb:2 * nb]
        send_sems, recv_sems, local_sems = refs[2 * nb:]
        x, y, c = _place()
        me, sibling = (x, y, c), (x, y, 1 - c)
        chips = _other_chips(x, y)

        def copy(q, k, block, to, src=None):
            dst = outs[q].at[4 * block[0] + 2 * block[1] + block[2]]
            return pltpu.make_async_remote_copy(
                src_ref=dst if src is None else src, dst_ref=dst,
                send_sem=send_sems.at[7 * q + k], recv_sem=recv_sems.at[7 * q + k], device_id=to, device_id_type=MESH)

        started = []
        for q in range(nb):
            mine = pltpu.make_async_copy(ins[q], outs[q].at[4 * x + 2 * y + c], local_sems.at[q])
            mine.start()
            started.append(mine)
        first = []
        for q in range(nb):
            first.append(copy(q, 0, me, sibling, src=ins[q]))
            first += [copy(q, 1 + j, me, (*chip, c), src=ins[q]) for j, chip in enumerate(chips)]
        for cp in first:
            cp.start()
        passed = []
        for j, chip in enumerate(chips):
            for q in range(nb):
                copy(q, 1 + j, (*chip, c), me).wait_recv()
                fwd = copy(q, 4 + j, (*chip, c), sibling)
                fwd.start()
                passed.append(fwd)
        for q in range(nb):
            copy(q, 0, sibling, me).wait_recv()
            for j, chip in enumerate(chips):
                copy(q, 4 + j, (*chip, 1 - c), me).wait_recv()
        for cp in first + passed:
            cp.wait_send()
        for mine in started:
            mine.wait()

    return pl.pallas_call(
        body, name=name, in_specs=[ANY] * nb, out_specs=tuple([ANY] * nb),
        out_shape=tuple(jax.ShapeDtypeStruct((N_DEV,) + s.shape, s.dtype) for s in shards),
        scratch_shapes=[pltpu.SemaphoreType.DMA((7 * nb,)), pltpu.SemaphoreType.DMA((7 * nb,)), pltpu.SemaphoreType.DMA((nb,))],
    )(*shards)


def _swap_core_halves(gs, name):
    nb = len(gs)

    def body(*refs):
        g_refs, o_refs = refs[:nb], refs[nb:2 * nb]
        send_sems, recv_sems = refs[2 * nb:]
        x, y, c = _place()
        cps = []
        for q in range(nb):
            cp = pltpu.make_async_remote_copy(
                src_ref=g_refs[q].at[:, 1 - c], dst_ref=o_refs[q], send_sem=send_sems.at[q], recv_sem=recv_sems.at[q],
                device_id=(x, y, 1 - c), device_id_type=MESH)
            cp.start()
            cps.append(cp)
        for cp in cps:
            cp.wait()

    return pl.pallas_call(
        body, name=name, in_specs=[ANY] * nb, out_specs=tuple([ANY] * nb),
        out_shape=tuple(jax.ShapeDtypeStruct((4,) + g.shape[2:], g.dtype) for g in gs),
        scratch_shapes=[pltpu.SemaphoreType.DMA((nb,)), pltpu.SemaphoreType.DMA((nb,))],
    )(*gs)


def _exchange_chips(parts, name):
    nb = len(parts)

    def body(*refs):
        p_refs, o_refs = refs[:nb], refs[nb:2 * nb]
        send_sems, recv_sems = refs[2 * nb:]
        x, y, c = _place()
        cps = []
        for j, chip in enumerate(_other_chips(x, y)):
            for q in range(nb):
                cp = pltpu.make_async_remote_copy(
                    src_ref=p_refs[q].at[j], dst_ref=o_refs[q].at[j], send_sem=send_sems.at[3 * q + j],
                    recv_sem=recv_sems.at[3 * q + j], device_id=(*chip, c), device_id_type=MESH)
                cp.start()
                cps.append(cp)
        for cp in cps:
            cp.wait()

    return pl.pallas_call(
        body, name=name, in_specs=[ANY] * nb, out_specs=tuple([ANY] * nb),
        out_shape=tuple(jax.ShapeDtypeStruct(p.shape, p.dtype) for p in parts),
        scratch_shapes=[pltpu.SemaphoreType.DMA((3 * nb,)), pltpu.SemaphoreType.DMA((3 * nb,))],
    )(*parts)


def _chip_partials(g, got, name):
    _, _, rows, cols = g.shape
    tr = _tile(rows, 512, 16)

    def other_chip(j):
        x, y, _ = _place()
        return 2 * jnp.where(j != 1, 1 - x, x) + jnp.where(j != 0, 1 - y, y)

    def body_own(g_ref, r_ref, o_ref):
        o_ref[...] = g_ref[0, 0] + r_ref[0]

    def body_parts(g_ref, r_ref, o_ref):
        o_ref[0] = (g_ref[0, 0] + r_ref[0]).astype(BF16)

    own = pl.pallas_call(
        body_own, name=name + "_own", grid=(rows // tr,),
        in_specs=[pl.BlockSpec((1, 1, tr, cols), lambda i: (2 * lax.axis_index("x") + lax.axis_index("y"), lax.axis_index("c"), i, 0)),
                  pl.BlockSpec((1, tr, cols), lambda i: (2 * lax.axis_index("x") + lax.axis_index("y"), i, 0))],
        out_specs=pl.BlockSpec((tr, cols), lambda i: (i, 0)),
        out_shape=jax.ShapeDtypeStruct((rows, cols), F32), compiler_params=_cp("parallel"),
    )(g, got)
    parts = pl.pallas_call(
        body_parts, name=name + "_parts", grid=(3, rows // tr),
        in_specs=[pl.BlockSpec((1, 1, tr, cols), lambda j, i: (other_chip(j), lax.axis_index("c"), i, 0)),
                  pl.BlockSpec((1, tr, cols), lambda j, i: (other_chip(j), i, 0))],
        out_specs=pl.BlockSpec((1, tr, cols), lambda j, i: (j, i, 0)),
        out_shape=jax.ShapeDtypeStruct((3, rows, cols), BF16), compiler_params=_cp("parallel", "parallel"),
    )(g, got)
    return own, parts


class _Shard:
    def __init__(self, name, axis):
        self.name, self.axis = name, axis


BIG = [_Shard("w_in", 2), _Shard("pool_w_group", 2), _Shard("w_pool_up", 1), _Shard("w_ssd_out", 1),
       _Shard("w_o", 1), _Shard("w_ff1", 2), _Shard("w_ff2", 1)]
SMALL_SHARDED = [_Shard("meta_tokens", 1), _Shard("conv_w", 2)]
REPLICATED = ["mix_norm_w", "b_gate", "pool_scale", "conv_b", "dt_bias", "a_log", "d_skip", "ssd_norm_w",
              "mlp_norm_w", "final_norm_w"]
WEIGHTS = ["meta_tokens", "mix_norm_w", "w_in", "b_gate", "pool_w_group", "pool_scale", "w_pool_up", "conv_w", "conv_b",
           "dt_bias", "a_log", "d_skip", "ssd_norm_w", "w_ssd_out", "w_o", "mlp_norm_w", "w_ff1", "w_ff2", "final_norm_w"]


def _rows2(a):
    return a.reshape(-1, a.shape[-1])


def _unshard(stack, shard_shape, axis):
    t = stack.reshape((N_DEV,) + tuple(shard_shape))
    return jnp.concatenate([t[d] for d in range(N_DEV)], axis=axis)


def _reshard(layers, axis):
    cut = [jnp.split(t, N_DEV, axis=axis - 1) for t in layers]
    blocks = [jnp.concatenate([_rows2(pieces[d]) for pieces in cut], axis=0) for d in range(N_DEV)]
    return jnp.stack(blocks).reshape((4, 2) + blocks[0].shape)


def _lane_rows(a):
    n = a.size
    if n % LANES:
        return jnp.pad(a.reshape(-1), (0, (-n) % LANES)).reshape(-1, LANES)
    return a.reshape(-1, LANES)


def _unpack_small(buf, spans, shapes):
    out = []
    for (o, r), shp in zip(spans, shapes):
        n = 1
        for d in shp:
            n *= d
        t = buf[o:o + r]
        out.append(t.reshape(shp) if n == r * LANES else t.reshape(-1)[:n].reshape(shp))
    return out


def _pack_small(parts, mult):
    mats = [_lane_rows(p) for p in parts]
    spans, o = [], 0
    for t in mats:
        spans.append((o, t.shape[0]))
        o += t.shape[0]
    fill = (-o) % mult
    if fill:
        mats.append(jnp.zeros((fill, LANES), mats[0].dtype))
    return jnp.concatenate(mats, axis=0), spans


def _layer_fwd(h, lw, cfg, tag):
    pad, n_heads = cfg["pad"], cfg["n_heads"]
    u = _rms_fwd(h, lw["mix_norm_w"], f"rms_mix_{tag}")
    p_xbc = _mm(u, lw["w_xbc"], "nn", f"proj_xbc_{tag}")
    p_z = _mm(u, lw["w_z"], "nn", f"proj_z_{tag}")
    p_gate = _mm(u, lw["w_gate"], "nn", f"proj_gate_{tag}")
    p_pool = _mm(u, lw["w_pool"], "nn", f"proj_pool_{tag}")
    p_dt = _mm(u, lw["w_dt"], "nn", f"proj_dt_{tag}")
    pooled, y1 = _pool_fwd(p_pool, lw["pool_w_group"], lw["pool_scale"], pad, f"pool_fwd_{tag}")
    y_pool = _mm(y1, lw["w_pool_up"], "nn", f"pool_up_{tag}")
    xbc = _conv_fwd(p_xbc, lw["conv_w"], lw["conv_b"], f"conv_fwd_{tag}")
    y, yn, prev = _ssd_fwd(xbc, p_dt, p_z, lw["dt_bias"], lw["a_log"], lw["d_skip"], lw["ssd_norm_w"], pad, n_heads, f"ssd_fwd_{tag}")
    y_ssd = _mm(yn, lw["w_ssd_out"], "nn", f"ssd_out_{tag}")
    mix = _gate_fwd(p_gate, lw["b_gate"], y_pool, y_ssd, f"gate_fwd_{tag}")
    h_mid = _mm(mix, lw["w_o"], "nn", f"mix_out_{tag}", res=h)
    v = _rms_fwd(h_mid, lw["mlp_norm_w"], f"rms_mlp_{tag}")
    hid, act = _mm(v, lw["w_ff1"], "nn", f"ff1_{tag}", epi="relu2")
    h_out = _mm(act, lw["w_ff2"], "nn", f"ff2_{tag}", res=h_mid)
    saved = dict(h=h, u=u, p_xbc=p_xbc, p_z=p_z, p_gate=p_gate, p_dt=p_dt, pooled=pooled, y1=y1, y_pool=y_pool, xbc=xbc,
                 y=y, yn=yn, prev=prev, y_ssd=y_ssd, mix=mix, h_mid=h_mid, v=v, hid=hid, act=act)
    return h_out, saved


def _layer_bwd(dh, lw, s, cfg, tag):
    pad, n_heads = cfg["pad"], cfg["n_heads"]
    g = {}
    dhid = _mm(dh, lw["w_ff2"], "nt", f"d_act_{tag}", epi="drelu2", aux=s["hid"])
    g["w_ff2"] = _mm(s["act"], dh, "tn", f"dw_ff2_{tag}")
    dv = _mm(dhid, lw["w_ff1"], "nt", f"d_v_{tag}")
    g["w_ff1"] = _mm(s["v"], dhid, "tn", f"dw_ff1_{tag}")
    dh_mid, g["mlp_norm_w"] = _rms_bwd(s["h_mid"], lw["mlp_norm_w"], dv, dh, f"rms_mlp_bwd_{tag}")
    dmix = _mm(dh_mid, lw["w_o"], "nt", f"d_mix_{tag}")
    g["w_o"] = _mm(s["mix"], dh_mid, "tn", f"dw_o_{tag}")
    dgate, dy_pool, dy_ssd, g["b_gate"] = _gate_bwd(s["p_gate"], lw["b_gate"], s["y_pool"], s["y_ssd"], dmix, f"gate_bwd_{tag}")
    dy1 = _mm(dy_pool, lw["w_pool_up"], "nt", f"d_y1_{tag}")
    g["w_pool_up"] = _mm(s["y1"], dy_pool, "tn", f"dw_pool_up_{tag}")
    dpool, g["pool_w_group"], g["pool_scale"] = _pool_bwd(s["pooled"], lw["pool_w_group"], lw["pool_scale"], dy1, pad, f"pool_bwd_{tag}")
    dyn = _mm(dy_ssd, lw["w_ssd_out"], "nt", f"d_yn_{tag}")
    g["w_ssd_out"] = _mm(s["yn"], dy_ssd, "tn", f"dw_ssd_out_{tag}")
    dact, dz, ddt, g["dt_bias"], g["a_log"], g["d_skip"], g["ssd_norm_w"] = _ssd_bwd(
        s["xbc"], s["p_dt"], s["p_z"], lw["dt_bias"], lw["a_log"], lw["d_skip"], lw["ssd_norm_w"], s["y"], s["prev"], dyn,
        pad, n_heads, f"ssd_bwd_{tag}")
    dxbc, g["conv_w"], g["conv_b"] = _conv_bwd(s["p_xbc"], lw["conv_w"], lw["conv_b"], dact, pad, f"conv_bwd_{tag}")
    u = s["u"]
    du = _mm(dxbc, lw["w_xbc"], "nt", f"du_xbc_{tag}")
    du = _mm(dz, lw["w_z"], "nt", f"du_z_{tag}", res=du)
    du = _mm(dgate, lw["w_gate"], "nt", f"du_gate_{tag}", res=du)
    du = _mm(dpool, lw["w_pool"], "nt", f"du_pool_{tag}", res=du)
    du = _mm(ddt, lw["w_dt"], "nt", f"du_dt_{tag}", res=du)
    g["w_xbc"] = _mm(u, dxbc, "tn", f"dw_xbc_{tag}")
    g["w_z"] = _mm(u, dz, "tn", f"dw_z_{tag}")
    g["w_gate"] = _mm(u, dgate, "tn", f"dw_gate_{tag}")
    g["w_pool"] = _mm(u, dpool, "tn", f"dw_pool_{tag}")
    g["w_dt"] = _mm(u, ddt, "tn", f"dw_dt_{tag}")
    dh_in, g["mix_norm_w"] = _rms_bwd(s["h"], lw["mix_norm_w"], du, dh_mid, f"rms_mix_bwd_{tag}")
    return dh_in, g


def _pad_lanes(v):
    return jnp.pad(v, (0, LANES - v.shape[0])).reshape(1, LANES)


def _local_step(x2, target, meta_full, full, rep, cfg):
    depth, pad, n_meta, H = cfg["depth"], cfg["pad"], cfg["n_meta"], cfg["n_heads"]
    D = x2.shape[1]
    di = rep["ssd_norm_w"].shape[1]
    c_pool, c_z, c_xbc, c_dt = cfg["cols"]
    h = jnp.concatenate([jnp.zeros((pad, D), F32), meta_full, x2], axis=0)
    lws, saves = [], []
    for i in range(depth):
        w_in = full["w_in"][i]
        o = 0
        w_pool = w_in[:, o:o + c_pool]; o += c_pool
        w_z = w_in[:, o:o + c_z]; o += c_z
        w_xbc = w_in[:, o:o + c_xbc]; o += c_xbc
        w_dt = jnp.pad(w_in[:, o:o + c_dt], ((0, 0), (0, LANES - c_dt))); o += c_dt
        w_gate = w_in[:, o:]
        lw = dict(
            w_pool=w_pool, w_z=w_z, w_xbc=w_xbc, w_dt=w_dt, w_gate=w_gate,
            pool_w_group=full["pool_w_group"][i], w_pool_up=full["w_pool_up"][i], w_ssd_out=full["w_ssd_out"][i],
            w_o=full["w_o"][i], w_ff1=full["w_ff1"][i], w_ff2=full["w_ff2"][i], conv_w=full["conv_w"][i],
            mix_norm_w=rep["mix_norm_w"][i], b_gate=rep["b_gate"][i], pool_scale=rep["pool_scale"][i].reshape(1, -1),
            conv_b=rep["conv_b"][i].reshape(1, -1), dt_bias=_pad_lanes(rep["dt_bias"][i]), a_log=_pad_lanes(rep["a_log"][i]),
            d_skip=_pad_lanes(rep["d_skip"][i]), ssd_norm_w=rep["ssd_norm_w"][i].reshape(1, di), mlp_norm_w=rep["mlp_norm_w"][i])
        lws.append(lw)
        h, s = _layer_fwd(h, lw, cfg, f"l{i}")
        saves.append(s)
    loss, dh, g_final = _final_loss(h, rep["final_norm_w"], target, pad + n_meta, "final_loss")
    per_layer = []
    for i in range(depth - 1, -1, -1):
        dh, g = _layer_bwd(dh, lws[i], saves[i], cfg, f"l{i}")
        per_layer.append(g)
    per_layer.reverse()

    def stack(key, fn=lambda t: t):
        return jnp.stack([fn(g[key]) for g in per_layer])

    def layers(key):
        return [g[key] for g in per_layer]

    grads = dict(
        w_in=[jnp.concatenate([g["w_pool"], g["w_z"], g["w_xbc"], g["w_dt"][:, :c_dt], g["w_gate"]], axis=1) for g in per_layer],
        pool_w_group=layers("pool_w_group"), w_pool_up=layers("w_pool_up"), w_ssd_out=layers("w_ssd_out"), w_o=layers("w_o"),
        w_ff1=layers("w_ff1"), w_ff2=layers("w_ff2"), conv_w=stack("conv_w"),
        mix_norm_w=stack("mix_norm_w", lambda t: t[0]), b_gate=stack("b_gate", lambda t: t[0]),
        pool_scale=stack("pool_scale", lambda t: t[0]), conv_b=stack("conv_b", lambda t: t[0]),
        dt_bias=stack("dt_bias", lambda t: t[0, :H]), a_log=stack("a_log", lambda t: t[0, :H]), d_skip=stack("d_skip", lambda t: t[0, :H]),
        ssd_norm_w=stack("ssd_norm_w", lambda t: t[0]), mlp_norm_w=stack("mlp_norm_w", lambda t: t[0]),
        final_norm_w=g_final[0], meta_tokens=dh[pad:pad + n_meta])
    return loss, dh[pad + n_meta:], grads


def kernel(x, meta_tokens, mix_norm_w, w_in, b_gate, pool_w_group, pool_scale, w_pool_up, conv_w, conv_b, dt_bias, a_log, d_skip, ssd_norm_w, w_ssd_out, w_o, mlp_norm_w, w_ff1, w_ff2, final_norm_w, loss_target, m_meta_tokens, m_mix_norm_w, m_w_in, m_b_gate, m_pool_w_group, m_pool_scale, m_w_pool_up, m_conv_w, m_conv_b, m_dt_bias, m_a_log, m_d_skip, m_ssd_norm_w, m_w_ssd_out, m_w_o, m_mlp_norm_w, m_w_ff1, m_w_ff2, m_final_norm_w, v_meta_tokens, v_mix_norm_w, v_w_in, v_b_gate, v_pool_w_group, v_pool_scale, v_w_pool_up, v_conv_w, v_conv_b, v_dt_bias, v_a_log, v_d_skip, v_ssd_norm_w, v_w_ssd_out, v_w_o, v_mlp_norm_w, v_w_ff1, v_w_ff2, v_final_norm_w):
    w = dict(meta_tokens=meta_tokens, mix_norm_w=mix_norm_w, w_in=w_in, b_gate=b_gate, pool_w_group=pool_w_group,
             pool_scale=pool_scale, w_pool_up=w_pool_up, conv_w=conv_w, conv_b=conv_b, dt_bias=dt_bias, a_log=a_log,
             d_skip=d_skip, ssd_norm_w=ssd_norm_w, w_ssd_out=w_ssd_out, w_o=w_o, mlp_norm_w=mlp_norm_w, w_ff1=w_ff1,
             w_ff2=w_ff2, final_norm_w=final_norm_w)
    m = dict(meta_tokens=m_meta_tokens, mix_norm_w=m_mix_norm_w, w_in=m_w_in, b_gate=m_b_gate, pool_w_group=m_pool_w_group,
             pool_scale=m_pool_scale, w_pool_up=m_w_pool_up, conv_w=m_conv_w, conv_b=m_conv_b, dt_bias=m_dt_bias, a_log=m_a_log,
             d_skip=m_d_skip, ssd_norm_w=m_ssd_norm_w, w_ssd_out=m_w_ssd_out, w_o=m_w_o, mlp_norm_w=m_mlp_norm_w, w_ff1=m_w_ff1,
             w_ff2=m_w_ff2, final_norm_w=m_final_norm_w)
    v = dict(meta_tokens=v_meta_tokens, mix_norm_w=v_mix_norm_w, w_in=v_w_in, b_gate=v_b_gate, pool_w_group=v_pool_w_group,
             pool_scale=v_pool_scale, w_pool_up=v_w_pool_up, conv_w=v_conv_w, conv_b=v_conv_b, dt_bias=v_dt_bias, a_log=v_a_log,
             d_skip=v_d_skip, ssd_norm_w=v_ssd_norm_w, w_ssd_out=v_w_ssd_out, w_o=v_w_o, mlp_norm_w=v_mlp_norm_w, w_ff1=v_w_ff1,
             w_ff2=v_w_ff2, final_norm_w=v_final_norm_w)

    _, seq, D = x.shape
    n_meta = meta_tokens.shape[0]
    depth = w_in.shape[0]
    n_heads = dt_bias.shape[1]
    d_inner = ssd_norm_w.shape[1]
    d_xbc = conv_b.shape[1]
    pool_width = pool_scale.shape[1]
    pad = (-n_meta) % CHUNK
    cfg = dict(depth=depth, pad=pad, n_meta=n_meta, n_heads=n_heads, cols=(pool_width, d_inner, d_xbc, n_heads))
    assert (pad + n_meta + seq) % CHUNK == 0 and pad + n_meta == CHUNK

    xi, yi, ci = _place()
    me = 4 * xi + 2 * yi + ci

    sharded = BIG + SMALL_SHARDED
    gathered = _all_gather([_rows2(w[s.name].astype(BF16)) for s in BIG] + [_rows2(w[s.name]) for s in SMALL_SHARDED], "gather_weights")
    full = {s.name: _unshard(t, w[s.name].shape, s.axis) for s, t in zip(sharded, gathered)}
    rep = {k: w[k] for k in REPLICATED}

    loss_part, dx, grads = _local_step(x[0], loss_target[0], full["meta_tokens"], full, rep, cfg)
    loss = lax.psum(loss_part[0, 0], ("x", "y", "c"))

    gstacks = [_reshard(grads[s.name], s.axis) for s in BIG]
    gots = _swap_core_halves(gstacks, "rs_sibling")
    sums = [_chip_partials(g4, got, f"rs_chip_sum_{s.name}") for s, g4, got in zip(BIG, gstacks, gots)]
    others = _exchange_chips([parts for _, parts in sums], "rs_chips")
    g_loc = {}
    for s, (own, _), oth in zip(BIG, sums, others):
        g_loc[s.name] = _final_sum(own, oth, f"rs_final_sum_{s.name}").reshape(w[s.name].shape)

    small_names = REPLICATED + [s.name for s in SMALL_SHARDED]
    sm_buf, sm_spans = _pack_small([grads[k] for k in small_names], 16)
    (sm_all,) = _all_gather([sm_buf], "gather_small_grads")
    sm_sum = _sum_stack([sm_all[d] for d in range(N_DEV)], "small_grads_sum")
    g_small = dict(zip(small_names, _unpack_small(sm_sum, sm_spans, [grads[k].shape for k in small_names])))
    for k in REPLICATED:
        g_loc[k] = g_small[k]
    for s in SMALL_SHARDED:
        blk = w[s.name].shape[s.axis]
        g_loc[s.name] = lax.dynamic_slice_in_dim(g_small[s.name], me * blk, blk, axis=s.axis)

    delta, new_m, new_v = {}, {}, {}
    for s in BIG:
        shp = w[s.name].shape
        d2, m2, v2 = _adamw(_rows2(w[s.name]), _rows2(g_loc[s.name]), _rows2(m[s.name]), _rows2(v[s.name]), f"adamw_{s.name}")
        delta[s.name], new_m[s.name], new_v[s.name] = d2.reshape(shp), m2.reshape(shp), v2.reshape(shp)
    loc_shapes = [w[k].shape for k in small_names]
    packed = [_pack_small([t[k] for k in small_names], 8) for t in (w, g_loc, m, v)]
    loc_spans = packed[0][1]
    outs = _adamw(*[p[0] for p in packed], "adamw_small")
    for res, buf in zip((delta, new_m, new_v), outs):
        res.update(zip(small_names, _unpack_small(buf, loc_spans, loc_shapes)))

    return (loss, dx[None], *[g_loc[k] for k in WEIGHTS], *[delta[k] for k in WEIGHTS],
            *[new_m[k] for k in WEIGHTS], *[new_v[k] for k in WEIGHTS])
```

```python
import functools

import jax
import jax.numpy as jnp
from jax import lax
from jax.experimental import pallas as pl
from jax.experimental.pallas import tpu as pltpu

F32 = jnp.float32
BF16 = jnp.bfloat16

EPS = 1e-5
D_STATE = 128
CHUNK = 128
LANES = 128
POOL_WINDOWS = (2, 4, 8, 16)
POOL_HALO = 16
CONV_WIDTH = 4
CONV_HALO = 8
N_DEV = 8
ADAM_LR = 0.001
ADAM_B1 = 0.9
ADAM_B2 = 0.999
ADAM_EPS = 1e-08
ADAM_WD = 0.01
ADAM_STEP = 10
VMEM_LIMIT = 52 * 1024 * 1024
MESH = pl.DeviceIdType.MESH
ANY = pl.BlockSpec(memory_space=pl.ANY)


def _cp(*sem):
    return pltpu.CompilerParams(dimension_semantics=sem, vmem_limit_bytes=VMEM_LIMIT)


def _tile(n, target, mult):
    best = None
    for t in range(mult, min(n, target) + 1, mult):
        if n % t == 0:
            best = t
    return best if best is not None else n


def _sigmoid(x):
    return jax.nn.sigmoid(x)


def _iota(shape, dim):
    return lax.broadcasted_iota(jnp.int32, shape, dim)


_DIMS = {"nn": (((1,), (0,)), ((), ())), "nt": (((1,), (1,)), ((), ())), "tn": (((0,), (0,)), ((), ()))}


def _dot(a, b, mode="nn"):
    return lax.dot_general(a.astype(BF16), b.astype(BF16), _DIMS[mode], preferred_element_type=F32)


DEP = pl.BlockSpec((8, LANES), lambda *_: (0, 0))


def _mm(a, b, mode, name, *, out_dtype=F32, res=None, epi=None, aux=None, dep=None):
    if mode == "nn":
        (M, K), (_, N) = a.shape, b.shape
    elif mode == "nt":
        (M, K), (N, _) = a.shape, b.shape
    else:
        (K, M), (_, N) = a.shape, b.shape
    if mode == "tn":
        tm, tn, tk = _tile(M, 512, 128), _tile(N, 1024, 128), _tile(K, 1056, 16)
    else:
        tm, tn, tk = _tile(M, 1056, 16), _tile(N, 512, 128), _tile(K, 4096, 128)
    nk = K // tk
    a_spec = pl.BlockSpec((tk, tm), lambda i, j, k: (k, i)) if mode == "tn" else pl.BlockSpec((tm, tk), lambda i, j, k: (i, k))
    b_spec = pl.BlockSpec((tn, tk), lambda i, j, k: (j, k)) if mode == "nt" else pl.BlockSpec((tk, tn), lambda i, j, k: (k, j))
    o_spec = pl.BlockSpec((tm, tn), lambda i, j, k: (i, j))
    extra = [t for t in (res, aux) if t is not None]
    deps = [] if dep is None else [dep]

    def body(*refs):
        a_ref, b_ref = refs[0], refs[1]
        x_ref = refs[2] if extra else None
        outs = refs[2 + len(extra) + len(deps):]
        p = _dot(a_ref[...], b_ref[...], mode)

        def finish(r):
            if res is not None:
                outs[0][...] = (x_ref[...] + r).astype(out_dtype)
            elif epi == "relu2":
                outs[0][...] = r
                hid = jnp.maximum(r, 0.0)
                outs[1][...] = (hid * hid).astype(BF16)
            elif epi == "drelu2":
                outs[0][...] = (r * (2.0 * jnp.maximum(x_ref[...], 0.0))).astype(BF16)
            else:
                outs[0][...] = r.astype(out_dtype)

        if nk == 1:
            finish(p)
        else:
            acc = outs[-1]
            k = pl.program_id(2)

            @pl.when(k == 0)
            def _():
                acc[...] = p

            @pl.when(k > 0)
            def _():
                acc[...] += p

            @pl.when(k == nk - 1)
            def _():
                finish(acc[...])

    if epi == "relu2":
        out_shape = (jax.ShapeDtypeStruct((M, N), F32), jax.ShapeDtypeStruct((M, N), BF16))
        out_specs = (o_spec, o_spec)
    elif epi == "drelu2":
        out_shape, out_specs = jax.ShapeDtypeStruct((M, N), BF16), o_spec
    else:
        out_shape, out_specs = jax.ShapeDtypeStruct((M, N), out_dtype), o_spec
    return pl.pallas_call(
        body, name=name, grid=(M // tm, N // tn, nk),
        in_specs=[a_spec, b_spec] + [o_spec] * len(extra) + [DEP] * len(deps),
        out_specs=out_specs, out_shape=out_shape,
        scratch_shapes=[pltpu.VMEM((tm, tn), F32)] if nk > 1 else [],
        compiler_params=_cp("parallel", "parallel", "arbitrary"),
    )(a, b, *extra, *deps)


def _rms_fwd(h, w, name, dep=None):
    T, D = h.shape
    tr = _tile(T, 1056, 16)
    deps = [] if dep is None else [dep]

    def body(h_ref, w_ref, *rest):
        x = h_ref[...]
        xn = x * lax.rsqrt(jnp.mean(x * x, axis=-1, keepdims=True) + EPS)
        rest[-1][...] = (xn * w_ref[...]).astype(BF16)

    return pl.pallas_call(
        body, name=name, grid=(T // tr,),
        in_specs=[pl.BlockSpec((tr, D), lambda i: (i, 0)), pl.BlockSpec((1, D), lambda i: (0, 0))] + [DEP] * len(deps),
        out_specs=pl.BlockSpec((tr, D), lambda i: (i, 0)), out_shape=jax.ShapeDtypeStruct((T, D), BF16),
        compiler_params=_cp("parallel"),
    )(h, w.reshape(1, D), *deps)


def _rms_bwd(h, w, dy, dres, name):
    T, D = h.shape
    tr = _tile(T, 528, 8)

    def body(h_ref, w_ref, dy_ref, dres_ref, dh_ref, dw_ref):
        x = h_ref[...]
        rstd = lax.rsqrt(jnp.mean(x * x, axis=-1, keepdims=True) + EPS)
        xn = x * rstd
        dy = dy_ref[...]
        dxn = dy * w_ref[...]
        dh_ref[...] = dres_ref[...] + rstd * (dxn - xn * jnp.mean(dxn * xn, axis=-1, keepdims=True))
        dw = jnp.sum(dy * xn, axis=0, keepdims=True)

        @pl.when(pl.program_id(0) == 0)
        def _():
            dw_ref[...] = dw

        @pl.when(pl.program_id(0) > 0)
        def _():
            dw_ref[...] += dw

    row = pl.BlockSpec((tr, D), lambda i: (i, 0))
    vec = pl.BlockSpec((1, D), lambda i: (0, 0))
    return pl.pallas_call(
        body, name=name, grid=(T // tr,),
        in_specs=[row, vec, row, row], out_specs=(row, vec),
        out_shape=(jax.ShapeDtypeStruct((T, D), F32), jax.ShapeDtypeStruct((1, D), F32)),
        compiler_params=_cp("arbitrary"),
    )(h, w.reshape(1, D), dy, dres)


def _final_loss(h, w, target, first_row, name):
    T, D = h.shape
    tr = CHUNK
    assert first_row == tr

    def body(h_ref, w_ref, t_ref, loss_ref, dh_ref, dw_ref):
        i = pl.program_id(0)
        x = h_ref[...]
        rstd = lax.rsqrt(jnp.mean(x * x, axis=-1, keepdims=True) + EPS)
        xn = x * rstd
        w = w_ref[...]
        live = i > 0
        err = jnp.where(live, xn * w - t_ref[...], 0.0)
        part = 0.5 * jnp.sum(jnp.mean(err * err, axis=-1, keepdims=True), axis=0, keepdims=True)
        dout = err * (1.0 / D)
        dxn = dout * w
        dh_ref[...] = rstd * (dxn - xn * jnp.mean(dxn * xn, axis=-1, keepdims=True))
        dw = jnp.sum(dout * xn, axis=0, keepdims=True)

        @pl.when(i == 0)
        def _():
            loss_ref[...] = part
            dw_ref[...] = dw

        @pl.when(i > 0)
        def _():
            loss_ref[...] += part
            dw_ref[...] += dw

    row = pl.BlockSpec((tr, D), lambda i: (i, 0))
    vec = pl.BlockSpec((1, D), lambda i: (0, 0))
    return pl.pallas_call(
        body, name=name, grid=(T // tr,),
        in_specs=[row, vec, pl.BlockSpec((tr, D), lambda i: (jnp.maximum(i - 1, 0), 0))],
        out_specs=(pl.BlockSpec((1, 1), lambda i: (0, 0)), row, vec),
        out_shape=(jax.ShapeDtypeStruct((1, 1), F32), jax.ShapeDtypeStruct((T, D), F32), jax.ShapeDtypeStruct((1, D), F32)),
        compiler_params=_cp("arbitrary"),
    )(h, w.reshape(1, D), target)


def _gate_fwd(pg, bg, y_pool, y_ssd, name):
    T, D = y_pool.shape
    tr = _tile(T, 528, 16)

    def body(gp_ref, gs_ref, bp_ref, bs_ref, yp_ref, ys_ref, o_ref):
        gp = _sigmoid(gp_ref[...] + bp_ref[...])
        gs = _sigmoid(gs_ref[...] + bs_ref[...])
        o_ref[...] = (gp * yp_ref[...] + gs * ys_ref[...]).astype(BF16)

    row = pl.BlockSpec((tr, D), lambda i: (i, 0))
    row1 = pl.BlockSpec((tr, D), lambda i: (i, 1))
    vec = pl.BlockSpec((1, D), lambda i: (0, 0))
    vec1 = pl.BlockSpec((1, D), lambda i: (0, 1))
    b2 = bg.reshape(1, 2 * D)
    return pl.pallas_call(
        body, name=name, grid=(T // tr,),
        in_specs=[row, row1, vec, vec1, row, row], out_specs=row,
        out_shape=jax.ShapeDtypeStruct((T, D), BF16), compiler_params=_cp("parallel"),
    )(pg, pg, b2, b2, y_pool, y_ssd)


def _gate_bwd(pg, bg, y_pool, y_ssd, dmix, name):
    T, D = y_pool.shape
    tr = _tile(T, 528, 16)

    def body(gp_ref, gs_ref, bp_ref, bs_ref, yp_ref, ys_ref, dm_ref, dg_ref, dyp_ref, dys_ref, db_ref):
        gp = _sigmoid(gp_ref[...] + bp_ref[...])
        gs = _sigmoid(gs_ref[...] + bs_ref[...])
        dm = dm_ref[...]
        dyp_ref[...] = (dm * gp).astype(BF16)
        dys_ref[...] = (dm * gs).astype(BF16)
        dgp = dm * yp_ref[...] * gp * (1.0 - gp)
        dgs = dm * ys_ref[...] * gs * (1.0 - gs)
        dg_ref[:, :D] = dgp.astype(BF16)
        dg_ref[:, D:] = dgs.astype(BF16)
        db = jnp.concatenate([jnp.sum(dgp, axis=0, keepdims=True), jnp.sum(dgs, axis=0, keepdims=True)], axis=1)

        @pl.when(pl.program_id(0) == 0)
        def _():
            db_ref[...] = db

        @pl.when(pl.program_id(0) > 0)
        def _():
            db_ref[...] += db

    row = pl.BlockSpec((tr, D), lambda i: (i, 0))
    row1 = pl.BlockSpec((tr, D), lambda i: (i, 1))
    wide = pl.BlockSpec((tr, 2 * D), lambda i: (i, 0))
    vec = pl.BlockSpec((1, D), lambda i: (0, 0))
    vec1 = pl.BlockSpec((1, D), lambda i: (0, 1))
    vec2 = pl.BlockSpec((1, 2 * D), lambda i: (0, 0))
    b2 = bg.reshape(1, 2 * D)
    return pl.pallas_call(
        body, name=name, grid=(T // tr,),
        in_specs=[row, row1, vec, vec1, row, row, row], out_specs=(wide, row, row, vec2),
        out_shape=(jax.ShapeDtypeStruct((T, 2 * D), BF16), jax.ShapeDtypeStruct((T, D), BF16),
                   jax.ShapeDtypeStruct((T, D), BF16), jax.ShapeDtypeStruct((1, 2 * D), F32)),
        compiler_params=_cp("arbitrary"),
    )(pg, pg, b2, b2, y_pool, y_ssd, dmix)


def _pool_count(c, pad, window):
    pos = c * CHUNK + _iota((CHUNK, 1), 0) - pad
    return jnp.clip(pos + 1, 1, window).astype(F32)


def _by_group(g, vals):
    out = vals[-1]
    for k in range(len(vals) - 2, -1, -1):
        out = jnp.where(g == k, vals[k], out)
    return out


def _by_row_block(rows, vals, block):
    out = vals[0]
    for r in range(1, len(vals)):
        out = jnp.where(rows >= r * block, vals[r], out)
    return out


def _pool_fwd(u, wg, scale, pad, name):
    T, C = u.shape
    G, Cg, _ = wg.shape
    nc = T // CHUNK

    def body(u_ref, wg_ref, s_ref, p_ref, y_ref):
        g = pl.program_id(0)
        window = _by_group(g, POOL_WINDOWS)

        def chunk(c, carry):
            r0 = pl.multiple_of(c * CHUNK, CHUNK)
            h0 = pl.multiple_of(jnp.maximum(r0 - POOL_HALO, 0), 8)
            halo = jnp.where(c > 0, u_ref[pl.ds(h0, POOL_HALO), :], 0.0)
            xc = u_ref[pl.ds(r0, CHUNK), :]
            s = jnp.concatenate([halo, xc], axis=0)
            sums = []
            k = 1
            while k < POOL_WINDOWS[-1]:
                s = s + pltpu.roll(s, k, 0)
                k *= 2
                if k in POOL_WINDOWS:
                    sums.append(s[POOL_HALO:])
            wsum = _by_group(g, sums)
            pooled = wsum / _pool_count(c, pad, window) - xc
            pb = pooled.astype(BF16)
            p_ref[pl.ds(r0, CHUNK), :] = pb
            y_ref[pl.ds(r0, CHUNK), :] = (_dot(pb, wg_ref[0]) * s_ref[...]).astype(BF16)
            return carry

        lax.fori_loop(0, nc, chunk, 0)

    col = pl.BlockSpec((T, Cg), lambda g: (0, g))
    return pl.pallas_call(
        body, name=name, grid=(G,),
        in_specs=[col, pl.BlockSpec((1, Cg, Cg), lambda g: (g, 0, 0)), pl.BlockSpec((1, Cg), lambda g: (0, g))],
        out_specs=(col, col),
        out_shape=(jax.ShapeDtypeStruct((T, C), BF16), jax.ShapeDtypeStruct((T, C), BF16)),
        compiler_params=_cp("parallel"),
    )(u, wg, scale)


def _pool_bwd(pooled, wg, scale, dy, pad, name):
    T, C = dy.shape
    G, Cg, _ = wg.shape
    nc = T // CHUNK

    def body(p_ref, wg_ref, s_ref, dy_ref, du_ref, dwg_ref, ds_ref, halo_ref):
        g = pl.program_id(0)
        window = _by_group(g, POOL_WINDOWS)
        halo_ref[...] = jnp.zeros_like(halo_ref)
        dwg_ref[...] = jnp.zeros_like(dwg_ref)
        ds_ref[...] = jnp.zeros_like(ds_ref)

        def chunk(i, carry):
            c = nc - 1 - i
            r0 = pl.multiple_of(c * CHUNK, CHUNK)
            pb = p_ref[pl.ds(r0, CHUNK), :]
            dyc = dy_ref[pl.ds(r0, CHUNK), :]
            w = wg_ref[0]
            ypre = _dot(pb, w)
            ds_ref[...] += jnp.sum(dyc * ypre, axis=0, keepdims=True)
            dyp = (dyc * s_ref[...]).astype(BF16)
            dwg_ref[0] += _dot(pb, dyp, "tn")
            dpool = _dot(dyp, w, "nt")
            q = dpool / _pool_count(c, pad, window)
            s = jnp.concatenate([q, halo_ref[...]], axis=0)
            n = CHUNK + POOL_HALO
            sums = []
            k = 1
            while k < POOL_WINDOWS[-1]:
                s = s + pltpu.roll(s, n - k, 0)
                k *= 2
                if k in POOL_WINDOWS:
                    sums.append(s[:CHUNK])
            du = _by_group(g, sums) - dpool
            rows = r0 + _iota((CHUNK, 1), 0)
            du_ref[pl.ds(r0, CHUNK), :] = jnp.where(rows >= pad, du, 0.0).astype(BF16)
            halo_ref[...] = q[:POOL_HALO]
            return carry

        lax.fori_loop(0, nc, chunk, 0)

    col = pl.BlockSpec((T, Cg), lambda g: (0, g))
    return pl.pallas_call(
        body, name=name, grid=(G,),
        in_specs=[col, pl.BlockSpec((1, Cg, Cg), lambda g: (g, 0, 0)), pl.BlockSpec((1, Cg), lambda g: (0, g)), col],
        out_specs=(col, pl.BlockSpec((1, Cg, Cg), lambda g: (g, 0, 0)), pl.BlockSpec((1, Cg), lambda g: (0, g))),
        out_shape=(jax.ShapeDtypeStruct((T, C), BF16), jax.ShapeDtypeStruct((G, Cg, Cg), F32), jax.ShapeDtypeStruct((1, C), F32)),
        scratch_shapes=[pltpu.VMEM((POOL_HALO, Cg), F32)],
        compiler_params=_cp("parallel"),
    )(pooled, wg, scale, dy)


def _conv_pre(x_ref, w_ref, b_ref, c, r0):
    h0 = pl.multiple_of(jnp.maximum(r0 - CONV_HALO, 0), 8)
    halo = jnp.where(c > 0, x_ref[pl.ds(h0, CONV_HALO), :], 0.0)
    xe = jnp.concatenate([halo, x_ref[pl.ds(r0, CHUNK), :]], axis=0)
    y = jnp.broadcast_to(b_ref[...], (CHUNK, xe.shape[1]))
    for k in range(CONV_WIDTH):
        shift = CONV_WIDTH - 1 - k
        xs = xe if shift == 0 else pltpu.roll(xe, shift, 0)
        y = y + xs[CONV_HALO:] * w_ref[k:k + 1, :]
    return y, xe


def _conv_fwd(x, w, b, name):
    T = x.shape[0]
    C = w.shape[1]
    tc = _tile(C, 256, 128)
    nc = T // CHUNK

    def body(x_ref, w_ref, b_ref, o_ref):
        def chunk(c, carry):
            r0 = pl.multiple_of(c * CHUNK, CHUNK)
            y, _ = _conv_pre(x_ref, w_ref, b_ref, c, r0)
            o_ref[pl.ds(r0, CHUNK), :] = y * _sigmoid(y)
            return carry

        lax.fori_loop(0, nc, chunk, 0)

    col = pl.BlockSpec((T, tc), lambda j: (0, j))
    return pl.pallas_call(
        body, name=name, grid=(C // tc,),
        in_specs=[col, pl.BlockSpec((CONV_WIDTH, tc), lambda j: (0, j)), pl.BlockSpec((1, tc), lambda j: (0, j))],
        out_specs=col, out_shape=jax.ShapeDtypeStruct((T, C), F32), compiler_params=_cp("parallel"),
    )(x, w, b)


def _conv_bwd(x, w, b, dact, pad, name):
    T = x.shape[0]
    C = w.shape[1]
    tc = _tile(C, 256, 128)
    nc = T // CHUNK

    def body(x_ref, w_ref, b_ref, da_ref, dx_ref, dw_ref, db_ref, halo_ref):
        halo_ref[...] = jnp.zeros_like(halo_ref)
        dw_ref[...] = jnp.zeros_like(dw_ref)
        db_ref[...] = jnp.zeros_like(db_ref)

        def chunk(i, carry):
            c = nc - 1 - i
            r0 = pl.multiple_of(c * CHUNK, CHUNK)
            y, xe = _conv_pre(x_ref, w_ref, b_ref, c, r0)
            sg = _sigmoid(y)
            dpre = da_ref[pl.ds(r0, CHUNK), :] * (sg * (1.0 + y * (1.0 - sg)))
            db_ref[...] += jnp.sum(dpre, axis=0, keepdims=True)
            de = jnp.concatenate([dpre, halo_ref[...]], axis=0)
            n = CHUNK + CONV_HALO
            dx = jnp.zeros_like(dpre)
            for k in range(CONV_WIDTH):
                shift = CONV_WIDTH - 1 - k
                xs = xe if shift == 0 else pltpu.roll(xe, shift, 0)
                dw_ref[k:k + 1, :] += jnp.sum(dpre * xs[CONV_HALO:], axis=0, keepdims=True)
                ds = de if shift == 0 else pltpu.roll(de, n - shift, 0)
                dx = dx + ds[:CHUNK] * w_ref[k:k + 1, :]
            rows = r0 + _iota((CHUNK, 1), 0)
            dx_ref[pl.ds(r0, CHUNK), :] = jnp.where(rows >= pad, dx, 0.0).astype(BF16)
            halo_ref[...] = dpre[:CONV_HALO]
            return carry

        lax.fori_loop(0, nc, chunk, 0)

    col = pl.BlockSpec((T, tc), lambda j: (0, j))
    wspec = pl.BlockSpec((CONV_WIDTH, tc), lambda j: (0, j))
    bspec = pl.BlockSpec((1, tc), lambda j: (0, j))
    return pl.pallas_call(
        body, name=name, grid=(C // tc,),
        in_specs=[col, wspec, bspec, col], out_specs=(col, wspec, bspec),
        out_shape=(jax.ShapeDtypeStruct((T, C), BF16), jax.ShapeDtypeStruct((CONV_WIDTH, C), F32), jax.ShapeDtypeStruct((1, C), F32)),
        scratch_shapes=[pltpu.VMEM((CONV_HALO, tc), F32)],
        compiler_params=_cp("parallel"),
    )(x, w, b, dact)


def _cumsum_rows(x, reverse=False):
    n = x.shape[0]
    idx = _iota(x.shape, 0)
    k = 1
    while k < n:
        if reverse:
            x = x + jnp.where(idx < n - k, pltpu.roll(x, n - k, 0), 0.0)
        else:
            x = x + jnp.where(idx >= k, pltpu.roll(x, k, 0), 0.0)
        k *= 2
    return x


def _softplus(x):
    return jnp.maximum(x, 0.0) + jnp.log1p(jnp.exp(-jnp.abs(x)))


def _head_selector(n_heads, width):
    lane = jnp.arange(n_heads * width)[None, :] // width
    return (lane == jnp.arange(LANES)[:, None]).astype(BF16)


def _dot_exact(x, sel, parts, mode="nn"):
    acc = None
    for _ in range(parts):
        piece = x.astype(BF16)
        x = x - piece.astype(F32)
        t = lax.dot_general(piece, sel, _DIMS[mode], preferred_element_type=F32)
        acc = t if acc is None else acc + t
    return acc


def _ssd_decays(dt, cs, sel_p_ref, sel_q_ref):
    cs_b = _dot_exact(cs, sel_q_ref[...], 3)
    dt_x = _dot_exact(dt, sel_p_ref[...], 3)
    cs_x = _dot_exact(cs, sel_p_ref[...], 3)
    return cs_b, dt_x, jnp.exp(cs_x), jnp.exp(cs_x[CHUNK - 1:CHUNK, :] - cs_x)


def _ssd_common(c, pad, n_heads, dtr_ref, dtb_ref, al_ref):
    rows = c * CHUNK + _iota((CHUNK, 1), 0)
    valid = rows >= pad
    live = jnp.logical_and(valid, _iota((1, LANES), 1) < n_heads)
    pre = dtr_ref[...] + dtb_ref[...]
    dt = jnp.where(live, _softplus(pre), 0.0)
    a = -jnp.exp(al_ref[...])
    cs = _cumsum_rows(dt * a)
    return valid, live, dt, a, cs, cs.T, _sigmoid(pre)


def _ssd_specs(T, DI, GN, cfirst):
    xcol = DI // GN

    def at(col):
        return lambda c: (cfirst(c), col)

    x = pl.BlockSpec((CHUNK, DI), at(0))
    b = pl.BlockSpec((CHUNK, GN), at(xcol))
    cm = pl.BlockSpec((CHUNK, GN), at(xcol + 1))
    dt = pl.BlockSpec((CHUNK, LANES), at(0))
    vec = pl.BlockSpec((1, LANES), lambda c: (0, 0))
    nw = pl.BlockSpec((1, DI), lambda c: (0, 0))
    return x, b, cm, dt, vec, nw


def _ssd_fwd(xbc, pdt, pz, dt_bias, a_log, d_skip, norm_w, pad, n_heads, name):
    T = xbc.shape[0]
    DI = pz.shape[1]
    P = DI // n_heads
    GN = (xbc.shape[1] - DI) // 2
    G = GN // D_STATE
    R = n_heads // G
    GW = R * P
    nc = T // CHUNK
    Q, N = CHUNK, D_STATE

    def body(x_ref, b_ref, c_ref, dtr_ref, z_ref, dtb_ref, al_ref, dsk_ref, nw_ref, sel_p_ref, sel_q_ref,
             y_ref, yn_ref, prev_ref, s_ref):
        c = pl.program_id(0)

        @pl.when(c == 0)
        def _():
            s_ref[...] = jnp.zeros_like(s_ref)

        valid, _, dt, _, cs, cst, _ = _ssd_common(c, pad, n_heads, dtr_ref, dtb_ref, al_ref)
        cs_b, dt_x, e_x, dec_x = _ssd_decays(dt, cs, sel_p_ref, sel_q_ref)
        e_last = jnp.exp(cs[Q - 1:Q, :])
        tri = _iota((Q, Q), 0) >= _iota((Q, Q), 1)
        state_rows = _iota((GW, 1), 0)
        head_lane = _iota((1, GW), 1)
        for g in range(G):
            gs = slice(g * GW, (g + 1) * GW)
            bg = jnp.where(valid, b_ref[:, g * N:(g + 1) * N], 0.0).astype(BF16)
            cg = jnp.where(valid, c_ref[:, g * N:(g + 1) * N], 0.0).astype(BF16)
            xg = jnp.where(valid, x_ref[:, gs], 0.0)
            sg = s_ref[gs, :]
            prev_ref[0, gs, :] = sg
            cb = _dot(cg, bg, "nt")
            xdt = xg * dt_x[:, gs]
            yg = _dot(cg, sg, "nt") * e_x[:, gs]
            for r in range(R):
                h = g * R + r
                lmat = jnp.exp(jnp.where(tri, cs_b[:, h * Q:(h + 1) * Q] - cst[h:h + 1, :], -jnp.inf))
                in_head = jnp.logical_and(head_lane >= r * P, head_lane < (r + 1) * P)
                yg = yg + _dot(cb * lmat, jnp.where(in_head, xdt, 0.0))
            y_ref[:, gs] = yg
            decay = _by_row_block(state_rows, [e_last[:, g * R + r:g * R + r + 1] for r in range(R)], P)
            s_ref[gs, :] = sg * decay + _dot(xdt * dec_x[:, gs], bg, "tn")
            z = z_ref[:, gs]
            gz = (yg + xg * dsk_ref[:, gs]) * (z * _sigmoid(z))
            rstd = lax.rsqrt(jnp.mean(gz * gz, axis=-1, keepdims=True) + EPS)
            yn_ref[:, gs] = ((gz * rstd) * nw_ref[:, gs]).astype(BF16)

    x_s, b_s, c_s, dt_s, vec, nw = _ssd_specs(T, DI, GN, lambda c: c)
    wide = pl.BlockSpec((Q, DI), lambda c: (c, 0))
    sel_p = pl.BlockSpec((LANES, DI), lambda c: (0, 0))
    sel_q = pl.BlockSpec((LANES, n_heads * Q), lambda c: (0, 0))
    return pl.pallas_call(
        body, name=name, grid=(nc,),
        in_specs=[x_s, b_s, c_s, dt_s, wide, vec, vec, nw, nw, sel_p, sel_q],
        out_specs=(wide, wide, pl.BlockSpec((1, DI, N), lambda c: (c, 0, 0))),
        out_shape=(jax.ShapeDtypeStruct((T, DI), F32), jax.ShapeDtypeStruct((T, DI), BF16), jax.ShapeDtypeStruct((nc, DI, N), F32)),
        scratch_shapes=[pltpu.VMEM((DI, N), F32)],
        compiler_params=_cp("arbitrary"),
    )(xbc, xbc, xbc, pdt, pz, dt_bias, a_log, jnp.repeat(d_skip[:, :n_heads], P, axis=1), norm_w,
      _head_selector(n_heads, P), _head_selector(n_heads, Q))


def _ssd_bwd(xbc, pdt, pz, dt_bias, a_log, d_skip, norm_w, y, prev, dyn, pad, n_heads, name):
    T, W = xbc.shape
    DI = pz.shape[1]
    P = DI // n_heads
    GN = (W - DI) // 2
    G = GN // D_STATE
    R = n_heads // G
    GW = R * P
    nc = T // CHUNK
    Q, N = CHUNK, D_STATE

    def body(x_ref, b_ref, c_ref, dtr_ref, z_ref, dtb_ref, al_ref, dsk_ref, nw_ref, y_ref, prev_ref, next_ref, dyn_ref,
             sel_p_ref, sel_q_ref, sel_pt_ref,
             dxbc_ref, dz_ref, ddt_ref, ddtb_ref, dal_ref, ddsk_ref, dnw_ref, ds_ref):
        i = pl.program_id(0)
        c = nc - 1 - i

        @pl.when(i == 0)
        def _():
            ds_ref[...] = jnp.zeros_like(ds_ref)
            ddtb_ref[...] = jnp.zeros_like(ddtb_ref)
            dal_ref[...] = jnp.zeros_like(dal_ref)
            ddsk_ref[...] = jnp.zeros_like(ddsk_ref)
            dnw_ref[...] = jnp.zeros_like(dnw_ref)

        valid, live, dt, a, cs, cst, sig_pre = _ssd_common(c, pad, n_heads, dtr_ref, dtb_ref, al_ref)
        cs_b, dt_x, e_x, dec_x = _ssd_decays(dt, cs, sel_p_ref, sel_q_ref)
        e_last = jnp.exp(cs[Q - 1:Q, :])
        tri = _iota((Q, Q), 0) >= _iota((Q, Q), 1)
        tri_t = _iota((Q, Q), 0) <= _iota((Q, Q), 1)
        state_rows = _iota((GW, 1), 0)
        head_lane = _iota((1, GW), 1)
        lane = _iota((1, LANES), 1)
        head_rows = _iota((LANES, 1), 0)
        s_dy_cs = jnp.zeros((Q, LANES), F32)
        s_x_bds = jnp.zeros((Q, LANES), F32)
        s_x_dxdt = jnp.zeros((Q, LANES), F32)
        dcs_rows = jnp.zeros((Q, LANES), F32)
        dcs_cols = jnp.zeros((LANES, Q), F32)
        c_end = jnp.zeros((1, LANES), F32)
        dsk_rows = []
        for g in range(G):
            gs = slice(g * GW, (g + 1) * GW)
            bg = jnp.where(valid, b_ref[:, g * N:(g + 1) * N], 0.0).astype(BF16)
            cg = jnp.where(valid, c_ref[:, g * N:(g + 1) * N], 0.0).astype(BF16)
            xg = jnp.where(valid, x_ref[:, gs], 0.0)
            s_prev = prev_ref[0, gs, :]
            dsg = ds_ref[gs, :]
            end = dsg * next_ref[0, gs, :]
            yg = y_ref[:, gs]
            dsk = dsk_ref[:, gs]
            ytot = yg + xg * dsk
            z = z_ref[:, gs]
            sz = _sigmoid(z)
            silu = z * sz
            gz = ytot * silu
            rstd = lax.rsqrt(jnp.mean(gz * gz, axis=-1, keepdims=True) + EPS)
            gn = gz * rstd
            dyn_g = dyn_ref[:, gs]
            dnw_ref[:, gs] += jnp.sum(dyn_g * gn, axis=0, keepdims=True)
            dgn = dyn_g * nw_ref[:, gs]
            dgz = rstd * (dgn - gn * jnp.mean(dgn * gn, axis=-1, keepdims=True))
            dz_ref[:, gs] = (dgz * ytot * (sz * (1.0 + z * (1.0 - sz)))).astype(BF16)
            dy = dgz * silu
            dsk_rows.append(jnp.sum(dy * xg, axis=0, keepdims=True))
            cb = _dot(cg, bg, "nt")
            cb_t = _dot(bg, cg, "nt")
            bds = _dot(bg, dsg, "nt") * dec_x[:, gs]
            csg = _dot(cg, s_prev, "nt")
            xdt = xg * dt_x[:, gs]
            dxdt = bds
            dcb = jnp.zeros((Q, Q), F32)
            dcb_t = jnp.zeros((Q, Q), F32)
            for r in range(R):
                h = g * R + r
                cs_col = cs_b[:, h * Q:(h + 1) * Q]
                cs_row = cst[h:h + 1, :]
                lmat = jnp.exp(jnp.where(tri, cs_col - cs_row, -jnp.inf))
                lmat_t = jnp.exp(jnp.where(tri_t, cs_row - cs_col, -jnp.inf))
                in_head = jnp.logical_and(head_lane >= r * P, head_lane < (r + 1) * P)
                dyr = jnp.where(in_head, dy, 0.0)
                dm = _dot(dyr, xdt, "nt")
                dcb = dcb + dm * lmat
                dcb_t = dcb_t + _dot(xdt, dyr, "nt") * lmat_t
                w_rc = dm * (cb * lmat)
                dcs_rows = jnp.where(lane == h, jnp.sum(w_rc, axis=1, keepdims=True), dcs_rows)
                dcs_cols = jnp.where(head_rows == h, jnp.sum(w_rc, axis=0, keepdims=True), dcs_cols)
                dxdt = dxdt + _dot(cb_t * lmat_t, dyr)
            sel_t = sel_pt_ref[gs, :]
            s_dy_cs = s_dy_cs + _dot_exact(dy * csg, sel_t, 2)
            s_x_bds = s_x_bds + _dot_exact(xg * bds, sel_t, 2)
            s_x_dxdt = s_x_dxdt + _dot_exact(xg * dxdt, sel_t, 2)
            c_end = c_end + jnp.sum(jnp.sum(end, axis=1, keepdims=True) * sel_t.astype(F32), axis=0, keepdims=True)
            dye = dy * e_x[:, gs]
            dc = _dot(dcb, bg) + _dot(dye, s_prev)
            db = _dot(dcb_t, cg) + _dot(xdt * dec_x[:, gs], dsg)
            decay = _by_row_block(state_rows, [e_last[:, g * R + r:g * R + r + 1] for r in range(R)], P)
            ds_ref[gs, :] = dsg * decay + _dot(dye, cg, "tn")
            dxbc_ref[:, gs] = jnp.where(valid, dxdt * dt_x[:, gs] + dy * dsk, 0.0)
            dxbc_ref[:, DI + g * N:DI + (g + 1) * N] = jnp.where(valid, db, 0.0)
            dxbc_ref[:, DI + GN + g * N:DI + GN + (g + 1) * N] = jnp.where(valid, dc, 0.0)
        dcs = s_dy_cs * jnp.exp(cs) - dt * s_x_bds
        da_cs = _cumsum_rows(dcs + (dcs_rows - dcs_cols.T), reverse=True) + c_end
        ddt_all = jnp.where(live, da_cs * a + s_x_dxdt, 0.0)
        ddt_raw = ddt_all * sig_pre
        ddt_ref[...] = ddt_raw.astype(BF16)
        ddtb_ref[...] += jnp.sum(ddt_raw, axis=0, keepdims=True)
        dal_ref[...] += jnp.sum(da_cs * dt, axis=0, keepdims=True) * a
        dsk_all = jnp.broadcast_to(jnp.concatenate(dsk_rows, axis=1), (8, DI))
        ddsk_ref[...] += _dot_exact(dsk_all, sel_pt_ref[...], 3)[0:1]

    rev = lambda i: nc - 1 - i
    x_s, b_s, c_s, dt_s, vec, nw = _ssd_specs(T, DI, GN, rev)
    wide = pl.BlockSpec((Q, DI), lambda i: (rev(i), 0))
    st = pl.BlockSpec((1, DI, N), lambda i: (rev(i), 0, 0))
    st_next = pl.BlockSpec((1, DI, N), lambda i: (jnp.minimum(rev(i) + 1, nc - 1), 0, 0))
    sel_p = pl.BlockSpec((LANES, DI), lambda i: (0, 0))
    sel_q = pl.BlockSpec((LANES, n_heads * Q), lambda i: (0, 0))
    sel_pt = pl.BlockSpec((DI, LANES), lambda i: (0, 0))
    sel = _head_selector(n_heads, P)
    return pl.pallas_call(
        body, name=name, grid=(nc,),
        in_specs=[x_s, b_s, c_s, dt_s, wide, vec, vec, nw, nw, wide, st, st_next, wide, sel_p, sel_q, sel_pt],
        out_specs=(pl.BlockSpec((Q, W), lambda i: (rev(i), 0)), wide, dt_s, vec, vec, vec, nw),
        out_shape=(jax.ShapeDtypeStruct((T, W), F32), jax.ShapeDtypeStruct((T, DI), BF16), jax.ShapeDtypeStruct((T, LANES), BF16),
                   jax.ShapeDtypeStruct((1, LANES), F32), jax.ShapeDtypeStruct((1, LANES), F32),
                   jax.ShapeDtypeStruct((1, LANES), F32), jax.ShapeDtypeStruct((1, DI), F32)),
        scratch_shapes=[pltpu.VMEM((DI, N), F32)],
        compiler_params=_cp("arbitrary"),
    )(xbc, xbc, xbc, pdt, pz, dt_bias, a_log, jnp.repeat(d_skip[:, :n_heads], P, axis=1), norm_w, y, prev, prev, dyn,
      sel, _head_selector(n_heads, Q), sel.T)


def _adamw(w, g, m, v, name):
    rows, cols = w.shape
    tr = _tile(rows, 256, 8)

    def body(w_ref, g_ref, m_ref, v_ref, d_ref, nm_ref, nv_ref):
        g = g_ref[...]
        m = ADAM_B1 * m_ref[...] + (1.0 - ADAM_B1) * g
        v = ADAM_B2 * v_ref[...] + (1.0 - ADAM_B2) * (g * g)
        m_hat = m / (1.0 - ADAM_B1 ** ADAM_STEP)
        v_hat = v / (1.0 - ADAM_B2 ** ADAM_STEP)
        d_ref[...] = -ADAM_LR * (m_hat / (jnp.sqrt(v_hat) + ADAM_EPS) + ADAM_WD * w_ref[...])
        nm_ref[...] = m
        nv_ref[...] = v

    blk = pl.BlockSpec((tr, cols), lambda i: (i, 0))
    out = jax.ShapeDtypeStruct((rows, cols), F32)
    return pl.pallas_call(
        body, name=name, grid=(rows // tr,), in_specs=[blk] * 4, out_specs=(blk,) * 3, out_shape=(out,) * 3,
        compiler_params=_cp("parallel"),
    )(w, g, m, v)


def _sum_stack(parts, name, out_dtype=F32):
    rows, cols = parts[0].shape
    tr = _tile(rows, 512, 16)

    def body(*refs):
        acc = refs[0][...].astype(F32)
        for r in refs[1:-1]:
            acc = acc + r[...].astype(F32)
        refs[-1][...] = acc.astype(out_dtype)

    blk = pl.BlockSpec((tr, cols), lambda i: (i, 0))
    return pl.pallas_call(
        body, name=name, grid=(rows // tr,), in_specs=[blk] * len(parts), out_specs=blk,
        out_shape=jax.ShapeDtypeStruct((rows, cols), out_dtype), compiler_params=_cp("parallel"),
    )(*parts)


def _place():
    return lax.axis_index("x"), lax.axis_index("y"), lax.axis_index("c")


def _other_chips(x, y):
    return [(1 - x, y), (x, 1 - y), (1 - x, 1 - y)]


def _all_gather(shards, name):
    nb = len(shards)

    def body(*refs):
        ins, outs = refs[:nb], refs[nb:2 * nb]
        send_sems, recv_sems, local_sems = refs[2 * nb:]
        x, y, c = _place()
        me, sibling = (x, y, c), (x, y, 1 - c)
        chips = _other_chips(x, y)

        def copy(q, k, block, to, src=None):
            dst = outs[q].at[4 * block[0] + 2 * block[1] + block[2]]
            return pltpu.make_async_remote_copy(
                src_ref=dst if src is None else src, dst_ref=dst,
                send_sem=send_sems.at[7 * q + k], recv_sem=recv_sems.at[7 * q + k], device_id=to, device_id_type=MESH)

        started = []
        for q in range(nb):
            mine = pltpu.make_async_copy(ins[q], outs[q].at[4 * x + 2 * y + c], local_sems.at[q])
            mine.start()
            started.append(mine)
        first = []
        for q in range(nb):
            first.append(copy(q, 0, me, sibling, src=ins[q]))
            first += [copy(q, 1 + j, me, (*chip, c), src=ins[q]) for j, chip in enumerate(chips)]
        for cp in first:
            cp.start()
        passed = []
        for j, chip in enumerate(chips):
            for q in range(nb):
                copy(q, 1 + j, (*chip, c), me).wait_recv()
                fwd = copy(q, 4 + j, (*chip, c), sibling)
                fwd.start()
                passed.append(fwd)
        for q in range(nb):
            copy(q, 0, sibling, me).wait_recv()
            for j, chip in enumerate(chips):
                copy(q, 4 + j, (*chip, 1 - c), me).wait_recv()
        for cp in first + passed:
            cp.wait_send()
        for mine in started:
            mine.wait()

    return pl.pallas_call(
        body, name=name, in_specs=[ANY] * nb, out_specs=tuple([ANY] * nb),
        out_shape=tuple(jax.ShapeDtypeStruct((N_DEV,) + s.shape, s.dtype) for s in shards),
        scratch_shapes=[pltpu.SemaphoreType.DMA((7 * nb,)), pltpu.SemaphoreType.DMA((7 * nb,)), pltpu.SemaphoreType.DMA((nb,))],
    )(*shards)


HBM = pl.BlockSpec(memory_space=pltpu.HBM)
SEM = pl.BlockSpec(memory_space=pltpu.SEMAPHORE)
N_PEERS = N_DEV - 1


def _peer(k, x, y, c):
    return (1 - x if k & 4 else x, 1 - y if k & 2 else y, 1 - c if k & 1 else c)


def _direct_copies(gather, srcs, lands, send_sems, recv_sems):
    x, y, c = _place()
    copies = []
    for q in range(len(srcs)):
        for k in range(1, N_DEV):
            px, py, pc = _peer(k, x, y, c)
            if gather:
                src, dst = srcs[q], lands[q].at[4 * x + 2 * y + c]
            else:
                src, dst = srcs[q].at[4 * px + 2 * py + pc], lands[q].at[k - 1]
            copies.append(pltpu.make_async_remote_copy(
                src_ref=src, dst_ref=dst, send_sem=send_sems.at[N_PEERS * q + k - 1], recv_sem=recv_sems.at[N_PEERS * q + k - 1],
                device_id=(px, py, pc), device_id_type=MESH))
    return copies


def _exchange_start(gather, srcs, lands, name):
    n = len(srcs)

    def body(*refs):
        send_sems, recv_sems = refs[2 * n], refs[2 * n + 1]
        for cp in _direct_copies(gather, refs[:n], refs[n:2 * n], send_sems, recv_sems):
            cp.start()
        refs[-1][...] = jnp.zeros_like(refs[-1])

    held = [pltpu.with_memory_space_constraint(t, pltpu.HBM) for t in list(srcs) + list(lands)]
    out = pl.pallas_call(
        body, name=name,
        out_shape=(pltpu.SemaphoreType.DMA((N_PEERS * n,)), pltpu.SemaphoreType.DMA((N_PEERS * n,)),
                   *[pltpu.HBM(t.shape, t.dtype) for t in held], jax.ShapeDtypeStruct((8, LANES), F32)),
        in_specs=[HBM] * (2 * n), out_specs=(SEM, SEM, *[HBM] * (2 * n), pl.BlockSpec(memory_space=pltpu.VMEM)),
        input_output_aliases={i: 2 + i for i in range(2 * n)},
        compiler_params=pltpu.CompilerParams(has_side_effects=pltpu.SideEffectType.DATAFLOW_SIDE_EFFECTING),
    )(*held)
    return out[0], out[1], list(out[2:2 + 2 * n]), out[-1]


def _exchange_wait(gather, started, after, name):
    send_sems, recv_sems, held, _ = started
    n = len(held) // 2

    def body(*refs):
        for cp in _direct_copies(gather, refs[:n], refs[n:2 * n], refs[2 * n], refs[2 * n + 1]):
            cp.wait_send()
            cp.wait_recv()

    out = pl.pallas_call(
        body, name=name, out_shape=tuple(pltpu.HBM(t.shape, t.dtype) for t in held),
        in_specs=[HBM] * (2 * n) + [SEM, SEM, pl.BlockSpec(memory_space=pl.ANY)], out_specs=tuple([HBM] * (2 * n)),
        input_output_aliases={i: i for i in range(2 * n)},
        compiler_params=pltpu.CompilerParams(has_side_effects=pltpu.SideEffectType.DATAFLOW_SIDE_EFFECTING),
    )(*held, send_sems, recv_sems, after)
    return list(out[n:])


def _own_plus_received(stack, got, name):
    _, rows, cols = stack.shape
    tr = _tile(rows, 512, 16)

    def body(*refs):
        acc = refs[0][0]
        for r in refs[1:-1]:
            acc = acc + r[0].astype(F32)
        refs[-1][...] = acc

    def mine(i):
        x, y, c = _place()
        return (4 * x + 2 * y + c, i, 0)

    return pl.pallas_call(
        body, name=name, grid=(rows // tr,),
        in_specs=[pl.BlockSpec((1, tr, cols), mine)]
        + [pl.BlockSpec((1, tr, cols), functools.partial(lambda k, i: (k, i, 0), k)) for k in range(N_PEERS)],
        out_specs=pl.BlockSpec((tr, cols), lambda i: (i, 0)),
        out_shape=jax.ShapeDtypeStruct((rows, cols), F32), compiler_params=_cp("parallel"),
    )(stack, *[got] * N_PEERS)


class _Shard:
    def __init__(self, name, axis):
        self.name, self.axis = name, axis


BIG = [_Shard("w_in", 2), _Shard("pool_w_group", 2), _Shard("w_pool_up", 1), _Shard("w_ssd_out", 1),
       _Shard("w_o", 1), _Shard("w_ff1", 2), _Shard("w_ff2", 1)]
SMALL_SHARDED = [_Shard("meta_tokens", 1), _Shard("conv_w", 2)]
REPLICATED = ["mix_norm_w", "b_gate", "pool_scale", "conv_b", "dt_bias", "a_log", "d_skip", "ssd_norm_w",
              "mlp_norm_w", "final_norm_w"]
WEIGHTS = ["meta_tokens", "mix_norm_w", "w_in", "b_gate", "pool_w_group", "pool_scale", "w_pool_up", "conv_w", "conv_b",
           "dt_bias", "a_log", "d_skip", "ssd_norm_w", "w_ssd_out", "w_o", "mlp_norm_w", "w_ff1", "w_ff2", "final_norm_w"]


def _rows2(a):
    return a.reshape(-1, a.shape[-1])


def _unshard(stack, shard_shape, axis):
    t = stack.reshape((N_DEV,) + tuple(shard_shape))
    return jnp.concatenate([t[d] for d in range(N_DEV)], axis=axis)


def _reshard(layers, axis):
    cut = [jnp.split(t, N_DEV, axis=axis - 1) for t in layers]
    blocks = [jnp.concatenate([_rows2(pieces[d]) for pieces in cut], axis=0) for d in range(N_DEV)]
    return jnp.stack(blocks).reshape((4, 2) + blocks[0].shape)


def _lane_rows(a):
    n = a.size
    tile = 8 * LANES
    if n % tile:
        return jnp.pad(a.reshape(-1), (0, (-n) % tile)).reshape(-1, LANES)
    return a.reshape(-1, LANES)


def _unpack_small(buf, spans, shapes):
    out = []
    for (o, r), shp in zip(spans, shapes):
        n = 1
        for d in shp:
            n *= d
        t = buf[o:o + r]
        out.append(t.reshape(shp) if n == r * LANES else t.reshape(-1)[:n].reshape(shp))
    return out


def _pack_small(parts, mult):
    mats = [_lane_rows(p) for p in parts]
    spans, o = [], 0
    for t in mats:
        spans.append((o, t.shape[0]))
        o += t.shape[0]
    fill = (-o) % mult
    if fill:
        mats.append(jnp.zeros((fill, LANES), mats[0].dtype))
    return jnp.concatenate(mats, axis=0), spans


def _layer_fwd(h, lw, cfg, tag, dep=None):
    pad, n_heads = cfg["pad"], cfg["n_heads"]
    u = _rms_fwd(h, lw["mix_norm_w"], f"rms_mix_{tag}", dep=dep)
    p_xbc = _mm(u, lw["w_xbc"], "nn", f"proj_xbc_{tag}")
    p_z = _mm(u, lw["w_z"], "nn", f"proj_z_{tag}")
    p_gate = _mm(u, lw["w_gate"], "nn", f"proj_gate_{tag}")
    p_pool = _mm(u, lw["w_pool"], "nn", f"proj_pool_{tag}")
    p_dt = _mm(u, lw["w_dt"], "nn", f"proj_dt_{tag}")
    pooled, y1 = _pool_fwd(p_pool, lw["pool_w_group"], lw["pool_scale"], pad, f"pool_fwd_{tag}")
    y_pool = _mm(y1, lw["w_pool_up"], "nn", f"pool_up_{tag}")
    xbc = _conv_fwd(p_xbc, lw["conv_w"], lw["conv_b"], f"conv_fwd_{tag}")
    y, yn, prev = _ssd_fwd(xbc, p_dt, p_z, lw["dt_bias"], lw["a_log"], lw["d_skip"], lw["ssd_norm_w"], pad, n_heads, f"ssd_fwd_{tag}")
    y_ssd = _mm(yn, lw["w_ssd_out"], "nn", f"ssd_out_{tag}")
    mix = _gate_fwd(p_gate, lw["b_gate"], y_pool, y_ssd, f"gate_fwd_{tag}")
    h_mid = _mm(mix, lw["w_o"], "nn", f"mix_out_{tag}", res=h)
    v = _rms_fwd(h_mid, lw["mlp_norm_w"], f"rms_mlp_{tag}")
    hid, act = _mm(v, lw["w_ff1"], "nn", f"ff1_{tag}", epi="relu2")
    h_out = _mm(act, lw["w_ff2"], "nn", f"ff2_{tag}", res=h_mid)
    saved = dict(h=h, u=u, p_xbc=p_xbc, p_z=p_z, p_gate=p_gate, p_dt=p_dt, pooled=pooled, y1=y1, y_pool=y_pool, xbc=xbc,
                 y=y, yn=yn, prev=prev, y_ssd=y_ssd, mix=mix, h_mid=h_mid, v=v, hid=hid, act=act)
    return h_out, saved


def _layer_bwd(dh, lw, s, cfg, tag, dep=None):
    pad, n_heads = cfg["pad"], cfg["n_heads"]
    g = {}
    dhid = _mm(dh, lw["w_ff2"], "nt", f"d_act_{tag}", epi="drelu2", aux=s["hid"], dep=dep)
    g["w_ff2"] = _mm(s["act"], dh, "tn", f"dw_ff2_{tag}")
    dv = _mm(dhid, lw["w_ff1"], "nt", f"d_v_{tag}")
    g["w_ff1"] = _mm(s["v"], dhid, "tn", f"dw_ff1_{tag}")
    dh_mid, g["mlp_norm_w"] = _rms_bwd(s["h_mid"], lw["mlp_norm_w"], dv, dh, f"rms_mlp_bwd_{tag}")
    dmix = _mm(dh_mid, lw["w_o"], "nt", f"d_mix_{tag}")
    g["w_o"] = _mm(s["mix"], dh_mid, "tn", f"dw_o_{tag}")
    dgate, dy_pool, dy_ssd, g["b_gate"] = _gate_bwd(s["p_gate"], lw["b_gate"], s["y_pool"], s["y_ssd"], dmix, f"gate_bwd_{tag}")
    dy1 = _mm(dy_pool, lw["w_pool_up"], "nt", f"d_y1_{tag}")
    g["w_pool_up"] = _mm(s["y1"], dy_pool, "tn", f"dw_pool_up_{tag}")
    dpool, g["pool_w_group"], g["pool_scale"] = _pool_bwd(s["pooled"], lw["pool_w_group"], lw["pool_scale"], dy1, pad, f"pool_bwd_{tag}")
    dyn = _mm(dy_ssd, lw["w_ssd_out"], "nt", f"d_yn_{tag}")
    g["w_ssd_out"] = _mm(s["yn"], dy_ssd, "tn", f"dw_ssd_out_{tag}")
    dact, dz, ddt, g["dt_bias"], g["a_log"], g["d_skip"], g["ssd_norm_w"] = _ssd_bwd(
        s["xbc"], s["p_dt"], s["p_z"], lw["dt_bias"], lw["a_log"], lw["d_skip"], lw["ssd_norm_w"], s["y"], s["prev"], dyn,
        pad, n_heads, f"ssd_bwd_{tag}")
    dxbc, g["conv_w"], g["conv_b"] = _conv_bwd(s["p_xbc"], lw["conv_w"], lw["conv_b"], dact, pad, f"conv_bwd_{tag}")
    u = s["u"]
    du = _mm(dxbc, lw["w_xbc"], "nt", f"du_xbc_{tag}")
    du = _mm(dz, lw["w_z"], "nt", f"du_z_{tag}", res=du)
    du = _mm(dgate, lw["w_gate"], "nt", f"du_gate_{tag}", res=du)
    du = _mm(dpool, lw["w_pool"], "nt", f"du_pool_{tag}", res=du)
    du = _mm(ddt, lw["w_dt"], "nt", f"du_dt_{tag}", res=du)
    g["w_xbc"] = _mm(u, dxbc, "tn", f"dw_xbc_{tag}")
    g["w_z"] = _mm(u, dz, "tn", f"dw_z_{tag}")
    g["w_gate"] = _mm(u, dgate, "tn", f"dw_gate_{tag}")
    g["w_pool"] = _mm(u, dpool, "tn", f"dw_pool_{tag}")
    g["w_dt"] = _mm(u, ddt, "tn", f"dw_dt_{tag}")
    dh_in, g["mix_norm_w"] = _rms_bwd(s["h"], lw["mix_norm_w"], du, dh_mid, f"rms_mix_bwd_{tag}")
    return dh_in, g


def _pad_lanes(v):
    return jnp.pad(v, (0, LANES - v.shape[0])).reshape(1, LANES)


class _WholeWeights:
    def __init__(self, full):
        self.full = full

    def fwd_begin(self, i):
        return {k: t[i] for k, t in self.full.items()}, None

    def fwd_end(self, i, h_out):
        pass

    def bwd_begin(self, i):
        return None

    def bwd_end(self, i, grads, dh_in):
        pass


def _local_step(x2, target, meta_full, traffic, rep, cfg):
    depth, pad, n_meta, H = cfg["depth"], cfg["pad"], cfg["n_meta"], cfg["n_heads"]
    D = x2.shape[1]
    di = rep["ssd_norm_w"].shape[1]
    c_pool, c_z, c_xbc, c_dt = cfg["cols"]
    h = jnp.concatenate([jnp.zeros((pad, D), F32), meta_full, x2], axis=0)
    lws, saves = [], []
    for i in range(depth):
        full, dep = traffic.fwd_begin(i)
        w_in = full["w_in"]
        o = 0
        w_pool = w_in[:, o:o + c_pool]; o += c_pool
        w_z = w_in[:, o:o + c_z]; o += c_z
        w_xbc = w_in[:, o:o + c_xbc]; o += c_xbc
        w_dt = jnp.pad(w_in[:, o:o + c_dt], ((0, 0), (0, LANES - c_dt))); o += c_dt
        w_gate = w_in[:, o:]
        lw = dict(
            w_pool=w_pool, w_z=w_z, w_xbc=w_xbc, w_dt=w_dt, w_gate=w_gate,
            pool_w_group=full["pool_w_group"], w_pool_up=full["w_pool_up"], w_ssd_out=full["w_ssd_out"],
            w_o=full["w_o"], w_ff1=full["w_ff1"], w_ff2=full["w_ff2"], conv_w=full["conv_w"],
            mix_norm_w=rep["mix_norm_w"][i], b_gate=rep["b_gate"][i], pool_scale=rep["pool_scale"][i].reshape(1, -1),
            conv_b=rep["conv_b"][i].reshape(1, -1), dt_bias=_pad_lanes(rep["dt_bias"][i]), a_log=_pad_lanes(rep["a_log"][i]),
            d_skip=_pad_lanes(rep["d_skip"][i]), ssd_norm_w=rep["ssd_norm_w"][i].reshape(1, di), mlp_norm_w=rep["mlp_norm_w"][i])
        lws.append(lw)
        h, s = _layer_fwd(h, lw, cfg, f"l{i}", dep)
        saves.append(s)
        traffic.fwd_end(i, h)
    loss, dh, g_final = _final_loss(h, rep["final_norm_w"], target, pad + n_meta, "final_loss")
    per_layer = []
    for i in range(depth - 1, -1, -1):
        dh, g = _layer_bwd(dh, lws[i], saves[i], cfg, f"l{i}", traffic.bwd_begin(i))
        g["w_in"] = jnp.concatenate([g["w_pool"], g["w_z"], g["w_xbc"], g["w_dt"][:, :c_dt], g["w_gate"]], axis=1)
        traffic.bwd_end(i, g, dh)
        per_layer.append(g)
    per_layer.reverse()

    def stack(key, fn=lambda t: t):
        return jnp.stack([fn(g[key]) for g in per_layer])

    def layers(key):
        return [g[key] for g in per_layer]

    grads = dict(
        w_in=layers("w_in"), pool_w_group=layers("pool_w_group"), w_pool_up=layers("w_pool_up"), w_ssd_out=layers("w_ssd_out"), w_o=layers("w_o"),
        w_ff1=layers("w_ff1"), w_ff2=layers("w_ff2"), conv_w=stack("conv_w"),
        mix_norm_w=stack("mix_norm_w", lambda t: t[0]), b_gate=stack("b_gate", lambda t: t[0]),
        pool_scale=stack("pool_scale", lambda t: t[0]), conv_b=stack("conv_b", lambda t: t[0]),
        dt_bias=stack("dt_bias", lambda t: t[0, :H]), a_log=stack("a_log", lambda t: t[0, :H]), d_skip=stack("d_skip", lambda t: t[0, :H]),
        ssd_norm_w=stack("ssd_norm_w", lambda t: t[0]), mlp_norm_w=stack("mlp_norm_w", lambda t: t[0]),
        final_norm_w=g_final[0], meta_tokens=dh[pad:pad + n_meta])
    return loss, dh[pad + n_meta:], grads


class _ShardedWeights:
    def __init__(self, w, me):
        self.w, self.me = w, me
        self.depth = w[BIG[0].name].shape[0]
        shards0 = [_rows2(w[s.name][0]).astype(BF16) for s in BIG]
        small = [_rows2(w[s.name]) for s in SMALL_SHARDED]
        landed = _all_gather(shards0 + small, "gather_l0")
        self.small = {s.name: _unshard(t, w[s.name].shape, s.axis) for s, t in zip(SMALL_SHARDED, landed[len(BIG):])}
        self.landed = landed[:len(BIG)]
        self.fetching = None
        self.sending, self.stacks, self.sent_layer = None, None, None
        self.local_grads = [None] * self.depth

    def fwd_begin(self, i):
        full = {s.name: _unshard(t, self.w[s.name].shape[1:], s.axis - 1) for s, t in zip(BIG, self.landed)}
        full["conv_w"] = self.small["conv_w"][i]
        dep = None
        if i + 1 < self.depth:
            shards = [_rows2(self.w[s.name][i + 1]).astype(BF16) for s in BIG]
            lands = [lax.dynamic_update_slice(lax.empty((N_DEV,) + t.shape, t.dtype), t[None], (self.me, 0, 0)) for t in shards]
            self.fetching = _exchange_start(True, shards, lands, f"gather_start_l{i + 1}")
            dep = self.fetching[3]
        return full, dep

    def fwd_end(self, i, h_out):
        if self.fetching is not None:
            self.landed = _exchange_wait(True, self.fetching, h_out, f"gather_wait_l{i + 1}")
            self.fetching = None

    def bwd_begin(self, i):
        return None if self.sending is None else self.sending[3]

    def _collect(self, after):
        j = self.sent_layer
        got = _exchange_wait(False, self.sending, after, f"rs_wait_l{j}")
        self.local_grads[j] = {s.name: _own_plus_received(st, g, f"rs_sum_{s.name}_l{j}")
                               for s, st, g in zip(BIG, self.stacks, got)}
        self.sending = None

    def bwd_end(self, i, grads, dh_in):
        if self.sending is not None:
            self._collect(dh_in)
        self.stacks = [_reshard([grads[s.name]], s.axis).reshape((N_DEV,) + _rows2(self.w[s.name][i]).shape) for s in BIG]
        sends = [t.astype(BF16) for t in self.stacks]
        lands = [lax.empty((N_PEERS,) + t.shape[1:], BF16) for t in sends]
        self.sending = _exchange_start(False, sends, lands, f"rs_start_l{i}")
        self.sent_layer = i

    def finish(self, after):
        self._collect(after)
        return {s.name: jnp.concatenate([g[s.name] for g in self.local_grads], axis=0) for s in BIG}


def kernel(x, meta_tokens, mix_norm_w, w_in, b_gate, pool_w_group, pool_scale, w_pool_up, conv_w, conv_b, dt_bias, a_log, d_skip, ssd_norm_w, w_ssd_out, w_o, mlp_norm_w, w_ff1, w_ff2, final_norm_w, loss_target, m_meta_tokens, m_mix_norm_w, m_w_in, m_b_gate, m_pool_w_group, m_pool_scale, m_w_pool_up, m_conv_w, m_conv_b, m_dt_bias, m_a_log, m_d_skip, m_ssd_norm_w, m_w_ssd_out, m_w_o, m_mlp_norm_w, m_w_ff1, m_w_ff2, m_final_norm_w, v_meta_tokens, v_mix_norm_w, v_w_in, v_b_gate, v_pool_w_group, v_pool_scale, v_w_pool_up, v_conv_w, v_conv_b, v_dt_bias, v_a_log, v_d_skip, v_ssd_norm_w, v_w_ssd_out, v_w_o, v_mlp_norm_w, v_w_ff1, v_w_ff2, v_final_norm_w):
    w = dict(meta_tokens=meta_tokens, mix_norm_w=mix_norm_w, w_in=w_in, b_gate=b_gate, pool_w_group=pool_w_group,
             pool_scale=pool_scale, w_pool_up=w_pool_up, conv_w=conv_w, conv_b=conv_b, dt_bias=dt_bias, a_log=a_log,
             d_skip=d_skip, ssd_norm_w=ssd_norm_w, w_ssd_out=w_ssd_out, w_o=w_o, mlp_norm_w=mlp_norm_w, w_ff1=w_ff1,
             w_ff2=w_ff2, final_norm_w=final_norm_w)
    m = dict(meta_tokens=m_meta_tokens, mix_norm_w=m_mix_norm_w, w_in=m_w_in, b_gate=m_b_gate, pool_w_group=m_pool_w_group,
             pool_scale=m_pool_scale, w_pool_up=m_w_pool_up, conv_w=m_conv_w, conv_b=m_conv_b, dt_bias=m_dt_bias, a_log=m_a_log,
             d_skip=m_d_skip, ssd_norm_w=m_ssd_norm_w, w_ssd_out=m_w_ssd_out, w_o=m_w_o, mlp_norm_w=m_mlp_norm_w, w_ff1=m_w_ff1,
             w_ff2=m_w_ff2, final_norm_w=m_final_norm_w)
    v = dict(meta_tokens=v_meta_tokens, mix_norm_w=v_mix_norm_w, w_in=v_w_in, b_gate=v_b_gate, pool_w_group=v_pool_w_group,
             pool_scale=v_pool_scale, w_pool_up=v_w_pool_up, conv_w=v_conv_w, conv_b=v_conv_b, dt_bias=v_dt_bias, a_log=v_a_log,
             d_skip=v_d_skip, ssd_norm_w=v_ssd_norm_w, w_ssd_out=v_w_ssd_out, w_o=v_w_o, mlp_norm_w=v_mlp_norm_w, w_ff1=v_w_ff1,
             w_ff2=v_w_ff2, final_norm_w=v_final_norm_w)

    _, seq, D = x.shape
    n_meta = meta_tokens.shape[0]
    depth = w_in.shape[0]
    n_heads = dt_bias.shape[1]
    d_inner = ssd_norm_w.shape[1]
    d_xbc = conv_b.shape[1]
    pool_width = pool_scale.shape[1]
    pad = (-n_meta) % CHUNK
    cfg = dict(depth=depth, pad=pad, n_meta=n_meta, n_heads=n_heads, cols=(pool_width, d_inner, d_xbc, n_heads))
    assert (pad + n_meta + seq) % CHUNK == 0 and pad + n_meta == CHUNK

    xi, yi, ci = _place()
    me = 4 * xi + 2 * yi + ci

    traffic = _ShardedWeights(w, me)
    rep = {k: w[k] for k in REPLICATED}
    loss_part, dx, grads = _local_step(x[0], loss_target[0], traffic.small["meta_tokens"], traffic, rep, cfg)
    loss = lax.psum(loss_part[0, 0], ("x", "y", "c"))
    g_loc = {k: t.reshape(w[k].shape) for k, t in traffic.finish(dx).items()}

    small_names = REPLICATED + [s.name for s in SMALL_SHARDED]
    sm_buf, sm_spans = _pack_small([grads[k] for k in small_names], 16)
    (sm_all,) = _all_gather([sm_buf], "gather_small_grads")
    sm_sum = _sum_stack([sm_all[d] for d in range(N_DEV)], "small_grads_sum")
    g_small = dict(zip(small_names, _unpack_small(sm_sum, sm_spans, [grads[k].shape for k in small_names])))
    for k in REPLICATED:
        g_loc[k] = g_small[k]
    for s in SMALL_SHARDED:
        blk = w[s.name].shape[s.axis]
        g_loc[s.name] = lax.dynamic_slice_in_dim(g_small[s.name], me * blk, blk, axis=s.axis)

    delta, new_m, new_v = {}, {}, {}
    for s in BIG:
        shp = w[s.name].shape
        d2, m2, v2 = _adamw(_rows2(w[s.name]), _rows2(g_loc[s.name]), _rows2(m[s.name]), _rows2(v[s.name]), f"adamw_{s.name}")
        delta[s.name], new_m[s.name], new_v[s.name] = d2.reshape(shp), m2.reshape(shp), v2.reshape(shp)
    loc_shapes = [w[k].shape for k in small_names]
    packed = [_pack_small([t[k] for k in small_names], 8) for t in (w, g_loc, m, v)]
    loc_spans = packed[0][1]
    outs = _adamw(*[p[0] for p in packed], "adamw_small")
    for res, buf in zip((delta, new_m, new_v), outs):
        res.update(zip(small_names, _unpack_small(buf, loc_spans, loc_shapes)))

    return (loss, dx[None], *[g_loc[k] for k in WEIGHTS], *[delta[k] for k in WEIGHTS],
            *[new_m[k] for k in WEIGHTS], *[new_v[k] for k in WEIGHTS])
```

```python
import functools

import jax
import jax.numpy as jnp
from jax import lax
from jax.experimental import pallas as pl
from jax.experimental.pallas import tpu as pltpu

F32 = jnp.float32
BF16 = jnp.bfloat16

EPS = 1e-5
D_STATE = 128
CHUNK = 128
LANES = 128
POOL_WINDOWS = (2, 4, 8, 16)
POOL_HALO = 16
CONV_WIDTH = 4
CONV_HALO = 8
N_DEV = 8
ADAM_LR = 0.001
ADAM_B1 = 0.9
ADAM_B2 = 0.999
ADAM_EPS = 1e-08
ADAM_WD = 0.01
ADAM_STEP = 10
VMEM_LIMIT = 52 * 1024 * 1024
MESH = pl.DeviceIdType.MESH
ANY = pl.BlockSpec(memory_space=pl.ANY)


def _cp(*sem):
    return pltpu.CompilerParams(dimension_semantics=sem, vmem_limit_bytes=VMEM_LIMIT)


def _tile(n, target, mult):
    best = None
    for t in range(mult, min(n, target) + 1, mult):
        if n % t == 0:
            best = t
    return best if best is not None else n


def _sigmoid(x):
    return jax.nn.sigmoid(x)


def _iota(shape, dim):
    return lax.broadcasted_iota(jnp.int32, shape, dim)


_DIMS = {"nn": (((1,), (0,)), ((), ())), "nt": (((1,), (1,)), ((), ())), "tn": (((0,), (0,)), ((), ()))}


def _dot(a, b, mode="nn"):
    return lax.dot_general(a.astype(BF16), b.astype(BF16), _DIMS[mode], preferred_element_type=F32)


DEP = pl.BlockSpec((8, LANES), lambda *_: (0, 0))


def _mm(a, b, mode, name, *, out_dtype=F32, res=None, epi=None, aux=None, dep=None):
    if mode == "nn":
        (M, K), (_, N) = a.shape, b.shape
    elif mode == "nt":
        (M, K), (N, _) = a.shape, b.shape
    else:
        (K, M), (_, N) = a.shape, b.shape
    if mode == "tn":
        tm, tn, tk = _tile(M, 512, 128), _tile(N, 1024, 128), _tile(K, 1056, 16)
    else:
        tm, tn, tk = _tile(M, 1056, 16), _tile(N, 512, 128), _tile(K, 4096, 128)
    nk = K // tk
    a_spec = pl.BlockSpec((tk, tm), lambda i, j, k: (k, i)) if mode == "tn" else pl.BlockSpec((tm, tk), lambda i, j, k: (i, k))
    b_spec = pl.BlockSpec((tn, tk), lambda i, j, k: (j, k)) if mode == "nt" else pl.BlockSpec((tk, tn), lambda i, j, k: (k, j))
    o_spec = pl.BlockSpec((tm, tn), lambda i, j, k: (i, j))
    extra = [t for t in (res, aux) if t is not None]
    deps = [] if dep is None else [dep]

    def body(*refs):
        a_ref, b_ref = refs[0], refs[1]
        x_ref = refs[2] if extra else None
        outs = refs[2 + len(extra) + len(deps):]
        p = _dot(a_ref[...], b_ref[...], mode)

        def finish(r):
            if res is not None:
                outs[0][...] = (x_ref[...] + r).astype(out_dtype)
            elif epi == "relu2":
                outs[0][...] = r
                hid = jnp.maximum(r, 0.0)
                outs[1][...] = (hid * hid).astype(BF16)
            elif epi == "drelu2":
                outs[0][...] = (r * (2.0 * jnp.maximum(x_ref[...], 0.0))).astype(BF16)
            else:
                outs[0][...] = r.astype(out_dtype)

        if nk == 1:
            finish(p)
        else:
            acc = outs[-1]
            k = pl.program_id(2)

            @pl.when(k == 0)
            def _():
                acc[...] = p

            @pl.when(k > 0)
            def _():
                acc[...] += p

            @pl.when(k == nk - 1)
            def _():
                finish(acc[...])

    if epi == "relu2":
        out_shape = (jax.ShapeDtypeStruct((M, N), F32), jax.ShapeDtypeStruct((M, N), BF16))
        out_specs = (o_spec, o_spec)
    elif epi == "drelu2":
        out_shape, out_specs = jax.ShapeDtypeStruct((M, N), BF16), o_spec
    else:
        out_shape, out_specs = jax.ShapeDtypeStruct((M, N), out_dtype), o_spec
    return pl.pallas_call(
        body, name=name, grid=(M // tm, N // tn, nk),
        in_specs=[a_spec, b_spec] + [o_spec] * len(extra) + [DEP] * len(deps),
        out_specs=out_specs, out_shape=out_shape,
        scratch_shapes=[pltpu.VMEM((tm, tn), F32)] if nk > 1 else [],
        compiler_params=_cp("parallel", "parallel", "arbitrary"),
    )(a, b, *extra, *deps)


def _rms_fwd(h, w, name, dep=None):
    T, D = h.shape
    tr = _tile(T, 1056, 16)
    deps = [] if dep is None else [dep]

    def body(h_ref, w_ref, *rest):
        x = h_ref[...]
        xn = x * lax.rsqrt(jnp.mean(x * x, axis=-1, keepdims=True) + EPS)
        rest[-1][...] = (xn * w_ref[...]).astype(BF16)

    return pl.pallas_call(
        body, name=name, grid=(T // tr,),
        in_specs=[pl.BlockSpec((tr, D), lambda i: (i, 0)), pl.BlockSpec((1, D), lambda i: (0, 0))] + [DEP] * len(deps),
        out_specs=pl.BlockSpec((tr, D), lambda i: (i, 0)), out_shape=jax.ShapeDtypeStruct((T, D), BF16),
        compiler_params=_cp("parallel"),
    )(h, w.reshape(1, D), *deps)


def _rms_bwd(h, w, dy, dres, name):
    T, D = h.shape
    tr = _tile(T, 528, 8)

    def body(h_ref, w_ref, dy_ref, dres_ref, dh_ref, dw_ref):
        x = h_ref[...]
        rstd = lax.rsqrt(jnp.mean(x * x, axis=-1, keepdims=True) + EPS)
        xn = x * rstd
        dy = dy_ref[...]
        dxn = dy * w_ref[...]
        dh_ref[...] = dres_ref[...] + rstd * (dxn - xn * jnp.mean(dxn * xn, axis=-1, keepdims=True))
        dw = jnp.sum(dy * xn, axis=0, keepdims=True)

        @pl.when(pl.program_id(0) == 0)
        def _():
            dw_ref[...] = dw

        @pl.when(pl.program_id(0) > 0)
        def _():
            dw_ref[...] += dw

    row = pl.BlockSpec((tr, D), lambda i: (i, 0))
    vec = pl.BlockSpec((1, D), lambda i: (0, 0))
    return pl.pallas_call(
        body, name=name, grid=(T // tr,),
        in_specs=[row, vec, row, row], out_specs=(row, vec),
        out_shape=(jax.ShapeDtypeStruct((T, D), F32), jax.ShapeDtypeStruct((1, D), F32)),
        compiler_params=_cp("arbitrary"),
    )(h, w.reshape(1, D), dy, dres)


def _final_loss(h, w, target, first_row, name):
    T, D = h.shape
    tr = CHUNK
    assert first_row == tr

    def body(h_ref, w_ref, t_ref, loss_ref, dh_ref, dw_ref):
        i = pl.program_id(0)
        x = h_ref[...]
        rstd = lax.rsqrt(jnp.mean(x * x, axis=-1, keepdims=True) + EPS)
        xn = x * rstd
        w = w_ref[...]
        live = i > 0
        err = jnp.where(live, xn * w - t_ref[...], 0.0)
        part = 0.5 * jnp.sum(jnp.mean(err * err, axis=-1, keepdims=True), axis=0, keepdims=True)
        dout = err * (1.0 / D)
        dxn = dout * w
        dh_ref[...] = rstd * (dxn - xn * jnp.mean(dxn * xn, axis=-1, keepdims=True))
        dw = jnp.sum(dout * xn, axis=0, keepdims=True)

        @pl.when(i == 0)
        def _():
            loss_ref[...] = part
            dw_ref[...] = dw

        @pl.when(i > 0)
        def _():
            loss_ref[...] += part
            dw_ref[...] += dw

    row = pl.BlockSpec((tr, D), lambda i: (i, 0))
    vec = pl.BlockSpec((1, D), lambda i: (0, 0))
    return pl.pallas_call(
        body, name=name, grid=(T // tr,),
        in_specs=[row, vec, pl.BlockSpec((tr, D), lambda i: (jnp.maximum(i - 1, 0), 0))],
        out_specs=(pl.BlockSpec((1, 1), lambda i: (0, 0)), row, vec),
        out_shape=(jax.ShapeDtypeStruct((1, 1), F32), jax.ShapeDtypeStruct((T, D), F32), jax.ShapeDtypeStruct((1, D), F32)),
        compiler_params=_cp("arbitrary"),
    )(h, w.reshape(1, D), target)


def _gate_fwd(pg, bg, y_pool, y_ssd, name):
    T, D = y_pool.shape
    tr = _tile(T, 528, 16)

    def body(gp_ref, gs_ref, bp_ref, bs_ref, yp_ref, ys_ref, o_ref):
        gp = _sigmoid(gp_ref[...] + bp_ref[...])
        gs = _sigmoid(gs_ref[...] + bs_ref[...])
        o_ref[...] = (gp * yp_ref[...] + gs * ys_ref[...]).astype(BF16)

    row = pl.BlockSpec((tr, D), lambda i: (i, 0))
    row1 = pl.BlockSpec((tr, D), lambda i: (i, 1))
    vec = pl.BlockSpec((1, D), lambda i: (0, 0))
    vec1 = pl.BlockSpec((1, D), lambda i: (0, 1))
    b2 = bg.reshape(1, 2 * D)
    return pl.pallas_call(
        body, name=name, grid=(T // tr,),
        in_specs=[row, row1, vec, vec1, row, row], out_specs=row,
        out_shape=jax.ShapeDtypeStruct((T, D), BF16), compiler_params=_cp("parallel"),
    )(pg, pg, b2, b2, y_pool, y_ssd)


def _gate_bwd(pg, bg, y_pool, y_ssd, dmix, name):
    T, D = y_pool.shape
    tr = _tile(T, 528, 16)

    def body(gp_ref, gs_ref, bp_ref, bs_ref, yp_ref, ys_ref, dm_ref, dg_ref, dyp_ref, dys_ref, db_ref):
        gp = _sigmoid(gp_ref[...] + bp_ref[...])
        gs = _sigmoid(gs_ref[...] + bs_ref[...])
        dm = dm_ref[...]
        dyp_ref[...] = (dm * gp).astype(BF16)
        dys_ref[...] = (dm * gs).astype(BF16)
        dgp = dm * yp_ref[...] * gp * (1.0 - gp)
        dgs = dm * ys_ref[...] * gs * (1.0 - gs)
        dg_ref[:, :D] = dgp.astype(BF16)
        dg_ref[:, D:] = dgs.astype(BF16)
        db = jnp.concatenate([jnp.sum(dgp, axis=0, keepdims=True), jnp.sum(dgs, axis=0, keepdims=True)], axis=1)

        @pl.when(pl.program_id(0) == 0)
        def _():
            db_ref[...] = db

        @pl.when(pl.program_id(0) > 0)
        def _():
            db_ref[...] += db

    row = pl.BlockSpec((tr, D), lambda i: (i, 0))
    row1 = pl.BlockSpec((tr, D), lambda i: (i, 1))
    wide = pl.BlockSpec((tr, 2 * D), lambda i: (i, 0))
    vec = pl.BlockSpec((1, D), lambda i: (0, 0))
    vec1 = pl.BlockSpec((1, D), lambda i: (0, 1))
    vec2 = pl.BlockSpec((1, 2 * D), lambda i: (0, 0))
    b2 = bg.reshape(1, 2 * D)
    return pl.pallas_call(
        body, name=name, grid=(T // tr,),
        in_specs=[row, row1, vec, vec1, row, row, row], out_specs=(wide, row, row, vec2),
        out_shape=(jax.ShapeDtypeStruct((T, 2 * D), BF16), jax.ShapeDtypeStruct((T, D), BF16),
                   jax.ShapeDtypeStruct((T, D), BF16), jax.ShapeDtypeStruct((1, 2 * D), F32)),
        compiler_params=_cp("arbitrary"),
    )(pg, pg, b2, b2, y_pool, y_ssd, dmix)


def _pool_count(c, pad, window):
    pos = c * CHUNK + _iota((CHUNK, 1), 0) - pad
    return jnp.clip(pos + 1, 1, window).astype(F32)


def _by_group(g, vals):
    out = vals[-1]
    for k in range(len(vals) - 2, -1, -1):
        out = jnp.where(g == k, vals[k], out)
    return out


def _by_row_block(rows, vals, block):
    out = vals[0]
    for r in range(1, len(vals)):
        out = jnp.where(rows >= r * block, vals[r], out)
    return out


def _pool_fwd(u, wg, scale, pad, name):
    T, C = u.shape
    G, Cg, _ = wg.shape
    nc = T // CHUNK

    def body(u_ref, wg_ref, s_ref, p_ref, y_ref):
        g = pl.program_id(0)
        window = _by_group(g, POOL_WINDOWS)

        def chunk(c, carry):
            r0 = pl.multiple_of(c * CHUNK, CHUNK)
            h0 = pl.multiple_of(jnp.maximum(r0 - POOL_HALO, 0), 8)
            halo = jnp.where(c > 0, u_ref[pl.ds(h0, POOL_HALO), :], 0.0)
            xc = u_ref[pl.ds(r0, CHUNK), :]
            s = jnp.concatenate([halo, xc], axis=0)
            sums = []
            k = 1
            while k < POOL_WINDOWS[-1]:
                s = s + pltpu.roll(s, k, 0)
                k *= 2
                if k in POOL_WINDOWS:
                    sums.append(s[POOL_HALO:])
            wsum = _by_group(g, sums)
            pooled = wsum / _pool_count(c, pad, window) - xc
            pb = pooled.astype(BF16)
            p_ref[pl.ds(r0, CHUNK), :] = pb
            y_ref[pl.ds(r0, CHUNK), :] = (_dot(pb, wg_ref[0]) * s_ref[...]).astype(BF16)
            return carry

        lax.fori_loop(0, nc, chunk, 0)

    col = pl.BlockSpec((T, Cg), lambda g: (0, g))
    return pl.pallas_call(
        body, name=name, grid=(G,),
        in_specs=[col, pl.BlockSpec((1, Cg, Cg), lambda g: (g, 0, 0)), pl.BlockSpec((1, Cg), lambda g: (0, g))],
        out_specs=(col, col),
        out_shape=(jax.ShapeDtypeStruct((T, C), BF16), jax.ShapeDtypeStruct((T, C), BF16)),
        compiler_params=_cp("parallel"),
    )(u, wg, scale)


def _pool_bwd(pooled, wg, scale, dy, pad, name):
    T, C = dy.shape
    G, Cg, _ = wg.shape
    nc = T // CHUNK

    def body(p_ref, wg_ref, s_ref, dy_ref, du_ref, dwg_ref, ds_ref, halo_ref):
        g = pl.program_id(0)
        window = _by_group(g, POOL_WINDOWS)
        halo_ref[...] = jnp.zeros_like(halo_ref)
        dwg_ref[...] = jnp.zeros_like(dwg_ref)
        ds_ref[...] = jnp.zeros_like(ds_ref)

        def chunk(i, carry):
            c = nc - 1 - i
            r0 = pl.multiple_of(c * CHUNK, CHUNK)
            pb = p_ref[pl.ds(r0, CHUNK), :]
            dyc = dy_ref[pl.ds(r0, CHUNK), :]
            w = wg_ref[0]
            ypre = _dot(pb, w)
            ds_ref[...] += jnp.sum(dyc * ypre, axis=0, keepdims=True)
            dyp = (dyc * s_ref[...]).astype(BF16)
            dwg_ref[0] += _dot(pb, dyp, "tn")
            dpool = _dot(dyp, w, "nt")
            q = dpool / _pool_count(c, pad, window)
            s = jnp.concatenate([q, halo_ref[...]], axis=0)
            n = CHUNK + POOL_HALO
            sums = []
            k = 1
            while k < POOL_WINDOWS[-1]:
                s = s + pltpu.roll(s, n - k, 0)
                k *= 2
                if k in POOL_WINDOWS:
                    sums.append(s[:CHUNK])
            du = _by_group(g, sums) - dpool
            rows = r0 + _iota((CHUNK, 1), 0)
            du_ref[pl.ds(r0, CHUNK), :] = jnp.where(rows >= pad, du, 0.0).astype(BF16)
            halo_ref[...] = q[:POOL_HALO]
            return carry

        lax.fori_loop(0, nc, chunk, 0)

    col = pl.BlockSpec((T, Cg), lambda g: (0, g))
    return pl.pallas_call(
        body, name=name, grid=(G,),
        in_specs=[col, pl.BlockSpec((1, Cg, Cg), lambda g: (g, 0, 0)), pl.BlockSpec((1, Cg), lambda g: (0, g)), col],
        out_specs=(col, pl.BlockSpec((1, Cg, Cg), lambda g: (g, 0, 0)), pl.BlockSpec((1, Cg), lambda g: (0, g))),
        out_shape=(jax.ShapeDtypeStruct((T, C), BF16), jax.ShapeDtypeStruct((G, Cg, Cg), F32), jax.ShapeDtypeStruct((1, C), F32)),
        scratch_shapes=[pltpu.VMEM((POOL_HALO, Cg), F32)],
        compiler_params=_cp("parallel"),
    )(pooled, wg, scale, dy)


def _conv_pre(x_ref, w_ref, b_ref, c, r0):
    h0 = pl.multiple_of(jnp.maximum(r0 - CONV_HALO, 0), 8)
    halo = jnp.where(c > 0, x_ref[pl.ds(h0, CONV_HALO), :], 0.0)
    xe = jnp.concatenate([halo, x_ref[pl.ds(r0, CHUNK), :]], axis=0)
    y = jnp.broadcast_to(b_ref[...], (CHUNK, xe.shape[1]))
    for k in range(CONV_WIDTH):
        shift = CONV_WIDTH - 1 - k
        xs = xe if shift == 0 else pltpu.roll(xe, shift, 0)
        y = y + xs[CONV_HALO:] * w_ref[k:k + 1, :]
    return y, xe


def _conv_fwd(x, w, b, name):
    T = x.shape[0]
    C = w.shape[1]
    tc = _tile(C, 256, 128)
    nc = T // CHUNK

    def body(x_ref, w_ref, b_ref, o_ref):
        def chunk(c, carry):
            r0 = pl.multiple_of(c * CHUNK, CHUNK)
            y, _ = _conv_pre(x_ref, w_ref, b_ref, c, r0)
            o_ref[pl.ds(r0, CHUNK), :] = y * _sigmoid(y)
            return carry

        lax.fori_loop(0, nc, chunk, 0)

    col = pl.BlockSpec((T, tc), lambda j: (0, j))
    return pl.pallas_call(
        body, name=name, grid=(C // tc,),
        in_specs=[col, pl.BlockSpec((CONV_WIDTH, tc), lambda j: (0, j)), pl.BlockSpec((1, tc), lambda j: (0, j))],
        out_specs=col, out_shape=jax.ShapeDtypeStruct((T, C), F32), compiler_params=_cp("parallel"),
    )(x, w, b)


def _conv_bwd(x, w, b, dact, pad, name):
    T = x.shape[0]
    C = w.shape[1]
    tc = _tile(C, 256, 128)
    nc = T // CHUNK

    def body(x_ref, w_ref, b_ref, da_ref, dx_ref, dw_ref, db_ref, halo_ref):
        halo_ref[...] = jnp.zeros_like(halo_ref)
        dw_ref[...] = jnp.zeros_like(dw_ref)
        db_ref[...] = jnp.zeros_like(db_ref)

        def chunk(i, carry):
            c = nc - 1 - i
            r0 = pl.multiple_of(c * CHUNK, CHUNK)
            y, xe = _conv_pre(x_ref, w_ref, b_ref, c, r0)
            sg = _sigmoid(y)
            dpre = da_ref[pl.ds(r0, CHUNK), :] * (sg * (1.0 + y * (1.0 - sg)))
            db_ref[...] += jnp.sum(dpre, axis=0, keepdims=True)
            de = jnp.concatenate([dpre, halo_ref[...]], axis=0)
            n = CHUNK + CONV_HALO
            dx = jnp.zeros_like(dpre)
            for k in range(CONV_WIDTH):
                shift = CONV_WIDTH - 1 - k
                xs = xe if shift == 0 else pltpu.roll(xe, shift, 0)
                dw_ref[k:k + 1, :] += jnp.sum(dpre * xs[CONV_HALO:], axis=0, keepdims=True)
                ds = de if shift == 0 else pltpu.roll(de, n - shift, 0)
                dx = dx + ds[:CHUNK] * w_ref[k:k + 1, :]
            rows = r0 + _iota((CHUNK, 1), 0)
            dx_ref[pl.ds(r0, CHUNK), :] = jnp.where(rows >= pad, dx, 0.0).astype(BF16)
            halo_ref[...] = dpre[:CONV_HALO]
            return carry

        lax.fori_loop(0, nc, chunk, 0)

    col = pl.BlockSpec((T, tc), lambda j: (0, j))
    wspec = pl.BlockSpec((CONV_WIDTH, tc), lambda j: (0, j))
    bspec = pl.BlockSpec((1, tc), lambda j: (0, j))
    return pl.pallas_call(
        body, name=name, grid=(C // tc,),
        in_specs=[col, wspec, bspec, col], out_specs=(col, wspec, bspec),
        out_shape=(jax.ShapeDtypeStruct((T, C), BF16), jax.ShapeDtypeStruct((CONV_WIDTH, C), F32), jax.ShapeDtypeStruct((1, C), F32)),
        scratch_shapes=[pltpu.VMEM((CONV_HALO, tc), F32)],
        compiler_params=_cp("parallel"),
    )(x, w, b, dact)


def _cumsum_rows(x, reverse=False):
    n = x.shape[0]
    idx = _iota(x.shape, 0)
    k = 1
    while k < n:
        if reverse:
            x = x + jnp.where(idx < n - k, pltpu.roll(x, n - k, 0), 0.0)
        else:
            x = x + jnp.where(idx >= k, pltpu.roll(x, k, 0), 0.0)
        k *= 2
    return x


def _softplus(x):
    return jnp.maximum(x, 0.0) + jnp.log1p(jnp.exp(-jnp.abs(x)))


def _head_selector(n_heads, width):
    lane = jnp.arange(n_heads * width)[None, :] // width
    return (lane == jnp.arange(LANES)[:, None]).astype(BF16)


def _dot_exact(x, sel, parts, mode="nn"):
    acc = None
    for _ in range(parts):
        piece = x.astype(BF16)
        x = x - piece.astype(F32)
        t = lax.dot_general(piece, sel, _DIMS[mode], preferred_element_type=F32)
        acc = t if acc is None else acc + t
    return acc


def _ssd_decays(dt, cs, sel_p_ref, sel_q_ref):
    cs_b = _dot_exact(cs, sel_q_ref[...], 3)
    dt_x = _dot_exact(dt, sel_p_ref[...], 3)
    cs_x = _dot_exact(cs, sel_p_ref[...], 3)
    return cs_b, dt_x, jnp.exp(cs_x), jnp.exp(cs_x[CHUNK - 1:CHUNK, :] - cs_x)


def _ssd_common(c, pad, n_heads, dtr_ref, dtb_ref, al_ref):
    rows = c * CHUNK + _iota((CHUNK, 1), 0)
    valid = rows >= pad
    live = jnp.logical_and(valid, _iota((1, LANES), 1) < n_heads)
    pre = dtr_ref[...] + dtb_ref[...]
    dt = jnp.where(live, _softplus(pre), 0.0)
    a = -jnp.exp(al_ref[...])
    cs = _cumsum_rows(dt * a)
    return valid, live, dt, a, cs, cs.T, _sigmoid(pre)


def _ssd_specs(T, DI, GN, cfirst):
    xcol = DI // GN

    def at(col):
        return lambda c: (cfirst(c), col)

    x = pl.BlockSpec((CHUNK, DI), at(0))
    b = pl.BlockSpec((CHUNK, GN), at(xcol))
    cm = pl.BlockSpec((CHUNK, GN), at(xcol + 1))
    dt = pl.BlockSpec((CHUNK, LANES), at(0))
    vec = pl.BlockSpec((1, LANES), lambda c: (0, 0))
    nw = pl.BlockSpec((1, DI), lambda c: (0, 0))
    return x, b, cm, dt, vec, nw


def _ssd_fwd(xbc, pdt, pz, dt_bias, a_log, d_skip, norm_w, pad, n_heads, name):
    T = xbc.shape[0]
    DI = pz.shape[1]
    P = DI // n_heads
    GN = (xbc.shape[1] - DI) // 2
    G = GN // D_STATE
    R = n_heads // G
    GW = R * P
    nc = T // CHUNK
    Q, N = CHUNK, D_STATE

    def body(x_ref, b_ref, c_ref, dtr_ref, z_ref, dtb_ref, al_ref, dsk_ref, nw_ref, sel_p_ref, sel_q_ref,
             y_ref, yn_ref, prev_ref, s_ref):
        c = pl.program_id(0)

        @pl.when(c == 0)
        def _():
            s_ref[...] = jnp.zeros_like(s_ref)

        valid, _, dt, _, cs, cst, _ = _ssd_common(c, pad, n_heads, dtr_ref, dtb_ref, al_ref)
        cs_b, dt_x, e_x, dec_x = _ssd_decays(dt, cs, sel_p_ref, sel_q_ref)
        e_last = jnp.exp(cs[Q - 1:Q, :])
        tri = _iota((Q, Q), 0) >= _iota((Q, Q), 1)
        state_rows = _iota((GW, 1), 0)
        head_lane = _iota((1, GW), 1)
        for g in range(G):
            gs = slice(g * GW, (g + 1) * GW)
            bg = jnp.where(valid, b_ref[:, g * N:(g + 1) * N], 0.0).astype(BF16)
            cg = jnp.where(valid, c_ref[:, g * N:(g + 1) * N], 0.0).astype(BF16)
            xg = jnp.where(valid, x_ref[:, gs], 0.0)
            sg = s_ref[gs, :]
            prev_ref[0, gs, :] = sg
            cb = _dot(cg, bg, "nt")
            xdt = xg * dt_x[:, gs]
            yg = _dot(cg, sg, "nt") * e_x[:, gs]
            for r in range(R):
                h = g * R + r
                lmat = jnp.exp(jnp.where(tri, cs_b[:, h * Q:(h + 1) * Q] - cst[h:h + 1, :], -jnp.inf))
                in_head = jnp.logical_and(head_lane >= r * P, head_lane < (r + 1) * P)
                yg = yg + _dot(cb * lmat, jnp.where(in_head, xdt, 0.0))
            y_ref[:, gs] = yg
            decay = _by_row_block(state_rows, [e_last[:, g * R + r:g * R + r + 1] for r in range(R)], P)
            s_ref[gs, :] = sg * decay + _dot(xdt * dec_x[:, gs], bg, "tn")
            z = z_ref[:, gs]
            gz = (yg + xg * dsk_ref[:, gs]) * (z * _sigmoid(z))
            rstd = lax.rsqrt(jnp.mean(gz * gz, axis=-1, keepdims=True) + EPS)
            yn_ref[:, gs] = ((gz * rstd) * nw_ref[:, gs]).astype(BF16)

    x_s, b_s, c_s, dt_s, vec, nw = _ssd_specs(T, DI, GN, lambda c: c)
    wide = pl.BlockSpec((Q, DI), lambda c: (c, 0))
    sel_p = pl.BlockSpec((LANES, DI), lambda c: (0, 0))
    sel_q = pl.BlockSpec((LANES, n_heads * Q), lambda c: (0, 0))
    return pl.pallas_call(
        body, name=name, grid=(nc,),
        in_specs=[x_s, b_s, c_s, dt_s, wide, vec, vec, nw, nw, sel_p, sel_q],
        out_specs=(wide, wide, pl.BlockSpec((1, DI, N), lambda c: (c, 0, 0))),
        out_shape=(jax.ShapeDtypeStruct((T, DI), F32), jax.ShapeDtypeStruct((T, DI), BF16), jax.ShapeDtypeStruct((nc, DI, N), F32)),
        scratch_shapes=[pltpu.VMEM((DI, N), F32)],
        compiler_params=_cp("arbitrary"),
    )(xbc, xbc, xbc, pdt, pz, dt_bias, a_log, jnp.repeat(d_skip[:, :n_heads], P, axis=1), norm_w,
      _head_selector(n_heads, P), _head_selector(n_heads, Q))


def _ssd_bwd(xbc, pdt, pz, dt_bias, a_log, d_skip, norm_w, y, prev, dyn, pad, n_heads, name):
    T, W = xbc.shape
    DI = pz.shape[1]
    P = DI // n_heads
    GN = (W - DI) // 2
    G = GN // D_STATE
    R = n_heads // G
    GW = R * P
    nc = T // CHUNK
    Q, N = CHUNK, D_STATE

    def body(x_ref, b_ref, c_ref, dtr_ref, z_ref, dtb_ref, al_ref, dsk_ref, nw_ref, y_ref, prev_ref, next_ref, dyn_ref,
             sel_p_ref, sel_q_ref, sel_pt_ref,
             dxbc_ref, dz_ref, ddt_ref, ddtb_ref, dal_ref, ddsk_ref, dnw_ref, ds_ref):
        i = pl.program_id(0)
        c = nc - 1 - i

        @pl.when(i == 0)
        def _():
            ds_ref[...] = jnp.zeros_like(ds_ref)
            ddtb_ref[...] = jnp.zeros_like(ddtb_ref)
            dal_ref[...] = jnp.zeros_like(dal_ref)
            ddsk_ref[...] = jnp.zeros_like(ddsk_ref)
            dnw_ref[...] = jnp.zeros_like(dnw_ref)

        valid, live, dt, a, cs, cst, sig_pre = _ssd_common(c, pad, n_heads, dtr_ref, dtb_ref, al_ref)
        cs_b, dt_x, e_x, dec_x = _ssd_decays(dt, cs, sel_p_ref, sel_q_ref)
        e_last = jnp.exp(cs[Q - 1:Q, :])
        tri = _iota((Q, Q), 0) >= _iota((Q, Q), 1)
        tri_t = _iota((Q, Q), 0) <= _iota((Q, Q), 1)
        state_rows = _iota((GW, 1), 0)
        head_lane = _iota((1, GW), 1)
        lane = _iota((1, LANES), 1)
        head_rows = _iota((LANES, 1), 0)
        s_dy_cs = jnp.zeros((Q, LANES), F32)
        s_x_bds = jnp.zeros((Q, LANES), F32)
        s_x_dxdt = jnp.zeros((Q, LANES), F32)
        dcs_rows = jnp.zeros((Q, LANES), F32)
        dcs_cols = jnp.zeros((LANES, Q), F32)
        c_end = jnp.zeros((1, LANES), F32)
        dsk_rows = []
        for g in range(G):
            gs = slice(g * GW, (g + 1) * GW)
            bg = jnp.where(valid, b_ref[:, g * N:(g + 1) * N], 0.0).astype(BF16)
            cg = jnp.where(valid, c_ref[:, g * N:(g + 1) * N], 0.0).astype(BF16)
            xg = jnp.where(valid, x_ref[:, gs], 0.0)
            s_prev = prev_ref[0, gs, :]
            dsg = ds_ref[gs, :]
            end = dsg * next_ref[0, gs, :]
            yg = y_ref[:, gs]
            dsk = dsk_ref[:, gs]
            ytot = yg + xg * dsk
            z = z_ref[:, gs]
            sz = _sigmoid(z)
            silu = z * sz
            gz = ytot * silu
            rstd = lax.rsqrt(jnp.mean(gz * gz, axis=-1, keepdims=True) + EPS)
            gn = gz * rstd
            dyn_g = dyn_ref[:, gs]
            dnw_ref[:, gs] += jnp.sum(dyn_g * gn, axis=0, keepdims=True)
            dgn = dyn_g * nw_ref[:, gs]
            dgz = rstd * (dgn - gn * jnp.mean(dgn * gn, axis=-1, keepdims=True))
            dz_ref[:, gs] = (dgz * ytot * (sz * (1.0 + z * (1.0 - sz)))).astype(BF16)
            dy = dgz * silu
            dsk_rows.append(jnp.sum(dy * xg, axis=0, keepdims=True))
            cb = _dot(cg, bg, "nt")
            cb_t = _dot(bg, cg, "nt")
            bds = _dot(bg, dsg, "nt") * dec_x[:, gs]
            csg = _dot(cg, s_prev, "nt")
            xdt = xg * dt_x[:, gs]
            dxdt = bds
            dcb = jnp.zeros((Q, Q), F32)
            dcb_t = jnp.zeros((Q, Q), F32)
            for r in range(R):
                h = g * R + r
                cs_col = cs_b[:, h * Q:(h + 1) * Q]
                cs_row = cst[h:h + 1, :]
                lmat = jnp.exp(jnp.where(tri, cs_col - cs_row, -jnp.inf))
                lmat_t = jnp.exp(jnp.where(tri_t, cs_row - cs_col, -jnp.inf))
                in_head = jnp.logical_and(head_lane >= r * P, head_lane < (r + 1) * P)
                dyr = jnp.where(in_head, dy, 0.0)
                dm = _dot(dyr, xdt, "nt")
                dcb = dcb + dm * lmat
                dcb_t = dcb_t + _dot(xdt, dyr, "nt") * lmat_t
                w_rc = dm * (cb * lmat)
                dcs_rows = jnp.where(lane == h, jnp.sum(w_rc, axis=1, keepdims=True), dcs_rows)
                dcs_cols = jnp.where(head_rows == h, jnp.sum(w_rc, axis=0, keepdims=True), dcs_cols)
                dxdt = dxdt + _dot(cb_t * lmat_t, dyr)
            sel_t = sel_pt_ref[gs, :]
            s_dy_cs = s_dy_cs + _dot_exact(dy * csg, sel_t, 2)
            s_x_bds = s_x_bds + _dot_exact(xg * bds, sel_t, 2)
            s_x_dxdt = s_x_dxdt + _dot_exact(xg * dxdt, sel_t, 2)
            c_end = c_end + jnp.sum(jnp.sum(end, axis=1, keepdims=True) * sel_t.astype(F32), axis=0, keepdims=True)
            dye = dy * e_x[:, gs]
            dc = _dot(dcb, bg) + _dot(dye, s_prev)
            db = _dot(dcb_t, cg) + _dot(xdt * dec_x[:, gs], dsg)
            decay = _by_row_block(state_rows, [e_last[:, g * R + r:g * R + r + 1] for r in range(R)], P)
            ds_ref[gs, :] = dsg * decay + _dot(dye, cg, "tn")
            dxbc_ref[:, gs] = jnp.where(valid, dxdt * dt_x[:, gs] + dy * dsk, 0.0)
            dxbc_ref[:, DI + g * N:DI + (g + 1) * N] = jnp.where(valid, db, 0.0)
            dxbc_ref[:, DI + GN + g * N:DI + GN + (g + 1) * N] = jnp.where(valid, dc, 0.0)
        dcs = s_dy_cs * jnp.exp(cs) - dt * s_x_bds
        da_cs = _cumsum_rows(dcs + (dcs_rows - dcs_cols.T), reverse=True) + c_end
        ddt_all = jnp.where(live, da_cs * a + s_x_dxdt, 0.0)
        ddt_raw = ddt_all * sig_pre
        ddt_ref[...] = ddt_raw.astype(BF16)
        ddtb_ref[...] += jnp.sum(ddt_raw, axis=0, keepdims=True)
        dal_ref[...] += jnp.sum(da_cs * dt, axis=0, keepdims=True) * a
        dsk_all = jnp.broadcast_to(jnp.concatenate(dsk_rows, axis=1), (8, DI))
        ddsk_ref[...] += _dot_exact(dsk_all, sel_pt_ref[...], 3)[0:1]

    rev = lambda i: nc - 1 - i
    x_s, b_s, c_s, dt_s, vec, nw = _ssd_specs(T, DI, GN, rev)
    wide = pl.BlockSpec((Q, DI), lambda i: (rev(i), 0))
    st = pl.BlockSpec((1, DI, N), lambda i: (rev(i), 0, 0))
    st_next = pl.BlockSpec((1, DI, N), lambda i: (jnp.minimum(rev(i) + 1, nc - 1), 0, 0))
    sel_p = pl.BlockSpec((LANES, DI), lambda i: (0, 0))
    sel_q = pl.BlockSpec((LANES, n_heads * Q), lambda i: (0, 0))
    sel_pt = pl.BlockSpec((DI, LANES), lambda i: (0, 0))
    sel = _head_selector(n_heads, P)
    return pl.pallas_call(
        body, name=name, grid=(nc,),
        in_specs=[x_s, b_s, c_s, dt_s, wide, vec, vec, nw, nw, wide, st, st_next, wide, sel_p, sel_q, sel_pt],
        out_specs=(pl.BlockSpec((Q, W), lambda i: (rev(i), 0)), wide, dt_s, vec, vec, vec, nw),
        out_shape=(jax.ShapeDtypeStruct((T, W), F32), jax.ShapeDtypeStruct((T, DI), BF16), jax.ShapeDtypeStruct((T, LANES), BF16),
                   jax.ShapeDtypeStruct((1, LANES), F32), jax.ShapeDtypeStruct((1, LANES), F32),
                   jax.ShapeDtypeStruct((1, LANES), F32), jax.ShapeDtypeStruct((1, DI), F32)),
        scratch_shapes=[pltpu.VMEM((DI, N), F32)],
        compiler_params=_cp("arbitrary"),
    )(xbc, xbc, xbc, pdt, pz, dt_bias, a_log, jnp.repeat(d_skip[:, :n_heads], P, axis=1), norm_w, y, prev, prev, dyn,
      sel, _head_selector(n_heads, Q), sel.T)


def _adamw(w, g, m, v, name):
    rows, cols = w.shape
    tr = _tile(rows, 256, 8)

    def body(w_ref, g_ref, m_ref, v_ref, d_ref, nm_ref, nv_ref):
        g = g_ref[...]
        m = ADAM_B1 * m_ref[...] + (1.0 - ADAM_B1) * g
        v = ADAM_B2 * v_ref[...] + (1.0 - ADAM_B2) * (g * g)
        m_hat = m / (1.0 - ADAM_B1 ** ADAM_STEP)
        v_hat = v / (1.0 - ADAM_B2 ** ADAM_STEP)
        d_ref[...] = -ADAM_LR * (m_hat / (jnp.sqrt(v_hat) + ADAM_EPS) + ADAM_WD * w_ref[...])
        nm_ref[...] = m
        nv_ref[...] = v

    blk = pl.BlockSpec((tr, cols), lambda i: (i, 0))
    out = jax.ShapeDtypeStruct((rows, cols), F32)
    return pl.pallas_call(
        body, name=name, grid=(rows // tr,), in_specs=[blk] * 4, out_specs=(blk,) * 3, out_shape=(out,) * 3,
        compiler_params=_cp("parallel"),
    )(w, g, m, v)


def _sum_stack(parts, name, out_dtype=F32):
    rows, cols = parts[0].shape
    tr = _tile(rows, 512, 16)

    def body(*refs):
        acc = refs[0][...].astype(F32)
        for r in refs[1:-1]:
            acc = acc + r[...].astype(F32)
        refs[-1][...] = acc.astype(out_dtype)

    blk = pl.BlockSpec((tr, cols), lambda i: (i, 0))
    return pl.pallas_call(
        body, name=name, grid=(rows // tr,), in_specs=[blk] * len(parts), out_specs=blk,
        out_shape=jax.ShapeDtypeStruct((rows, cols), out_dtype), compiler_params=_cp("parallel"),
    )(*parts)


def _place():
    return lax.axis_index("x"), lax.axis_index("y"), lax.axis_index("c")


def _other_chips(x, y):
    return [(1 - x, y), (x, 1 - y), (1 - x, 1 - y)]


def _all_gather(shards, name):
    nb = len(shards)

    def body(*refs):
        ins, outs = refs[:nb], refs[nb:2 * nb]
        send_sems, recv_sems, local_sems = refs[2 * nb:]
        x, y, c = _place()
        me, sibling = (x, y, c), (x, y, 1 - c)
        chips = _other_chips(x, y)

        def copy(q, k, block, to, src=None):
            dst = outs[q].at[4 * block[0] + 2 * block[1] + block[2]]
            return pltpu.make_async_remote_copy(
                src_ref=dst if src is None else src, dst_ref=dst,
                send_sem=send_sems.at[7 * q + k], recv_sem=recv_sems.at[7 * q + k], device_id=to, device_id_type=MESH)

        started = []
        for q in range(nb):
            mine = pltpu.make_async_copy(ins[q], outs[q].at[4 * x + 2 * y + c], local_sems.at[q])
            mine.start()
            started.append(mine)
        first = []
        for q in range(nb):
            first.append(copy(q, 0, me, sibling, src=ins[q]))
            first += [copy(q, 1 + j, me, (*chip, c), src=ins[q]) for j, chip in enumerate(chips)]
        for cp in first:
            cp.start()
        passed = []
        for j, chip in enumerate(chips):
            for q in range(nb):
                copy(q, 1 + j, (*chip, c), me).wait_recv()
                fwd = copy(q, 4 + j, (*chip, c), sibling)
                fwd.start()
                passed.append(fwd)
        for q in range(nb):
            copy(q, 0, sibling, me).wait_recv()
            for j, chip in enumerate(chips):
                copy(q, 4 + j, (*chip, 1 - c), me).wait_recv()
        for cp in first + passed:
            cp.wait_send()
        for mine in started:
            mine.wait()

    return pl.pallas_call(
        body, name=name, in_specs=[ANY] * nb, out_specs=tuple([ANY] * nb),
        out_shape=tuple(jax.ShapeDtypeStruct((N_DEV,) + s.shape, s.dtype) for s in shards),
        scratch_shapes=[pltpu.SemaphoreType.DMA((7 * nb,)), pltpu.SemaphoreType.DMA((7 * nb,)), pltpu.SemaphoreType.DMA((nb,))],
    )(*shards)


HBM = pl.BlockSpec(memory_space=pltpu.HBM)
SEM = pl.BlockSpec(memory_space=pltpu.SEMAPHORE)
N_PEERS = N_DEV - 1


def _peer(k, x, y, c):
    return (1 - x if k & 4 else x, 1 - y if k & 2 else y, 1 - c if k & 1 else c)


def _direct_copies(gather, srcs, lands, send_sems, recv_sems):
    x, y, c = _place()
    copies = []
    for q in range(len(srcs)):
        for k in range(1, N_DEV):
            px, py, pc = _peer(k, x, y, c)
            if gather:
                src, dst = srcs[q], lands[q].at[4 * x + 2 * y + c]
            else:
                src, dst = srcs[q].at[4 * px + 2 * py + pc], lands[q].at[k - 1]
            copies.append(pltpu.make_async_remote_copy(
                src_ref=src, dst_ref=dst, send_sem=send_sems.at[N_PEERS * q + k - 1], recv_sem=recv_sems.at[N_PEERS * q + k - 1],
                device_id=(px, py, pc), device_id_type=MESH))
    return copies


def _exchange_start(gather, srcs, lands, name):
    n = len(srcs)

    def body(*refs):
        send_sems, recv_sems = refs[2 * n], refs[2 * n + 1]
        for cp in _direct_copies(gather, refs[:n], refs[n:2 * n], send_sems, recv_sems):
            cp.start()
        refs[-1][...] = jnp.zeros_like(refs[-1])

    held = [pltpu.with_memory_space_constraint(t, pltpu.HBM) for t in list(srcs) + list(lands)]
    out = pl.pallas_call(
        body, name=name,
        out_shape=(pltpu.SemaphoreType.DMA((N_PEERS * n,)), pltpu.SemaphoreType.DMA((N_PEERS * n,)),
                   *[pltpu.HBM(t.shape, t.dtype) for t in held], jax.ShapeDtypeStruct((8, LANES), F32)),
        in_specs=[HBM] * (2 * n), out_specs=(SEM, SEM, *[HBM] * (2 * n), pl.BlockSpec(memory_space=pltpu.VMEM)),
        input_output_aliases={i: 2 + i for i in range(2 * n)},
        compiler_params=pltpu.CompilerParams(has_side_effects=pltpu.SideEffectType.DATAFLOW_SIDE_EFFECTING),
    )(*held)
    return out[0], out[1], list(out[2:2 + 2 * n]), out[-1]


def _exchange_wait(gather, started, after, name):
    send_sems, recv_sems, held, _ = started
    n = len(held) // 2

    def body(*refs):
        for cp in _direct_copies(gather, refs[:n], refs[n:2 * n], refs[2 * n], refs[2 * n + 1]):
            cp.wait_send()
            cp.wait_recv()

    out = pl.pallas_call(
        body, name=name, out_shape=tuple(pltpu.HBM(t.shape, t.dtype) for t in held),
        in_specs=[HBM] * (2 * n) + [SEM, SEM, pl.BlockSpec(memory_space=pl.ANY)], out_specs=tuple([HBM] * (2 * n)),
        input_output_aliases={i: i for i in range(2 * n)},
        compiler_params=pltpu.CompilerParams(has_side_effects=pltpu.SideEffectType.DATAFLOW_SIDE_EFFECTING),
    )(*held, send_sems, recv_sems, after)
    return list(out[n:])


def _own_plus_received(stack, got, name):
    _, rows, cols = stack.shape
    tr = _tile(rows, 512, 16)

    def body(*refs):
        acc = refs[0][0]
        for r in refs[1:-1]:
            acc = acc + r[0].astype(F32)
        refs[-1][...] = acc

    def mine(i):
        x, y, c = _place()
        return (4 * x + 2 * y + c, i, 0)

    return pl.pallas_call(
        body, name=name, grid=(rows // tr,),
        in_specs=[pl.BlockSpec((1, tr, cols), mine)]
        + [pl.BlockSpec((1, tr, cols), functools.partial(lambda k, i: (k, i, 0), k)) for k in range(N_PEERS)],
        out_specs=pl.BlockSpec((tr, cols), lambda i: (i, 0)),
        out_shape=jax.ShapeDtypeStruct((rows, cols), F32), compiler_params=_cp("parallel"),
    )(stack, *[got] * N_PEERS)


class _Shard:
    def __init__(self, name, axis):
        self.name, self.axis = name, axis


BIG = [_Shard("w_in", 2), _Shard("pool_w_group", 2), _Shard("w_pool_up", 1), _Shard("w_ssd_out", 1),
       _Shard("w_o", 1), _Shard("w_ff1", 2), _Shard("w_ff2", 1)]
SMALL_SHARDED = [_Shard("meta_tokens", 1), _Shard("conv_w", 2)]
REPLICATED = ["mix_norm_w", "b_gate", "pool_scale", "conv_b", "dt_bias", "a_log", "d_skip", "ssd_norm_w",
              "mlp_norm_w", "final_norm_w"]
WEIGHTS = ["meta_tokens", "mix_norm_w", "w_in", "b_gate", "pool_w_group", "pool_scale", "w_pool_up", "conv_w", "conv_b",
           "dt_bias", "a_log", "d_skip", "ssd_norm_w", "w_ssd_out", "w_o", "mlp_norm_w", "w_ff1", "w_ff2", "final_norm_w"]


def _rows2(a):
    return a.reshape(-1, a.shape[-1])


def _unshard(stack, shard_shape, axis):
    t = stack.reshape((N_DEV,) + tuple(shard_shape))
    return jnp.concatenate([t[d] for d in range(N_DEV)], axis=axis)


def _reshard(layers, axis):
    cut = [jnp.split(t, N_DEV, axis=axis - 1) for t in layers]
    blocks = [jnp.concatenate([_rows2(pieces[d]) for pieces in cut], axis=0) for d in range(N_DEV)]
    return jnp.stack(blocks).reshape((4, 2) + blocks[0].shape)


def _lane_rows(a):
    n = a.size
    tile = 8 * LANES
    if n % tile:
        return jnp.pad(a.reshape(-1), (0, (-n) % tile)).reshape(-1, LANES)
    return a.reshape(-1, LANES)


def _unpack_small(buf, spans, shapes):
    out = []
    for (o, r), shp in zip(spans, shapes):
        n = 1
        for d in shp:
            n *= d
        t = buf[o:o + r]
        out.append(t.reshape(shp) if n == r * LANES else t.reshape(-1)[:n].reshape(shp))
    return out


def _pack_small(parts, mult):
    mats = [_lane_rows(p) for p in parts]
    spans, o = [], 0
    for t in mats:
        spans.append((o, t.shape[0]))
        o += t.shape[0]
    fill = (-o) % mult
    if fill:
        mats.append(jnp.zeros((fill, LANES), mats[0].dtype))
    return jnp.concatenate(mats, axis=0), spans


def _layer_fwd(h, lw, cfg, tag, dep=None):
    pad, n_heads = cfg["pad"], cfg["n_heads"]
    u = _rms_fwd(h, lw["mix_norm_w"], f"rms_mix_{tag}", dep=dep)
    p_xbc = _mm(u, lw["w_xbc"], "nn", f"proj_xbc_{tag}")
    p_z = _mm(u, lw["w_z"], "nn", f"proj_z_{tag}")
    p_gate = _mm(u, lw["w_gate"], "nn", f"proj_gate_{tag}")
    p_pool = _mm(u, lw["w_pool"], "nn", f"proj_pool_{tag}")
    p_dt = _mm(u, lw["w_dt"], "nn", f"proj_dt_{tag}")
    pooled, y1 = _pool_fwd(p_pool, lw["pool_w_group"], lw["pool_scale"], pad, f"pool_fwd_{tag}")
    y_pool = _mm(y1, lw["w_pool_up"], "nn", f"pool_up_{tag}")
    xbc = _conv_fwd(p_xbc, lw["conv_w"], lw["conv_b"], f"conv_fwd_{tag}")
    y, yn, prev = _ssd_fwd(xbc, p_dt, p_z, lw["dt_bias"], lw["a_log"], lw["d_skip"], lw["ssd_norm_w"], pad, n_heads, f"ssd_fwd_{tag}")
    y_ssd = _mm(yn, lw["w_ssd_out"], "nn", f"ssd_out_{tag}")
    mix = _gate_fwd(p_gate, lw["b_gate"], y_pool, y_ssd, f"gate_fwd_{tag}")
    h_mid = _mm(mix, lw["w_o"], "nn", f"mix_out_{tag}", res=h)
    v = _rms_fwd(h_mid, lw["mlp_norm_w"], f"rms_mlp_{tag}")
    hid, act = _mm(v, lw["w_ff1"], "nn", f"ff1_{tag}", epi="relu2")
    h_out = _mm(act, lw["w_ff2"], "nn", f"ff2_{tag}", res=h_mid)
    saved = dict(h=h, u=u, p_xbc=p_xbc, p_z=p_z, p_gate=p_gate, p_dt=p_dt, pooled=pooled, y1=y1, y_pool=y_pool, xbc=xbc,
                 y=y, yn=yn, prev=prev, y_ssd=y_ssd, mix=mix, h_mid=h_mid, v=v, hid=hid, act=act)
    return h_out, saved


def _layer_bwd(dh, lw, s, cfg, tag, traffic, i):
    pad, n_heads = cfg["pad"], cfg["n_heads"]
    g = {}
    dhid = _mm(dh, lw["w_ff2"], "nt", f"d_act_{tag}", epi="drelu2", aux=s["hid"], dep=traffic.bwd_begin(i))
    g["w_ff2"] = _mm(s["act"], dh, "tn", f"dw_ff2_{tag}")
    dv = _mm(dhid, lw["w_ff1"], "nt", f"d_v_{tag}")
    g["w_ff1"] = _mm(s["v"], dhid, "tn", f"dw_ff1_{tag}")
    dep = traffic.grads_ready(i, {k: g[k] for k in ("w_ff1", "w_ff2")}, "mlp")
    dh_mid, g["mlp_norm_w"] = _rms_bwd(s["h_mid"], lw["mlp_norm_w"], dv, dh, f"rms_mlp_bwd_{tag}")
    dmix = _mm(dh_mid, lw["w_o"], "nt", f"d_mix_{tag}", dep=dep)
    g["w_o"] = _mm(s["mix"], dh_mid, "tn", f"dw_o_{tag}")
    dgate, dy_pool, dy_ssd, g["b_gate"] = _gate_bwd(s["p_gate"], lw["b_gate"], s["y_pool"], s["y_ssd"], dmix, f"gate_bwd_{tag}")
    dy1 = _mm(dy_pool, lw["w_pool_up"], "nt", f"d_y1_{tag}")
    g["w_pool_up"] = _mm(s["y1"], dy_pool, "tn", f"dw_pool_up_{tag}")
    dpool, g["pool_w_group"], g["pool_scale"] = _pool_bwd(s["pooled"], lw["pool_w_group"], lw["pool_scale"], dy1, pad, f"pool_bwd_{tag}")
    g["w_ssd_out"] = _mm(s["yn"], dy_ssd, "tn", f"dw_ssd_out_{tag}")
    dep = traffic.grads_ready(i, {k: g[k] for k in ("pool_w_group", "w_pool_up", "w_ssd_out", "w_o")}, "mix")
    dyn = _mm(dy_ssd, lw["w_ssd_out"], "nt", f"d_yn_{tag}", dep=dep)
    dact, dz, ddt, g["dt_bias"], g["a_log"], g["d_skip"], g["ssd_norm_w"] = _ssd_bwd(
        s["xbc"], s["p_dt"], s["p_z"], lw["dt_bias"], lw["a_log"], lw["d_skip"], lw["ssd_norm_w"], s["y"], s["prev"], dyn,
        pad, n_heads, f"ssd_bwd_{tag}")
    dxbc, g["conv_w"], g["conv_b"] = _conv_bwd(s["p_xbc"], lw["conv_w"], lw["conv_b"], dact, pad, f"conv_bwd_{tag}")
    u = s["u"]
    du = _mm(dxbc, lw["w_xbc"], "nt", f"du_xbc_{tag}")
    du = _mm(dz, lw["w_z"], "nt", f"du_z_{tag}", res=du)
    du = _mm(dgate, lw["w_gate"], "nt", f"du_gate_{tag}", res=du)
    du = _mm(dpool, lw["w_pool"], "nt", f"du_pool_{tag}", res=du)
    du = _mm(ddt, lw["w_dt"], "nt", f"du_dt_{tag}", res=du)
    g["w_xbc"] = _mm(u, dxbc, "tn", f"dw_xbc_{tag}")
    g["w_z"] = _mm(u, dz, "tn", f"dw_z_{tag}")
    g["w_gate"] = _mm(u, dgate, "tn", f"dw_gate_{tag}")
    g["w_pool"] = _mm(u, dpool, "tn", f"dw_pool_{tag}")
    g["w_dt"] = _mm(u, ddt, "tn", f"dw_dt_{tag}")
    c_dt = cfg["cols"][3]
    g["w_in"] = jnp.concatenate([g["w_pool"], g["w_z"], g["w_xbc"], g["w_dt"][:, :c_dt], g["w_gate"]], axis=1)
    traffic.grads_ready(i, {"w_in": g["w_in"]}, "in")
    dh_in, g["mix_norm_w"] = _rms_bwd(s["h"], lw["mix_norm_w"], du, dh_mid, f"rms_mix_bwd_{tag}")
    traffic.bwd_end(i, dh_in)
    return dh_in, g


def _pad_lanes(v):
    return jnp.pad(v, (0, LANES - v.shape[0])).reshape(1, LANES)


class _WholeWeights:
    def __init__(self, full):
        self.full = full

    def fwd_begin(self, i):
        return {k: t[i] for k, t in self.full.items()}, None

    def fwd_end(self, i, h_out):
        pass

    def bwd_begin(self, i):
        return None

    def grads_ready(self, i, grads, tag):
        return None

    def bwd_end(self, i, dh_in):
        pass


def _local_step(x2, target, meta_full, traffic, rep, cfg):
    depth, pad, n_meta, H = cfg["depth"], cfg["pad"], cfg["n_meta"], cfg["n_heads"]
    D = x2.shape[1]
    di = rep["ssd_norm_w"].shape[1]
    c_pool, c_z, c_xbc, c_dt = cfg["cols"]
    h = jnp.concatenate([jnp.zeros((pad, D), F32), meta_full, x2], axis=0)
    lws, saves = [], []
    for i in range(depth):
        full, dep = traffic.fwd_begin(i)
        w_in = full["w_in"]
        o = 0
        w_pool = w_in[:, o:o + c_pool]; o += c_pool
        w_z = w_in[:, o:o + c_z]; o += c_z
        w_xbc = w_in[:, o:o + c_xbc]; o += c_xbc
        w_dt = jnp.pad(w_in[:, o:o + c_dt], ((0, 0), (0, LANES - c_dt))); o += c_dt
        w_gate = w_in[:, o:]
        lw = dict(
            w_pool=w_pool, w_z=w_z, w_xbc=w_xbc, w_dt=w_dt, w_gate=w_gate,
            pool_w_group=full["pool_w_group"], w_pool_up=full["w_pool_up"], w_ssd_out=full["w_ssd_out"],
            w_o=full["w_o"], w_ff1=full["w_ff1"], w_ff2=full["w_ff2"], conv_w=full["conv_w"],
            mix_norm_w=rep["mix_norm_w"][i], b_gate=rep["b_gate"][i], pool_scale=rep["pool_scale"][i].reshape(1, -1),
            conv_b=rep["conv_b"][i].reshape(1, -1), dt_bias=_pad_lanes(rep["dt_bias"][i]), a_log=_pad_lanes(rep["a_log"][i]),
            d_skip=_pad_lanes(rep["d_skip"][i]), ssd_norm_w=rep["ssd_norm_w"][i].reshape(1, di), mlp_norm_w=rep["mlp_norm_w"][i])
        lws.append(lw)
        h, s = _layer_fwd(h, lw, cfg, f"l{i}", dep)
        saves.append(s)
        traffic.fwd_end(i, h)
    loss, dh, g_final = _final_loss(h, rep["final_norm_w"], target, pad + n_meta, "final_loss")
    per_layer = []
    for i in range(depth - 1, -1, -1):
        dh, g = _layer_bwd(dh, lws[i], saves[i], cfg, f"l{i}", traffic, i)
        per_layer.append(g)
    per_layer.reverse()

    def stack(key, fn=lambda t: t):
        return jnp.stack([fn(g[key]) for g in per_layer])

    def layers(key):
        return [g[key] for g in per_layer]

    grads = dict(
        w_in=layers("w_in"), pool_w_group=layers("pool_w_group"), w_pool_up=layers("w_pool_up"), w_ssd_out=layers("w_ssd_out"), w_o=layers("w_o"),
        w_ff1=layers("w_ff1"), w_ff2=layers("w_ff2"), conv_w=stack("conv_w"),
        mix_norm_w=stack("mix_norm_w", lambda t: t[0]), b_gate=stack("b_gate", lambda t: t[0]),
        pool_scale=stack("pool_scale", lambda t: t[0]), conv_b=stack("conv_b", lambda t: t[0]),
        dt_bias=stack("dt_bias", lambda t: t[0, :H]), a_log=stack("a_log", lambda t: t[0, :H]), d_skip=stack("d_skip", lambda t: t[0, :H]),
        ssd_norm_w=stack("ssd_norm_w", lambda t: t[0]), mlp_norm_w=stack("mlp_norm_w", lambda t: t[0]),
        final_norm_w=g_final[0], meta_tokens=dh[pad:pad + n_meta])
    return loss, dh[pad + n_meta:], grads


class _ShardedWeights:
    def __init__(self, w, me):
        self.w, self.me = w, me
        self.depth = w[BIG[0].name].shape[0]
        shards0 = [_rows2(w[s.name][0]).astype(BF16) for s in BIG]
        small = [_rows2(w[s.name]) for s in SMALL_SHARDED]
        landed = _all_gather(shards0 + small, "gather_l0")
        self.small = {s.name: _unshard(t, w[s.name].shape, s.axis) for s, t in zip(SMALL_SHARDED, landed[len(BIG):])}
        self.landed = landed[:len(BIG)]
        self.fetching = None
        self.sending, self.token = [], None
        self.local_grads = [{} for _ in range(self.depth)]

    def fwd_begin(self, i):
        full = {s.name: _unshard(t, self.w[s.name].shape[1:], s.axis - 1) for s, t in zip(BIG, self.landed)}
        full["conv_w"] = self.small["conv_w"][i]
        dep = None
        if i + 1 < self.depth:
            shards = [_rows2(self.w[s.name][i + 1]).astype(BF16) for s in BIG]
            lands = [lax.dynamic_update_slice(lax.empty((N_DEV,) + t.shape, t.dtype), t[None], (self.me, 0, 0)) for t in shards]
            self.fetching = _exchange_start(True, shards, lands, f"gather_start_l{i + 1}")
            dep = self.fetching[3]
        return full, dep

    def fwd_end(self, i, h_out):
        if self.fetching is not None:
            self.landed = _exchange_wait(True, self.fetching, h_out, f"gather_wait_l{i + 1}")
            self.fetching = None

    def bwd_begin(self, i):
        return self.token

    def grads_ready(self, i, grads, tag):
        shards = [s for s in BIG if s.name in grads]
        stacks = [_reshard([grads[s.name]], s.axis).reshape((N_DEV,) + _rows2(self.w[s.name][i]).shape) for s in shards]
        sends = [t.astype(BF16) for t in stacks]
        lands = [lax.empty((N_PEERS,) + t.shape[1:], BF16) for t in sends]
        started = _exchange_start(False, sends, lands, f"rs_start_{tag}_l{i}")
        self.sending.append((i, tag, shards, stacks, started))
        self.token = started[3]
        return self.token

    def _collect(self, entry, after):
        i, tag, shards, stacks, started = entry
        got = _exchange_wait(False, started, after, f"rs_wait_{tag}_l{i}")
        for s, st, g in zip(shards, stacks, got):
            self.local_grads[i][s.name] = _own_plus_received(st, g, f"rs_sum_{s.name}_l{i}")

    def bwd_end(self, i, dh_in):
        for entry in [e for e in self.sending if e[0] > i]:
            self._collect(entry, dh_in)
        self.sending = [e for e in self.sending if e[0] <= i]

    def finish(self, after, names):
        for entry in [e for e in self.sending if e[2][0].name in names]:
            self._collect(entry, after)
        self.sending = [e for e in self.sending if e[2][0].name not in names]
        return {k: jnp.concatenate([g[k] for g in self.local_grads], axis=0) for k in names}


def kernel(x, meta_tokens, mix_norm_w, w_in, b_gate, pool_w_group, pool_scale, w_pool_up, conv_w, conv_b, dt_bias, a_log, d_skip, ssd_norm_w, w_ssd_out, w_o, mlp_norm_w, w_ff1, w_ff2, final_norm_w, loss_target, m_meta_tokens, m_mix_norm_w, m_w_in, m_b_gate, m_pool_w_group, m_pool_scale, m_w_pool_up, m_conv_w, m_conv_b, m_dt_bias, m_a_log, m_d_skip, m_ssd_norm_w, m_w_ssd_out, m_w_o, m_mlp_norm_w, m_w_ff1, m_w_ff2, m_final_norm_w, v_meta_tokens, v_mix_norm_w, v_w_in, v_b_gate, v_pool_w_group, v_pool_scale, v_w_pool_up, v_conv_w, v_conv_b, v_dt_bias, v_a_log, v_d_skip, v_ssd_norm_w, v_w_ssd_out, v_w_o, v_mlp_norm_w, v_w_ff1, v_w_ff2, v_final_norm_w):
    w = dict(meta_tokens=meta_tokens, mix_norm_w=mix_norm_w, w_in=w_in, b_gate=b_gate, pool_w_group=pool_w_group,
             pool_scale=pool_scale, w_pool_up=w_pool_up, conv_w=conv_w, conv_b=conv_b, dt_bias=dt_bias, a_log=a_log,
             d_skip=d_skip, ssd_norm_w=ssd_norm_w, w_ssd_out=w_ssd_out, w_o=w_o, mlp_norm_w=mlp_norm_w, w_ff1=w_ff1,
             w_ff2=w_ff2, final_norm_w=final_norm_w)
    m = dict(meta_tokens=m_meta_tokens, mix_norm_w=m_mix_norm_w, w_in=m_w_in, b_gate=m_b_gate, pool_w_group=m_pool_w_group,
             pool_scale=m_pool_scale, w_pool_up=m_w_pool_up, conv_w=m_conv_w, conv_b=m_conv_b, dt_bias=m_dt_bias, a_log=m_a_log,
             d_skip=m_d_skip, ssd_norm_w=m_ssd_norm_w, w_ssd_out=m_w_ssd_out, w_o=m_w_o, mlp_norm_w=m_mlp_norm_w, w_ff1=m_w_ff1,
             w_ff2=m_w_ff2, final_norm_w=m_final_norm_w)
    v = dict(meta_tokens=v_meta_tokens, mix_norm_w=v_mix_norm_w, w_in=v_w_in, b_gate=v_b_gate, pool_w_group=v_pool_w_group,
             pool_scale=v_pool_scale, w_pool_up=v_w_pool_up, conv_w=v_conv_w, conv_b=v_conv_b, dt_bias=v_dt_bias, a_log=v_a_log,
             d_skip=v_d_skip, ssd_norm_w=v_ssd_norm_w, w_ssd_out=v_w_ssd_out, w_o=v_w_o, mlp_norm_w=v_mlp_norm_w, w_ff1=v_w_ff1,
             w_ff2=v_w_ff2, final_norm_w=v_final_norm_w)

    _, seq, D = x.shape
    n_meta = meta_tokens.shape[0]
    depth = w_in.shape[0]
    n_heads = dt_bias.shape[1]
    d_inner = ssd_norm_w.shape[1]
    d_xbc = conv_b.shape[1]
    pool_width = pool_scale.shape[1]
    pad = (-n_meta) % CHUNK
    cfg = dict(depth=depth, pad=pad, n_meta=n_meta, n_heads=n_heads, cols=(pool_width, d_inner, d_xbc, n_heads))
    assert (pad + n_meta + seq) % CHUNK == 0 and pad + n_meta == CHUNK

    xi, yi, ci = _place()
    me = 4 * xi + 2 * yi + ci

    traffic = _ShardedWeights(w, me)
    rep = {k: w[k] for k in REPLICATED}
    loss_part, dx, grads = _local_step(x[0], loss_target[0], traffic.small["meta_tokens"], traffic, rep, cfg)
    loss = lax.psum(loss_part[0, 0], ("x", "y", "c"))

    small_names = REPLICATED + [s.name for s in SMALL_SHARDED]
    sm_buf, sm_spans = _pack_small([grads[k] for k in small_names], 16)
    (sm_all,) = _all_gather([sm_buf], "gather_small_grads")
    sm_sum = _sum_stack([sm_all[d] for d in range(N_DEV)], "small_grads_sum")
    g_small = dict(zip(small_names, _unpack_small(sm_sum, sm_spans, [grads[k].shape for k in small_names])))
    g_loc = {k: g_small[k] for k in REPLICATED}
    for s in SMALL_SHARDED:
        blk = w[s.name].shape[s.axis]
        g_loc[s.name] = lax.dynamic_slice_in_dim(g_small[s.name], me * blk, blk, axis=s.axis)

    delta, new_m, new_v = {}, {}, {}
    loc_shapes = [w[k].shape for k in small_names]
    packed = [_pack_small([t[k] for k in small_names], 8) for t in (w, g_loc, m, v)]
    loc_spans = packed[0][1]
    outs = _adamw(*[p[0] for p in packed], "adamw_small")
    for res, buf in zip((delta, new_m, new_v), outs):
        res.update(zip(small_names, _unpack_small(buf, loc_spans, loc_shapes)))
    after = outs[0]
    for names in ([s.name for s in BIG if s.name != "w_in"], ["w_in"]):
        for k, t in traffic.finish(after, names).items():
            shp = w[k].shape
            d2, m2, v2 = _adamw(_rows2(w[k]), t, _rows2(m[k]), _rows2(v[k]), f"adamw_{k}")
            g_loc[k], delta[k], new_m[k], new_v[k] = t.reshape(shp), d2.reshape(shp), m2.reshape(shp), v2.reshape(shp)
            after = d2

    return (loss, dx[None], *[g_loc[k] for k in WEIGHTS], *[delta[k] for k in WEIGHTS],
            *[new_m[k] for k in WEIGHTS], *[new_v[k] for k in WEIGHTS])
```

```python
import functools

import jax
import jax.numpy as jnp
from jax import lax
from jax.experimental import pallas as pl
from jax.experimental.pallas import tpu as pltpu

F32 = jnp.float32
BF16 = jnp.bfloat16

EPS = 1e-5
D_STATE = 128
CHUNK = 128
LANES = 128
POOL_WINDOWS = (2, 4, 8, 16)
POOL_HALO = 16
CONV_WIDTH = 4
CONV_HALO = 8
N_DEV = 8
ADAM_LR = 0.001
ADAM_B1 = 0.9
ADAM_B2 = 0.999
ADAM_EPS = 1e-08
ADAM_WD = 0.01
ADAM_STEP = 10
VMEM_LIMIT = 52 * 1024 * 1024
MESH = pl.DeviceIdType.MESH
ANY = pl.BlockSpec(memory_space=pl.ANY)


def _cp(*sem):
    return pltpu.CompilerParams(dimension_semantics=sem, vmem_limit_bytes=VMEM_LIMIT)


def _tile(n, target, mult):
    best = None
    for t in range(mult, min(n, target) + 1, mult):
        if n % t == 0:
            best = t
    return best if best is not None else n


def _sigmoid(x):
    return jax.nn.sigmoid(x)


def _iota(shape, dim):
    return lax.broadcasted_iota(jnp.int32, shape, dim)


_DIMS = {"nn": (((1,), (0,)), ((), ())), "nt": (((1,), (1,)), ((), ())), "tn": (((0,), (0,)), ((), ()))}


def _dot(a, b, mode="nn"):
    return lax.dot_general(a.astype(BF16), b.astype(BF16), _DIMS[mode], preferred_element_type=F32)


DEP = pl.BlockSpec((8, LANES), lambda *_: (0, 0))


def _mm(a, b, mode, name, *, out_dtype=F32, res=None, epi=None, aux=None, dep=None):
    if mode == "nn":
        (M, K), (_, N) = a.shape, b.shape
    elif mode == "nt":
        (M, K), (N, _) = a.shape, b.shape
    else:
        (K, M), (_, N) = a.shape, b.shape
    if mode == "tn":
        tm, tn, tk = _tile(M, 512, 128), _tile(N, 512, 128), K
    else:
        tm, tn, tk = _tile(M, 1056, 16), _tile(N, 512, 128), _tile(K, 4096, 128)
    nk = K // tk
    a_spec = pl.BlockSpec((tk, tm), lambda i, j, k: (k, i)) if mode == "tn" else pl.BlockSpec((tm, tk), lambda i, j, k: (i, k))
    b_spec = pl.BlockSpec((tn, tk), lambda i, j, k: (j, k)) if mode == "nt" else pl.BlockSpec((tk, tn), lambda i, j, k: (k, j))
    o_spec = pl.BlockSpec((tm, tn), lambda i, j, k: (i, j))
    extra = [t for t in (res, aux) if t is not None]
    deps = [] if dep is None else [dep]

    def body(*refs):
        a_ref, b_ref = refs[0], refs[1]
        x_ref = refs[2] if extra else None
        outs = refs[2 + len(extra) + len(deps):]
        p = _dot(a_ref[...], b_ref[...], mode)

        def finish(r):
            if res is not None:
                outs[0][...] = (x_ref[...] + r).astype(out_dtype)
            elif epi == "relu2":
                outs[0][...] = r
                hid = jnp.maximum(r, 0.0)
                outs[1][...] = (hid * hid).astype(BF16)
            elif epi == "drelu2":
                outs[0][...] = (r * (2.0 * jnp.maximum(x_ref[...], 0.0))).astype(BF16)
            else:
                outs[0][...] = r.astype(out_dtype)

        if nk == 1:
            finish(p)
        else:
            acc = outs[-1]
            k = pl.program_id(2)

            @pl.when(k == 0)
            def _():
                acc[...] = p

            @pl.when(k > 0)
            def _():
                acc[...] += p

            @pl.when(k == nk - 1)
            def _():
                finish(acc[...])

    if epi == "relu2":
        out_shape = (jax.ShapeDtypeStruct((M, N), F32), jax.ShapeDtypeStruct((M, N), BF16))
        out_specs = (o_spec, o_spec)
    elif epi == "drelu2":
        out_shape, out_specs = jax.ShapeDtypeStruct((M, N), BF16), o_spec
    else:
        out_shape, out_specs = jax.ShapeDtypeStruct((M, N), out_dtype), o_spec
    return pl.pallas_call(
        body, name=name, grid=(M // tm, N // tn, nk),
        in_specs=[a_spec, b_spec] + [o_spec] * len(extra) + [DEP] * len(deps),
        out_specs=out_specs, out_shape=out_shape,
        scratch_shapes=[pltpu.VMEM((tm, tn), F32)] if nk > 1 else [],
        compiler_params=_cp("parallel", "parallel", "arbitrary"),
    )(a, b, *extra, *deps)


def _rms_fwd(h, w, name, dep=None):
    T, D = h.shape
    tr = _tile(T, 1056, 16)
    deps = [] if dep is None else [dep]

    def body(h_ref, w_ref, *rest):
        x = h_ref[...]
        xn = x * lax.rsqrt(jnp.mean(x * x, axis=-1, keepdims=True) + EPS)
        rest[-1][...] = (xn * w_ref[...]).astype(BF16)

    return pl.pallas_call(
        body, name=name, grid=(T // tr,),
        in_specs=[pl.BlockSpec((tr, D), lambda i: (i, 0)), pl.BlockSpec((1, D), lambda i: (0, 0))] + [DEP] * len(deps),
        out_specs=pl.BlockSpec((tr, D), lambda i: (i, 0)), out_shape=jax.ShapeDtypeStruct((T, D), BF16),
        compiler_params=_cp("parallel"),
    )(h, w.reshape(1, D), *deps)


def _rms_bwd(h, w, dy, dres, name):
    T, D = h.shape
    tr = _tile(T, 528, 16)

    def body(h_ref, w_ref, dy_ref, dres_ref, dh_ref, dhb_ref, dw_ref):
        x = h_ref[...]
        rstd = lax.rsqrt(jnp.mean(x * x, axis=-1, keepdims=True) + EPS)
        xn = x * rstd
        dy = dy_ref[...]
        dxn = dy * w_ref[...]
        dh = dres_ref[...] + rstd * (dxn - xn * jnp.mean(dxn * xn, axis=-1, keepdims=True))
        dh_ref[...] = dh
        dhb_ref[...] = dh.astype(BF16)
        dw = jnp.sum(dy * xn, axis=0, keepdims=True)

        @pl.when(pl.program_id(0) == 0)
        def _():
            dw_ref[...] = dw

        @pl.when(pl.program_id(0) > 0)
        def _():
            dw_ref[...] += dw

    row = pl.BlockSpec((tr, D), lambda i: (i, 0))
    vec = pl.BlockSpec((1, D), lambda i: (0, 0))
    return pl.pallas_call(
        body, name=name, grid=(T // tr,),
        in_specs=[row, vec, row, row], out_specs=(row, row, vec),
        out_shape=(jax.ShapeDtypeStruct((T, D), F32), jax.ShapeDtypeStruct((T, D), BF16), jax.ShapeDtypeStruct((1, D), F32)),
        compiler_params=_cp("arbitrary"),
    )(h, w.reshape(1, D), dy, dres)


def _final_loss(h, w, target, first_row, name):
    T, D = h.shape
    tr = CHUNK
    assert first_row == tr

    def body(h_ref, w_ref, t_ref, loss_ref, dh_ref, dhb_ref, dw_ref):
        i = pl.program_id(0)
        x = h_ref[...]
        rstd = lax.rsqrt(jnp.mean(x * x, axis=-1, keepdims=True) + EPS)
        xn = x * rstd
        w = w_ref[...]
        live = i > 0
        err = jnp.where(live, xn * w - t_ref[...], 0.0)
        part = 0.5 * jnp.sum(jnp.mean(err * err, axis=-1, keepdims=True), axis=0, keepdims=True)
        dout = err * (1.0 / D)
        dxn = dout * w
        dh = rstd * (dxn - xn * jnp.mean(dxn * xn, axis=-1, keepdims=True))
        dh_ref[...] = dh
        dhb_ref[...] = dh.astype(BF16)
        dw = jnp.sum(dout * xn, axis=0, keepdims=True)

        @pl.when(i == 0)
        def _():
            loss_ref[...] = part
            dw_ref[...] = dw

        @pl.when(i > 0)
        def _():
            loss_ref[...] += part
            dw_ref[...] += dw

    row = pl.BlockSpec((tr, D), lambda i: (i, 0))
    vec = pl.BlockSpec((1, D), lambda i: (0, 0))
    return pl.pallas_call(
        body, name=name, grid=(T // tr,),
        in_specs=[row, vec, pl.BlockSpec((tr, D), lambda i: (jnp.maximum(i - 1, 0), 0))],
        out_specs=(pl.BlockSpec((1, 1), lambda i: (0, 0)), row, row, vec),
        out_shape=(jax.ShapeDtypeStruct((1, 1), F32), jax.ShapeDtypeStruct((T, D), F32), jax.ShapeDtypeStruct((T, D), BF16),
                   jax.ShapeDtypeStruct((1, D), F32)),
        compiler_params=_cp("arbitrary"),
    )(h, w.reshape(1, D), target)


def _gate_fwd(pg, bg, y_pool, y_ssd, name):
    T, D = y_pool.shape
    tr = _tile(T, 528, 16)

    def body(gp_ref, gs_ref, bp_ref, bs_ref, yp_ref, ys_ref, o_ref):
        gp = _sigmoid(gp_ref[...] + bp_ref[...])
        gs = _sigmoid(gs_ref[...] + bs_ref[...])
        o_ref[...] = (gp * yp_ref[...] + gs * ys_ref[...]).astype(BF16)

    row = pl.BlockSpec((tr, D), lambda i: (i, 0))
    row1 = pl.BlockSpec((tr, D), lambda i: (i, 1))
    vec = pl.BlockSpec((1, D), lambda i: (0, 0))
    vec1 = pl.BlockSpec((1, D), lambda i: (0, 1))
    b2 = bg.reshape(1, 2 * D)
    return pl.pallas_call(
        body, name=name, grid=(T // tr,),
        in_specs=[row, row1, vec, vec1, row, row], out_specs=row,
        out_shape=jax.ShapeDtypeStruct((T, D), BF16), compiler_params=_cp("parallel"),
    )(pg, pg, b2, b2, y_pool, y_ssd)


def _gate_bwd(pg, bg, y_pool, y_ssd, dmix, name):
    T, D = y_pool.shape
    tr = _tile(T, 528, 16)

    def body(gp_ref, gs_ref, bp_ref, bs_ref, yp_ref, ys_ref, dm_ref, dg_ref, dyp_ref, dys_ref, db_ref):
        gp = _sigmoid(gp_ref[...] + bp_ref[...])
        gs = _sigmoid(gs_ref[...] + bs_ref[...])
        dm = dm_ref[...]
        dyp_ref[...] = (dm * gp).astype(BF16)
        dys_ref[...] = (dm * gs).astype(BF16)
        dgp = dm * yp_ref[...] * gp * (1.0 - gp)
        dgs = dm * ys_ref[...] * gs * (1.0 - gs)
        dg_ref[:, :D] = dgp.astype(BF16)
        dg_ref[:, D:] = dgs.astype(BF16)
        db = jnp.concatenate([jnp.sum(dgp, axis=0, keepdims=True), jnp.sum(dgs, axis=0, keepdims=True)], axis=1)

        @pl.when(pl.program_id(0) == 0)
        def _():
            db_ref[...] = db

        @pl.when(pl.program_id(0) > 0)
        def _():
            db_ref[...] += db

    row = pl.BlockSpec((tr, D), lambda i: (i, 0))
    row1 = pl.BlockSpec((tr, D), lambda i: (i, 1))
    wide = pl.BlockSpec((tr, 2 * D), lambda i: (i, 0))
    vec = pl.BlockSpec((1, D), lambda i: (0, 0))
    vec1 = pl.BlockSpec((1, D), lambda i: (0, 1))
    vec2 = pl.BlockSpec((1, 2 * D), lambda i: (0, 0))
    b2 = bg.reshape(1, 2 * D)
    return pl.pallas_call(
        body, name=name, grid=(T // tr,),
        in_specs=[row, row1, vec, vec1, row, row, row], out_specs=(wide, row, row, vec2),
        out_shape=(jax.ShapeDtypeStruct((T, 2 * D), BF16), jax.ShapeDtypeStruct((T, D), BF16),
                   jax.ShapeDtypeStruct((T, D), BF16), jax.ShapeDtypeStruct((1, 2 * D), F32)),
        compiler_params=_cp("arbitrary"),
    )(pg, pg, b2, b2, y_pool, y_ssd, dmix)


def _pool_count(c, pad, window):
    pos = c * CHUNK + _iota((CHUNK, 1), 0) - pad
    return jnp.clip(pos + 1, 1, window).astype(F32)


def _by_group(g, vals):
    out = vals[-1]
    for k in range(len(vals) - 2, -1, -1):
        out = jnp.where(g == k, vals[k], out)
    return out


def _by_row_block(rows, vals, block):
    out = vals[0]
    for r in range(1, len(vals)):
        out = jnp.where(rows >= r * block, vals[r], out)
    return out


def _pool_fwd(u, wg, scale, pad, name):
    T, C = u.shape
    G, Cg, _ = wg.shape
    nc = T // CHUNK

    def body(u_ref, wg_ref, s_ref, p_ref, y_ref):
        g = pl.program_id(0)
        window = _by_group(g, POOL_WINDOWS)

        def chunk(c, carry):
            r0 = pl.multiple_of(c * CHUNK, CHUNK)
            h0 = pl.multiple_of(jnp.maximum(r0 - POOL_HALO, 0), 8)
            halo = jnp.where(c > 0, u_ref[pl.ds(h0, POOL_HALO), :], 0.0)
            xc = u_ref[pl.ds(r0, CHUNK), :]
            s = jnp.concatenate([halo, xc], axis=0)
            sums = []
            k = 1
            while k < POOL_WINDOWS[-1]:
                s = s + pltpu.roll(s, k, 0)
                k *= 2
                if k in POOL_WINDOWS:
                    sums.append(s[POOL_HALO:])
            wsum = _by_group(g, sums)
            pooled = wsum / _pool_count(c, pad, window) - xc
            pb = pooled.astype(BF16)
            p_ref[pl.ds(r0, CHUNK), :] = pb
            y_ref[pl.ds(r0, CHUNK), :] = (_dot(pb, wg_ref[0]) * s_ref[...]).astype(BF16)
            return carry

        lax.fori_loop(0, nc, chunk, 0)

    col = pl.BlockSpec((T, Cg), lambda g: (0, g))
    return pl.pallas_call(
        body, name=name, grid=(G,),
        in_specs=[col, pl.BlockSpec((1, Cg, Cg), lambda g: (g, 0, 0)), pl.BlockSpec((1, Cg), lambda g: (0, g))],
        out_specs=(col, col),
        out_shape=(jax.ShapeDtypeStruct((T, C), BF16), jax.ShapeDtypeStruct((T, C), BF16)),
        compiler_params=_cp("parallel"),
    )(u, wg, scale)


def _pool_bwd(pooled, wg, scale, dy, pad, name):
    T, C = dy.shape
    G, Cg, _ = wg.shape
    nc = T // CHUNK

    def body(p_ref, wg_ref, s_ref, dy_ref, du_ref, dwg_ref, ds_ref, halo_ref):
        g = pl.program_id(0)
        window = _by_group(g, POOL_WINDOWS)
        halo_ref[...] = jnp.zeros_like(halo_ref)
        dwg_ref[...] = jnp.zeros_like(dwg_ref)
        ds_ref[...] = jnp.zeros_like(ds_ref)

        def chunk(i, carry):
            c = nc - 1 - i
            r0 = pl.multiple_of(c * CHUNK, CHUNK)
            pb = p_ref[pl.ds(r0, CHUNK), :]
            dyc = dy_ref[pl.ds(r0, CHUNK), :]
            w = wg_ref[0]
            ypre = _dot(pb, w)
            ds_ref[...] += jnp.sum(dyc * ypre, axis=0, keepdims=True)
            dyp = (dyc * s_ref[...]).astype(BF16)
            dwg_ref[0] += _dot(pb, dyp, "tn")
            dpool = _dot(dyp, w, "nt")
            q = dpool / _pool_count(c, pad, window)
            s = jnp.concatenate([q, halo_ref[...]], axis=0)
            n = CHUNK + POOL_HALO
            sums = []
            k = 1
            while k < POOL_WINDOWS[-1]:
                s = s + pltpu.roll(s, n - k, 0)
                k *= 2
                if k in POOL_WINDOWS:
                    sums.append(s[:CHUNK])
            du = _by_group(g, sums) - dpool
            rows = r0 + _iota((CHUNK, 1), 0)
            du_ref[pl.ds(r0, CHUNK), :] = jnp.where(rows >= pad, du, 0.0).astype(BF16)
            halo_ref[...] = q[:POOL_HALO]
            return carry

        lax.fori_loop(0, nc, chunk, 0)

    col = pl.BlockSpec((T, Cg), lambda g: (0, g))
    return pl.pallas_call(
        body, name=name, grid=(G,),
        in_specs=[col, pl.BlockSpec((1, Cg, Cg), lambda g: (g, 0, 0)), pl.BlockSpec((1, Cg), lambda g: (0, g)), col],
        out_specs=(col, pl.BlockSpec((1, Cg, Cg), lambda g: (g, 0, 0)), pl.BlockSpec((1, Cg), lambda g: (0, g))),
        out_shape=(jax.ShapeDtypeStruct((T, C), BF16), jax.ShapeDtypeStruct((G, Cg, Cg), F32), jax.ShapeDtypeStruct((1, C), F32)),
        scratch_shapes=[pltpu.VMEM((POOL_HALO, Cg), F32)],
        compiler_params=_cp("parallel"),
    )(pooled, wg, scale, dy)


def _conv_pre(x_ref, w_ref, b_ref, c, r0):
    h0 = pl.multiple_of(jnp.maximum(r0 - CONV_HALO, 0), 8)
    halo = jnp.where(c > 0, x_ref[pl.ds(h0, CONV_HALO), :], 0.0)
    xe = jnp.concatenate([halo, x_ref[pl.ds(r0, CHUNK), :]], axis=0)
    y = jnp.broadcast_to(b_ref[...], (CHUNK, xe.shape[1]))
    for k in range(CONV_WIDTH):
        shift = CONV_WIDTH - 1 - k
        xs = xe if shift == 0 else pltpu.roll(xe, shift, 0)
        y = y + xs[CONV_HALO:] * w_ref[k:k + 1, :]
    return y, xe


def _conv_fwd(x, w, b, name):
    T = x.shape[0]
    C = w.shape[1]
    tc = _tile(C, 256, 128)
    nc = T // CHUNK

    def body(x_ref, w_ref, b_ref, o_ref):
        def chunk(c, carry):
            r0 = pl.multiple_of(c * CHUNK, CHUNK)
            y, _ = _conv_pre(x_ref, w_ref, b_ref, c, r0)
            o_ref[pl.ds(r0, CHUNK), :] = y * _sigmoid(y)
            return carry

        lax.fori_loop(0, nc, chunk, 0)

    col = pl.BlockSpec((T, tc), lambda j: (0, j))
    return pl.pallas_call(
        body, name=name, grid=(C // tc,),
        in_specs=[col, pl.BlockSpec((CONV_WIDTH, tc), lambda j: (0, j)), pl.BlockSpec((1, tc), lambda j: (0, j))],
        out_specs=col, out_shape=jax.ShapeDtypeStruct((T, C), F32), compiler_params=_cp("parallel"),
    )(x, w, b)


def _conv_bwd(x, w, b, dact, pad, name):
    T = x.shape[0]
    C = w.shape[1]
    tc = _tile(C, 256, 128)
    nc = T // CHUNK

    def body(x_ref, w_ref, b_ref, da_ref, dx_ref, dw_ref, db_ref, halo_ref):
        halo_ref[...] = jnp.zeros_like(halo_ref)
        dw_ref[...] = jnp.zeros_like(dw_ref)
        db_ref[...] = jnp.zeros_like(db_ref)

        def chunk(i, carry):
            c = nc - 1 - i
            r0 = pl.multiple_of(c * CHUNK, CHUNK)
            y, xe = _conv_pre(x_ref, w_ref, b_ref, c, r0)
            sg = _sigmoid(y)
            dpre = da_ref[pl.ds(r0, CHUNK), :] * (sg * (1.0 + y * (1.0 - sg)))
            db_ref[...] += jnp.sum(dpre, axis=0, keepdims=True)
            de = jnp.concatenate([dpre, halo_ref[...]], axis=0)
            n = CHUNK + CONV_HALO
            dx = jnp.zeros_like(dpre)
            for k in range(CONV_WIDTH):
                shift = CONV_WIDTH - 1 - k
                xs = xe if shift == 0 else pltpu.roll(xe, shift, 0)
                dw_ref[k:k + 1, :] += jnp.sum(dpre * xs[CONV_HALO:], axis=0, keepdims=True)
                ds = de if shift == 0 else pltpu.roll(de, n - shift, 0)
                dx = dx + ds[:CHUNK] * w_ref[k:k + 1, :]
            rows = r0 + _iota((CHUNK, 1), 0)
            dx_ref[pl.ds(r0, CHUNK), :] = jnp.where(rows >= pad, dx, 0.0).astype(BF16)
            halo_ref[...] = dpre[:CONV_HALO]
            return carry

        lax.fori_loop(0, nc, chunk, 0)

    col = pl.BlockSpec((T, tc), lambda j: (0, j))
    wspec = pl.BlockSpec((CONV_WIDTH, tc), lambda j: (0, j))
    bspec = pl.BlockSpec((1, tc), lambda j: (0, j))
    return pl.pallas_call(
        body, name=name, grid=(C // tc,),
        in_specs=[col, wspec, bspec, col], out_specs=(col, wspec, bspec),
        out_shape=(jax.ShapeDtypeStruct((T, C), BF16), jax.ShapeDtypeStruct((CONV_WIDTH, C), F32), jax.ShapeDtypeStruct((1, C), F32)),
        scratch_shapes=[pltpu.VMEM((CONV_HALO, tc), F32)],
        compiler_params=_cp("parallel"),
    )(x, w, b, dact)


def _cumsum_rows(x, reverse=False):
    n = x.shape[0]
    idx = _iota(x.shape, 0)
    k = 1
    while k < n:
        if reverse:
            x = x + jnp.where(idx < n - k, pltpu.roll(x, n - k, 0), 0.0)
        else:
            x = x + jnp.where(idx >= k, pltpu.roll(x, k, 0), 0.0)
        k *= 2
    return x


def _softplus(x):
    return jnp.maximum(x, 0.0) + jnp.log1p(jnp.exp(-jnp.abs(x)))


def _head_selector(n_heads, width):
    lane = jnp.arange(n_heads * width)[None, :] // width
    return (lane == jnp.arange(LANES)[:, None]).astype(BF16)


def _dot_exact(x, sel, parts, mode="nn"):
    acc = None
    for _ in range(parts):
        piece = x.astype(BF16)
        x = x - piece.astype(F32)
        t = lax.dot_general(piece, sel, _DIMS[mode], preferred_element_type=F32)
        acc = t if acc is None else acc + t
    return acc


def _ssd_decays(dt, cs, sel_p_ref, sel_q_ref):
    cs_b = _dot_exact(cs, sel_q_ref[...], 3)
    dt_x = _dot_exact(dt, sel_p_ref[...], 3)
    cs_x = _dot_exact(cs, sel_p_ref[...], 3)
    return cs_b, dt_x, jnp.exp(cs_x), jnp.exp(cs_x[CHUNK - 1:CHUNK, :] - cs_x)


def _ssd_common(c, pad, n_heads, dtr_ref, dtb_ref, al_ref):
    rows = c * CHUNK + _iota((CHUNK, 1), 0)
    valid = rows >= pad
    live = jnp.logical_and(valid, _iota((1, LANES), 1) < n_heads)
    pre = dtr_ref[...] + dtb_ref[...]
    dt = jnp.where(live, _softplus(pre), 0.0)
    a = -jnp.exp(al_ref[...])
    cs = _cumsum_rows(dt * a)
    return valid, live, dt, a, cs, cs.T, _sigmoid(pre)


def _ssd_specs(T, DI, GN, cfirst):
    xcol = DI // GN

    def at(col):
        return lambda c: (cfirst(c), col)

    x = pl.BlockSpec((CHUNK, DI), at(0))
    b = pl.BlockSpec((CHUNK, GN), at(xcol))
    cm = pl.BlockSpec((CHUNK, GN), at(xcol + 1))
    dt = pl.BlockSpec((CHUNK, LANES), at(0))
    vec = pl.BlockSpec((1, LANES), lambda c: (0, 0))
    nw = pl.BlockSpec((1, DI), lambda c: (0, 0))
    return x, b, cm, dt, vec, nw


def _ssd_fwd(xbc, pdt, pz, dt_bias, a_log, d_skip, norm_w, pad, n_heads, name):
    T = xbc.shape[0]
    DI = pz.shape[1]
    P = DI // n_heads
    GN = (xbc.shape[1] - DI) // 2
    G = GN // D_STATE
    R = n_heads // G
    GW = R * P
    nc = T // CHUNK
    Q, N = CHUNK, D_STATE

    def body(x_ref, b_ref, c_ref, dtr_ref, z_ref, dtb_ref, al_ref, dsk_ref, nw_ref, sel_p_ref, sel_q_ref,
             y_ref, yn_ref, prev_ref, s_ref):
        c = pl.program_id(0)

        @pl.when(c == 0)
        def _():
            s_ref[...] = jnp.zeros_like(s_ref)

        valid, _, dt, _, cs, cst, _ = _ssd_common(c, pad, n_heads, dtr_ref, dtb_ref, al_ref)
        cs_b, dt_x, e_x, dec_x = _ssd_decays(dt, cs, sel_p_ref, sel_q_ref)
        e_last = jnp.exp(cs[Q - 1:Q, :])
        tri = _iota((Q, Q), 0) >= _iota((Q, Q), 1)
        state_rows = _iota((GW, 1), 0)
        head_lane = _iota((1, GW), 1)
        for g in range(G):
            gs = slice(g * GW, (g + 1) * GW)
            bg = jnp.where(valid, b_ref[:, g * N:(g + 1) * N], 0.0).astype(BF16)
            cg = jnp.where(valid, c_ref[:, g * N:(g + 1) * N], 0.0).astype(BF16)
            xg = jnp.where(valid, x_ref[:, gs], 0.0)
            sg = s_ref[gs, :]
            prev_ref[0, gs, :] = sg
            cb = _dot(cg, bg, "nt")
            xdt = xg * dt_x[:, gs]
            yg = _dot(cg, sg, "nt") * e_x[:, gs]
            for r in range(R):
                h = g * R + r
                lmat = jnp.exp(jnp.where(tri, cs_b[:, h * Q:(h + 1) * Q] - cst[h:h + 1, :], -jnp.inf))
                in_head = jnp.logical_and(head_lane >= r * P, head_lane < (r + 1) * P)
                yg = yg + _dot(cb * lmat, jnp.where(in_head, xdt, 0.0))
            y_ref[:, gs] = yg
            decay = _by_row_block(state_rows, [e_last[:, g * R + r:g * R + r + 1] for r in range(R)], P)
            s_ref[gs, :] = sg * decay + _dot(xdt * dec_x[:, gs], bg, "tn")
            z = z_ref[:, gs]
            gz = (yg + xg * dsk_ref[:, gs]) * (z * _sigmoid(z))
            rstd = lax.rsqrt(jnp.mean(gz * gz, axis=-1, keepdims=True) + EPS)
            yn_ref[:, gs] = ((gz * rstd) * nw_ref[:, gs]).astype(BF16)

    x_s, b_s, c_s, dt_s, vec, nw = _ssd_specs(T, DI, GN, lambda c: c)
    wide = pl.BlockSpec((Q, DI), lambda c: (c, 0))
    sel_p = pl.BlockSpec((LANES, DI), lambda c: (0, 0))
    sel_q = pl.BlockSpec((LANES, n_heads * Q), lambda c: (0, 0))
    return pl.pallas_call(
        body, name=name, grid=(nc,),
        in_specs=[x_s, b_s, c_s, dt_s, wide, vec, vec, nw, nw, sel_p, sel_q],
        out_specs=(wide, wide, pl.BlockSpec((1, DI, N), lambda c: (c, 0, 0))),
        out_shape=(jax.ShapeDtypeStruct((T, DI), F32), jax.ShapeDtypeStruct((T, DI), BF16), jax.ShapeDtypeStruct((nc, DI, N), F32)),
        scratch_shapes=[pltpu.VMEM((DI, N), F32)],
        compiler_params=_cp("arbitrary"),
    )(xbc, xbc, xbc, pdt, pz, dt_bias, a_log, jnp.repeat(d_skip[:, :n_heads], P, axis=1), norm_w,
      _head_selector(n_heads, P), _head_selector(n_heads, Q))


def _ssd_bwd(xbc, pdt, pz, dt_bias, a_log, d_skip, norm_w, y, prev, dyn, pad, n_heads, name):
    T, W = xbc.shape
    DI = pz.shape[1]
    P = DI // n_heads
    GN = (W - DI) // 2
    G = GN // D_STATE
    R = n_heads // G
    GW = R * P
    nc = T // CHUNK
    Q, N = CHUNK, D_STATE

    def body(x_ref, b_ref, c_ref, dtr_ref, z_ref, dtb_ref, al_ref, dsk_ref, nw_ref, y_ref, prev_ref, next_ref, dyn_ref,
             sel_p_ref, sel_q_ref, sel_pt_ref,
             dxbc_ref, dz_ref, ddt_ref, ddtb_ref, dal_ref, ddsk_ref, dnw_ref, ds_ref):
        i = pl.program_id(0)
        c = nc - 1 - i

        @pl.when(i == 0)
        def _():
            ds_ref[...] = jnp.zeros_like(ds_ref)
            ddtb_ref[...] = jnp.zeros_like(ddtb_ref)
            dal_ref[...] = jnp.zeros_like(dal_ref)
            ddsk_ref[...] = jnp.zeros_like(ddsk_ref)
            dnw_ref[...] = jnp.zeros_like(dnw_ref)

        valid, live, dt, a, cs, cst, sig_pre = _ssd_common(c, pad, n_heads, dtr_ref, dtb_ref, al_ref)
        cs_b, dt_x, e_x, dec_x = _ssd_decays(dt, cs, sel_p_ref, sel_q_ref)
        e_last = jnp.exp(cs[Q - 1:Q, :])
        tri = _iota((Q, Q), 0) >= _iota((Q, Q), 1)
        tri_t = _iota((Q, Q), 0) <= _iota((Q, Q), 1)
        state_rows = _iota((GW, 1), 0)
        head_lane = _iota((1, GW), 1)
        lane = _iota((1, LANES), 1)
        head_rows = _iota((LANES, 1), 0)
        s_dy_cs = jnp.zeros((Q, LANES), F32)
        s_x_bds = jnp.zeros((Q, LANES), F32)
        s_x_dxdt = jnp.zeros((Q, LANES), F32)
        dcs_rows = jnp.zeros((Q, LANES), F32)
        dcs_cols = jnp.zeros((LANES, Q), F32)
        c_end = jnp.zeros((1, LANES), F32)
        dsk_rows = []
        for g in range(G):
            gs = slice(g * GW, (g + 1) * GW)
            bg = jnp.where(valid, b_ref[:, g * N:(g + 1) * N], 0.0).astype(BF16)
            cg = jnp.where(valid, c_ref[:, g * N:(g + 1) * N], 0.0).astype(BF16)
            xg = jnp.where(valid, x_ref[:, gs], 0.0)
            s_prev = prev_ref[0, gs, :]
            dsg = ds_ref[gs, :]
            end = dsg * next_ref[0, gs, :]
            yg = y_ref[:, gs]
            dsk = dsk_ref[:, gs]
            ytot = yg + xg * dsk
            z = z_ref[:, gs]
            sz = _sigmoid(z)
            silu = z * sz
            gz = ytot * silu
            rstd = lax.rsqrt(jnp.mean(gz * gz, axis=-1, keepdims=True) + EPS)
            gn = gz * rstd
            dyn_g = dyn_ref[:, gs]
            dnw_ref[:, gs] += jnp.sum(dyn_g * gn, axis=0, keepdims=True)
            dgn = dyn_g * nw_ref[:, gs]
            dgz = rstd * (dgn - gn * jnp.mean(dgn * gn, axis=-1, keepdims=True))
            dz_ref[:, gs] = (dgz * ytot * (sz * (1.0 + z * (1.0 - sz)))).astype(BF16)
            dy = dgz * silu
            dsk_rows.append(jnp.sum(dy * xg, axis=0, keepdims=True))
            cb = _dot(cg, bg, "nt")
            cb_t = _dot(bg, cg, "nt")
            bds = _dot(bg, dsg, "nt") * dec_x[:, gs]
            csg = _dot(cg, s_prev, "nt")
            xdt = xg * dt_x[:, gs]
            dxdt = bds
            dcb = jnp.zeros((Q, Q), F32)
            dcb_t = jnp.zeros((Q, Q), F32)
            for r in range(R):
                h = g * R + r
                cs_col = cs_b[:, h * Q:(h + 1) * Q]
                cs_row = cst[h:h + 1, :]
                lmat = jnp.exp(jnp.where(tri, cs_col - cs_row, -jnp.inf))
                lmat_t = jnp.exp(jnp.where(tri_t, cs_row - cs_col, -jnp.inf))
                in_head = jnp.logical_and(head_lane >= r * P, head_lane < (r + 1) * P)
                dyr = jnp.where(in_head, dy, 0.0)
                dm = _dot(dyr, xdt, "nt")
                dcb = dcb + dm * lmat
                dcb_t = dcb_t + _dot(xdt, dyr, "nt") * lmat_t
                w_rc = dm * (cb * lmat)
                dcs_rows = jnp.where(lane == h, jnp.sum(w_rc, axis=1, keepdims=True), dcs_rows)
                dcs_cols = jnp.where(head_rows == h, jnp.sum(w_rc, axis=0, keepdims=True), dcs_cols)
                dxdt = dxdt + _dot(cb_t * lmat_t, dyr)
            sel_t = sel_pt_ref[gs, :]
            s_dy_cs = s_dy_cs + _dot_exact(dy * csg, sel_t, 2)
            s_x_bds = s_x_bds + _dot_exact(xg * bds, sel_t, 2)
            s_x_dxdt = s_x_dxdt + _dot_exact(xg * dxdt, sel_t, 2)
            c_end = c_end + jnp.sum(jnp.sum(end, axis=1, keepdims=True) * sel_t.astype(F32), axis=0, keepdims=True)
            dye = dy * e_x[:, gs]
            dc = _dot(dcb, bg) + _dot(dye, s_prev)
            db = _dot(dcb_t, cg) + _dot(xdt * dec_x[:, gs], dsg)
            decay = _by_row_block(state_rows, [e_last[:, g * R + r:g * R + r + 1] for r in range(R)], P)
            ds_ref[gs, :] = dsg * decay + _dot(dye, cg, "tn")
            dxbc_ref[:, gs] = jnp.where(valid, dxdt * dt_x[:, gs] + dy * dsk, 0.0)
            dxbc_ref[:, DI + g * N:DI + (g + 1) * N] = jnp.where(valid, db, 0.0)
            dxbc_ref[:, DI + GN + g * N:DI + GN + (g + 1) * N] = jnp.where(valid, dc, 0.0)
        dcs = s_dy_cs * jnp.exp(cs) - dt * s_x_bds
        da_cs = _cumsum_rows(dcs + (dcs_rows - dcs_cols.T), reverse=True) + c_end
        ddt_all = jnp.where(live, da_cs * a + s_x_dxdt, 0.0)
        ddt_raw = ddt_all * sig_pre
        ddt_ref[...] = ddt_raw.astype(BF16)
        ddtb_ref[...] += jnp.sum(ddt_raw, axis=0, keepdims=True)
        dal_ref[...] += jnp.sum(da_cs * dt, axis=0, keepdims=True) * a
        dsk_all = jnp.broadcast_to(jnp.concatenate(dsk_rows, axis=1), (8, DI))
        ddsk_ref[...] += _dot_exact(dsk_all, sel_pt_ref[...], 3)[0:1]

    rev = lambda i: nc - 1 - i
    x_s, b_s, c_s, dt_s, vec, nw = _ssd_specs(T, DI, GN, rev)
    wide = pl.BlockSpec((Q, DI), lambda i: (rev(i), 0))
    st = pl.BlockSpec((1, DI, N), lambda i: (rev(i), 0, 0))
    st_next = pl.BlockSpec((1, DI, N), lambda i: (jnp.minimum(rev(i) + 1, nc - 1), 0, 0))
    sel_p = pl.BlockSpec((LANES, DI), lambda i: (0, 0))
    sel_q = pl.BlockSpec((LANES, n_heads * Q), lambda i: (0, 0))
    sel_pt = pl.BlockSpec((DI, LANES), lambda i: (0, 0))
    sel = _head_selector(n_heads, P)
    return pl.pallas_call(
        body, name=name, grid=(nc,),
        in_specs=[x_s, b_s, c_s, dt_s, wide, vec, vec, nw, nw, wide, st, st_next, wide, sel_p, sel_q, sel_pt],
        out_specs=(pl.BlockSpec((Q, W), lambda i: (rev(i), 0)), wide, dt_s, vec, vec, vec, nw),
        out_shape=(jax.ShapeDtypeStruct((T, W), F32), jax.ShapeDtypeStruct((T, DI), BF16), jax.ShapeDtypeStruct((T, LANES), BF16),
                   jax.ShapeDtypeStruct((1, LANES), F32), jax.ShapeDtypeStruct((1, LANES), F32),
                   jax.ShapeDtypeStruct((1, LANES), F32), jax.ShapeDtypeStruct((1, DI), F32)),
        scratch_shapes=[pltpu.VMEM((DI, N), F32)],
        compiler_params=_cp("arbitrary"),
    )(xbc, xbc, xbc, pdt, pz, dt_bias, a_log, jnp.repeat(d_skip[:, :n_heads], P, axis=1), norm_w, y, prev, prev, dyn,
      sel, _head_selector(n_heads, Q), sel.T)


def _adamw(w, g, m, v, name):
    rows, cols = w.shape
    tr = _tile(rows, 256, 8)

    def body(w_ref, g_ref, m_ref, v_ref, d_ref, nm_ref, nv_ref):
        g = g_ref[...]
        m = ADAM_B1 * m_ref[...] + (1.0 - ADAM_B1) * g
        v = ADAM_B2 * v_ref[...] + (1.0 - ADAM_B2) * (g * g)
        m_hat = m / (1.0 - ADAM_B1 ** ADAM_STEP)
        v_hat = v / (1.0 - ADAM_B2 ** ADAM_STEP)
        d_ref[...] = -ADAM_LR * (m_hat / (jnp.sqrt(v_hat) + ADAM_EPS) + ADAM_WD * w_ref[...])
        nm_ref[...] = m
        nv_ref[...] = v

    blk = pl.BlockSpec((tr, cols), lambda i: (i, 0))
    out = jax.ShapeDtypeStruct((rows, cols), F32)
    return pl.pallas_call(
        body, name=name, grid=(rows // tr,), in_specs=[blk] * 4, out_specs=(blk,) * 3, out_shape=(out,) * 3,
        compiler_params=_cp("parallel"),
    )(w, g, m, v)


def _sum_stack(parts, name, out_dtype=F32):
    rows, cols = parts[0].shape
    tr = _tile(rows, 512, 16)

    def body(*refs):
        acc = refs[0][...].astype(F32)
        for r in refs[1:-1]:
            acc = acc + r[...].astype(F32)
        refs[-1][...] = acc.astype(out_dtype)

    blk = pl.BlockSpec((tr, cols), lambda i: (i, 0))
    return pl.pallas_call(
        body, name=name, grid=(rows // tr,), in_specs=[blk] * len(parts), out_specs=blk,
        out_shape=jax.ShapeDtypeStruct((rows, cols), out_dtype), compiler_params=_cp("parallel"),
    )(*parts)


def _place():
    return lax.axis_index("x"), lax.axis_index("y"), lax.axis_index("c")


def _other_chips(x, y):
    return [(1 - x, y), (x, 1 - y), (1 - x, 1 - y)]


def _all_gather(shards, name):
    nb = len(shards)

    def body(*refs):
        ins, outs = refs[:nb], refs[nb:2 * nb]
        send_sems, recv_sems, local_sems = refs[2 * nb:]
        x, y, c = _place()
        me, sibling = (x, y, c), (x, y, 1 - c)
        chips = _other_chips(x, y)

        def copy(q, k, block, to, src=None):
            dst = outs[q].at[4 * block[0] + 2 * block[1] + block[2]]
            return pltpu.make_async_remote_copy(
                src_ref=dst if src is None else src, dst_ref=dst,
                send_sem=send_sems.at[7 * q + k], recv_sem=recv_sems.at[7 * q + k], device_id=to, device_id_type=MESH)

        started = []
        for q in range(nb):
            mine = pltpu.make_async_copy(ins[q], outs[q].at[4 * x + 2 * y + c], local_sems.at[q])
            mine.start()
            started.append(mine)
        first = []
        for q in range(nb):
            first.append(copy(q, 0, me, sibling, src=ins[q]))
            first += [copy(q, 1 + j, me, (*chip, c), src=ins[q]) for j, chip in enumerate(chips)]
        for cp in first:
            cp.start()
        passed = []
        for j, chip in enumerate(chips):
            for q in range(nb):
                copy(q, 1 + j, (*chip, c), me).wait_recv()
                fwd = copy(q, 4 + j, (*chip, c), sibling)
                fwd.start()
                passed.append(fwd)
        for q in range(nb):
            copy(q, 0, sibling, me).wait_recv()
            for j, chip in enumerate(chips):
                copy(q, 4 + j, (*chip, 1 - c), me).wait_recv()
        for cp in first + passed:
            cp.wait_send()
        for mine in started:
            mine.wait()

    return pl.pallas_call(
        body, name=name, in_specs=[ANY] * nb, out_specs=tuple([ANY] * nb),
        out_shape=tuple(jax.ShapeDtypeStruct((N_DEV,) + s.shape, s.dtype) for s in shards),
        scratch_shapes=[pltpu.SemaphoreType.DMA((7 * nb,)), pltpu.SemaphoreType.DMA((7 * nb,)), pltpu.SemaphoreType.DMA((nb,))],
    )(*shards)


HBM = pl.BlockSpec(memory_space=pltpu.HBM)
SEM = pl.BlockSpec(memory_space=pltpu.SEMAPHORE)
N_PEERS = N_DEV - 1


def _peer(k, x, y, c):
    return (1 - x if k & 4 else x, 1 - y if k & 2 else y, 1 - c if k & 1 else c)


def _direct_copies(gather, srcs, lands, send_sems, recv_sems):
    x, y, c = _place()
    copies = []
    for q in range(len(srcs)):
        for k in range(1, N_DEV):
            px, py, pc = _peer(k, x, y, c)
            if gather:
                src, dst = srcs[q], lands[q].at[4 * x + 2 * y + c]
            else:
                src, dst = srcs[q].at[4 * px + 2 * py + pc], lands[q].at[k - 1]
            copies.append(pltpu.make_async_remote_copy(
                src_ref=src, dst_ref=dst, send_sem=send_sems.at[N_PEERS * q + k - 1], recv_sem=recv_sems.at[N_PEERS * q + k - 1],
                device_id=(px, py, pc), device_id_type=MESH))
    return copies


def _exchange_start(gather, srcs, lands, name):
    n = len(srcs)

    def body(*refs):
        send_sems, recv_sems = refs[2 * n], refs[2 * n + 1]
        for cp in _direct_copies(gather, refs[:n], refs[n:2 * n], send_sems, recv_sems):
            cp.start()
        refs[-1][...] = jnp.zeros_like(refs[-1])

    held = [pltpu.with_memory_space_constraint(t, pltpu.HBM) for t in list(srcs) + list(lands)]
    out = pl.pallas_call(
        body, name=name,
        out_shape=(pltpu.SemaphoreType.DMA((N_PEERS * n,)), pltpu.SemaphoreType.DMA((N_PEERS * n,)),
                   *[pltpu.HBM(t.shape, t.dtype) for t in held], jax.ShapeDtypeStruct((8, LANES), F32)),
        in_specs=[HBM] * (2 * n), out_specs=(SEM, SEM, *[HBM] * (2 * n), pl.BlockSpec(memory_space=pltpu.VMEM)),
        input_output_aliases={i: 2 + i for i in range(2 * n)},
        compiler_params=pltpu.CompilerParams(has_side_effects=pltpu.SideEffectType.DATAFLOW_SIDE_EFFECTING),
    )(*held)
    return out[0], out[1], list(out[2:2 + 2 * n]), out[-1]


def _exchange_wait(gather, started, after, name):
    send_sems, recv_sems, held, _ = started
    n = len(held) // 2

    def body(*refs):
        for cp in _direct_copies(gather, refs[:n], refs[n:2 * n], refs[2 * n], refs[2 * n + 1]):
            cp.wait_send()
            cp.wait_recv()

    out = pl.pallas_call(
        body, name=name, out_shape=tuple(pltpu.HBM(t.shape, t.dtype) for t in held),
        in_specs=[HBM] * (2 * n) + [SEM, SEM, pl.BlockSpec(memory_space=pl.ANY)], out_specs=tuple([HBM] * (2 * n)),
        input_output_aliases={i: i for i in range(2 * n)},
        compiler_params=pltpu.CompilerParams(has_side_effects=pltpu.SideEffectType.DATAFLOW_SIDE_EFFECTING),
    )(*held, send_sems, recv_sems, after)
    return list(out[n:])


def _own_plus_received(stack, got, name):
    _, rows, cols = stack.shape
    tr = _tile(rows, 512, 16)

    def body(*refs):
        acc = refs[0][0]
        for r in refs[1:-1]:
            acc = acc + r[0].astype(F32)
        refs[-1][...] = acc

    def mine(i):
        x, y, c = _place()
        return (4 * x + 2 * y + c, i, 0)

    return pl.pallas_call(
        body, name=name, grid=(rows // tr,),
        in_specs=[pl.BlockSpec((1, tr, cols), mine)]
        + [pl.BlockSpec((1, tr, cols), functools.partial(lambda k, i: (k, i, 0), k)) for k in range(N_PEERS)],
        out_specs=pl.BlockSpec((tr, cols), lambda i: (i, 0)),
        out_shape=jax.ShapeDtypeStruct((rows, cols), F32), compiler_params=_cp("parallel"),
    )(stack, *[got] * N_PEERS)


class _Shard:
    def __init__(self, name, axis):
        self.name, self.axis = name, axis


BIG = [_Shard("w_in", 2), _Shard("pool_w_group", 2), _Shard("w_pool_up", 1), _Shard("w_ssd_out", 1),
       _Shard("w_o", 1), _Shard("w_ff1", 2), _Shard("w_ff2", 1)]
SMALL_SHARDED = [_Shard("meta_tokens", 1), _Shard("conv_w", 2)]
REPLICATED = ["mix_norm_w", "b_gate", "pool_scale", "conv_b", "dt_bias", "a_log", "d_skip", "ssd_norm_w",
              "mlp_norm_w", "final_norm_w"]
WEIGHTS = ["meta_tokens", "mix_norm_w", "w_in", "b_gate", "pool_w_group", "pool_scale", "w_pool_up", "conv_w", "conv_b",
           "dt_bias", "a_log", "d_skip", "ssd_norm_w", "w_ssd_out", "w_o", "mlp_norm_w", "w_ff1", "w_ff2", "final_norm_w"]


def _rows2(a):
    return a.reshape(-1, a.shape[-1])


def _unshard(stack, shard_shape, axis):
    t = stack.reshape((N_DEV,) + tuple(shard_shape))
    return jnp.concatenate([t[d] for d in range(N_DEV)], axis=axis)


def _reshard(layers, axis):
    cut = [jnp.split(t, N_DEV, axis=axis - 1) for t in layers]
    blocks = [jnp.concatenate([_rows2(pieces[d]) for pieces in cut], axis=0) for d in range(N_DEV)]
    return jnp.stack(blocks).reshape((4, 2) + blocks[0].shape)


def _lane_rows(a):
    n = a.size
    tile = 8 * LANES
    if n % tile:
        return jnp.pad(a.reshape(-1), (0, (-n) % tile)).reshape(-1, LANES)
    return a.reshape(-1, LANES)


def _unpack_small(buf, spans, shapes):
    out = []
    for (o, r), shp in zip(spans, shapes):
        n = 1
        for d in shp:
            n *= d
        t = buf[o:o + r]
        out.append(t.reshape(shp) if n == r * LANES else t.reshape(-1)[:n].reshape(shp))
    return out


def _pack_small(parts, mult):
    mats = [_lane_rows(p) for p in parts]
    spans, o = [], 0
    for t in mats:
        spans.append((o, t.shape[0]))
        o += t.shape[0]
    fill = (-o) % mult
    if fill:
        mats.append(jnp.zeros((fill, LANES), mats[0].dtype))
    return jnp.concatenate(mats, axis=0), spans


def _layer_fwd(h, lw, cfg, tag, traffic, i, deps):
    pad, n_heads = cfg["pad"], cfg["n_heads"]
    u = _rms_fwd(h, lw["mix_norm_w"], f"rms_mix_{tag}", dep=deps[0])
    p_xbc = _mm(u, lw["w_xbc"], "nn", f"proj_xbc_{tag}", dep=deps[1])
    p_z = _mm(u, lw["w_z"], "nn", f"proj_z_{tag}")
    p_gate = _mm(u, lw["w_gate"], "nn", f"proj_gate_{tag}")
    p_pool = _mm(u, lw["w_pool"], "nn", f"proj_pool_{tag}")
    p_dt = _mm(u, lw["w_dt"], "nn", f"proj_dt_{tag}")
    lw.update(traffic.fwd_late(i, p_dt))
    pooled, y1 = _pool_fwd(p_pool, lw["pool_w_group"], lw["pool_scale"], pad, f"pool_fwd_{tag}")
    y_pool = _mm(y1, lw["w_pool_up"], "nn", f"pool_up_{tag}")
    xbc = _conv_fwd(p_xbc, lw["conv_w"], lw["conv_b"], f"conv_fwd_{tag}")
    y, yn, prev = _ssd_fwd(xbc, p_dt, p_z, lw["dt_bias"], lw["a_log"], lw["d_skip"], lw["ssd_norm_w"], pad, n_heads, f"ssd_fwd_{tag}")
    y_ssd = _mm(yn, lw["w_ssd_out"], "nn", f"ssd_out_{tag}")
    mix = _gate_fwd(p_gate, lw["b_gate"], y_pool, y_ssd, f"gate_fwd_{tag}")
    h_mid = _mm(mix, lw["w_o"], "nn", f"mix_out_{tag}", res=h)
    v = _rms_fwd(h_mid, lw["mlp_norm_w"], f"rms_mlp_{tag}")
    hid, act = _mm(v, lw["w_ff1"], "nn", f"ff1_{tag}", epi="relu2")
    h_out = _mm(act, lw["w_ff2"], "nn", f"ff2_{tag}", res=h_mid)
    saved = dict(h=h, u=u, p_xbc=p_xbc, p_z=p_z, p_gate=p_gate, p_dt=p_dt, pooled=pooled, y1=y1, y_pool=y_pool, xbc=xbc,
                 y=y, yn=yn, prev=prev, y_ssd=y_ssd, mix=mix, h_mid=h_mid, v=v, hid=hid, act=act)
    return h_out, saved


def _layer_bwd(dh, dh_b, lw, s, cfg, tag, traffic, i):
    pad, n_heads = cfg["pad"], cfg["n_heads"]
    g = {}
    dhid = _mm(dh_b, lw["w_ff2"], "nt", f"d_act_{tag}", epi="drelu2", aux=s["hid"], dep=traffic.bwd_begin(i))
    g["w_ff2"] = _mm(s["act"], dh_b, "tn", f"dw_ff2_{tag}")
    dv = _mm(dhid, lw["w_ff1"], "nt", f"d_v_{tag}")
    g["w_ff1"] = _mm(s["v"], dhid, "tn", f"dw_ff1_{tag}")
    dep = traffic.grads_ready(i, {k: g[k] for k in ("w_ff1", "w_ff2")}, "mlp")
    dh_mid, dh_mid_b, g["mlp_norm_w"] = _rms_bwd(s["h_mid"], lw["mlp_norm_w"], dv, dh, f"rms_mlp_bwd_{tag}")
    dmix = _mm(dh_mid_b, lw["w_o"], "nt", f"d_mix_{tag}", dep=dep)
    g["w_o"] = _mm(s["mix"], dh_mid_b, "tn", f"dw_o_{tag}")
    dgate, dy_pool, dy_ssd, g["b_gate"] = _gate_bwd(s["p_gate"], lw["b_gate"], s["y_pool"], s["y_ssd"], dmix, f"gate_bwd_{tag}")
    dy1 = _mm(dy_pool, lw["w_pool_up"], "nt", f"d_y1_{tag}")
    g["w_pool_up"] = _mm(s["y1"], dy_pool, "tn", f"dw_pool_up_{tag}")
    dpool, g["pool_w_group"], g["pool_scale"] = _pool_bwd(s["pooled"], lw["pool_w_group"], lw["pool_scale"], dy1, pad, f"pool_bwd_{tag}")
    g["w_ssd_out"] = _mm(s["yn"], dy_ssd, "tn", f"dw_ssd_out_{tag}")
    dep = traffic.grads_ready(i, {k: g[k] for k in ("pool_w_group", "w_pool_up", "w_ssd_out", "w_o")}, "mix")
    dyn = _mm(dy_ssd, lw["w_ssd_out"], "nt", f"d_yn_{tag}", dep=dep)
    dact, dz, ddt, g["dt_bias"], g["a_log"], g["d_skip"], g["ssd_norm_w"] = _ssd_bwd(
        s["xbc"], s["p_dt"], s["p_z"], lw["dt_bias"], lw["a_log"], lw["d_skip"], lw["ssd_norm_w"], s["y"], s["prev"], dyn,
        pad, n_heads, f"ssd_bwd_{tag}")
    dxbc, g["conv_w"], g["conv_b"] = _conv_bwd(s["p_xbc"], lw["conv_w"], lw["conv_b"], dact, pad, f"conv_bwd_{tag}")
    u = s["u"]
    g["w_xbc"] = _mm(u, dxbc, "tn", f"dw_xbc_{tag}")
    g["w_z"] = _mm(u, dz, "tn", f"dw_z_{tag}")
    g["w_gate"] = _mm(u, dgate, "tn", f"dw_gate_{tag}")
    g["w_pool"] = _mm(u, dpool, "tn", f"dw_pool_{tag}")
    g["w_dt"] = _mm(u, ddt, "tn", f"dw_dt_{tag}")
    c_dt = cfg["cols"][3]
    g["w_in"] = jnp.concatenate([g["w_pool"], g["w_z"], g["w_xbc"], g["w_dt"][:, :c_dt], g["w_gate"]], axis=1)
    dep = traffic.grads_ready(i, {"w_in": g["w_in"]}, "in")
    du = _mm(dxbc, lw["w_xbc"], "nt", f"du_xbc_{tag}", dep=dep)
    du = _mm(dz, lw["w_z"], "nt", f"du_z_{tag}", res=du)
    du = _mm(dgate, lw["w_gate"], "nt", f"du_gate_{tag}", res=du)
    du = _mm(dpool, lw["w_pool"], "nt", f"du_pool_{tag}", res=du)
    du = _mm(ddt, lw["w_dt"], "nt", f"du_dt_{tag}", res=du)
    dh_in, dh_in_b, g["mix_norm_w"] = _rms_bwd(s["h"], lw["mix_norm_w"], du, dh_mid, f"rms_mix_bwd_{tag}")
    traffic.bwd_end(i, dh_in)
    return dh_in, dh_in_b, g


def _pad_lanes(v):
    return jnp.pad(v, (0, LANES - v.shape[0])).reshape(1, LANES)


class _WholeWeights:
    def __init__(self, full):
        self.full = full

    def fwd_begin(self, i):
        return {k: self.full[k][i] for k in ("w_in", "conv_w")}, (None, None)

    def fwd_late(self, i, after):
        return {k: t[i] for k, t in self.full.items() if k not in ("w_in", "conv_w")}

    def fwd_end(self, i, h_out):
        pass

    def bwd_begin(self, i):
        return None

    def grads_ready(self, i, grads, tag):
        return None

    def bwd_end(self, i, dh_in):
        pass


def _local_step(x2, target, meta_full, traffic, rep, cfg):
    depth, pad, n_meta, H = cfg["depth"], cfg["pad"], cfg["n_meta"], cfg["n_heads"]
    D = x2.shape[1]
    di = rep["ssd_norm_w"].shape[1]
    c_pool, c_z, c_xbc, c_dt = cfg["cols"]
    h = jnp.concatenate([jnp.zeros((pad, D), F32), meta_full, x2], axis=0)
    lws, saves = [], []
    for i in range(depth):
        full, deps = traffic.fwd_begin(i)
        w_in = full["w_in"]
        o = 0
        w_pool = w_in[:, o:o + c_pool]; o += c_pool
        w_z = w_in[:, o:o + c_z]; o += c_z
        w_xbc = w_in[:, o:o + c_xbc]; o += c_xbc
        w_dt = jnp.pad(w_in[:, o:o + c_dt], ((0, 0), (0, LANES - c_dt))); o += c_dt
        w_gate = w_in[:, o:]
        lw = dict(
            w_pool=w_pool, w_z=w_z, w_xbc=w_xbc, w_dt=w_dt, w_gate=w_gate, conv_w=full["conv_w"],
            mix_norm_w=rep["mix_norm_w"][i], b_gate=rep["b_gate"][i], pool_scale=rep["pool_scale"][i].reshape(1, -1),
            conv_b=rep["conv_b"][i].reshape(1, -1), dt_bias=_pad_lanes(rep["dt_bias"][i]), a_log=_pad_lanes(rep["a_log"][i]),
            d_skip=_pad_lanes(rep["d_skip"][i]), ssd_norm_w=rep["ssd_norm_w"][i].reshape(1, di), mlp_norm_w=rep["mlp_norm_w"][i])
        lws.append(lw)
        h, s = _layer_fwd(h, lw, cfg, f"l{i}", traffic, i, deps)
        saves.append(s)
        traffic.fwd_end(i, h)
    loss, dh, dh_b, g_final = _final_loss(h, rep["final_norm_w"], target, pad + n_meta, "final_loss")
    per_layer = []
    for i in range(depth - 1, -1, -1):
        dh, dh_b, g = _layer_bwd(dh, dh_b, lws[i], saves[i], cfg, f"l{i}", traffic, i)
        per_layer.append(g)
    per_layer.reverse()

    def stack(key, fn=lambda t: t):
        return jnp.stack([fn(g[key]) for g in per_layer])

    def layers(key):
        return [g[key] for g in per_layer]

    grads = dict(
        w_in=layers("w_in"), pool_w_group=layers("pool_w_group"), w_pool_up=layers("w_pool_up"), w_ssd_out=layers("w_ssd_out"), w_o=layers("w_o"),
        w_ff1=layers("w_ff1"), w_ff2=layers("w_ff2"), conv_w=stack("conv_w"),
        mix_norm_w=stack("mix_norm_w", lambda t: t[0]), b_gate=stack("b_gate", lambda t: t[0]),
        pool_scale=stack("pool_scale", lambda t: t[0]), conv_b=stack("conv_b", lambda t: t[0]),
        dt_bias=stack("dt_bias", lambda t: t[0, :H]), a_log=stack("a_log", lambda t: t[0, :H]), d_skip=stack("d_skip", lambda t: t[0, :H]),
        ssd_norm_w=stack("ssd_norm_w", lambda t: t[0]), mlp_norm_w=stack("mlp_norm_w", lambda t: t[0]),
        final_norm_w=g_final[0], meta_tokens=dh[pad:pad + n_meta])
    return loss, dh[pad + n_meta:], grads


class _ShardedWeights:
    def __init__(self, w, me):
        self.w, self.me = w, me
        self.depth = w[BIG[0].name].shape[0]
        first, self.later = BIG[:1], BIG[1:]
        assert first[0].name == "w_in"
        small = [_rows2(w[s.name]) for s in SMALL_SHARDED]
        landed = _all_gather([_rows2(w["w_in"][0]).astype(BF16)] + small, "gather_l0")
        self.small = {s.name: _unshard(t, w[s.name].shape, s.axis) for s, t in zip(SMALL_SHARDED, landed[1:])}
        self.landed = {"w_in": landed[0]}
        self.late0 = self._fetch(self.later, 0, "gather_start_late_l0")
        self.fetching = None
        self.sending, self.token = [], None
        self.local_grads = [{} for _ in range(self.depth)]

    def _fetch(self, shards, i, name):
        mine = [_rows2(self.w[s.name][i]).astype(BF16) for s in shards]
        lands = [lax.dynamic_update_slice(lax.empty((N_DEV,) + t.shape, t.dtype), t[None], (self.me, 0, 0)) for t in mine]
        return _exchange_start(True, mine, lands, name)

    def _whole(self, shards):
        return {s.name: _unshard(self.landed[s.name], self.w[s.name].shape[1:], s.axis - 1) for s in shards}

    def fwd_begin(self, i):
        full = self._whole(BIG[:1])
        full["conv_w"] = self.small["conv_w"][i]
        deps = [None, None]
        if i == 0:
            deps[0] = self.late0[3]
        if i + 1 < self.depth:
            self.fetching = self._fetch(BIG, i + 1, f"gather_start_l{i + 1}")
            deps[1] = self.fetching[3]
        return full, deps

    def fwd_late(self, i, after):
        if i == 0:
            self.landed.update(zip([s.name for s in self.later], _exchange_wait(True, self.late0, after, "gather_wait_late_l0")))
        return self._whole(self.later)

    def fwd_end(self, i, h_out):
        if self.fetching is not None:
            self.landed = dict(zip([s.name for s in BIG], _exchange_wait(True, self.fetching, h_out, f"gather_wait_l{i + 1}")))
            self.fetching = None

    def bwd_begin(self, i):
        return self.token

    def grads_ready(self, i, grads, tag):
        shards = [s for s in BIG if s.name in grads]
        stacks = [_reshard([grads[s.name]], s.axis).reshape((N_DEV,) + _rows2(self.w[s.name][i]).shape) for s in shards]
        sends = [t.astype(BF16) for t in stacks]
        lands = [lax.empty((N_PEERS,) + t.shape[1:], BF16) for t in sends]
        started = _exchange_start(False, sends, lands, f"rs_start_{tag}_l{i}")
        self.sending.append((i, tag, shards, stacks, started))
        self.token = started[3]
        return self.token

    def _collect(self, entry, after):
        i, tag, shards, stacks, started = entry
        got = _exchange_wait(False, started, after, f"rs_wait_{tag}_l{i}")
        for s, st, g in zip(shards, stacks, got):
            self.local_grads[i][s.name] = _own_plus_received(st, g, f"rs_sum_{s.name}_l{i}")

    def bwd_end(self, i, dh_in):
        for entry in [e for e in self.sending if e[0] > i]:
            self._collect(entry, dh_in)
        self.sending = [e for e in self.sending if e[0] <= i]

    def finish(self, after, names):
        for entry in [e for e in self.sending if e[2][0].name in names]:
            self._collect(entry, after)
        self.sending = [e for e in self.sending if e[2][0].name not in names]
        return {k: jnp.concatenate([g[k] for g in self.local_grads], axis=0) for k in names}


def kernel(x, meta_tokens, mix_norm_w, w_in, b_gate, pool_w_group, pool_scale, w_pool_up, conv_w, conv_b, dt_bias, a_log, d_skip, ssd_norm_w, w_ssd_out, w_o, mlp_norm_w, w_ff1, w_ff2, final_norm_w, loss_target, m_meta_tokens, m_mix_norm_w, m_w_in, m_b_gate, m_pool_w_group, m_pool_scale, m_w_pool_up, m_conv_w, m_conv_b, m_dt_bias, m_a_log, m_d_skip, m_ssd_norm_w, m_w_ssd_out, m_w_o, m_mlp_norm_w, m_w_ff1, m_w_ff2, m_final_norm_w, v_meta_tokens, v_mix_norm_w, v_w_in, v_b_gate, v_pool_w_group, v_pool_scale, v_w_pool_up, v_conv_w, v_conv_b, v_dt_bias, v_a_log, v_d_skip, v_ssd_norm_w, v_w_ssd_out, v_w_o, v_mlp_norm_w, v_w_ff1, v_w_ff2, v_final_norm_w):
    w = dict(meta_tokens=meta_tokens, mix_norm_w=mix_norm_w, w_in=w_in, b_gate=b_gate, pool_w_group=pool_w_group,
             pool_scale=pool_scale, w_pool_up=w_pool_up, conv_w=conv_w, conv_b=conv_b, dt_bias=dt_bias, a_log=a_log,
             d_skip=d_skip, ssd_norm_w=ssd_norm_w, w_ssd_out=w_ssd_out, w_o=w_o, mlp_norm_w=mlp_norm_w, w_ff1=w_ff1,
             w_ff2=w_ff2, final_norm_w=final_norm_w)
    m = dict(meta_tokens=m_meta_tokens, mix_norm_w=m_mix_norm_w, w_in=m_w_in, b_gate=m_b_gate, pool_w_group=m_pool_w_group,
             pool_scale=m_pool_scale, w_pool_up=m_w_pool_up, conv_w=m_conv_w, conv_b=m_conv_b, dt_bias=m_dt_bias, a_log=m_a_log,
             d_skip=m_d_skip, ssd_norm_w=m_ssd_norm_w, w_ssd_out=m_w_ssd_out, w_o=m_w_o, mlp_norm_w=m_mlp_norm_w, w_ff1=m_w_ff1,
             w_ff2=m_w_ff2, final_norm_w=m_final_norm_w)
    v = dict(meta_tokens=v_meta_tokens, mix_norm_w=v_mix_norm_w, w_in=v_w_in, b_gate=v_b_gate, pool_w_group=v_pool_w_group,
             pool_scale=v_pool_scale, w_pool_up=v_w_pool_up, conv_w=v_conv_w, conv_b=v_conv_b, dt_bias=v_dt_bias, a_log=v_a_log,
             d_skip=v_d_skip, ssd_norm_w=v_ssd_norm_w, w_ssd_out=v_w_ssd_out, w_o=v_w_o, mlp_norm_w=v_mlp_norm_w, w_ff1=v_w_ff1,
             w_ff2=v_w_ff2, final_norm_w=v_final_norm_w)

    _, seq, D = x.shape
    n_meta = meta_tokens.shape[0]
    depth = w_in.shape[0]
    n_heads = dt_bias.shape[1]
    d_inner = ssd_norm_w.shape[1]
    d_xbc = conv_b.shape[1]
    pool_width = pool_scale.shape[1]
    pad = (-n_meta) % CHUNK
    cfg = dict(depth=depth, pad=pad, n_meta=n_meta, n_heads=n_heads, cols=(pool_width, d_inner, d_xbc, n_heads))
    assert (pad + n_meta + seq) % CHUNK == 0 and pad + n_meta == CHUNK

    xi, yi, ci = _place()
    me = 4 * xi + 2 * yi + ci

    traffic = _ShardedWeights(w, me)
    rep = {k: w[k] for k in REPLICATED}
    loss_part, dx, grads = _local_step(x[0], loss_target[0], traffic.small["meta_tokens"], traffic, rep, cfg)
    loss = lax.psum(loss_part[0, 0], ("x", "y", "c"))

    small_names = REPLICATED + [s.name for s in SMALL_SHARDED]
    sm_buf, sm_spans = _pack_small([grads[k] for k in small_names], 16)
    (sm_all,) = _all_gather([sm_buf], "gather_small_grads")
    sm_sum = _sum_stack([sm_all[d] for d in range(N_DEV)], "small_grads_sum")
    g_small = dict(zip(small_names, _unpack_small(sm_sum, sm_spans, [grads[k].shape for k in small_names])))
    g_loc = {k: g_small[k] for k in REPLICATED}
    for s in SMALL_SHARDED:
        blk = w[s.name].shape[s.axis]
        g_loc[s.name] = lax.dynamic_slice_in_dim(g_small[s.name], me * blk, blk, axis=s.axis)

    delta, new_m, new_v = {}, {}, {}
    loc_shapes = [w[k].shape for k in small_names]
    packed = [_pack_small([t[k] for k in small_names], 8) for t in (w, g_loc, m, v)]
    loc_spans = packed[0][1]
    outs = _adamw(*[p[0] for p in packed], "adamw_small")
    for res, buf in zip((delta, new_m, new_v), outs):
        res.update(zip(small_names, _unpack_small(buf, loc_spans, loc_shapes)))
    after = outs[0]
    for names in ([s.name for s in BIG if s.name != "w_in"], ["w_in"]):
        for k, t in traffic.finish(after, names).items():
            shp = w[k].shape
            d2, m2, v2 = _adamw(_rows2(w[k]), t, _rows2(m[k]), _rows2(v[k]), f"adamw_{k}")
            g_loc[k], delta[k], new_m[k], new_v[k] = t.reshape(shp), d2.reshape(shp), m2.reshape(shp), v2.reshape(shp)
            after = d2

    return (loss, dx[None], *[g_loc[k] for k in WEIGHTS], *[delta[k] for k in WEIGHTS],
            *[new_m[k] for k in WEIGHTS], *[new_v[k] for k in WEIGHTS])
```

```python
import functools

import jax
import jax.numpy as jnp
from jax import lax
from jax.experimental import pallas as pl
from jax.experimental.pallas import tpu as pltpu

F32 = jnp.float32
BF16 = jnp.bfloat16

EPS = 1e-5
D_STATE = 128
CHUNK = 128
LANES = 128
POOL_WINDOWS = (2, 4, 8, 16)
POOL_HALO = 16
CONV_WIDTH = 4
CONV_HALO = 8
N_DEV = 8
ADAM_LR = 0.001
ADAM_B1 = 0.9
ADAM_B2 = 0.999
ADAM_EPS = 1e-08
ADAM_WD = 0.01
ADAM_STEP = 10
VMEM_LIMIT = 52 * 1024 * 1024
MESH = pl.DeviceIdType.MESH
ANY = pl.BlockSpec(memory_space=pl.ANY)


def _cp(*sem):
    return pltpu.CompilerParams(dimension_semantics=sem, vmem_limit_bytes=VMEM_LIMIT)


def _tile(n, target, mult):
    best = None
    for t in range(mult, min(n, target) + 1, mult):
        if n % t == 0:
            best = t
    return best if best is not None else n


def _sigmoid(x):
    return jax.nn.sigmoid(x)


def _iota(shape, dim):
    return lax.broadcasted_iota(jnp.int32, shape, dim)


_DIMS = {"nn": (((1,), (0,)), ((), ())), "nt": (((1,), (1,)), ((), ())), "tn": (((0,), (0,)), ((), ()))}


def _dot(a, b, mode="nn"):
    return lax.dot_general(a.astype(BF16), b.astype(BF16), _DIMS[mode], preferred_element_type=F32)


DEP = pl.BlockSpec((8, LANES), lambda *_: (0, 0))


def _mm(a, b, mode, name, *, out_dtype=F32, res=None, epi=None, aux=None, dep=None):
    if mode == "nn":
        (M, K), (_, N) = a.shape, b.shape
    elif mode == "nt":
        (M, K), (N, _) = a.shape, b.shape
    else:
        (K, M), (_, N) = a.shape, b.shape
    if mode == "tn":
        tm, tn, tk = _tile(M, 512, 128), _tile(N, 512, 128), K
    else:
        tm, tn, tk = _tile(M, 1056, 16), _tile(N, 512, 128), _tile(K, 4096, 128)
    nk = K // tk
    a_spec = pl.BlockSpec((tk, tm), lambda i, j, k: (k, i)) if mode == "tn" else pl.BlockSpec((tm, tk), lambda i, j, k: (i, k))
    b_spec = pl.BlockSpec((tn, tk), lambda i, j, k: (j, k)) if mode == "nt" else pl.BlockSpec((tk, tn), lambda i, j, k: (k, j))
    o_spec = pl.BlockSpec((tm, tn), lambda i, j, k: (i, j))
    extra = [t for t in (res, aux) if t is not None]
    deps = [] if dep is None else [dep]

    def body(*refs):
        a_ref, b_ref = refs[0], refs[1]
        x_ref = refs[2] if extra else None
        outs = refs[2 + len(extra) + len(deps):]
        p = _dot(a_ref[...], b_ref[...], mode)

        def finish(r):
            if res is not None:
                outs[0][...] = (x_ref[...] + r).astype(out_dtype)
            elif epi == "relu2":
                outs[0][...] = r
                hid = jnp.maximum(r, 0.0)
                outs[1][...] = (hid * hid).astype(BF16)
            elif epi == "twin":
                outs[0][...] = r
                outs[1][...] = r.astype(BF16)
            elif epi == "drelu2":
                outs[0][...] = (r * (2.0 * jnp.maximum(x_ref[...], 0.0))).astype(BF16)
            else:
                outs[0][...] = r.astype(out_dtype)

        if nk == 1:
            finish(p)
        else:
            acc = outs[-1]
            k = pl.program_id(2)

            @pl.when(k == 0)
            def _():
                acc[...] = p

            @pl.when(k > 0)
            def _():
                acc[...] += p

            @pl.when(k == nk - 1)
            def _():
                finish(acc[...])

    if epi in ("relu2", "twin"):
        out_shape = (jax.ShapeDtypeStruct((M, N), F32), jax.ShapeDtypeStruct((M, N), BF16))
        out_specs = (o_spec, o_spec)
    elif epi == "drelu2":
        out_shape, out_specs = jax.ShapeDtypeStruct((M, N), BF16), o_spec
    else:
        out_shape, out_specs = jax.ShapeDtypeStruct((M, N), out_dtype), o_spec
    return pl.pallas_call(
        body, name=name, grid=(M // tm, N // tn, nk),
        in_specs=[a_spec, b_spec] + [o_spec] * len(extra) + [DEP] * len(deps),
        out_specs=out_specs, out_shape=out_shape,
        scratch_shapes=[pltpu.VMEM((tm, tn), F32)] if nk > 1 else [],
        compiler_params=_cp("parallel", "parallel", "arbitrary"),
    )(a, b, *extra, *deps)


def _rms_fwd(h, w, name, dep=None):
    T, D = h.shape
    tr = _tile(T, 1056, 16)
    deps = [] if dep is None else [dep]

    def body(h_ref, w_ref, *rest):
        x = h_ref[...]
        xn = x * lax.rsqrt(jnp.mean(x * x, axis=-1, keepdims=True) + EPS)
        rest[-1][...] = (xn * w_ref[...]).astype(BF16)

    return pl.pallas_call(
        body, name=name, grid=(T // tr,),
        in_specs=[pl.BlockSpec((tr, D), lambda i: (i, 0)), pl.BlockSpec((1, D), lambda i: (0, 0))] + [DEP] * len(deps),
        out_specs=pl.BlockSpec((tr, D), lambda i: (i, 0)), out_shape=jax.ShapeDtypeStruct((T, D), BF16),
        compiler_params=_cp("parallel"),
    )(h, w.reshape(1, D), *deps)


def _rms_bwd(h, w, dy, dres, name):
    T, D = h.shape
    tr = _tile(T, 528, 16)

    def body(h_ref, w_ref, dy_ref, dres_ref, dh_ref, dhb_ref, dw_ref):
        x = h_ref[...]
        rstd = lax.rsqrt(jnp.mean(x * x, axis=-1, keepdims=True) + EPS)
        xn = x * rstd
        dy = dy_ref[...]
        dxn = dy * w_ref[...]
        dh = dres_ref[...] + rstd * (dxn - xn * jnp.mean(dxn * xn, axis=-1, keepdims=True))
        dh_ref[...] = dh
        dhb_ref[...] = dh.astype(BF16)
        dw = jnp.sum(dy * xn, axis=0, keepdims=True)

        @pl.when(pl.program_id(0) == 0)
        def _():
            dw_ref[...] = dw

        @pl.when(pl.program_id(0) > 0)
        def _():
            dw_ref[...] += dw

    row = pl.BlockSpec((tr, D), lambda i: (i, 0))
    vec = pl.BlockSpec((1, D), lambda i: (0, 0))
    return pl.pallas_call(
        body, name=name, grid=(T // tr,),
        in_specs=[row, vec, row, row], out_specs=(row, row, vec),
        out_shape=(jax.ShapeDtypeStruct((T, D), F32), jax.ShapeDtypeStruct((T, D), BF16), jax.ShapeDtypeStruct((1, D), F32)),
        compiler_params=_cp("arbitrary"),
    )(h, w.reshape(1, D), dy, dres)


def _final_loss(h, w, target, first_row, name):
    T, D = h.shape
    tr = CHUNK
    assert first_row == tr

    def body(h_ref, w_ref, t_ref, loss_ref, dh_ref, dhb_ref, dw_ref):
        i = pl.program_id(0)
        x = h_ref[...]
        rstd = lax.rsqrt(jnp.mean(x * x, axis=-1, keepdims=True) + EPS)
        xn = x * rstd
        w = w_ref[...]
        live = i > 0
        err = jnp.where(live, xn * w - t_ref[...], 0.0)
        part = 0.5 * jnp.sum(jnp.mean(err * err, axis=-1, keepdims=True), axis=0, keepdims=True)
        dout = err * (1.0 / D)
        dxn = dout * w
        dh = rstd * (dxn - xn * jnp.mean(dxn * xn, axis=-1, keepdims=True))
        dh_ref[...] = dh
        dhb_ref[...] = dh.astype(BF16)
        dw = jnp.sum(dout * xn, axis=0, keepdims=True)

        @pl.when(i == 0)
        def _():
            loss_ref[...] = part
            dw_ref[...] = dw

        @pl.when(i > 0)
        def _():
            loss_ref[...] += part
            dw_ref[...] += dw

    row = pl.BlockSpec((tr, D), lambda i: (i, 0))
    vec = pl.BlockSpec((1, D), lambda i: (0, 0))
    return pl.pallas_call(
        body, name=name, grid=(T // tr,),
        in_specs=[row, vec, pl.BlockSpec((tr, D), lambda i: (jnp.maximum(i - 1, 0), 0))],
        out_specs=(pl.BlockSpec((1, 1), lambda i: (0, 0)), row, row, vec),
        out_shape=(jax.ShapeDtypeStruct((1, 1), F32), jax.ShapeDtypeStruct((T, D), F32), jax.ShapeDtypeStruct((T, D), BF16),
                   jax.ShapeDtypeStruct((1, D), F32)),
        compiler_params=_cp("arbitrary"),
    )(h, w.reshape(1, D), target)


def _gate_fwd(pg, bg, y_pool, y_ssd, name):
    T, D = y_pool.shape
    tr = _tile(T, 528, 16)

    def body(gp_ref, gs_ref, bp_ref, bs_ref, yp_ref, ys_ref, o_ref):
        gp = _sigmoid(gp_ref[...] + bp_ref[...])
        gs = _sigmoid(gs_ref[...] + bs_ref[...])
        o_ref[...] = (gp * yp_ref[...] + gs * ys_ref[...]).astype(BF16)

    row = pl.BlockSpec((tr, D), lambda i: (i, 0))
    row1 = pl.BlockSpec((tr, D), lambda i: (i, 1))
    vec = pl.BlockSpec((1, D), lambda i: (0, 0))
    vec1 = pl.BlockSpec((1, D), lambda i: (0, 1))
    b2 = bg.reshape(1, 2 * D)
    return pl.pallas_call(
        body, name=name, grid=(T // tr,),
        in_specs=[row, row1, vec, vec1, row, row], out_specs=row,
        out_shape=jax.ShapeDtypeStruct((T, D), BF16), compiler_params=_cp("parallel"),
    )(pg, pg, b2, b2, y_pool, y_ssd)


def _gate_bwd(pg, bg, y_pool, y_ssd, dmix, name):
    T, D = y_pool.shape
    tr = _tile(T, 528, 16)

    def body(gp_ref, gs_ref, bp_ref, bs_ref, yp_ref, ys_ref, dm_ref, dg_ref, dyp_ref, dys_ref, db_ref):
        gp = _sigmoid(gp_ref[...] + bp_ref[...])
        gs = _sigmoid(gs_ref[...] + bs_ref[...])
        dm = dm_ref[...]
        dyp_ref[...] = (dm * gp).astype(BF16)
        dys_ref[...] = (dm * gs).astype(BF16)
        dgp = dm * yp_ref[...] * gp * (1.0 - gp)
        dgs = dm * ys_ref[...] * gs * (1.0 - gs)
        dg_ref[:, :D] = dgp.astype(BF16)
        dg_ref[:, D:] = dgs.astype(BF16)
        db = jnp.concatenate([jnp.sum(dgp, axis=0, keepdims=True), jnp.sum(dgs, axis=0, keepdims=True)], axis=1)

        @pl.when(pl.program_id(0) == 0)
        def _():
            db_ref[...] = db

        @pl.when(pl.program_id(0) > 0)
        def _():
            db_ref[...] += db

    row = pl.BlockSpec((tr, D), lambda i: (i, 0))
    row1 = pl.BlockSpec((tr, D), lambda i: (i, 1))
    wide = pl.BlockSpec((tr, 2 * D), lambda i: (i, 0))
    vec = pl.BlockSpec((1, D), lambda i: (0, 0))
    vec1 = pl.BlockSpec((1, D), lambda i: (0, 1))
    vec2 = pl.BlockSpec((1, 2 * D), lambda i: (0, 0))
    b2 = bg.reshape(1, 2 * D)
    return pl.pallas_call(
        body, name=name, grid=(T // tr,),
        in_specs=[row, row1, vec, vec1, row, row, row], out_specs=(wide, row, row, vec2),
        out_shape=(jax.ShapeDtypeStruct((T, 2 * D), BF16), jax.ShapeDtypeStruct((T, D), BF16),
                   jax.ShapeDtypeStruct((T, D), BF16), jax.ShapeDtypeStruct((1, 2 * D), F32)),
        compiler_params=_cp("arbitrary"),
    )(pg, pg, b2, b2, y_pool, y_ssd, dmix)


def _pool_count(c, pad, window):
    pos = c * CHUNK + _iota((CHUNK, 1), 0) - pad
    return jnp.clip(pos + 1, 1, window).astype(F32)


def _by_group(g, vals):
    out = vals[-1]
    for k in range(len(vals) - 2, -1, -1):
        out = jnp.where(g == k, vals[k], out)
    return out


def _by_row_block(rows, vals, block):
    out = vals[0]
    for r in range(1, len(vals)):
        out = jnp.where(rows >= r * block, vals[r], out)
    return out


def _pool_fwd(u, wg, scale, pad, name):
    T, C = u.shape
    G, Cg, _ = wg.shape
    nc = T // CHUNK

    def body(u_ref, wg_ref, s_ref, p_ref, y_ref):
        g = pl.program_id(0)
        window = _by_group(g, POOL_WINDOWS)

        def chunk(c, carry):
            r0 = pl.multiple_of(c * CHUNK, CHUNK)
            h0 = pl.multiple_of(jnp.maximum(r0 - POOL_HALO, 0), 8)
            halo = jnp.where(c > 0, u_ref[pl.ds(h0, POOL_HALO), :], 0.0)
            xc = u_ref[pl.ds(r0, CHUNK), :]
            s = jnp.concatenate([halo, xc], axis=0)
            sums = []
            k = 1
            while k < POOL_WINDOWS[-1]:
                s = s + pltpu.roll(s, k, 0)
                k *= 2
                if k in POOL_WINDOWS:
                    sums.append(s[POOL_HALO:])
            wsum = _by_group(g, sums)
            pooled = wsum / _pool_count(c, pad, window) - xc
            pb = pooled.astype(BF16)
            p_ref[pl.ds(r0, CHUNK), :] = pb
            y_ref[pl.ds(r0, CHUNK), :] = (_dot(pb, wg_ref[0]) * s_ref[...]).astype(BF16)
            return carry

        lax.fori_loop(0, nc, chunk, 0)

    col = pl.BlockSpec((T, Cg), lambda g: (0, g))
    return pl.pallas_call(
        body, name=name, grid=(G,),
        in_specs=[col, pl.BlockSpec((1, Cg, Cg), lambda g: (g, 0, 0)), pl.BlockSpec((1, Cg), lambda g: (0, g))],
        out_specs=(col, col),
        out_shape=(jax.ShapeDtypeStruct((T, C), BF16), jax.ShapeDtypeStruct((T, C), BF16)),
        compiler_params=_cp("parallel"),
    )(u, wg, scale)


def _pool_bwd(pooled, wg, scale, dy, pad, name):
    T, C = dy.shape
    G, Cg, _ = wg.shape
    nc = T // CHUNK

    def body(p_ref, wg_ref, s_ref, dy_ref, du_ref, dwg_ref, ds_ref, halo_ref):
        g = pl.program_id(0)
        window = _by_group(g, POOL_WINDOWS)
        halo_ref[...] = jnp.zeros_like(halo_ref)
        dwg_ref[...] = jnp.zeros_like(dwg_ref)
        ds_ref[...] = jnp.zeros_like(ds_ref)

        def chunk(i, carry):
            c = nc - 1 - i
            r0 = pl.multiple_of(c * CHUNK, CHUNK)
            pb = p_ref[pl.ds(r0, CHUNK), :]
            dyc = dy_ref[pl.ds(r0, CHUNK), :]
            w = wg_ref[0]
            ypre = _dot(pb, w)
            ds_ref[...] += jnp.sum(dyc * ypre, axis=0, keepdims=True)
            dyp = (dyc * s_ref[...]).astype(BF16)
            dwg_ref[0] += _dot(pb, dyp, "tn")
            dpool = _dot(dyp, w, "nt")
            q = dpool / _pool_count(c, pad, window)
            s = jnp.concatenate([q, halo_ref[...]], axis=0)
            n = CHUNK + POOL_HALO
            sums = []
            k = 1
            while k < POOL_WINDOWS[-1]:
                s = s + pltpu.roll(s, n - k, 0)
                k *= 2
                if k in POOL_WINDOWS:
                    sums.append(s[:CHUNK])
            du = _by_group(g, sums) - dpool
            rows = r0 + _iota((CHUNK, 1), 0)
            du_ref[pl.ds(r0, CHUNK), :] = jnp.where(rows >= pad, du, 0.0).astype(BF16)
            halo_ref[...] = q[:POOL_HALO]
            return carry

        lax.fori_loop(0, nc, chunk, 0)

    col = pl.BlockSpec((T, Cg), lambda g: (0, g))
    return pl.pallas_call(
        body, name=name, grid=(G,),
        in_specs=[col, pl.BlockSpec((1, Cg, Cg), lambda g: (g, 0, 0)), pl.BlockSpec((1, Cg), lambda g: (0, g)), col],
        out_specs=(col, pl.BlockSpec((1, Cg, Cg), lambda g: (g, 0, 0)), pl.BlockSpec((1, Cg), lambda g: (0, g))),
        out_shape=(jax.ShapeDtypeStruct((T, C), BF16), jax.ShapeDtypeStruct((G, Cg, Cg), F32), jax.ShapeDtypeStruct((1, C), F32)),
        scratch_shapes=[pltpu.VMEM((POOL_HALO, Cg), F32)],
        compiler_params=_cp("parallel"),
    )(pooled, wg, scale, dy)


def _conv_pre(x_ref, w_ref, b_ref, c, r0):
    h0 = pl.multiple_of(jnp.maximum(r0 - CONV_HALO, 0), 8)
    halo = jnp.where(c > 0, x_ref[pl.ds(h0, CONV_HALO), :], 0.0)
    xe = jnp.concatenate([halo, x_ref[pl.ds(r0, CHUNK), :]], axis=0)
    y = jnp.broadcast_to(b_ref[...], (CHUNK, xe.shape[1]))
    for k in range(CONV_WIDTH):
        shift = CONV_WIDTH - 1 - k
        xs = xe if shift == 0 else pltpu.roll(xe, shift, 0)
        y = y + xs[CONV_HALO:] * w_ref[k:k + 1, :]
    return y, xe


def _conv_fwd(x, w, b, name):
    T = x.shape[0]
    C = w.shape[1]
    tc = _tile(C, 256, 128)
    nc = T // CHUNK

    def body(x_ref, w_ref, b_ref, o_ref):
        def chunk(c, carry):
            r0 = pl.multiple_of(c * CHUNK, CHUNK)
            y, _ = _conv_pre(x_ref, w_ref, b_ref, c, r0)
            o_ref[pl.ds(r0, CHUNK), :] = y * _sigmoid(y)
            return carry

        lax.fori_loop(0, nc, chunk, 0)

    col = pl.BlockSpec((T, tc), lambda j: (0, j))
    return pl.pallas_call(
        body, name=name, grid=(C // tc,),
        in_specs=[col, pl.BlockSpec((CONV_WIDTH, tc), lambda j: (0, j)), pl.BlockSpec((1, tc), lambda j: (0, j))],
        out_specs=col, out_shape=jax.ShapeDtypeStruct((T, C), F32), compiler_params=_cp("parallel"),
    )(x, w, b)


def _conv_bwd(x, w, b, dact, pad, name):
    T = x.shape[0]
    C = w.shape[1]
    tc = _tile(C, 256, 128)
    nc = T // CHUNK

    def body(x_ref, w_ref, b_ref, da_ref, dx_ref, dw_ref, db_ref, halo_ref):
        halo_ref[...] = jnp.zeros_like(halo_ref)
        dw_ref[...] = jnp.zeros_like(dw_ref)
        db_ref[...] = jnp.zeros_like(db_ref)

        def chunk(i, carry):
            c = nc - 1 - i
            r0 = pl.multiple_of(c * CHUNK, CHUNK)
            y, xe = _conv_pre(x_ref, w_ref, b_ref, c, r0)
            sg = _sigmoid(y)
            dpre = da_ref[pl.ds(r0, CHUNK), :] * (sg * (1.0 + y * (1.0 - sg)))
            db_ref[...] += jnp.sum(dpre, axis=0, keepdims=True)
            de = jnp.concatenate([dpre, halo_ref[...]], axis=0)
            n = CHUNK + CONV_HALO
            dx = jnp.zeros_like(dpre)
            for k in range(CONV_WIDTH):
                shift = CONV_WIDTH - 1 - k
                xs = xe if shift == 0 else pltpu.roll(xe, shift, 0)
                dw_ref[k:k + 1, :] += jnp.sum(dpre * xs[CONV_HALO:], axis=0, keepdims=True)
                ds = de if shift == 0 else pltpu.roll(de, n - shift, 0)
                dx = dx + ds[:CHUNK] * w_ref[k:k + 1, :]
            rows = r0 + _iota((CHUNK, 1), 0)
            dx_ref[pl.ds(r0, CHUNK), :] = jnp.where(rows >= pad, dx, 0.0).astype(BF16)
            halo_ref[...] = dpre[:CONV_HALO]
            return carry

        lax.fori_loop(0, nc, chunk, 0)

    col = pl.BlockSpec((T, tc), lambda j: (0, j))
    wspec = pl.BlockSpec((CONV_WIDTH, tc), lambda j: (0, j))
    bspec = pl.BlockSpec((1, tc), lambda j: (0, j))
    return pl.pallas_call(
        body, name=name, grid=(C // tc,),
        in_specs=[col, wspec, bspec, col], out_specs=(col, wspec, bspec),
        out_shape=(jax.ShapeDtypeStruct((T, C), BF16), jax.ShapeDtypeStruct((CONV_WIDTH, C), F32), jax.ShapeDtypeStruct((1, C), F32)),
        scratch_shapes=[pltpu.VMEM((CONV_HALO, tc), F32)],
        compiler_params=_cp("parallel"),
    )(x, w, b, dact)


def _cumsum_rows(x, reverse=False):
    n = x.shape[0]
    idx = _iota(x.shape, 0)
    k = 1
    while k < n:
        if reverse:
            x = x + jnp.where(idx < n - k, pltpu.roll(x, n - k, 0), 0.0)
        else:
            x = x + jnp.where(idx >= k, pltpu.roll(x, k, 0), 0.0)
        k *= 2
    return x


def _softplus(x):
    return jnp.maximum(x, 0.0) + jnp.log1p(jnp.exp(-jnp.abs(x)))


def _head_selector(n_heads, width):
    lane = jnp.arange(n_heads * width)[None, :] // width
    return (lane == jnp.arange(LANES)[:, None]).astype(BF16)


def _dot_exact(x, sel, parts, mode="nn"):
    acc = None
    for _ in range(parts):
        piece = x.astype(BF16)
        x = x - piece.astype(F32)
        t = lax.dot_general(piece, sel, _DIMS[mode], preferred_element_type=F32)
        acc = t if acc is None else acc + t
    return acc


def _ssd_decays(dt, cs, sel_p_ref, sel_q_ref):
    cs_b = _dot_exact(cs, sel_q_ref[...], 3)
    dt_x = _dot_exact(dt, sel_p_ref[...], 3)
    cs_x = _dot_exact(cs, sel_p_ref[...], 3)
    return cs_b, dt_x, jnp.exp(cs_x), jnp.exp(cs_x[CHUNK - 1:CHUNK, :] - cs_x)


def _ssd_common(c, pad, n_heads, dtr_ref, dtb_ref, al_ref):
    rows = c * CHUNK + _iota((CHUNK, 1), 0)
    valid = rows >= pad
    live = jnp.logical_and(valid, _iota((1, LANES), 1) < n_heads)
    pre = dtr_ref[...] + dtb_ref[...]
    dt = jnp.where(live, _softplus(pre), 0.0)
    a = -jnp.exp(al_ref[...])
    cs = _cumsum_rows(dt * a)
    return valid, live, dt, a, cs, cs.T, _sigmoid(pre)


def _ssd_specs(T, DI, GN, cfirst):
    xcol = DI // GN

    def at(col):
        return lambda c: (cfirst(c), col)

    x = pl.BlockSpec((CHUNK, DI), at(0))
    b = pl.BlockSpec((CHUNK, GN), at(xcol))
    cm = pl.BlockSpec((CHUNK, GN), at(xcol + 1))
    dt = pl.BlockSpec((CHUNK, LANES), at(0))
    vec = pl.BlockSpec((1, LANES), lambda c: (0, 0))
    nw = pl.BlockSpec((1, DI), lambda c: (0, 0))
    return x, b, cm, dt, vec, nw


def _ssd_fwd(xbc, pdt, pz, dt_bias, a_log, d_skip, norm_w, pad, n_heads, name):
    T = xbc.shape[0]
    DI = pz.shape[1]
    P = DI // n_heads
    GN = (xbc.shape[1] - DI) // 2
    G = GN // D_STATE
    R = n_heads // G
    GW = R * P
    nc = T // CHUNK
    Q, N = CHUNK, D_STATE

    def body(x_ref, b_ref, c_ref, dtr_ref, z_ref, dtb_ref, al_ref, dsk_ref, nw_ref, sel_p_ref, sel_q_ref,
             y_ref, yn_ref, prev_ref, s_ref):
        c = pl.program_id(0)

        @pl.when(c == 0)
        def _():
            s_ref[...] = jnp.zeros_like(s_ref)

        valid, _, dt, _, cs, cst, _ = _ssd_common(c, pad, n_heads, dtr_ref, dtb_ref, al_ref)
        cs_b, dt_x, e_x, dec_x = _ssd_decays(dt, cs, sel_p_ref, sel_q_ref)
        e_last = jnp.exp(cs[Q - 1:Q, :])
        tri = _iota((Q, Q), 0) >= _iota((Q, Q), 1)
        state_rows = _iota((GW, 1), 0)
        head_lane = _iota((1, GW), 1)
        for g in range(G):
            gs = slice(g * GW, (g + 1) * GW)
            bg = jnp.where(valid, b_ref[:, g * N:(g + 1) * N], 0.0).astype(BF16)
            cg = jnp.where(valid, c_ref[:, g * N:(g + 1) * N], 0.0).astype(BF16)
            xg = jnp.where(valid, x_ref[:, gs], 0.0)
            sg = s_ref[gs, :]
            prev_ref[0, gs, :] = sg
            cb = _dot(cg, bg, "nt")
            xdt = xg * dt_x[:, gs]
            yg = _dot(cg, sg, "nt") * e_x[:, gs]
            for r in range(R):
                h = g * R + r
                lmat = jnp.exp(jnp.where(tri, cs_b[:, h * Q:(h + 1) * Q] - cst[h:h + 1, :], -jnp.inf))
                in_head = jnp.logical_and(head_lane >= r * P, head_lane < (r + 1) * P)
                yg = yg + _dot(cb * lmat, jnp.where(in_head, xdt, 0.0))
            y_ref[:, gs] = yg
            decay = _by_row_block(state_rows, [e_last[:, g * R + r:g * R + r + 1] for r in range(R)], P)
            s_ref[gs, :] = sg * decay + _dot(xdt * dec_x[:, gs], bg, "tn")
            z = z_ref[:, gs]
            gz = (yg + xg * dsk_ref[:, gs]) * (z * _sigmoid(z))
            rstd = lax.rsqrt(jnp.mean(gz * gz, axis=-1, keepdims=True) + EPS)
            yn_ref[:, gs] = ((gz * rstd) * nw_ref[:, gs]).astype(BF16)

    x_s, b_s, c_s, dt_s, vec, nw = _ssd_specs(T, DI, GN, lambda c: c)
    wide = pl.BlockSpec((Q, DI), lambda c: (c, 0))
    sel_p = pl.BlockSpec((LANES, DI), lambda c: (0, 0))
    sel_q = pl.BlockSpec((LANES, n_heads * Q), lambda c: (0, 0))
    return pl.pallas_call(
        body, name=name, grid=(nc,),
        in_specs=[x_s, b_s, c_s, dt_s, wide, vec, vec, nw, nw, sel_p, sel_q],
        out_specs=(wide, wide, pl.BlockSpec((1, DI, N), lambda c: (c, 0, 0))),
        out_shape=(jax.ShapeDtypeStruct((T, DI), F32), jax.ShapeDtypeStruct((T, DI), BF16), jax.ShapeDtypeStruct((nc, DI, N), F32)),
        scratch_shapes=[pltpu.VMEM((DI, N), F32)],
        compiler_params=_cp("arbitrary"),
    )(xbc, xbc, xbc, pdt, pz, dt_bias, a_log, jnp.repeat(d_skip[:, :n_heads], P, axis=1), norm_w,
      _head_selector(n_heads, P), _head_selector(n_heads, Q))


def _ssd_bwd(xbc, pdt, pz, dt_bias, a_log, d_skip, norm_w, y, prev, dyn, pad, n_heads, name):
    T, W = xbc.shape
    DI = pz.shape[1]
    P = DI // n_heads
    GN = (W - DI) // 2
    G = GN // D_STATE
    R = n_heads // G
    GW = R * P
    nc = T // CHUNK
    Q, N = CHUNK, D_STATE

    def body(x_ref, b_ref, c_ref, dtr_ref, z_ref, dtb_ref, al_ref, dsk_ref, nw_ref, y_ref, prev_ref, next_ref, dyn_ref,
             sel_p_ref, sel_q_ref, sel_pt_ref,
             dxbc_ref, dz_ref, ddt_ref, ddtb_ref, dal_ref, ddsk_ref, dnw_ref, ds_ref):
        i = pl.program_id(0)
        c = nc - 1 - i

        @pl.when(i == 0)
        def _():
            ds_ref[...] = jnp.zeros_like(ds_ref)
            ddtb_ref[...] = jnp.zeros_like(ddtb_ref)
            dal_ref[...] = jnp.zeros_like(dal_ref)
            ddsk_ref[...] = jnp.zeros_like(ddsk_ref)
            dnw_ref[...] = jnp.zeros_like(dnw_ref)

        valid, live, dt, a, cs, cst, sig_pre = _ssd_common(c, pad, n_heads, dtr_ref, dtb_ref, al_ref)
        cs_b, dt_x, e_x, dec_x = _ssd_decays(dt, cs, sel_p_ref, sel_q_ref)
        e_last = jnp.exp(cs[Q - 1:Q, :])
        tri = _iota((Q, Q), 0) >= _iota((Q, Q), 1)
        tri_t = _iota((Q, Q), 0) <= _iota((Q, Q), 1)
        state_rows = _iota((GW, 1), 0)
        head_lane = _iota((1, GW), 1)
        lane = _iota((1, LANES), 1)
        head_rows = _iota((LANES, 1), 0)
        s_dy_cs = jnp.zeros((Q, LANES), F32)
        s_x_bds = jnp.zeros((Q, LANES), F32)
        s_x_dxdt = jnp.zeros((Q, LANES), F32)
        dcs_rows = jnp.zeros((Q, LANES), F32)
        dcs_cols = jnp.zeros((LANES, Q), F32)
        c_end = jnp.zeros((1, LANES), F32)
        dsk_rows = []
        for g in range(G):
            gs = slice(g * GW, (g + 1) * GW)
            bg = jnp.where(valid, b_ref[:, g * N:(g + 1) * N], 0.0).astype(BF16)
            cg = jnp.where(valid, c_ref[:, g * N:(g + 1) * N], 0.0).astype(BF16)
            xg = jnp.where(valid, x_ref[:, gs], 0.0)
            s_prev = prev_ref[0, gs, :]
            dsg = ds_ref[gs, :]
            end = dsg * next_ref[0, gs, :]
            yg = y_ref[:, gs]
            dsk = dsk_ref[:, gs]
            ytot = yg + xg * dsk
            z = z_ref[:, gs]
            sz = _sigmoid(z)
            silu = z * sz
            gz = ytot * silu
            rstd = lax.rsqrt(jnp.mean(gz * gz, axis=-1, keepdims=True) + EPS)
            gn = gz * rstd
            dyn_g = dyn_ref[:, gs]
            dnw_ref[:, gs] += jnp.sum(dyn_g * gn, axis=0, keepdims=True)
            dgn = dyn_g * nw_ref[:, gs]
            dgz = rstd * (dgn - gn * jnp.mean(dgn * gn, axis=-1, keepdims=True))
            dz_ref[:, gs] = (dgz * ytot * (sz * (1.0 + z * (1.0 - sz)))).astype(BF16)
            dy = dgz * silu
            dsk_rows.append(jnp.sum(dy * xg, axis=0, keepdims=True))
            cb = _dot(cg, bg, "nt")
            cb_t = _dot(bg, cg, "nt")
            bds = _dot(bg, dsg, "nt") * dec_x[:, gs]
            csg = _dot(cg, s_prev, "nt")
            xdt = xg * dt_x[:, gs]
            dxdt = bds
            dcb = jnp.zeros((Q, Q), F32)
            dcb_t = jnp.zeros((Q, Q), F32)
            for r in range(R):
                h = g * R + r
                cs_col = cs_b[:, h * Q:(h + 1) * Q]
                cs_row = cst[h:h + 1, :]
                lmat = jnp.exp(jnp.where(tri, cs_col - cs_row, -jnp.inf))
                lmat_t = jnp.exp(jnp.where(tri_t, cs_row - cs_col, -jnp.inf))
                in_head = jnp.logical_and(head_lane >= r * P, head_lane < (r + 1) * P)
                dyr = jnp.where(in_head, dy, 0.0)
                dm = _dot(dyr, xdt, "nt")
                dcb = dcb + dm * lmat
                dcb_t = dcb_t + _dot(xdt, dyr, "nt") * lmat_t
                w_rc = dm * (cb * lmat)
                dcs_rows = jnp.where(lane == h, jnp.sum(w_rc, axis=1, keepdims=True), dcs_rows)
                dcs_cols = jnp.where(head_rows == h, jnp.sum(w_rc, axis=0, keepdims=True), dcs_cols)
                dxdt = dxdt + _dot(cb_t * lmat_t, dyr)
            sel_t = sel_pt_ref[gs, :]
            s_dy_cs = s_dy_cs + _dot_exact(dy * csg, sel_t, 2)
            s_x_bds = s_x_bds + _dot_exact(xg * bds, sel_t, 2)
            s_x_dxdt = s_x_dxdt + _dot_exact(xg * dxdt, sel_t, 2)
            c_end = c_end + jnp.sum(jnp.sum(end, axis=1, keepdims=True) * sel_t.astype(F32), axis=0, keepdims=True)
            dye = dy * e_x[:, gs]
            dc = _dot(dcb, bg) + _dot(dye, s_prev)
            db = _dot(dcb_t, cg) + _dot(xdt * dec_x[:, gs], dsg)
            decay = _by_row_block(state_rows, [e_last[:, g * R + r:g * R + r + 1] for r in range(R)], P)
            ds_ref[gs, :] = dsg * decay + _dot(dye, cg, "tn")
            dxbc_ref[:, gs] = jnp.where(valid, dxdt * dt_x[:, gs] + dy * dsk, 0.0)
            dxbc_ref[:, DI + g * N:DI + (g + 1) * N] = jnp.where(valid, db, 0.0)
            dxbc_ref[:, DI + GN + g * N:DI + GN + (g + 1) * N] = jnp.where(valid, dc, 0.0)
        dcs = s_dy_cs * jnp.exp(cs) - dt * s_x_bds
        da_cs = _cumsum_rows(dcs + (dcs_rows - dcs_cols.T), reverse=True) + c_end
        ddt_all = jnp.where(live, da_cs * a + s_x_dxdt, 0.0)
        ddt_raw = ddt_all * sig_pre
        ddt_ref[...] = ddt_raw.astype(BF16)
        ddtb_ref[...] += jnp.sum(ddt_raw, axis=0, keepdims=True)
        dal_ref[...] += jnp.sum(da_cs * dt, axis=0, keepdims=True) * a
        dsk_all = jnp.broadcast_to(jnp.concatenate(dsk_rows, axis=1), (8, DI))
        ddsk_ref[...] += _dot_exact(dsk_all, sel_pt_ref[...], 3)[0:1]

    rev = lambda i: nc - 1 - i
    x_s, b_s, c_s, dt_s, vec, nw = _ssd_specs(T, DI, GN, rev)
    wide = pl.BlockSpec((Q, DI), lambda i: (rev(i), 0))
    st = pl.BlockSpec((1, DI, N), lambda i: (rev(i), 0, 0))
    st_next = pl.BlockSpec((1, DI, N), lambda i: (jnp.minimum(rev(i) + 1, nc - 1), 0, 0))
    sel_p = pl.BlockSpec((LANES, DI), lambda i: (0, 0))
    sel_q = pl.BlockSpec((LANES, n_heads * Q), lambda i: (0, 0))
    sel_pt = pl.BlockSpec((DI, LANES), lambda i: (0, 0))
    sel = _head_selector(n_heads, P)
    return pl.pallas_call(
        body, name=name, grid=(nc,),
        in_specs=[x_s, b_s, c_s, dt_s, wide, vec, vec, nw, nw, wide, st, st_next, wide, sel_p, sel_q, sel_pt],
        out_specs=(pl.BlockSpec((Q, W), lambda i: (rev(i), 0)), wide, dt_s, vec, vec, vec, nw),
        out_shape=(jax.ShapeDtypeStruct((T, W), F32), jax.ShapeDtypeStruct((T, DI), BF16), jax.ShapeDtypeStruct((T, LANES), BF16),
                   jax.ShapeDtypeStruct((1, LANES), F32), jax.ShapeDtypeStruct((1, LANES), F32),
                   jax.ShapeDtypeStruct((1, LANES), F32), jax.ShapeDtypeStruct((1, DI), F32)),
        scratch_shapes=[pltpu.VMEM((DI, N), F32)],
        compiler_params=_cp("arbitrary"),
    )(xbc, xbc, xbc, pdt, pz, dt_bias, a_log, jnp.repeat(d_skip[:, :n_heads], P, axis=1), norm_w, y, prev, prev, dyn,
      sel, _head_selector(n_heads, Q), sel.T)


def _adamw(w, g, m, v, name):
    rows, cols = w.shape
    tr = _tile(rows, 256, 8)

    def body(w_ref, g_ref, m_ref, v_ref, d_ref, nm_ref, nv_ref):
        g = g_ref[...]
        m = ADAM_B1 * m_ref[...] + (1.0 - ADAM_B1) * g
        v = ADAM_B2 * v_ref[...] + (1.0 - ADAM_B2) * (g * g)
        m_hat = m / (1.0 - ADAM_B1 ** ADAM_STEP)
        v_hat = v / (1.0 - ADAM_B2 ** ADAM_STEP)
        d_ref[...] = -ADAM_LR * (m_hat / (jnp.sqrt(v_hat) + ADAM_EPS) + ADAM_WD * w_ref[...])
        nm_ref[...] = m
        nv_ref[...] = v

    blk = pl.BlockSpec((tr, cols), lambda i: (i, 0))
    out = jax.ShapeDtypeStruct((rows, cols), F32)
    return pl.pallas_call(
        body, name=name, grid=(rows // tr,), in_specs=[blk] * 4, out_specs=(blk,) * 3, out_shape=(out,) * 3,
        compiler_params=_cp("parallel"),
    )(w, g, m, v)


def _sum_stack(parts, name, out_dtype=F32):
    rows, cols = parts[0].shape
    tr = _tile(rows, 512, 16)

    def body(*refs):
        acc = refs[0][...].astype(F32)
        for r in refs[1:-1]:
            acc = acc + r[...].astype(F32)
        refs[-1][...] = acc.astype(out_dtype)

    blk = pl.BlockSpec((tr, cols), lambda i: (i, 0))
    return pl.pallas_call(
        body, name=name, grid=(rows // tr,), in_specs=[blk] * len(parts), out_specs=blk,
        out_shape=jax.ShapeDtypeStruct((rows, cols), out_dtype), compiler_params=_cp("parallel"),
    )(*parts)


def _place():
    return lax.axis_index("x"), lax.axis_index("y"), lax.axis_index("c")


def _other_chips(x, y):
    return [(1 - x, y), (x, 1 - y), (1 - x, 1 - y)]


def _all_gather(shards, name):
    nb = len(shards)

    def body(*refs):
        ins, outs = refs[:nb], refs[nb:2 * nb]
        send_sems, recv_sems, local_sems = refs[2 * nb:]
        x, y, c = _place()
        me, sibling = (x, y, c), (x, y, 1 - c)
        chips = _other_chips(x, y)

        def copy(q, k, block, to, src=None):
            dst = outs[q].at[4 * block[0] + 2 * block[1] + block[2]]
            return pltpu.make_async_remote_copy(
                src_ref=dst if src is None else src, dst_ref=dst,
                send_sem=send_sems.at[7 * q + k], recv_sem=recv_sems.at[7 * q + k], device_id=to, device_id_type=MESH)

        started = []
        for q in range(nb):
            mine = pltpu.make_async_copy(ins[q], outs[q].at[4 * x + 2 * y + c], local_sems.at[q])
            mine.start()
            started.append(mine)
        first = []
        for q in range(nb):
            first.append(copy(q, 0, me, sibling, src=ins[q]))
            first += [copy(q, 1 + j, me, (*chip, c), src=ins[q]) for j, chip in enumerate(chips)]
        for cp in first:
            cp.start()
        passed = []
        for j, chip in enumerate(chips):
            for q in range(nb):
                copy(q, 1 + j, (*chip, c), me).wait_recv()
                fwd = copy(q, 4 + j, (*chip, c), sibling)
                fwd.start()
                passed.append(fwd)
        for q in range(nb):
            copy(q, 0, sibling, me).wait_recv()
            for j, chip in enumerate(chips):
                copy(q, 4 + j, (*chip, 1 - c), me).wait_recv()
        for cp in first + passed:
            cp.wait_send()
        for mine in started:
            mine.wait()

    return pl.pallas_call(
        body, name=name, in_specs=[ANY] * nb, out_specs=tuple([ANY] * nb),
        out_shape=tuple(jax.ShapeDtypeStruct((N_DEV,) + s.shape, s.dtype) for s in shards),
        scratch_shapes=[pltpu.SemaphoreType.DMA((7 * nb,)), pltpu.SemaphoreType.DMA((7 * nb,)), pltpu.SemaphoreType.DMA((nb,))],
    )(*shards)


HBM = pl.BlockSpec(memory_space=pltpu.HBM)
SEM = pl.BlockSpec(memory_space=pltpu.SEMAPHORE)
N_PEERS = N_DEV - 1


def _peer(k, x, y, c):
    return (1 - x if k & 4 else x, 1 - y if k & 2 else y, 1 - c if k & 1 else c)


def _direct_copies(gather, srcs, lands, send_sems, recv_sems):
    x, y, c = _place()
    copies = []
    for q in range(len(srcs)):
        for k in range(1, N_DEV):
            px, py, pc = _peer(k, x, y, c)
            if gather:
                src, dst = srcs[q], lands[q].at[4 * x + 2 * y + c]
            else:
                src, dst = srcs[q].at[4 * px + 2 * py + pc], lands[q].at[k - 1]
            copies.append(pltpu.make_async_remote_copy(
                src_ref=src, dst_ref=dst, send_sem=send_sems.at[N_PEERS * q + k - 1], recv_sem=recv_sems.at[N_PEERS * q + k - 1],
                device_id=(px, py, pc), device_id_type=MESH))
    return copies


def _exchange_start(gather, srcs, lands, name):
    n = len(srcs)

    def body(*refs):
        send_sems, recv_sems = refs[2 * n], refs[2 * n + 1]
        for cp in _direct_copies(gather, refs[:n], refs[n:2 * n], send_sems, recv_sems):
            cp.start()
        refs[-1][...] = jnp.zeros_like(refs[-1])

    held = [pltpu.with_memory_space_constraint(t, pltpu.HBM) for t in list(srcs) + list(lands)]
    out = pl.pallas_call(
        body, name=name,
        out_shape=(pltpu.SemaphoreType.DMA((N_PEERS * n,)), pltpu.SemaphoreType.DMA((N_PEERS * n,)),
                   *[pltpu.HBM(t.shape, t.dtype) for t in held], jax.ShapeDtypeStruct((8, LANES), F32)),
        in_specs=[HBM] * (2 * n), out_specs=(SEM, SEM, *[HBM] * (2 * n), pl.BlockSpec(memory_space=pltpu.VMEM)),
        input_output_aliases={i: 2 + i for i in range(2 * n)},
        compiler_params=pltpu.CompilerParams(has_side_effects=pltpu.SideEffectType.DATAFLOW_SIDE_EFFECTING),
    )(*held)
    return out[0], out[1], list(out[2:2 + 2 * n]), out[-1]


def _exchange_wait(gather, started, after, name):
    send_sems, recv_sems, held, _ = started
    n = len(held) // 2

    def body(*refs):
        for cp in _direct_copies(gather, refs[:n], refs[n:2 * n], refs[2 * n], refs[2 * n + 1]):
            cp.wait_send()
            cp.wait_recv()

    out = pl.pallas_call(
        body, name=name, out_shape=tuple(pltpu.HBM(t.shape, t.dtype) for t in held),
        in_specs=[HBM] * (2 * n) + [SEM, SEM, pl.BlockSpec(memory_space=pl.ANY)], out_specs=tuple([HBM] * (2 * n)),
        input_output_aliases={i: i for i in range(2 * n)},
        compiler_params=pltpu.CompilerParams(has_side_effects=pltpu.SideEffectType.DATAFLOW_SIDE_EFFECTING),
    )(*held, send_sems, recv_sems, after)
    return list(out[n:])


def _own_plus_received(stack, got, name):
    _, rows, cols = stack.shape
    tr = _tile(rows, 512, 16)

    def body(*refs):
        acc = refs[0][0]
        for r in refs[1:-1]:
            acc = acc + r[0].astype(F32)
        refs[-1][...] = acc

    def mine(i):
        x, y, c = _place()
        return (4 * x + 2 * y + c, i, 0)

    return pl.pallas_call(
        body, name=name, grid=(rows // tr,),
        in_specs=[pl.BlockSpec((1, tr, cols), mine)]
        + [pl.BlockSpec((1, tr, cols), functools.partial(lambda k, i: (k, i, 0), k)) for k in range(N_PEERS)],
        out_specs=pl.BlockSpec((tr, cols), lambda i: (i, 0)),
        out_shape=jax.ShapeDtypeStruct((rows, cols), F32), compiler_params=_cp("parallel"),
    )(stack, *[got] * N_PEERS)


class _Shard:
    def __init__(self, name, axis):
        self.name, self.axis = name, axis


BIG = [_Shard("w_in", 2), _Shard("pool_w_group", 2), _Shard("w_pool_up", 1), _Shard("w_ssd_out", 1),
       _Shard("w_o", 1), _Shard("w_ff1", 2), _Shard("w_ff2", 1)]
SMALL_SHARDED = [_Shard("meta_tokens", 1), _Shard("conv_w", 2)]
REPLICATED = ["mix_norm_w", "b_gate", "pool_scale", "conv_b", "dt_bias", "a_log", "d_skip", "ssd_norm_w",
              "mlp_norm_w", "final_norm_w"]
WEIGHTS = ["meta_tokens", "mix_norm_w", "w_in", "b_gate", "pool_w_group", "pool_scale", "w_pool_up", "conv_w", "conv_b",
           "dt_bias", "a_log", "d_skip", "ssd_norm_w", "w_ssd_out", "w_o", "mlp_norm_w", "w_ff1", "w_ff2", "final_norm_w"]


def _columns(blocks, a, b):
    pieces, o = [], 0
    for t in blocks:
        n = t.shape[-1]
        lo, hi = max(a, o), min(b, o + n)
        if lo < hi:
            pieces.append(t[..., lo - o:hi - o])
        o += n
    return pieces[0] if len(pieces) == 1 else jnp.concatenate(pieces, axis=-1)


def _rows2(a):
    return a.reshape(-1, a.shape[-1])


def _unshard(stack, shard_shape, axis):
    t = stack.reshape((N_DEV,) + tuple(shard_shape))
    return jnp.concatenate([t[d] for d in range(N_DEV)], axis=axis)


def _reshard(layers, axis):
    cut = [jnp.split(t, N_DEV, axis=axis - 1) for t in layers]
    blocks = [jnp.concatenate([_rows2(pieces[d]) for pieces in cut], axis=0) for d in range(N_DEV)]
    return jnp.stack(blocks).reshape((4, 2) + blocks[0].shape)


def _lane_rows(a):
    n = a.size
    tile = 8 * LANES
    if n % tile:
        return jnp.pad(a.reshape(-1), (0, (-n) % tile)).reshape(-1, LANES)
    return a.reshape(-1, LANES)


def _unpack_small(buf, spans, shapes):
    out = []
    for (o, r), shp in zip(spans, shapes):
        n = 1
        for d in shp:
            n *= d
        t = buf[o:o + r]
        out.append(t.reshape(shp) if n == r * LANES else t.reshape(-1)[:n].reshape(shp))
    return out


def _pack_small(parts, mult):
    mats = [_lane_rows(p) for p in parts]
    spans, o = [], 0
    for t in mats:
        spans.append((o, t.shape[0]))
        o += t.shape[0]
    fill = (-o) % mult
    if fill:
        mats.append(jnp.zeros((fill, LANES), mats[0].dtype))
    return jnp.concatenate(mats, axis=0), spans


def _layer_fwd(h, lw, cfg, tag, dep=None):
    pad, n_heads = cfg["pad"], cfg["n_heads"]
    u = _rms_fwd(h, lw["mix_norm_w"], f"rms_mix_{tag}", dep=dep)
    p_xbc = _mm(u, lw["w_xbc"], "nn", f"proj_xbc_{tag}")
    p_z = _mm(u, lw["w_z"], "nn", f"proj_z_{tag}")
    p_gate = _mm(u, lw["w_gate"], "nn", f"proj_gate_{tag}")
    p_pool = _mm(u, lw["w_pool"], "nn", f"proj_pool_{tag}")
    p_dt = _mm(u, lw["w_dt"], "nn", f"proj_dt_{tag}")
    pooled, y1 = _pool_fwd(p_pool, lw["pool_w_group"], lw["pool_scale"], pad, f"pool_fwd_{tag}")
    y_pool = _mm(y1, lw["w_pool_up"], "nn", f"pool_up_{tag}")
    xbc = _conv_fwd(p_xbc, lw["conv_w"], lw["conv_b"], f"conv_fwd_{tag}")
    y, yn, prev = _ssd_fwd(xbc, p_dt, p_z, lw["dt_bias"], lw["a_log"], lw["d_skip"], lw["ssd_norm_w"], pad, n_heads, f"ssd_fwd_{tag}")
    y_ssd = _mm(yn, lw["w_ssd_out"], "nn", f"ssd_out_{tag}")
    mix = _gate_fwd(p_gate, lw["b_gate"], y_pool, y_ssd, f"gate_fwd_{tag}")
    h_mid = _mm(mix, lw["w_o"], "nn", f"mix_out_{tag}", res=h)
    v = _rms_fwd(h_mid, lw["mlp_norm_w"], f"rms_mlp_{tag}")
    hid, act = _mm(v, lw["w_ff1"], "nn", f"ff1_{tag}", epi="relu2")
    h_out = _mm(act, lw["w_ff2"], "nn", f"ff2_{tag}", res=h_mid)
    saved = dict(h=h, u=u, p_xbc=p_xbc, p_z=p_z, p_gate=p_gate, p_dt=p_dt, pooled=pooled, y1=y1, y_pool=y_pool, xbc=xbc,
                 y=y, yn=yn, prev=prev, y_ssd=y_ssd, mix=mix, h_mid=h_mid, v=v, hid=hid, act=act)
    return h_out, saved


def _layer_bwd(dh, dh_b, lw, s, cfg, tag, traffic, i):
    pad, n_heads = cfg["pad"], cfg["n_heads"]
    g = {}
    dhid = _mm(dh_b, lw["w_ff2"], "nt", f"d_act_{tag}", epi="drelu2", aux=s["hid"], dep=traffic.bwd_begin(i))
    gb = {}
    g["w_ff2"], gb["w_ff2"] = _mm(s["act"], dh_b, "tn", f"dw_ff2_{tag}", epi="twin")
    dv = _mm(dhid, lw["w_ff1"], "nt", f"d_v_{tag}")
    g["w_ff1"], gb["w_ff1"] = _mm(s["v"], dhid, "tn", f"dw_ff1_{tag}", epi="twin")
    dep = traffic.grads_ready(i, {k: (g[k], gb[k]) for k in ("w_ff1", "w_ff2")}, "mlp")
    dh_mid, dh_mid_b, g["mlp_norm_w"] = _rms_bwd(s["h_mid"], lw["mlp_norm_w"], dv, dh, f"rms_mlp_bwd_{tag}")
    dmix = _mm(dh_mid_b, lw["w_o"], "nt", f"d_mix_{tag}", dep=dep)
    g["w_o"], gb["w_o"] = _mm(s["mix"], dh_mid_b, "tn", f"dw_o_{tag}", epi="twin")
    dgate, dy_pool, dy_ssd, g["b_gate"] = _gate_bwd(s["p_gate"], lw["b_gate"], s["y_pool"], s["y_ssd"], dmix, f"gate_bwd_{tag}")
    dy1 = _mm(dy_pool, lw["w_pool_up"], "nt", f"d_y1_{tag}")
    g["w_pool_up"], gb["w_pool_up"] = _mm(s["y1"], dy_pool, "tn", f"dw_pool_up_{tag}", epi="twin")
    dpool, g["pool_w_group"], g["pool_scale"] = _pool_bwd(s["pooled"], lw["pool_w_group"], lw["pool_scale"], dy1, pad, f"pool_bwd_{tag}")
    g["w_ssd_out"], gb["w_ssd_out"] = _mm(s["yn"], dy_ssd, "tn", f"dw_ssd_out_{tag}", epi="twin")
    gb["pool_w_group"] = g["pool_w_group"].astype(BF16)
    dep = traffic.grads_ready(i, {k: (g[k], gb[k]) for k in ("pool_w_group", "w_pool_up", "w_ssd_out", "w_o")}, "mix")
    dyn = _mm(dy_ssd, lw["w_ssd_out"], "nt", f"d_yn_{tag}", dep=dep)
    dact, dz, ddt, g["dt_bias"], g["a_log"], g["d_skip"], g["ssd_norm_w"] = _ssd_bwd(
        s["xbc"], s["p_dt"], s["p_z"], lw["dt_bias"], lw["a_log"], lw["d_skip"], lw["ssd_norm_w"], s["y"], s["prev"], dyn,
        pad, n_heads, f"ssd_bwd_{tag}")
    dxbc, g["conv_w"], g["conv_b"] = _conv_bwd(s["p_xbc"], lw["conv_w"], lw["conv_b"], dact, pad, f"conv_bwd_{tag}")
    u = s["u"]
    for k, d in (("w_xbc", dxbc), ("w_z", dz), ("w_gate", dgate), ("w_pool", dpool), ("w_dt", ddt)):
        g[k], gb[k] = _mm(u, d, "tn", f"d{k}_{tag}", epi="twin")
    c_dt = cfg["cols"][3]
    parts, parts_b = ([t["w_pool"], t["w_z"], t["w_xbc"], t["w_dt"][:, :c_dt], t["w_gate"]] for t in (g, gb))
    g["w_in"] = jnp.concatenate(parts, axis=1)
    dep = traffic.grads_ready(i, {"w_in": (parts, parts_b)}, "in")
    du = _mm(dxbc, lw["w_xbc"], "nt", f"du_xbc_{tag}", dep=dep)
    du = _mm(dz, lw["w_z"], "nt", f"du_z_{tag}", res=du)
    du = _mm(dgate, lw["w_gate"], "nt", f"du_gate_{tag}", res=du)
    du = _mm(dpool, lw["w_pool"], "nt", f"du_pool_{tag}", res=du)
    du = _mm(ddt, lw["w_dt"], "nt", f"du_dt_{tag}", res=du)
    dh_in, dh_in_b, g["mix_norm_w"] = _rms_bwd(s["h"], lw["mix_norm_w"], du, dh_mid, f"rms_mix_bwd_{tag}")
    traffic.bwd_end(i, dh_in)
    return dh_in, dh_in_b, g


def _pad_lanes(v):
    return jnp.pad(v, (0, LANES - v.shape[0])).reshape(1, LANES)


class _WholeWeights:
    def __init__(self, full):
        self.full = full

    def fwd_begin(self, i):
        full = {k: t[i] for k, t in self.full.items()}
        full["w_in"] = [full["w_in"]]
        return full, None

    def fwd_end(self, i, h_out):
        pass

    def bwd_begin(self, i):
        return None

    def grads_ready(self, i, grads, tag):
        return None

    def bwd_end(self, i, dh_in):
        pass


def _local_step(x2, target, meta_full, traffic, rep, cfg):
    depth, pad, n_meta, H = cfg["depth"], cfg["pad"], cfg["n_meta"], cfg["n_heads"]
    D = x2.shape[1]
    di = rep["ssd_norm_w"].shape[1]
    c_pool, c_z, c_xbc, c_dt = cfg["cols"]
    h = jnp.concatenate([jnp.zeros((pad, D), F32), meta_full, x2], axis=0)
    lws, saves = [], []
    for i in range(depth):
        full, dep = traffic.fwd_begin(i)
        blocks = full.pop("w_in")
        o_z, o_xbc, o_dt, o_gate = c_pool, c_pool + c_z, c_pool + c_z + c_xbc, c_pool + c_z + c_xbc + c_dt
        w_dt = jnp.pad(_columns(blocks, o_dt, o_gate), ((0, 0), (0, LANES - c_dt)))
        lw = dict(
            full, w_pool=_columns(blocks, 0, o_z), w_z=_columns(blocks, o_z, o_xbc), w_xbc=_columns(blocks, o_xbc, o_dt),
            w_dt=w_dt, w_gate=_columns(blocks, o_gate, o_gate + 2 * D),
            mix_norm_w=rep["mix_norm_w"][i], b_gate=rep["b_gate"][i], pool_scale=rep["pool_scale"][i].reshape(1, -1),
            conv_b=rep["conv_b"][i].reshape(1, -1), dt_bias=_pad_lanes(rep["dt_bias"][i]), a_log=_pad_lanes(rep["a_log"][i]),
            d_skip=_pad_lanes(rep["d_skip"][i]), ssd_norm_w=rep["ssd_norm_w"][i].reshape(1, di), mlp_norm_w=rep["mlp_norm_w"][i])
        lws.append(lw)
        h, s = _layer_fwd(h, lw, cfg, f"l{i}", dep)
        saves.append(s)
        traffic.fwd_end(i, h)
    loss, dh, dh_b, g_final = _final_loss(h, rep["final_norm_w"], target, pad + n_meta, "final_loss")
    per_layer = []
    for i in range(depth - 1, -1, -1):
        dh, dh_b, g = _layer_bwd(dh, dh_b, lws[i], saves[i], cfg, f"l{i}", traffic, i)
        per_layer.append(g)
    per_layer.reverse()

    def stack(key, fn=lambda t: t):
        return jnp.stack([fn(g[key]) for g in per_layer])

    def layers(key):
        return [g[key] for g in per_layer]

    grads = dict(
        w_in=layers("w_in"), pool_w_group=layers("pool_w_group"), w_pool_up=layers("w_pool_up"), w_ssd_out=layers("w_ssd_out"), w_o=layers("w_o"),
        w_ff1=layers("w_ff1"), w_ff2=layers("w_ff2"), conv_w=stack("conv_w"),
        mix_norm_w=stack("mix_norm_w", lambda t: t[0]), b_gate=stack("b_gate", lambda t: t[0]),
        pool_scale=stack("pool_scale", lambda t: t[0]), conv_b=stack("conv_b", lambda t: t[0]),
        dt_bias=stack("dt_bias", lambda t: t[0, :H]), a_log=stack("a_log", lambda t: t[0, :H]), d_skip=stack("d_skip", lambda t: t[0, :H]),
        ssd_norm_w=stack("ssd_norm_w", lambda t: t[0]), mlp_norm_w=stack("mlp_norm_w", lambda t: t[0]),
        final_norm_w=g_final[0], meta_tokens=dh[pad:pad + n_meta])
    return loss, dh[pad + n_meta:], grads


class _ShardedWeights:
    def __init__(self, w, me):
        self.w, self.me = w, me
        self.depth = w[BIG[0].name].shape[0]
        small = [_rows2(w[s.name]) for s in SMALL_SHARDED]
        landed = _all_gather([_rows2(w[s.name][0]).astype(BF16) for s in BIG] + small, "gather_l0")
        self.small = {s.name: _unshard(t, w[s.name].shape, s.axis) for s, t in zip(SMALL_SHARDED, landed[len(BIG):])}
        self.landed = dict(zip([s.name for s in BIG], landed))
        self.fetching = None
        self.sending, self.token = [], None
        self.local_grads = [{} for _ in range(self.depth)]

    def _fetch(self, shards, i, name):
        mine = [_rows2(self.w[s.name][i]).astype(BF16) for s in shards]
        lands = [lax.dynamic_update_slice(lax.empty((N_DEV,) + t.shape, t.dtype), t[None], (self.me, 0, 0)) for t in mine]
        return _exchange_start(True, mine, lands, name)

    def _whole(self, shards):
        return {s.name: _unshard(self.landed[s.name], self.w[s.name].shape[1:], s.axis - 1) for s in shards}

    def fwd_begin(self, i):
        rows, cols = self.w["w_in"].shape[1:]
        full = {"w_in": list(self.landed["w_in"].reshape(N_DEV, rows, cols)), "conv_w": self.small["conv_w"][i]}
        full.update(self._whole(BIG[1:]))
        dep = None
        if i + 1 < self.depth:
            self.fetching = self._fetch(BIG, i + 1, f"gather_start_l{i + 1}")
            dep = self.fetching[3]
        return full, dep

    def fwd_end(self, i, h_out):
        if self.fetching is not None:
            self.landed = dict(zip([s.name for s in BIG], _exchange_wait(True, self.fetching, h_out, f"gather_wait_l{i + 1}")))
            self.fetching = None

    def bwd_begin(self, i):
        return self.token

    def grads_ready(self, i, grads, tag):
        shards = [s for s in BIG if s.name in grads]

        def blocks(s, val):
            shape = (N_DEV,) + _rows2(self.w[s.name][i]).shape
            if isinstance(val, list):
                return jnp.stack([_columns(val, d * shape[2], (d + 1) * shape[2]) for d in range(N_DEV)])
            return _reshard([val], s.axis).reshape(shape)

        stacks = [blocks(s, grads[s.name][0]) for s in shards]
        sends = [blocks(s, grads[s.name][1]) for s in shards]
        lands = [lax.empty((N_PEERS,) + t.shape[1:], BF16) for t in sends]
        started = _exchange_start(False, sends, lands, f"rs_start_{tag}_l{i}")
        self.sending.append((i, tag, shards, stacks, started))
        self.token = started[3]
        return self.token

    def _collect(self, entry, after):
        i, tag, shards, stacks, started = entry
        got = _exchange_wait(False, started, after, f"rs_wait_{tag}_l{i}")
        for s, st, g in zip(shards, stacks, got):
            self.local_grads[i][s.name] = _own_plus_received(st, g, f"rs_sum_{s.name}_l{i}")

    def bwd_end(self, i, dh_in):
        for entry in [e for e in self.sending if e[0] > i]:
            self._collect(entry, dh_in)
        self.sending = [e for e in self.sending if e[0] <= i]

    def finish(self, after, names):
        for entry in [e for e in self.sending if e[2][0].name in names]:
            self._collect(entry, after)
        self.sending = [e for e in self.sending if e[2][0].name not in names]
        return {k: jnp.concatenate([g[k] for g in self.local_grads], axis=0) for k in names}


def kernel(x, meta_tokens, mix_norm_w, w_in, b_gate, pool_w_group, pool_scale, w_pool_up, conv_w, conv_b, dt_bias, a_log, d_skip, ssd_norm_w, w_ssd_out, w_o, mlp_norm_w, w_ff1, w_ff2, final_norm_w, loss_target, m_meta_tokens, m_mix_norm_w, m_w_in, m_b_gate, m_pool_w_group, m_pool_scale, m_w_pool_up, m_conv_w, m_conv_b, m_dt_bias, m_a_log, m_d_skip, m_ssd_norm_w, m_w_ssd_out, m_w_o, m_mlp_norm_w, m_w_ff1, m_w_ff2, m_final_norm_w, v_meta_tokens, v_mix_norm_w, v_w_in, v_b_gate, v_pool_w_group, v_pool_scale, v_w_pool_up, v_conv_w, v_conv_b, v_dt_bias, v_a_log, v_d_skip, v_ssd_norm_w, v_w_ssd_out, v_w_o, v_mlp_norm_w, v_w_ff1, v_w_ff2, v_final_norm_w):
    w = dict(meta_tokens=meta_tokens, mix_norm_w=mix_norm_w, w_in=w_in, b_gate=b_gate, pool_w_group=pool_w_group,
             pool_scale=pool_scale, w_pool_up=w_pool_up, conv_w=conv_w, conv_b=conv_b, dt_bias=dt_bias, a_log=a_log,
             d_skip=d_skip, ssd_norm_w=ssd_norm_w, w_ssd_out=w_ssd_out, w_o=w_o, mlp_norm_w=mlp_norm_w, w_ff1=w_ff1,
             w_ff2=w_ff2, final_norm_w=final_norm_w)
    m = dict(meta_tokens=m_meta_tokens, mix_norm_w=m_mix_norm_w, w_in=m_w_in, b_gate=m_b_gate, pool_w_group=m_pool_w_group,
             pool_scale=m_pool_scale, w_pool_up=m_w_pool_up, conv_w=m_conv_w, conv_b=m_conv_b, dt_bias=m_dt_bias, a_log=m_a_log,
             d_skip=m_d_skip, ssd_norm_w=m_ssd_norm_w, w_ssd_out=m_w_ssd_out, w_o=m_w_o, mlp_norm_w=m_mlp_norm_w, w_ff1=m_w_ff1,
             w_ff2=m_w_ff2, final_norm_w=m_final_norm_w)
    v = dict(meta_tokens=v_meta_tokens, mix_norm_w=v_mix_norm_w, w_in=v_w_in, b_gate=v_b_gate, pool_w_group=v_pool_w_group,
             pool_scale=v_pool_scale, w_pool_up=v_w_pool_up, conv_w=v_conv_w, conv_b=v_conv_b, dt_bias=v_dt_bias, a_log=v_a_log,
             d_skip=v_d_skip, ssd_norm_w=v_ssd_norm_w, w_ssd_out=v_w_ssd_out, w_o=v_w_o, mlp_norm_w=v_mlp_norm_w, w_ff1=v_w_ff1,
             w_ff2=v_w_ff2, final_norm_w=v_final_norm_w)

    _, seq, D = x.shape
    n_meta = meta_tokens.shape[0]
    depth = w_in.shape[0]
    n_heads = dt_bias.shape[1]
    d_inner = ssd_norm_w.shape[1]
    d_xbc = conv_b.shape[1]
    pool_width = pool_scale.shape[1]
    pad = (-n_meta) % CHUNK
    cfg = dict(depth=depth, pad=pad, n_meta=n_meta, n_heads=n_heads, cols=(pool_width, d_inner, d_xbc, n_heads))
    assert (pad + n_meta + seq) % CHUNK == 0 and pad + n_meta == CHUNK

    xi, yi, ci = _place()
    me = 4 * xi + 2 * yi + ci

    traffic = _ShardedWeights(w, me)
    rep = {k: w[k] for k in REPLICATED}
    loss_part, dx, grads = _local_step(x[0], loss_target[0], traffic.small["meta_tokens"], traffic, rep, cfg)
    loss = lax.psum(loss_part[0, 0], ("x", "y", "c"))

    small_names = REPLICATED + [s.name for s in SMALL_SHARDED]
    sm_buf, sm_spans = _pack_small([grads[k] for k in small_names], 16)
    (sm_all,) = _all_gather([sm_buf], "gather_small_grads")
    sm_sum = _sum_stack([sm_all[d] for d in range(N_DEV)], "small_grads_sum")
    g_small = dict(zip(small_names, _unpack_small(sm_sum, sm_spans, [grads[k].shape for k in small_names])))
    g_loc = {k: g_small[k] for k in REPLICATED}
    for s in SMALL_SHARDED:
        blk = w[s.name].shape[s.axis]
        g_loc[s.name] = lax.dynamic_slice_in_dim(g_small[s.name], me * blk, blk, axis=s.axis)

    delta, new_m, new_v = {}, {}, {}
    loc_shapes = [w[k].shape for k in small_names]
    packed = [_pack_small([t[k] for k in small_names], 8) for t in (w, g_loc, m, v)]
    loc_spans = packed[0][1]
    outs = _adamw(*[p[0] for p in packed], "adamw_small")
    for res, buf in zip((delta, new_m, new_v), outs):
        res.update(zip(small_names, _unpack_small(buf, loc_spans, loc_shapes)))
    after = outs[0]
    for names in ([s.name for s in BIG if s.name != "w_in"], ["w_in"]):
        for k, t in traffic.finish(after, names).items():
            shp = w[k].shape
            d2, m2, v2 = _adamw(_rows2(w[k]), t, _rows2(m[k]), _rows2(v[k]), f"adamw_{k}")
            g_loc[k], delta[k], new_m[k], new_v[k] = t.reshape(shp), d2.reshape(shp), m2.reshape(shp), v2.reshape(shp)
            after = d2

    return (loss, dx[None], *[g_loc[k] for k in WEIGHTS], *[delta[k] for k in WEIGHTS],
            *[new_m[k] for k in WEIGHTS], *[new_v[k] for k in WEIGHTS])
```

```python
import functools

import jax
import jax.numpy as jnp
from jax import lax
from jax.experimental import pallas as pl
from jax.experimental.pallas import tpu as pltpu

F32 = jnp.float32
BF16 = jnp.bfloat16

EPS = 1e-5
D_STATE = 128
CHUNK = 128
LANES = 128
POOL_WINDOWS = (2, 4, 8, 16)
POOL_HALO = 16
CONV_WIDTH = 4
CONV_HALO = 8
N_DEV = 8
ADAM_LR = 0.001
ADAM_B1 = 0.9
ADAM_B2 = 0.999
ADAM_EPS = 1e-08
ADAM_WD = 0.01
ADAM_STEP = 10
VMEM_LIMIT = 52 * 1024 * 1024
MESH = pl.DeviceIdType.MESH
ANY = pl.BlockSpec(memory_space=pl.ANY)


def _cp(*sem):
    return pltpu.CompilerParams(dimension_semantics=sem, vmem_limit_bytes=VMEM_LIMIT)


def _tile(n, target, mult):
    best = None
    for t in range(mult, min(n, target) + 1, mult):
        if n % t == 0:
            best = t
    return best if best is not None else n


def _sigmoid(x):
    return jax.nn.sigmoid(x)


def _iota(shape, dim):
    return lax.broadcasted_iota(jnp.int32, shape, dim)


_DIMS = {"nn": (((1,), (0,)), ((), ())), "nt": (((1,), (1,)), ((), ())), "tn": (((0,), (0,)), ((), ()))}


def _dot(a, b, mode="nn"):
    return lax.dot_general(a.astype(BF16), b.astype(BF16), _DIMS[mode], preferred_element_type=F32)


DEP = pl.BlockSpec((8, LANES), lambda *_: (0, 0))


def _mm(a, b, mode, name, *, out_dtype=F32, res=None, epi=None, aux=None, dep=None):
    if mode == "nn":
        (M, K), (_, N) = a.shape, b.shape
    elif mode == "nt":
        (M, K), (N, _) = a.shape, b.shape
    else:
        (K, M), (_, N) = a.shape, b.shape
    if mode == "tn":
        tm, tn, tk = _tile(M, 512, 128), _tile(N, 512, 128), K
    else:
        tm, tn, tk = _tile(M, 1056, 16), _tile(N, 512, 128), _tile(K, 4096, 128)
    nk = K // tk
    a_spec = pl.BlockSpec((tk, tm), lambda i, j, k: (k, i)) if mode == "tn" else pl.BlockSpec((tm, tk), lambda i, j, k: (i, k))
    b_spec = pl.BlockSpec((tn, tk), lambda i, j, k: (j, k)) if mode == "nt" else pl.BlockSpec((tk, tn), lambda i, j, k: (k, j))
    o_spec = pl.BlockSpec((tm, tn), lambda i, j, k: (i, j))
    extra = [t for t in (res, aux) if t is not None]
    deps = [] if dep is None else [dep]

    def body(*refs):
        a_ref, b_ref = refs[0], refs[1]
        x_ref = refs[2] if extra else None
        outs = refs[2 + len(extra) + len(deps):]
        p = _dot(a_ref[...], b_ref[...], mode)

        def finish(r):
            if res is not None:
                outs[0][...] = (x_ref[...] + r).astype(out_dtype)
            elif epi == "relu2":
                outs[0][...] = r
                hid = jnp.maximum(r, 0.0)
                outs[1][...] = (hid * hid).astype(BF16)
            elif epi == "twin":
                outs[0][...] = r
                outs[1][...] = r.astype(BF16)
            elif epi == "drelu2":
                outs[0][...] = (r * (2.0 * jnp.maximum(x_ref[...], 0.0))).astype(BF16)
            else:
                outs[0][...] = r.astype(out_dtype)

        if nk == 1:
            finish(p)
        else:
            acc = outs[-1]
            k = pl.program_id(2)

            @pl.when(k == 0)
            def _():
                acc[...] = p

            @pl.when(k > 0)
            def _():
                acc[...] += p

            @pl.when(k == nk - 1)
            def _():
                finish(acc[...])

    if epi in ("relu2", "twin"):
        out_shape = (jax.ShapeDtypeStruct((M, N), F32), jax.ShapeDtypeStruct((M, N), BF16))
        out_specs = (o_spec, o_spec)
    elif epi == "drelu2":
        out_shape, out_specs = jax.ShapeDtypeStruct((M, N), BF16), o_spec
    else:
        out_shape, out_specs = jax.ShapeDtypeStruct((M, N), out_dtype), o_spec
    return pl.pallas_call(
        body, name=name, grid=(M // tm, N // tn, nk),
        in_specs=[a_spec, b_spec] + [o_spec] * len(extra) + [DEP] * len(deps),
        out_specs=out_specs, out_shape=out_shape,
        scratch_shapes=[pltpu.VMEM((tm, tn), F32)] if nk > 1 else [],
        compiler_params=_cp("parallel", "parallel", "arbitrary"),
    )(a, b, *extra, *deps)


def _rms_fwd(h, w, name, dep=None):
    T, D = h.shape
    tr = _tile(T, 1056, 16)
    deps = [] if dep is None else [dep]

    def body(h_ref, w_ref, *rest):
        x = h_ref[...]
        xn = x * lax.rsqrt(jnp.mean(x * x, axis=-1, keepdims=True) + EPS)
        rest[-1][...] = (xn * w_ref[...]).astype(BF16)

    return pl.pallas_call(
        body, name=name, grid=(T // tr,),
        in_specs=[pl.BlockSpec((tr, D), lambda i: (i, 0)), pl.BlockSpec((1, D), lambda i: (0, 0))] + [DEP] * len(deps),
        out_specs=pl.BlockSpec((tr, D), lambda i: (i, 0)), out_shape=jax.ShapeDtypeStruct((T, D), BF16),
        compiler_params=_cp("parallel"),
    )(h, w.reshape(1, D), *deps)


def _rms_bwd(h, w, dy, dres, name):
    T, D = h.shape
    tr = _tile(T, 528, 16)

    def body(h_ref, w_ref, dy_ref, dres_ref, dh_ref, dhb_ref, dw_ref):
        x = h_ref[...]
        rstd = lax.rsqrt(jnp.mean(x * x, axis=-1, keepdims=True) + EPS)
        xn = x * rstd
        dy = dy_ref[...]
        dxn = dy * w_ref[...]
        dh = dres_ref[...] + rstd * (dxn - xn * jnp.mean(dxn * xn, axis=-1, keepdims=True))
        dh_ref[...] = dh
        dhb_ref[...] = dh.astype(BF16)
        dw = jnp.sum(dy * xn, axis=0, keepdims=True)

        @pl.when(pl.program_id(0) == 0)
        def _():
            dw_ref[...] = dw

        @pl.when(pl.program_id(0) > 0)
        def _():
            dw_ref[...] += dw

    row = pl.BlockSpec((tr, D), lambda i: (i, 0))
    vec = pl.BlockSpec((1, D), lambda i: (0, 0))
    return pl.pallas_call(
        body, name=name, grid=(T // tr,),
        in_specs=[row, vec, row, row], out_specs=(row, row, vec),
        out_shape=(jax.ShapeDtypeStruct((T, D), F32), jax.ShapeDtypeStruct((T, D), BF16), jax.ShapeDtypeStruct((1, D), F32)),
        compiler_params=_cp("arbitrary"),
    )(h, w.reshape(1, D), dy, dres)


def _final_loss(h, w, target, first_row, name):
    T, D = h.shape
    tr = CHUNK
    assert first_row == tr

    def body(h_ref, w_ref, t_ref, loss_ref, dh_ref, dhb_ref, dw_ref):
        i = pl.program_id(0)
        x = h_ref[...]
        rstd = lax.rsqrt(jnp.mean(x * x, axis=-1, keepdims=True) + EPS)
        xn = x * rstd
        w = w_ref[...]
        live = i > 0
        err = jnp.where(live, xn * w - t_ref[...], 0.0)
        part = 0.5 * jnp.sum(jnp.mean(err * err, axis=-1, keepdims=True), axis=0, keepdims=True)
        dout = err * (1.0 / D)
        dxn = dout * w
        dh = rstd * (dxn - xn * jnp.mean(dxn * xn, axis=-1, keepdims=True))
        dh_ref[...] = dh
        dhb_ref[...] = dh.astype(BF16)
        dw = jnp.sum(dout * xn, axis=0, keepdims=True)

        @pl.when(i == 0)
        def _():
            loss_ref[...] = part
            dw_ref[...] = dw

        @pl.when(i > 0)
        def _():
            loss_ref[...] += part
            dw_ref[...] += dw

    row = pl.BlockSpec((tr, D), lambda i: (i, 0))
    vec = pl.BlockSpec((1, D), lambda i: (0, 0))
    return pl.pallas_call(
        body, name=name, grid=(T // tr,),
        in_specs=[row, vec, pl.BlockSpec((tr, D), lambda i: (jnp.maximum(i - 1, 0), 0))],
        out_specs=(pl.BlockSpec((1, 1), lambda i: (0, 0)), row, row, vec),
        out_shape=(jax.ShapeDtypeStruct((1, 1), F32), jax.ShapeDtypeStruct((T, D), F32), jax.ShapeDtypeStruct((T, D), BF16),
                   jax.ShapeDtypeStruct((1, D), F32)),
        compiler_params=_cp("arbitrary"),
    )(h, w.reshape(1, D), target)


def _gate_fwd(pg, bg, y_pool, y_ssd, name):
    T, D = y_pool.shape
    tr = _tile(T, 528, 16)

    def body(gp_ref, gs_ref, bp_ref, bs_ref, yp_ref, ys_ref, o_ref):
        gp = _sigmoid(gp_ref[...] + bp_ref[...])
        gs = _sigmoid(gs_ref[...] + bs_ref[...])
        o_ref[...] = (gp * yp_ref[...] + gs * ys_ref[...]).astype(BF16)

    row = pl.BlockSpec((tr, D), lambda i: (i, 0))
    row1 = pl.BlockSpec((tr, D), lambda i: (i, 1))
    vec = pl.BlockSpec((1, D), lambda i: (0, 0))
    vec1 = pl.BlockSpec((1, D), lambda i: (0, 1))
    b2 = bg.reshape(1, 2 * D)
    return pl.pallas_call(
        body, name=name, grid=(T // tr,),
        in_specs=[row, row1, vec, vec1, row, row], out_specs=row,
        out_shape=jax.ShapeDtypeStruct((T, D), BF16), compiler_params=_cp("parallel"),
    )(pg, pg, b2, b2, y_pool, y_ssd)


def _gate_bwd(pg, bg, y_pool, y_ssd, dmix, name):
    T, D = y_pool.shape
    tr = _tile(T, 528, 16)

    def body(gp_ref, gs_ref, bp_ref, bs_ref, yp_ref, ys_ref, dm_ref, dg_ref, dyp_ref, dys_ref, db_ref):
        gp = _sigmoid(gp_ref[...] + bp_ref[...])
        gs = _sigmoid(gs_ref[...] + bs_ref[...])
        dm = dm_ref[...]
        dyp_ref[...] = (dm * gp).astype(BF16)
        dys_ref[...] = (dm * gs).astype(BF16)
        dgp = dm * yp_ref[...] * gp * (1.0 - gp)
        dgs = dm * ys_ref[...] * gs * (1.0 - gs)
        dg_ref[:, :D] = dgp.astype(BF16)
        dg_ref[:, D:] = dgs.astype(BF16)
        db = jnp.concatenate([jnp.sum(dgp, axis=0, keepdims=True), jnp.sum(dgs, axis=0, keepdims=True)], axis=1)

        @pl.when(pl.program_id(0) == 0)
        def _():
            db_ref[...] = db

        @pl.when(pl.program_id(0) > 0)
        def _():
            db_ref[...] += db

    row = pl.BlockSpec((tr, D), lambda i: (i, 0))
    row1 = pl.BlockSpec((tr, D), lambda i: (i, 1))
    wide = pl.BlockSpec((tr, 2 * D), lambda i: (i, 0))
    vec = pl.BlockSpec((1, D), lambda i: (0, 0))
    vec1 = pl.BlockSpec((1, D), lambda i: (0, 1))
    vec2 = pl.BlockSpec((1, 2 * D), lambda i: (0, 0))
    b2 = bg.reshape(1, 2 * D)
    return pl.pallas_call(
        body, name=name, grid=(T // tr,),
        in_specs=[row, row1, vec, vec1, row, row, row], out_specs=(wide, row, row, vec2),
        out_shape=(jax.ShapeDtypeStruct((T, 2 * D), BF16), jax.ShapeDtypeStruct((T, D), BF16),
                   jax.ShapeDtypeStruct((T, D), BF16), jax.ShapeDtypeStruct((1, 2 * D), F32)),
        compiler_params=_cp("arbitrary"),
    )(pg, pg, b2, b2, y_pool, y_ssd, dmix)


def _pool_count(c, pad, window):
    pos = c * CHUNK + _iota((CHUNK, 1), 0) - pad
    return jnp.clip(pos + 1, 1, window).astype(F32)


def _by_group(g, vals):
    out = vals[-1]
    for k in range(len(vals) - 2, -1, -1):
        out = jnp.where(g == k, vals[k], out)
    return out


def _by_row_block(rows, vals, block):
    out = vals[0]
    for r in range(1, len(vals)):
        out = jnp.where(rows >= r * block, vals[r], out)
    return out


def _pool_fwd(u, wg, scale, pad, name):
    T, C = u.shape
    G, Cg, _ = wg.shape
    nc = T // CHUNK

    def body(u_ref, wg_ref, s_ref, p_ref, y_ref):
        g = pl.program_id(0)
        window = _by_group(g, POOL_WINDOWS)

        def chunk(c, carry):
            r0 = pl.multiple_of(c * CHUNK, CHUNK)
            h0 = pl.multiple_of(jnp.maximum(r0 - POOL_HALO, 0), 8)
            halo = jnp.where(c > 0, u_ref[pl.ds(h0, POOL_HALO), :], 0.0)
            xc = u_ref[pl.ds(r0, CHUNK), :]
            s = jnp.concatenate([halo, xc], axis=0)
            sums = []
            k = 1
            while k < POOL_WINDOWS[-1]:
                s = s + pltpu.roll(s, k, 0)
                k *= 2
                if k in POOL_WINDOWS:
                    sums.append(s[POOL_HALO:])
            wsum = _by_group(g, sums)
            pooled = wsum / _pool_count(c, pad, window) - xc
            pb = pooled.astype(BF16)
            p_ref[pl.ds(r0, CHUNK), :] = pb
            y_ref[pl.ds(r0, CHUNK), :] = (_dot(pb, wg_ref[0]) * s_ref[...]).astype(BF16)
            return carry

        lax.fori_loop(0, nc, chunk, 0)

    col = pl.BlockSpec((T, Cg), lambda g: (0, g))
    return pl.pallas_call(
        body, name=name, grid=(G,),
        in_specs=[col, pl.BlockSpec((1, Cg, Cg), lambda g: (g, 0, 0)), pl.BlockSpec((1, Cg), lambda g: (0, g))],
        out_specs=(col, col),
        out_shape=(jax.ShapeDtypeStruct((T, C), BF16), jax.ShapeDtypeStruct((T, C), BF16)),
        compiler_params=_cp("parallel"),
    )(u, wg, scale)


def _pool_bwd(pooled, wg, scale, dy, pad, name):
    T, C = dy.shape
    G, Cg, _ = wg.shape
    nc = T // CHUNK

    def body(p_ref, wg_ref, s_ref, dy_ref, du_ref, dwg_ref, ds_ref, halo_ref):
        g = pl.program_id(0)
        window = _by_group(g, POOL_WINDOWS)
        halo_ref[...] = jnp.zeros_like(halo_ref)
        dwg_ref[...] = jnp.zeros_like(dwg_ref)
        ds_ref[...] = jnp.zeros_like(ds_ref)

        def chunk(i, carry):
            c = nc - 1 - i
            r0 = pl.multiple_of(c * CHUNK, CHUNK)
            pb = p_ref[pl.ds(r0, CHUNK), :]
            dyc = dy_ref[pl.ds(r0, CHUNK), :]
            w = wg_ref[0]
            ypre = _dot(pb, w)
            ds_ref[...] += jnp.sum(dyc * ypre, axis=0, keepdims=True)
            dyp = (dyc * s_ref[...]).astype(BF16)
            dwg_ref[0] += _dot(pb, dyp, "tn")
            dpool = _dot(dyp, w, "nt")
            q = dpool / _pool_count(c, pad, window)
            s = jnp.concatenate([q, halo_ref[...]], axis=0)
            n = CHUNK + POOL_HALO
            sums = []
            k = 1
            while k < POOL_WINDOWS[-1]:
                s = s + pltpu.roll(s, n - k, 0)
                k *= 2
                if k in POOL_WINDOWS:
                    sums.append(s[:CHUNK])
            du = _by_group(g, sums) - dpool
            rows = r0 + _iota((CHUNK, 1), 0)
            du_ref[pl.ds(r0, CHUNK), :] = jnp.where(rows >= pad, du, 0.0).astype(BF16)
            halo_ref[...] = q[:POOL_HALO]
            return carry

        lax.fori_loop(0, nc, chunk, 0)

    col = pl.BlockSpec((T, Cg), lambda g: (0, g))
    return pl.pallas_call(
        body, name=name, grid=(G,),
        in_specs=[col, pl.BlockSpec((1, Cg, Cg), lambda g: (g, 0, 0)), pl.BlockSpec((1, Cg), lambda g: (0, g)), col],
        out_specs=(col, pl.BlockSpec((1, Cg, Cg), lambda g: (g, 0, 0)), pl.BlockSpec((1, Cg), lambda g: (0, g))),
        out_shape=(jax.ShapeDtypeStruct((T, C), BF16), jax.ShapeDtypeStruct((G, Cg, Cg), F32), jax.ShapeDtypeStruct((1, C), F32)),
        scratch_shapes=[pltpu.VMEM((POOL_HALO, Cg), F32)],
        compiler_params=_cp("parallel"),
    )(pooled, wg, scale, dy)


def _conv_pre(x_ref, w_ref, b_ref, c, r0):
    h0 = pl.multiple_of(jnp.maximum(r0 - CONV_HALO, 0), 8)
    halo = jnp.where(c > 0, x_ref[pl.ds(h0, CONV_HALO), :], 0.0)
    xe = jnp.concatenate([halo, x_ref[pl.ds(r0, CHUNK), :]], axis=0)
    y = jnp.broadcast_to(b_ref[...], (CHUNK, xe.shape[1]))
    for k in range(CONV_WIDTH):
        shift = CONV_WIDTH - 1 - k
        xs = xe if shift == 0 else pltpu.roll(xe, shift, 0)
        y = y + xs[CONV_HALO:] * w_ref[k:k + 1, :]
    return y, xe


def _conv_fwd(x, w, b, name):
    T = x.shape[0]
    C = w.shape[1]
    tc = _tile(C, 256, 128)
    nc = T // CHUNK

    def body(x_ref, w_ref, b_ref, o_ref):
        def chunk(c, carry):
            r0 = pl.multiple_of(c * CHUNK, CHUNK)
            y, _ = _conv_pre(x_ref, w_ref, b_ref, c, r0)
            o_ref[pl.ds(r0, CHUNK), :] = y * _sigmoid(y)
            return carry

        lax.fori_loop(0, nc, chunk, 0)

    col = pl.BlockSpec((T, tc), lambda j: (0, j))
    return pl.pallas_call(
        body, name=name, grid=(C // tc,),
        in_specs=[col, pl.BlockSpec((CONV_WIDTH, tc), lambda j: (0, j)), pl.BlockSpec((1, tc), lambda j: (0, j))],
        out_specs=col, out_shape=jax.ShapeDtypeStruct((T, C), F32), compiler_params=_cp("parallel"),
    )(x, w, b)


def _conv_bwd(x, w, b, dact, pad, name):
    T = x.shape[0]
    C = w.shape[1]
    tc = _tile(C, 256, 128)
    nc = T // CHUNK

    def body(x_ref, w_ref, b_ref, da_ref, dx_ref, dw_ref, db_ref, halo_ref):
        halo_ref[...] = jnp.zeros_like(halo_ref)
        dw_ref[...] = jnp.zeros_like(dw_ref)
        db_ref[...] = jnp.zeros_like(db_ref)

        def chunk(i, carry):
            c = nc - 1 - i
            r0 = pl.multiple_of(c * CHUNK, CHUNK)
            y, xe = _conv_pre(x_ref, w_ref, b_ref, c, r0)
            sg = _sigmoid(y)
            dpre = da_ref[pl.ds(r0, CHUNK), :] * (sg * (1.0 + y * (1.0 - sg)))
            db_ref[...] += jnp.sum(dpre, axis=0, keepdims=True)
            de = jnp.concatenate([dpre, halo_ref[...]], axis=0)
            n = CHUNK + CONV_HALO
            dx = jnp.zeros_like(dpre)
            for k in range(CONV_WIDTH):
                shift = CONV_WIDTH - 1 - k
                xs = xe if shift == 0 else pltpu.roll(xe, shift, 0)
                dw_ref[k:k + 1, :] += jnp.sum(dpre * xs[CONV_HALO:], axis=0, keepdims=True)
                ds = de if shift == 0 else pltpu.roll(de, n - shift, 0)
                dx = dx + ds[:CHUNK] * w_ref[k:k + 1, :]
            rows = r0 + _iota((CHUNK, 1), 0)
            dx_ref[pl.ds(r0, CHUNK), :] = jnp.where(rows >= pad, dx, 0.0).astype(BF16)
            halo_ref[...] = dpre[:CONV_HALO]
            return carry

        lax.fori_loop(0, nc, chunk, 0)

    col = pl.BlockSpec((T, tc), lambda j: (0, j))
    wspec = pl.BlockSpec((CONV_WIDTH, tc), lambda j: (0, j))
    bspec = pl.BlockSpec((1, tc), lambda j: (0, j))
    return pl.pallas_call(
        body, name=name, grid=(C // tc,),
        in_specs=[col, wspec, bspec, col], out_specs=(col, wspec, bspec),
        out_shape=(jax.ShapeDtypeStruct((T, C), BF16), jax.ShapeDtypeStruct((CONV_WIDTH, C), F32), jax.ShapeDtypeStruct((1, C), F32)),
        scratch_shapes=[pltpu.VMEM((CONV_HALO, tc), F32)],
        compiler_params=_cp("parallel"),
    )(x, w, b, dact)


def _cumsum_rows(x, reverse=False):
    n = x.shape[0]
    idx = _iota(x.shape, 0)
    k = 1
    while k < n:
        if reverse:
            x = x + jnp.where(idx < n - k, pltpu.roll(x, n - k, 0), 0.0)
        else:
            x = x + jnp.where(idx >= k, pltpu.roll(x, k, 0), 0.0)
        k *= 2
    return x


def _softplus(x):
    return jnp.maximum(x, 0.0) + jnp.log1p(jnp.exp(-jnp.abs(x)))


def _head_selector(n_heads, width):
    lane = jnp.arange(n_heads * width)[None, :] // width
    return (lane == jnp.arange(LANES)[:, None]).astype(BF16)


def _dot_exact(x, sel, parts, mode="nn"):
    acc = None
    for _ in range(parts):
        piece = x.astype(BF16)
        x = x - piece.astype(F32)
        t = lax.dot_general(piece, sel, _DIMS[mode], preferred_element_type=F32)
        acc = t if acc is None else acc + t
    return acc


def _ssd_decays(dt, cs, sel_p_ref, sel_q_ref):
    cs_b = _dot_exact(cs, sel_q_ref[...], 3)
    dt_x = _dot_exact(dt, sel_p_ref[...], 3)
    cs_x = _dot_exact(cs, sel_p_ref[...], 3)
    return cs_b, dt_x, jnp.exp(cs_x), jnp.exp(cs_x[CHUNK - 1:CHUNK, :] - cs_x)


def _ssd_common(c, pad, n_heads, dtr_ref, dtb_ref, al_ref):
    rows = c * CHUNK + _iota((CHUNK, 1), 0)
    valid = rows >= pad
    live = jnp.logical_and(valid, _iota((1, LANES), 1) < n_heads)
    pre = dtr_ref[...] + dtb_ref[...]
    dt = jnp.where(live, _softplus(pre), 0.0)
    a = -jnp.exp(al_ref[...])
    cs = _cumsum_rows(dt * a)
    return valid, live, dt, a, cs, cs.T, _sigmoid(pre)


def _ssd_specs(T, DI, GN, cfirst):
    xcol = DI // GN

    def at(col):
        return lambda c: (cfirst(c), col)

    x = pl.BlockSpec((CHUNK, DI), at(0))
    b = pl.BlockSpec((CHUNK, GN), at(xcol))
    cm = pl.BlockSpec((CHUNK, GN), at(xcol + 1))
    dt = pl.BlockSpec((CHUNK, LANES), at(0))
    vec = pl.BlockSpec((1, LANES), lambda c: (0, 0))
    nw = pl.BlockSpec((1, DI), lambda c: (0, 0))
    return x, b, cm, dt, vec, nw


def _ssd_fwd(xbc, pdt, pz, dt_bias, a_log, d_skip, norm_w, pad, n_heads, name):
    T = xbc.shape[0]
    DI = pz.shape[1]
    P = DI // n_heads
    GN = (xbc.shape[1] - DI) // 2
    G = GN // D_STATE
    R = n_heads // G
    GW = R * P
    nc = T // CHUNK
    Q, N = CHUNK, D_STATE

    def body(x_ref, b_ref, c_ref, dtr_ref, z_ref, dtb_ref, al_ref, dsk_ref, nw_ref, sel_p_ref, sel_q_ref,
             y_ref, yn_ref, prev_ref, s_ref):
        c = pl.program_id(0)

        @pl.when(c == 0)
        def _():
            s_ref[...] = jnp.zeros_like(s_ref)

        valid, _, dt, _, cs, cst, _ = _ssd_common(c, pad, n_heads, dtr_ref, dtb_ref, al_ref)
        cs_b, dt_x, e_x, dec_x = _ssd_decays(dt, cs, sel_p_ref, sel_q_ref)
        e_last = jnp.exp(cs[Q - 1:Q, :])
        tri = _iota((Q, Q), 0) >= _iota((Q, Q), 1)
        state_rows = _iota((GW, 1), 0)
        head_lane = _iota((1, GW), 1)
        for g in range(G):
            gs = slice(g * GW, (g + 1) * GW)
            bg = jnp.where(valid, b_ref[:, g * N:(g + 1) * N], 0.0).astype(BF16)
            cg = jnp.where(valid, c_ref[:, g * N:(g + 1) * N], 0.0).astype(BF16)
            xg = jnp.where(valid, x_ref[:, gs], 0.0)
            sg = s_ref[gs, :]
            prev_ref[0, gs, :] = sg
            cb = _dot(cg, bg, "nt")
            xdt = xg * dt_x[:, gs]
            yg = _dot(cg, sg, "nt") * e_x[:, gs]
            for r in range(R):
                h = g * R + r
                lmat = jnp.exp(jnp.where(tri, cs_b[:, h * Q:(h + 1) * Q] - cst[h:h + 1, :], -jnp.inf))
                in_head = jnp.logical_and(head_lane >= r * P, head_lane < (r + 1) * P)
                yg = yg + _dot(cb * lmat, jnp.where(in_head, xdt, 0.0))
            y_ref[:, gs] = yg
            decay = _by_row_block(state_rows, [e_last[:, g * R + r:g * R + r + 1] for r in range(R)], P)
            s_ref[gs, :] = sg * decay + _dot(xdt * dec_x[:, gs], bg, "tn")
            z = z_ref[:, gs]
            gz = (yg + xg * dsk_ref[:, gs]) * (z * _sigmoid(z))
            rstd = lax.rsqrt(jnp.mean(gz * gz, axis=-1, keepdims=True) + EPS)
            yn_ref[:, gs] = ((gz * rstd) * nw_ref[:, gs]).astype(BF16)

    x_s, b_s, c_s, dt_s, vec, nw = _ssd_specs(T, DI, GN, lambda c: c)
    wide = pl.BlockSpec((Q, DI), lambda c: (c, 0))
    sel_p = pl.BlockSpec((LANES, DI), lambda c: (0, 0))
    sel_q = pl.BlockSpec((LANES, n_heads * Q), lambda c: (0, 0))
    return pl.pallas_call(
        body, name=name, grid=(nc,),
        in_specs=[x_s, b_s, c_s, dt_s, wide, vec, vec, nw, nw, sel_p, sel_q],
        out_specs=(wide, wide, pl.BlockSpec((1, DI, N), lambda c: (c, 0, 0))),
        out_shape=(jax.ShapeDtypeStruct((T, DI), F32), jax.ShapeDtypeStruct((T, DI), BF16), jax.ShapeDtypeStruct((nc, DI, N), F32)),
        scratch_shapes=[pltpu.VMEM((DI, N), F32)],
        compiler_params=_cp("arbitrary"),
    )(xbc, xbc, xbc, pdt, pz, dt_bias, a_log, jnp.repeat(d_skip[:, :n_heads], P, axis=1), norm_w,
      _head_selector(n_heads, P), _head_selector(n_heads, Q))


def _ssd_bwd(xbc, pdt, pz, dt_bias, a_log, d_skip, norm_w, y, prev, dyn, pad, n_heads, name):
    T, W = xbc.shape
    DI = pz.shape[1]
    P = DI // n_heads
    GN = (W - DI) // 2
    G = GN // D_STATE
    R = n_heads // G
    GW = R * P
    nc = T // CHUNK
    Q, N = CHUNK, D_STATE

    def body(x_ref, b_ref, c_ref, dtr_ref, z_ref, dtb_ref, al_ref, dsk_ref, nw_ref, y_ref, prev_ref, next_ref, dyn_ref,
             sel_p_ref, sel_q_ref, sel_pt_ref,
             dxbc_ref, dz_ref, ddt_ref, ddtb_ref, dal_ref, ddsk_ref, dnw_ref, ds_ref):
        i = pl.program_id(0)
        c = nc - 1 - i

        @pl.when(i == 0)
        def _():
            ds_ref[...] = jnp.zeros_like(ds_ref)
            ddtb_ref[...] = jnp.zeros_like(ddtb_ref)
            dal_ref[...] = jnp.zeros_like(dal_ref)
            ddsk_ref[...] = jnp.zeros_like(ddsk_ref)
            dnw_ref[...] = jnp.zeros_like(dnw_ref)

        valid, live, dt, a, cs, cst, sig_pre = _ssd_common(c, pad, n_heads, dtr_ref, dtb_ref, al_ref)
        cs_b, dt_x, e_x, dec_x = _ssd_decays(dt, cs, sel_p_ref, sel_q_ref)
        e_last = jnp.exp(cs[Q - 1:Q, :])
        tri = _iota((Q, Q), 0) >= _iota((Q, Q), 1)
        tri_t = _iota((Q, Q), 0) <= _iota((Q, Q), 1)
        state_rows = _iota((GW, 1), 0)
        head_lane = _iota((1, GW), 1)
        lane = _iota((1, LANES), 1)
        head_rows = _iota((LANES, 1), 0)
        s_dy_cs = jnp.zeros((Q, LANES), F32)
        s_x_bds = jnp.zeros((Q, LANES), F32)
        s_x_dxdt = jnp.zeros((Q, LANES), F32)
        dcs_rows = jnp.zeros((Q, LANES), F32)
        dcs_cols = jnp.zeros((LANES, Q), F32)
        c_end = jnp.zeros((1, LANES), F32)
        dsk_rows = []
        for g in range(G):
            gs = slice(g * GW, (g + 1) * GW)
            bg = jnp.where(valid, b_ref[:, g * N:(g + 1) * N], 0.0).astype(BF16)
            cg = jnp.where(valid, c_ref[:, g * N:(g + 1) * N], 0.0).astype(BF16)
            xg = jnp.where(valid, x_ref[:, gs], 0.0)
            s_prev = prev_ref[0, gs, :]
            dsg = ds_ref[gs, :]
            end = dsg * next_ref[0, gs, :]
            yg = y_ref[:, gs]
            dsk = dsk_ref[:, gs]
            ytot = yg + xg * dsk
            z = z_ref[:, gs]
            sz = _sigmoid(z)
            silu = z * sz
            gz = ytot * silu
            rstd = lax.rsqrt(jnp.mean(gz * gz, axis=-1, keepdims=True) + EPS)
            gn = gz * rstd
            dyn_g = dyn_ref[:, gs]
            dnw_ref[:, gs] += jnp.sum(dyn_g * gn, axis=0, keepdims=True)
            dgn = dyn_g * nw_ref[:, gs]
            dgz = rstd * (dgn - gn * jnp.mean(dgn * gn, axis=-1, keepdims=True))
            dz_ref[:, gs] = (dgz * ytot * (sz * (1.0 + z * (1.0 - sz)))).astype(BF16)
            dy = dgz * silu
            dsk_rows.append(jnp.sum(dy * xg, axis=0, keepdims=True))
            cb = _dot(cg, bg, "nt")
            cb_t = _dot(bg, cg, "nt")
            bds = _dot(bg, dsg, "nt") * dec_x[:, gs]
            csg = _dot(cg, s_prev, "nt")
            xdt = xg * dt_x[:, gs]
            dxdt = bds
            dcb = jnp.zeros((Q, Q), F32)
            dcb_t = jnp.zeros((Q, Q), F32)
            for r in range(R):
                h = g * R + r
                cs_col = cs_b[:, h * Q:(h + 1) * Q]
                cs_row = cst[h:h + 1, :]
                lmat = jnp.exp(jnp.where(tri, cs_col - cs_row, -jnp.inf))
                lmat_t = jnp.exp(jnp.where(tri_t, cs_row - cs_col, -jnp.inf))
                in_head = jnp.logical_and(head_lane >= r * P, head_lane < (r + 1) * P)
                dyr = jnp.where(in_head, dy, 0.0)
                dm = _dot(dyr, xdt, "nt")
                dcb = dcb + dm * lmat
                dcb_t = dcb_t + _dot(xdt, dyr, "nt") * lmat_t
                w_rc = dm * (cb * lmat)
                dcs_rows = jnp.where(lane == h, jnp.sum(w_rc, axis=1, keepdims=True), dcs_rows)
                dcs_cols = jnp.where(head_rows == h, jnp.sum(w_rc, axis=0, keepdims=True), dcs_cols)
                dxdt = dxdt + _dot(cb_t * lmat_t, dyr)
            sel_t = sel_pt_ref[gs, :]
            s_dy_cs = s_dy_cs + _dot_exact(dy * csg, sel_t, 2)
            s_x_bds = s_x_bds + _dot_exact(xg * bds, sel_t, 2)
            s_x_dxdt = s_x_dxdt + _dot_exact(xg * dxdt, sel_t, 2)
            c_end = c_end + jnp.sum(jnp.sum(end, axis=1, keepdims=True) * sel_t.astype(F32), axis=0, keepdims=True)
            dye = dy * e_x[:, gs]
            dc = _dot(dcb, bg) + _dot(dye, s_prev)
            db = _dot(dcb_t, cg) + _dot(xdt * dec_x[:, gs], dsg)
            decay = _by_row_block(state_rows, [e_last[:, g * R + r:g * R + r + 1] for r in range(R)], P)
            ds_ref[gs, :] = dsg * decay + _dot(dye, cg, "tn")
            dxbc_ref[:, gs] = jnp.where(valid, dxdt * dt_x[:, gs] + dy * dsk, 0.0)
            dxbc_ref[:, DI + g * N:DI + (g + 1) * N] = jnp.where(valid, db, 0.0)
            dxbc_ref[:, DI + GN + g * N:DI + GN + (g + 1) * N] = jnp.where(valid, dc, 0.0)
        dcs = s_dy_cs * jnp.exp(cs) - dt * s_x_bds
        da_cs = _cumsum_rows(dcs + (dcs_rows - dcs_cols.T), reverse=True) + c_end
        ddt_all = jnp.where(live, da_cs * a + s_x_dxdt, 0.0)
        ddt_raw = ddt_all * sig_pre
        ddt_ref[...] = ddt_raw.astype(BF16)
        ddtb_ref[...] += jnp.sum(ddt_raw, axis=0, keepdims=True)
        dal_ref[...] += jnp.sum(da_cs * dt, axis=0, keepdims=True) * a
        dsk_all = jnp.broadcast_to(jnp.concatenate(dsk_rows, axis=1), (8, DI))
        ddsk_ref[...] += _dot_exact(dsk_all, sel_pt_ref[...], 3)[0:1]

    rev = lambda i: nc - 1 - i
    x_s, b_s, c_s, dt_s, vec, nw = _ssd_specs(T, DI, GN, rev)
    wide = pl.BlockSpec((Q, DI), lambda i: (rev(i), 0))
    st = pl.BlockSpec((1, DI, N), lambda i: (rev(i), 0, 0))
    st_next = pl.BlockSpec((1, DI, N), lambda i: (jnp.minimum(rev(i) + 1, nc - 1), 0, 0))
    sel_p = pl.BlockSpec((LANES, DI), lambda i: (0, 0))
    sel_q = pl.BlockSpec((LANES, n_heads * Q), lambda i: (0, 0))
    sel_pt = pl.BlockSpec((DI, LANES), lambda i: (0, 0))
    sel = _head_selector(n_heads, P)
    return pl.pallas_call(
        body, name=name, grid=(nc,),
        in_specs=[x_s, b_s, c_s, dt_s, wide, vec, vec, nw, nw, wide, st, st_next, wide, sel_p, sel_q, sel_pt],
        out_specs=(pl.BlockSpec((Q, W), lambda i: (rev(i), 0)), wide, dt_s, vec, vec, vec, nw),
        out_shape=(jax.ShapeDtypeStruct((T, W), F32), jax.ShapeDtypeStruct((T, DI), BF16), jax.ShapeDtypeStruct((T, LANES), BF16),
                   jax.ShapeDtypeStruct((1, LANES), F32), jax.ShapeDtypeStruct((1, LANES), F32),
                   jax.ShapeDtypeStruct((1, LANES), F32), jax.ShapeDtypeStruct((1, DI), F32)),
        scratch_shapes=[pltpu.VMEM((DI, N), F32)],
        compiler_params=_cp("arbitrary"),
    )(xbc, xbc, xbc, pdt, pz, dt_bias, a_log, jnp.repeat(d_skip[:, :n_heads], P, axis=1), norm_w, y, prev, prev, dyn,
      sel, _head_selector(n_heads, Q), sel.T)


def _adamw(w, g, m, v, name):
    layers, rows, cols = w.shape
    tr = _tile(rows, 256, 8)

    def body(w_ref, g_ref, m_ref, v_ref, d_ref, nm_ref, nv_ref):
        g = g_ref[...]
        m = ADAM_B1 * m_ref[...] + (1.0 - ADAM_B1) * g
        v = ADAM_B2 * v_ref[...] + (1.0 - ADAM_B2) * (g * g)
        m_hat = m / (1.0 - ADAM_B1 ** ADAM_STEP)
        v_hat = v / (1.0 - ADAM_B2 ** ADAM_STEP)
        d_ref[...] = -ADAM_LR * (m_hat / (jnp.sqrt(v_hat) + ADAM_EPS) + ADAM_WD * w_ref[...])
        nm_ref[...] = m
        nv_ref[...] = v

    blk = pl.BlockSpec((1, tr, cols), lambda l, i: (l, i, 0))
    out = jax.ShapeDtypeStruct((layers, rows, cols), F32)
    return pl.pallas_call(
        body, name=name, grid=(layers, rows // tr), in_specs=[blk] * 4, out_specs=(blk,) * 3, out_shape=(out,) * 3,
        compiler_params=_cp("parallel", "parallel"),
    )(w, g, m, v)


def _sum_stack(parts, name, out_dtype=F32):
    rows, cols = parts[0].shape
    tr = _tile(rows, 512, 16)

    def body(*refs):
        acc = refs[0][...].astype(F32)
        for r in refs[1:-1]:
            acc = acc + r[...].astype(F32)
        refs[-1][...] = acc.astype(out_dtype)

    blk = pl.BlockSpec((tr, cols), lambda i: (i, 0))
    return pl.pallas_call(
        body, name=name, grid=(rows // tr,), in_specs=[blk] * len(parts), out_specs=blk,
        out_shape=jax.ShapeDtypeStruct((rows, cols), out_dtype), compiler_params=_cp("parallel"),
    )(*parts)


def _place():
    return lax.axis_index("x"), lax.axis_index("y"), lax.axis_index("c")


def _other_chips(x, y):
    return [(1 - x, y), (x, 1 - y), (1 - x, 1 - y)]


def _all_gather(shards, name):
    nb = len(shards)

    def body(*refs):
        ins, outs = refs[:nb], refs[nb:2 * nb]
        send_sems, recv_sems, local_sems = refs[2 * nb:]
        x, y, c = _place()
        me, sibling = (x, y, c), (x, y, 1 - c)
        chips = _other_chips(x, y)

        def copy(q, k, block, to, src=None):
            dst = outs[q].at[4 * block[0] + 2 * block[1] + block[2]]
            return pltpu.make_async_remote_copy(
                src_ref=dst if src is None else src, dst_ref=dst,
                send_sem=send_sems.at[7 * q + k], recv_sem=recv_sems.at[7 * q + k], device_id=to, device_id_type=MESH)

        started = []
        for q in range(nb):
            mine = pltpu.make_async_copy(ins[q], outs[q].at[4 * x + 2 * y + c], local_sems.at[q])
            mine.start()
            started.append(mine)
        first = []
        for q in range(nb):
            first.append(copy(q, 0, me, sibling, src=ins[q]))
            first += [copy(q, 1 + j, me, (*chip, c), src=ins[q]) for j, chip in enumerate(chips)]
        for cp in first:
            cp.start()
        passed = []
        for j, chip in enumerate(chips):
            for q in range(nb):
                copy(q, 1 + j, (*chip, c), me).wait_recv()
                fwd = copy(q, 4 + j, (*chip, c), sibling)
                fwd.start()
                passed.append(fwd)
        for q in range(nb):
            copy(q, 0, sibling, me).wait_recv()
            for j, chip in enumerate(chips):
                copy(q, 4 + j, (*chip, 1 - c), me).wait_recv()
        for cp in first + passed:
            cp.wait_send()
        for mine in started:
            mine.wait()

    return pl.pallas_call(
        body, name=name, in_specs=[ANY] * nb, out_specs=tuple([ANY] * nb),
        out_shape=tuple(jax.ShapeDtypeStruct((N_DEV,) + s.shape, s.dtype) for s in shards),
        scratch_shapes=[pltpu.SemaphoreType.DMA((7 * nb,)), pltpu.SemaphoreType.DMA((7 * nb,)), pltpu.SemaphoreType.DMA((nb,))],
    )(*shards)


HBM = pl.BlockSpec(memory_space=pltpu.HBM)
SEM = pl.BlockSpec(memory_space=pltpu.SEMAPHORE)
N_PEERS = N_DEV - 1


def _peer(k, x, y, c):
    return (1 - x if k & 4 else x, 1 - y if k & 2 else y, 1 - c if k & 1 else c)


def _direct_copies(gather, srcs, lands, send_sems, recv_sems):
    x, y, c = _place()
    copies = []
    for q in range(len(srcs)):
        for k in range(1, N_DEV):
            px, py, pc = _peer(k, x, y, c)
            if gather:
                src, dst = srcs[q], lands[q].at[4 * x + 2 * y + c]
            else:
                src, dst = srcs[q].at[4 * px + 2 * py + pc], lands[q].at[k - 1]
            copies.append(pltpu.make_async_remote_copy(
                src_ref=src, dst_ref=dst, send_sem=send_sems.at[N_PEERS * q + k - 1], recv_sem=recv_sems.at[N_PEERS * q + k - 1],
                device_id=(px, py, pc), device_id_type=MESH))
    return copies


def _exchange_start(gather, srcs, lands, name):
    n = len(srcs)

    def body(*refs):
        send_sems, recv_sems = refs[2 * n], refs[2 * n + 1]
        for cp in _direct_copies(gather, refs[:n], refs[n:2 * n], send_sems, recv_sems):
            cp.start()
        refs[-1][...] = jnp.zeros_like(refs[-1])

    held = [pltpu.with_memory_space_constraint(t, pltpu.HBM) for t in list(srcs) + list(lands)]
    out = pl.pallas_call(
        body, name=name,
        out_shape=(pltpu.SemaphoreType.DMA((N_PEERS * n,)), pltpu.SemaphoreType.DMA((N_PEERS * n,)),
                   *[pltpu.HBM(t.shape, t.dtype) for t in held], jax.ShapeDtypeStruct((8, LANES), F32)),
        in_specs=[HBM] * (2 * n), out_specs=(SEM, SEM, *[HBM] * (2 * n), pl.BlockSpec(memory_space=pltpu.VMEM)),
        input_output_aliases={i: 2 + i for i in range(2 * n)},
        compiler_params=pltpu.CompilerParams(has_side_effects=pltpu.SideEffectType.DATAFLOW_SIDE_EFFECTING),
    )(*held)
    return out[0], out[1], list(out[2:2 + 2 * n]), out[-1]


def _exchange_wait(gather, started, after, name):
    send_sems, recv_sems, held, _ = started
    n = len(held) // 2

    def body(*refs):
        for cp in _direct_copies(gather, refs[:n], refs[n:2 * n], refs[2 * n], refs[2 * n + 1]):
            cp.wait_send()
            cp.wait_recv()

    out = pl.pallas_call(
        body, name=name, out_shape=tuple(pltpu.HBM(t.shape, t.dtype) for t in held),
        in_specs=[HBM] * (2 * n) + [SEM, SEM, pl.BlockSpec(memory_space=pl.ANY)], out_specs=tuple([HBM] * (2 * n)),
        input_output_aliases={i: i for i in range(2 * n)},
        compiler_params=pltpu.CompilerParams(has_side_effects=pltpu.SideEffectType.DATAFLOW_SIDE_EFFECTING),
    )(*held, send_sems, recv_sems, after)
    return list(out[n:])


def _own_plus_received(stack, got, name):
    n_blocks, rows, cols = stack.shape
    tr = _tile(rows, 512, 16)

    def body(*refs):
        acc = refs[0][0]
        for r in refs[1:-1]:
            acc = acc + r[0].astype(F32)
        refs[-1][...] = acc

    def mine(i):
        if n_blocks == 1:
            return (0, i, 0)
        x, y, c = _place()
        return (4 * x + 2 * y + c, i, 0)

    return pl.pallas_call(
        body, name=name, grid=(rows // tr,),
        in_specs=[pl.BlockSpec((1, tr, cols), mine)]
        + [pl.BlockSpec((1, tr, cols), functools.partial(lambda k, i: (k, i, 0), k)) for k in range(N_PEERS)],
        out_specs=pl.BlockSpec((tr, cols), lambda i: (i, 0)),
        out_shape=jax.ShapeDtypeStruct((rows, cols), F32), compiler_params=_cp("parallel"),
    )(stack, *[got] * N_PEERS)


class _Shard:
    def __init__(self, name, axis):
        self.name, self.axis = name, axis


BIG = [_Shard("w_in", 2), _Shard("pool_w_group", 2), _Shard("w_pool_up", 1), _Shard("w_ssd_out", 1),
       _Shard("w_o", 1), _Shard("w_ff1", 2), _Shard("w_ff2", 1)]
SMALL_SHARDED = [_Shard("meta_tokens", 1), _Shard("conv_w", 2)]
REPLICATED = ["mix_norm_w", "b_gate", "pool_scale", "conv_b", "dt_bias", "a_log", "d_skip", "ssd_norm_w",
              "mlp_norm_w", "final_norm_w"]
WEIGHTS = ["meta_tokens", "mix_norm_w", "w_in", "b_gate", "pool_w_group", "pool_scale", "w_pool_up", "conv_w", "conv_b",
           "dt_bias", "a_log", "d_skip", "ssd_norm_w", "w_ssd_out", "w_o", "mlp_norm_w", "w_ff1", "w_ff2", "final_norm_w"]


def _columns(blocks, a, b):
    pieces, o = [], 0
    for t in blocks:
        n = t.shape[-1]
        lo, hi = max(a, o), min(b, o + n)
        if lo < hi:
            pieces.append(t[..., lo - o:hi - o])
        o += n
    return pieces[0] if len(pieces) == 1 else jnp.concatenate(pieces, axis=-1)


def _rows2(a):
    return a.reshape(-1, a.shape[-1])


def _unshard(stack, shard_shape, axis):
    t = stack.reshape((N_DEV,) + tuple(shard_shape))
    return jnp.concatenate([t[d] for d in range(N_DEV)], axis=axis)


def _reshard(layers, axis):
    cut = [jnp.split(t, N_DEV, axis=axis - 1) for t in layers]
    blocks = [jnp.concatenate([_rows2(pieces[d]) for pieces in cut], axis=0) for d in range(N_DEV)]
    return jnp.stack(blocks).reshape((4, 2) + blocks[0].shape)


def _lane_rows(a):
    n = a.size
    tile = 8 * LANES
    if n % tile:
        return jnp.pad(a.reshape(-1), (0, (-n) % tile)).reshape(-1, LANES)
    return a.reshape(-1, LANES)


def _unpack_small(buf, spans, shapes):
    out = []
    for (o, r), shp in zip(spans, shapes):
        n = 1
        for d in shp:
            n *= d
        t = buf[o:o + r]
        out.append(t.reshape(shp) if n == r * LANES else t.reshape(-1)[:n].reshape(shp))
    return out


def _pack_small(parts, mult):
    mats = [_lane_rows(p) for p in parts]
    spans, o = [], 0
    for t in mats:
        spans.append((o, t.shape[0]))
        o += t.shape[0]
    fill = (-o) % mult
    if fill:
        mats.append(jnp.zeros((fill, LANES), mats[0].dtype))
    return jnp.concatenate(mats, axis=0), spans


def _layer_fwd(h, lw, cfg, tag, dep=None):
    pad, n_heads = cfg["pad"], cfg["n_heads"]
    u = _rms_fwd(h, lw["mix_norm_w"], f"rms_mix_{tag}", dep=dep)
    p_xbc = _mm(u, lw["w_xbc"], "nn", f"proj_xbc_{tag}")
    p_z = _mm(u, lw["w_z"], "nn", f"proj_z_{tag}")
    p_gate = _mm(u, lw["w_gate"], "nn", f"proj_gate_{tag}")
    p_pool = _mm(u, lw["w_pool"], "nn", f"proj_pool_{tag}")
    p_dt = _mm(u, lw["w_dt"], "nn", f"proj_dt_{tag}")
    pooled, y1 = _pool_fwd(p_pool, lw["pool_w_group"], lw["pool_scale"], pad, f"pool_fwd_{tag}")
    y_pool = _mm(y1, lw["w_pool_up"], "nn", f"pool_up_{tag}")
    xbc = _conv_fwd(p_xbc, lw["conv_w"], lw["conv_b"], f"conv_fwd_{tag}")
    y, yn, prev = _ssd_fwd(xbc, p_dt, p_z, lw["dt_bias"], lw["a_log"], lw["d_skip"], lw["ssd_norm_w"], pad, n_heads, f"ssd_fwd_{tag}")
    y_ssd = _mm(yn, lw["w_ssd_out"], "nn", f"ssd_out_{tag}")
    mix = _gate_fwd(p_gate, lw["b_gate"], y_pool, y_ssd, f"gate_fwd_{tag}")
    h_mid = _mm(mix, lw["w_o"], "nn", f"mix_out_{tag}", res=h)
    v = _rms_fwd(h_mid, lw["mlp_norm_w"], f"rms_mlp_{tag}")
    hid, act = _mm(v, lw["w_ff1"], "nn", f"ff1_{tag}", epi="relu2")
    h_out = _mm(act, lw["w_ff2"], "nn", f"ff2_{tag}", res=h_mid)
    saved = dict(h=h, u=u, p_xbc=p_xbc, p_z=p_z, p_gate=p_gate, p_dt=p_dt, pooled=pooled, y1=y1, y_pool=y_pool, xbc=xbc,
                 y=y, yn=yn, prev=prev, y_ssd=y_ssd, mix=mix, h_mid=h_mid, v=v, hid=hid, act=act)
    return h_out, saved


def _layer_bwd(dh, dh_b, lw, s, cfg, tag, traffic, i):
    pad, n_heads = cfg["pad"], cfg["n_heads"]
    g = {}
    dhid = _mm(dh_b, lw["w_ff2"], "nt", f"d_act_{tag}", epi="drelu2", aux=s["hid"], dep=traffic.bwd_begin(i))
    gb = {}
    g["w_ff2"], gb["w_ff2"] = _mm(s["act"], dh_b, "tn", f"dw_ff2_{tag}", epi="twin")
    dv = _mm(dhid, lw["w_ff1"], "nt", f"d_v_{tag}")
    g["w_ff1"], gb["w_ff1"] = _mm(s["v"], dhid, "tn", f"dw_ff1_{tag}", epi="twin")
    dep = traffic.grads_ready(i, {k: (g[k], gb[k]) for k in ("w_ff1", "w_ff2")}, "mlp")
    dh_mid, dh_mid_b, g["mlp_norm_w"] = _rms_bwd(s["h_mid"], lw["mlp_norm_w"], dv, dh, f"rms_mlp_bwd_{tag}")
    dmix = _mm(dh_mid_b, lw["w_o"], "nt", f"d_mix_{tag}", dep=dep)
    g["w_o"], gb["w_o"] = _mm(s["mix"], dh_mid_b, "tn", f"dw_o_{tag}", epi="twin")
    dgate, dy_pool, dy_ssd, g["b_gate"] = _gate_bwd(s["p_gate"], lw["b_gate"], s["y_pool"], s["y_ssd"], dmix, f"gate_bwd_{tag}")
    dy1 = _mm(dy_pool, lw["w_pool_up"], "nt", f"d_y1_{tag}")
    g["w_pool_up"], gb["w_pool_up"] = _mm(s["y1"], dy_pool, "tn", f"dw_pool_up_{tag}", epi="twin")
    dpool, g["pool_w_group"], g["pool_scale"] = _pool_bwd(s["pooled"], lw["pool_w_group"], lw["pool_scale"], dy1, pad, f"pool_bwd_{tag}")
    g["w_ssd_out"], gb["w_ssd_out"] = _mm(s["yn"], dy_ssd, "tn", f"dw_ssd_out_{tag}", epi="twin")
    gb["pool_w_group"] = g["pool_w_group"].astype(BF16)
    dep = traffic.grads_ready(i, {k: (g[k], gb[k]) for k in ("pool_w_group", "w_pool_up", "w_ssd_out", "w_o")}, "mix")
    dyn = _mm(dy_ssd, lw["w_ssd_out"], "nt", f"d_yn_{tag}", dep=dep)
    dact, dz, ddt, g["dt_bias"], g["a_log"], g["d_skip"], g["ssd_norm_w"] = _ssd_bwd(
        s["xbc"], s["p_dt"], s["p_z"], lw["dt_bias"], lw["a_log"], lw["d_skip"], lw["ssd_norm_w"], s["y"], s["prev"], dyn,
        pad, n_heads, f"ssd_bwd_{tag}")
    dxbc, g["conv_w"], g["conv_b"] = _conv_bwd(s["p_xbc"], lw["conv_w"], lw["conv_b"], dact, pad, f"conv_bwd_{tag}")
    u = s["u"]
    for k, d in (("w_xbc", dxbc), ("w_z", dz), ("w_gate", dgate), ("w_pool", dpool), ("w_dt", ddt)):
        g[k], gb[k] = _mm(u, d, "tn", f"d{k}_{tag}", epi="twin")
    c_dt = cfg["cols"][3]
    parts, parts_b = ([t["w_pool"], t["w_z"], t["w_xbc"], t["w_dt"][:, :c_dt], t["w_gate"]] for t in (g, gb))
    g["w_in"] = jnp.concatenate(parts, axis=1)
    dep = traffic.grads_ready(i, {"w_in": (parts, parts_b)}, "in")
    du = _mm(dxbc, lw["w_xbc"], "nt", f"du_xbc_{tag}", dep=dep)
    du = _mm(dz, lw["w_z"], "nt", f"du_z_{tag}", res=du)
    du = _mm(dgate, lw["w_gate"], "nt", f"du_gate_{tag}", res=du)
    du = _mm(dpool, lw["w_pool"], "nt", f"du_pool_{tag}", res=du)
    du = _mm(ddt, lw["w_dt"], "nt", f"du_dt_{tag}", res=du)
    dh_in, dh_in_b, g["mix_norm_w"] = _rms_bwd(s["h"], lw["mix_norm_w"], du, dh_mid, f"rms_mix_bwd_{tag}")
    traffic.bwd_end(i, dh_in)
    return dh_in, dh_in_b, g


def _pad_lanes(v):
    return jnp.pad(v, (0, LANES - v.shape[0])).reshape(1, LANES)


class _WholeWeights:
    def __init__(self, full):
        self.full = full

    def fwd_begin(self, i):
        full = {k: t[i] for k, t in self.full.items()}
        full["w_in"] = [full["w_in"]]
        return full, None

    def fwd_end(self, i, h_out):
        pass

    def bwd_begin(self, i):
        return None

    def grads_ready(self, i, grads, tag):
        return None

    def bwd_end(self, i, dh_in):
        pass


def _local_step(x2, target, meta_full, traffic, rep, cfg):
    depth, pad, n_meta, H = cfg["depth"], cfg["pad"], cfg["n_meta"], cfg["n_heads"]
    D = x2.shape[1]
    di = rep["ssd_norm_w"].shape[1]
    c_pool, c_z, c_xbc, c_dt = cfg["cols"]
    h = jnp.concatenate([jnp.zeros((pad, D), F32), meta_full, x2], axis=0)
    lws, saves = [], []
    for i in range(depth):
        full, dep = traffic.fwd_begin(i)
        blocks = full.pop("w_in")
        o_z, o_xbc, o_dt, o_gate = c_pool, c_pool + c_z, c_pool + c_z + c_xbc, c_pool + c_z + c_xbc + c_dt
        w_dt = jnp.pad(_columns(blocks, o_dt, o_gate), ((0, 0), (0, LANES - c_dt)))
        lw = dict(
            full, w_pool=_columns(blocks, 0, o_z), w_z=_columns(blocks, o_z, o_xbc), w_xbc=_columns(blocks, o_xbc, o_dt),
            w_dt=w_dt, w_gate=_columns(blocks, o_gate, o_gate + 2 * D),
            mix_norm_w=rep["mix_norm_w"][i], b_gate=rep["b_gate"][i], pool_scale=rep["pool_scale"][i].reshape(1, -1),
            conv_b=rep["conv_b"][i].reshape(1, -1), dt_bias=_pad_lanes(rep["dt_bias"][i]), a_log=_pad_lanes(rep["a_log"][i]),
            d_skip=_pad_lanes(rep["d_skip"][i]), ssd_norm_w=rep["ssd_norm_w"][i].reshape(1, di), mlp_norm_w=rep["mlp_norm_w"][i])
        lws.append(lw)
        h, s = _layer_fwd(h, lw, cfg, f"l{i}", dep)
        saves.append(s)
        traffic.fwd_end(i, h)
    loss, dh, dh_b, g_final = _final_loss(h, rep["final_norm_w"], target, pad + n_meta, "final_loss")
    per_layer = []
    for i in range(depth - 1, -1, -1):
        dh, dh_b, g = _layer_bwd(dh, dh_b, lws[i], saves[i], cfg, f"l{i}", traffic, i)
        per_layer.append(g)
    per_layer.reverse()

    def stack(key, fn=lambda t: t):
        return jnp.stack([fn(g[key]) for g in per_layer])

    def layers(key):
        return [g[key] for g in per_layer]

    grads = dict(
        w_in=layers("w_in"), pool_w_group=layers("pool_w_group"), w_pool_up=layers("w_pool_up"), w_ssd_out=layers("w_ssd_out"), w_o=layers("w_o"),
        w_ff1=layers("w_ff1"), w_ff2=layers("w_ff2"), conv_w=stack("conv_w"),
        mix_norm_w=stack("mix_norm_w", lambda t: t[0]), b_gate=stack("b_gate", lambda t: t[0]),
        pool_scale=stack("pool_scale", lambda t: t[0]), conv_b=stack("conv_b", lambda t: t[0]),
        dt_bias=stack("dt_bias", lambda t: t[0, :H]), a_log=stack("a_log", lambda t: t[0, :H]), d_skip=stack("d_skip", lambda t: t[0, :H]),
        ssd_norm_w=stack("ssd_norm_w", lambda t: t[0]), mlp_norm_w=stack("mlp_norm_w", lambda t: t[0]),
        final_norm_w=g_final[0], meta_tokens=dh[pad:pad + n_meta])
    return loss, dh[pad + n_meta:], grads


class _ShardedWeights:
    def __init__(self, w, me):
        self.w, self.me = w, me
        self.depth = w[BIG[0].name].shape[0]
        small = [_rows2(w[s.name]) for s in SMALL_SHARDED]
        landed = _all_gather([_rows2(w[s.name][0]).astype(BF16) for s in BIG] + small, "gather_l0")
        self.small = {s.name: _unshard(t, w[s.name].shape, s.axis) for s, t in zip(SMALL_SHARDED, landed[len(BIG):])}
        self.landed = dict(zip([s.name for s in BIG], landed))
        self.fetching = None
        self.sending, self.token = [], None
        self.local_grads = [{} for _ in range(self.depth)]

    def _fetch(self, shards, i, name):
        mine = [_rows2(self.w[s.name][i]).astype(BF16) for s in shards]
        lands = [lax.dynamic_update_slice(lax.empty((N_DEV,) + t.shape, t.dtype), t[None], (self.me, 0, 0)) for t in mine]
        return _exchange_start(True, mine, lands, name)

    def _whole(self, shards):
        return {s.name: _unshard(self.landed[s.name], self.w[s.name].shape[1:], s.axis - 1) for s in shards}

    def fwd_begin(self, i):
        rows, cols = self.w["w_in"].shape[1:]
        full = {"w_in": list(self.landed["w_in"].reshape(N_DEV, rows, cols)), "conv_w": self.small["conv_w"][i]}
        full.update(self._whole(BIG[1:]))
        dep = None
        if i + 1 < self.depth:
            self.fetching = self._fetch(BIG, i + 1, f"gather_start_l{i + 1}")
            dep = self.fetching[3]
        return full, dep

    def fwd_end(self, i, h_out):
        if self.fetching is not None:
            self.landed = dict(zip([s.name for s in BIG], _exchange_wait(True, self.fetching, h_out, f"gather_wait_l{i + 1}")))
            self.fetching = None

    def bwd_begin(self, i):
        return self.token

    def grads_ready(self, i, grads, tag):
        shards = [s for s in BIG if s.name in grads]

        def blocks(s, val):
            shape = (N_DEV,) + _rows2(self.w[s.name][i]).shape
            if isinstance(val, list):
                return jnp.stack([_columns(val, d * shape[2], (d + 1) * shape[2]) for d in range(N_DEV)])
            return _reshard([val], s.axis).reshape(shape)

        def own(s, val):
            if isinstance(val, list):
                cols = self.w[s.name].shape[2]
                return lax.dynamic_slice_in_dim(jnp.concatenate(val, axis=1), self.me * cols, cols, axis=1)[None]
            return blocks(s, val)

        stacks = [own(s, grads[s.name][0]) for s in shards]
        sends = [blocks(s, grads[s.name][1]) for s in shards]
        lands = [lax.empty((N_PEERS,) + t.shape[1:], BF16) for t in sends]
        started = _exchange_start(False, sends, lands, f"rs_start_{tag}_l{i}")
        self.sending.append((i, tag, shards, stacks, started))
        self.token = started[3]
        return self.token

    def _collect(self, entry, after):
        i, tag, shards, stacks, started = entry
        got = _exchange_wait(False, started, after, f"rs_wait_{tag}_l{i}")
        for s, st, g in zip(shards, stacks, got):
            self.local_grads[i][s.name] = _own_plus_received(st, g, f"rs_sum_{s.name}_l{i}")

    def bwd_end(self, i, dh_in):
        for entry in [e for e in self.sending if e[0] > i]:
            self._collect(entry, dh_in)
        self.sending = [e for e in self.sending if e[0] <= i]

    def finish(self, after, names):
        for entry in [e for e in self.sending if e[2][0].name in names]:
            self._collect(entry, after)
        self.sending = [e for e in self.sending if e[2][0].name not in names]
        return {k: jnp.concatenate([g[k] for g in self.local_grads], axis=0) for k in names}


def kernel(x, meta_tokens, mix_norm_w, w_in, b_gate, pool_w_group, pool_scale, w_pool_up, conv_w, conv_b, dt_bias, a_log, d_skip, ssd_norm_w, w_ssd_out, w_o, mlp_norm_w, w_ff1, w_ff2, final_norm_w, loss_target, m_meta_tokens, m_mix_norm_w, m_w_in, m_b_gate, m_pool_w_group, m_pool_scale, m_w_pool_up, m_conv_w, m_conv_b, m_dt_bias, m_a_log, m_d_skip, m_ssd_norm_w, m_w_ssd_out, m_w_o, m_mlp_norm_w, m_w_ff1, m_w_ff2, m_final_norm_w, v_meta_tokens, v_mix_norm_w, v_w_in, v_b_gate, v_pool_w_group, v_pool_scale, v_w_pool_up, v_conv_w, v_conv_b, v_dt_bias, v_a_log, v_d_skip, v_ssd_norm_w, v_w_ssd_out, v_w_o, v_mlp_norm_w, v_w_ff1, v_w_ff2, v_final_norm_w):
    w = dict(meta_tokens=meta_tokens, mix_norm_w=mix_norm_w, w_in=w_in, b_gate=b_gate, pool_w_group=pool_w_group,
             pool_scale=pool_scale, w_pool_up=w_pool_up, conv_w=conv_w, conv_b=conv_b, dt_bias=dt_bias, a_log=a_log,
             d_skip=d_skip, ssd_norm_w=ssd_norm_w, w_ssd_out=w_ssd_out, w_o=w_o, mlp_norm_w=mlp_norm_w, w_ff1=w_ff1,
             w_ff2=w_ff2, final_norm_w=final_norm_w)
    m = dict(meta_tokens=m_meta_tokens, mix_norm_w=m_mix_norm_w, w_in=m_w_in, b_gate=m_b_gate, pool_w_group=m_pool_w_group,
             pool_scale=m_pool_scale, w_pool_up=m_w_pool_up, conv_w=m_conv_w, conv_b=m_conv_b, dt_bias=m_dt_bias, a_log=m_a_log,
             d_skip=m_d_skip, ssd_norm_w=m_ssd_norm_w, w_ssd_out=m_w_ssd_out, w_o=m_w_o, mlp_norm_w=m_mlp_norm_w, w_ff1=m_w_ff1,
             w_ff2=m_w_ff2, final_norm_w=m_final_norm_w)
    v = dict(meta_tokens=v_meta_tokens, mix_norm_w=v_mix_norm_w, w_in=v_w_in, b_gate=v_b_gate, pool_w_group=v_pool_w_group,
             pool_scale=v_pool_scale, w_pool_up=v_w_pool_up, conv_w=v_conv_w, conv_b=v_conv_b, dt_bias=v_dt_bias, a_log=v_a_log,
             d_skip=v_d_skip, ssd_norm_w=v_ssd_norm_w, w_ssd_out=v_w_ssd_out, w_o=v_w_o, mlp_norm_w=v_mlp_norm_w, w_ff1=v_w_ff1,
             w_ff2=v_w_ff2, final_norm_w=v_final_norm_w)

    _, seq, D = x.shape
    n_meta = meta_tokens.shape[0]
    depth = w_in.shape[0]
    n_heads = dt_bias.shape[1]
    d_inner = ssd_norm_w.shape[1]
    d_xbc = conv_b.shape[1]
    pool_width = pool_scale.shape[1]
    pad = (-n_meta) % CHUNK
    cfg = dict(depth=depth, pad=pad, n_meta=n_meta, n_heads=n_heads, cols=(pool_width, d_inner, d_xbc, n_heads))
    assert (pad + n_meta + seq) % CHUNK == 0 and pad + n_meta == CHUNK

    xi, yi, ci = _place()
    me = 4 * xi + 2 * yi + ci

    traffic = _ShardedWeights(w, me)
    rep = {k: w[k] for k in REPLICATED}
    loss_part, dx, grads = _local_step(x[0], loss_target[0], traffic.small["meta_tokens"], traffic, rep, cfg)
    loss = lax.psum(loss_part[0, 0], ("x", "y", "c"))

    small_names = REPLICATED + [s.name for s in SMALL_SHARDED]
    sm_buf, sm_spans = _pack_small([grads[k] for k in small_names], 16)
    (sm_all,) = _all_gather([sm_buf], "gather_small_grads")
    sm_sum = _sum_stack([sm_all[d] for d in range(N_DEV)], "small_grads_sum")
    g_small = dict(zip(small_names, _unpack_small(sm_sum, sm_spans, [grads[k].shape for k in small_names])))
    g_loc = {k: g_small[k] for k in REPLICATED}
    for s in SMALL_SHARDED:
        blk = w[s.name].shape[s.axis]
        g_loc[s.name] = lax.dynamic_slice_in_dim(g_small[s.name], me * blk, blk, axis=s.axis)

    delta, new_m, new_v = {}, {}, {}
    loc_shapes = [w[k].shape for k in small_names]
    packed = [_pack_small([t[k] for k in small_names], 8) for t in (w, g_loc, m, v)]
    loc_spans = packed[0][1]
    outs = _adamw(*[p[0][None] for p in packed], "adamw_small")
    for res, buf in zip((delta, new_m, new_v), outs):
        res.update(zip(small_names, _unpack_small(buf[0], loc_spans, loc_shapes)))
    after = outs[0]
    for names in ([s.name for s in BIG if s.name != "w_in"], ["w_in"]):
        for k, t in traffic.finish(after, names).items():
            shp = w[k].shape
            by_layer = (shp[0], -1, shp[-1])
            d3, m3, v3 = _adamw(w[k].reshape(by_layer), t.reshape(by_layer), m[k].reshape(by_layer), v[k].reshape(by_layer), f"adamw_{k}")
            g_loc[k], delta[k], new_m[k], new_v[k] = t.reshape(shp), d3.reshape(shp), m3.reshape(shp), v3.reshape(shp)
            after = d3

    return (loss, dx[None], *[g_loc[k] for k in WEIGHTS], *[delta[k] for k in WEIGHTS],
            *[new_m[k] for k in WEIGHTS], *[new_v[k] for k in WEIGHTS])
```

```python
import functools

import jax
import jax.numpy as jnp
from jax import lax
from jax.experimental import pallas as pl
from jax.experimental.pallas import tpu as pltpu

F32 = jnp.float32
BF16 = jnp.bfloat16

EPS = 1e-5
D_STATE = 128
CHUNK = 128
LANES = 128
POOL_WINDOWS = (2, 4, 8, 16)
POOL_HALO = 16
CONV_WIDTH = 4
CONV_HALO = 8
N_DEV = 8
ADAM_LR = 0.001
ADAM_B1 = 0.9
ADAM_B2 = 0.999
ADAM_EPS = 1e-08
ADAM_WD = 0.01
ADAM_STEP = 10
VMEM_LIMIT = 52 * 1024 * 1024
MM_VMEM_BUDGET = 36 * 1024 * 1024
MESH = pl.DeviceIdType.MESH
ANY = pl.BlockSpec(memory_space=pl.ANY)


def _cp(*sem):
    return pltpu.CompilerParams(dimension_semantics=sem, vmem_limit_bytes=VMEM_LIMIT)


def _tile(n, target, mult):
    best = None
    for t in range(mult, min(n, target) + 1, mult):
        if n % t == 0:
            best = t
    return best if best is not None else n


def _sigmoid(x):
    return jax.nn.sigmoid(x)


def _iota(shape, dim):
    return lax.broadcasted_iota(jnp.int32, shape, dim)


_DIMS = {"nn": (((1,), (0,)), ((), ())), "nt": (((1,), (1,)), ((), ())), "tn": (((0,), (0,)), ((), ()))}


def _dot(a, b, mode="nn"):
    return lax.dot_general(a.astype(BF16), b.astype(BF16), _DIMS[mode], preferred_element_type=F32)


DEP = pl.BlockSpec((8, LANES), lambda *_: (0, 0))


def _mm(a, b, mode, name, *, out_dtype=F32, res=None, epi=None, aux=None, dep=None):
    if mode == "nn":
        (M, K), (_, N) = a.shape, b.shape
    elif mode == "nt":
        (M, K), (N, _) = a.shape, b.shape
    else:
        (K, M), (_, N) = a.shape, b.shape
    if mode == "tn":
        tm, tn, tk = _tile(M, 512, 128), _tile(N, 512, 128), K
    else:
        tm, tk = _tile(M, 1056, 16), _tile(K, 4096, 128)
        out_bytes = {None: jnp.dtype(out_dtype).itemsize, "relu2": 6, "drelu2": 2, "twin": 6}[epi]
        extra_bytes = 4 * sum(t is not None for t in (res, aux))
        for tn in (_tile(N, 1024, 128), _tile(N, 512, 128)):
            need = 2 * (tm * tk * a.dtype.itemsize + tk * tn * b.dtype.itemsize + tm * tn * (out_bytes + extra_bytes))
            if need <= MM_VMEM_BUDGET:
                break
    nk = K // tk
    a_spec = pl.BlockSpec((tk, tm), lambda i, j, k: (k, i)) if mode == "tn" else pl.BlockSpec((tm, tk), lambda i, j, k: (i, k))
    b_spec = pl.BlockSpec((tn, tk), lambda i, j, k: (j, k)) if mode == "nt" else pl.BlockSpec((tk, tn), lambda i, j, k: (k, j))
    o_spec = pl.BlockSpec((tm, tn), lambda i, j, k: (i, j))
    extra = [t for t in (res, aux) if t is not None]
    deps = [] if dep is None else [dep]

    def body(*refs):
        a_ref, b_ref = refs[0], refs[1]
        x_ref = refs[2] if extra else None
        outs = refs[2 + len(extra) + len(deps):]
        p = _dot(a_ref[...], b_ref[...], mode)

        def finish(r):
            if res is not None:
                outs[0][...] = (x_ref[...] + r).astype(out_dtype)
            elif epi == "relu2":
                outs[0][...] = r
                hid = jnp.maximum(r, 0.0)
                outs[1][...] = (hid * hid).astype(BF16)
            elif epi == "twin":
                outs[0][...] = r
                outs[1][...] = r.astype(BF16)
            elif epi == "drelu2":
                outs[0][...] = (r * (2.0 * jnp.maximum(x_ref[...], 0.0))).astype(BF16)
            else:
                outs[0][...] = r.astype(out_dtype)

        if nk == 1:
            finish(p)
        else:
            acc = outs[-1]
            k = pl.program_id(2)

            @pl.when(k == 0)
            def _():
                acc[...] = p

            @pl.when(k > 0)
            def _():
                acc[...] += p

            @pl.when(k == nk - 1)
            def _():
                finish(acc[...])

    if epi in ("relu2", "twin"):
        out_shape = (jax.ShapeDtypeStruct((M, N), F32), jax.ShapeDtypeStruct((M, N), BF16))
        out_specs = (o_spec, o_spec)
    elif epi == "drelu2":
        out_shape, out_specs = jax.ShapeDtypeStruct((M, N), BF16), o_spec
    else:
        out_shape, out_specs = jax.ShapeDtypeStruct((M, N), out_dtype), o_spec
    return pl.pallas_call(
        body, name=name, grid=(M // tm, N // tn, nk),
        in_specs=[a_spec, b_spec] + [o_spec] * len(extra) + [DEP] * len(deps),
        out_specs=out_specs, out_shape=out_shape,
        scratch_shapes=[pltpu.VMEM((tm, tn), F32)] if nk > 1 else [],
        compiler_params=_cp("parallel", "parallel", "arbitrary"),
    )(a, b, *extra, *deps)


def _rms_fwd(h, w, name, dep=None):
    T, D = h.shape
    tr = _tile(T, 1056, 16)
    deps = [] if dep is None else [dep]

    def body(h_ref, w_ref, *rest):
        x = h_ref[...]
        xn = x * lax.rsqrt(jnp.mean(x * x, axis=-1, keepdims=True) + EPS)
        rest[-1][...] = (xn * w_ref[...]).astype(BF16)

    return pl.pallas_call(
        body, name=name, grid=(T // tr,),
        in_specs=[pl.BlockSpec((tr, D), lambda i: (i, 0)), pl.BlockSpec((1, D), lambda i: (0, 0))] + [DEP] * len(deps),
        out_specs=pl.BlockSpec((tr, D), lambda i: (i, 0)), out_shape=jax.ShapeDtypeStruct((T, D), BF16),
        compiler_params=_cp("parallel"),
    )(h, w.reshape(1, D), *deps)


def _rms_bwd(h, w, dy, dres, name):
    T, D = h.shape
    tr = _tile(T, 528, 16)

    def body(h_ref, w_ref, dy_ref, dres_ref, dh_ref, dhb_ref, dw_ref):
        x = h_ref[...]
        rstd = lax.rsqrt(jnp.mean(x * x, axis=-1, keepdims=True) + EPS)
        xn = x * rstd
        dy = dy_ref[...]
        dxn = dy * w_ref[...]
        dh = dres_ref[...] + rstd * (dxn - xn * jnp.mean(dxn * xn, axis=-1, keepdims=True))
        dh_ref[...] = dh
        dhb_ref[...] = dh.astype(BF16)
        dw = jnp.sum(dy * xn, axis=0, keepdims=True)

        @pl.when(pl.program_id(0) == 0)
        def _():
            dw_ref[...] = dw

        @pl.when(pl.program_id(0) > 0)
        def _():
            dw_ref[...] += dw

    row = pl.BlockSpec((tr, D), lambda i: (i, 0))
    vec = pl.BlockSpec((1, D), lambda i: (0, 0))
    return pl.pallas_call(
        body, name=name, grid=(T // tr,),
        in_specs=[row, vec, row, row], out_specs=(row, row, vec),
        out_shape=(jax.ShapeDtypeStruct((T, D), F32), jax.ShapeDtypeStruct((T, D), BF16), jax.ShapeDtypeStruct((1, D), F32)),
        compiler_params=_cp("arbitrary"),
    )(h, w.reshape(1, D), dy, dres)


def _final_loss(h, w, target, first_row, name):
    T, D = h.shape
    tr = CHUNK
    assert first_row == tr

    def body(h_ref, w_ref, t_ref, loss_ref, dh_ref, dhb_ref, dw_ref):
        i = pl.program_id(0)
        x = h_ref[...]
        rstd = lax.rsqrt(jnp.mean(x * x, axis=-1, keepdims=True) + EPS)
        xn = x * rstd
        w = w_ref[...]
        live = i > 0
        err = jnp.where(live, xn * w - t_ref[...], 0.0)
        part = 0.5 * jnp.sum(jnp.mean(err * err, axis=-1, keepdims=True), axis=0, keepdims=True)
        dout = err * (1.0 / D)
        dxn = dout * w
        dh = rstd * (dxn - xn * jnp.mean(dxn * xn, axis=-1, keepdims=True))
        dh_ref[...] = dh
        dhb_ref[...] = dh.astype(BF16)
        dw = jnp.sum(dout * xn, axis=0, keepdims=True)

        @pl.when(i == 0)
        def _():
            loss_ref[...] = part
            dw_ref[...] = dw

        @pl.when(i > 0)
        def _():
            loss_ref[...] += part
            dw_ref[...] += dw

    row = pl.BlockSpec((tr, D), lambda i: (i, 0))
    vec = pl.BlockSpec((1, D), lambda i: (0, 0))
    return pl.pallas_call(
        body, name=name, grid=(T // tr,),
        in_specs=[row, vec, pl.BlockSpec((tr, D), lambda i: (jnp.maximum(i - 1, 0), 0))],
        out_specs=(pl.BlockSpec((1, 1), lambda i: (0, 0)), row, row, vec),
        out_shape=(jax.ShapeDtypeStruct((1, 1), F32), jax.ShapeDtypeStruct((T, D), F32), jax.ShapeDtypeStruct((T, D), BF16),
                   jax.ShapeDtypeStruct((1, D), F32)),
        compiler_params=_cp("arbitrary"),
    )(h, w.reshape(1, D), target)


def _gate_fwd(pg, bg, y_pool, y_ssd, name):
    T, D = y_pool.shape
    tr = _tile(T, 528, 16)

    def body(gp_ref, gs_ref, bp_ref, bs_ref, yp_ref, ys_ref, o_ref):
        gp = _sigmoid(gp_ref[...] + bp_ref[...])
        gs = _sigmoid(gs_ref[...] + bs_ref[...])
        o_ref[...] = (gp * yp_ref[...] + gs * ys_ref[...]).astype(BF16)

    row = pl.BlockSpec((tr, D), lambda i: (i, 0))
    row1 = pl.BlockSpec((tr, D), lambda i: (i, 1))
    vec = pl.BlockSpec((1, D), lambda i: (0, 0))
    vec1 = pl.BlockSpec((1, D), lambda i: (0, 1))
    b2 = bg.reshape(1, 2 * D)
    return pl.pallas_call(
        body, name=name, grid=(T // tr,),
        in_specs=[row, row1, vec, vec1, row, row], out_specs=row,
        out_shape=jax.ShapeDtypeStruct((T, D), BF16), compiler_params=_cp("parallel"),
    )(pg, pg, b2, b2, y_pool, y_ssd)


def _gate_bwd(pg, bg, y_pool, y_ssd, dmix, name):
    T, D = y_pool.shape
    tr = _tile(T, 528, 16)

    def body(gp_ref, gs_ref, bp_ref, bs_ref, yp_ref, ys_ref, dm_ref, dg_ref, dyp_ref, dys_ref, db_ref):
        gp = _sigmoid(gp_ref[...] + bp_ref[...])
        gs = _sigmoid(gs_ref[...] + bs_ref[...])
        dm = dm_ref[...]
        dyp_ref[...] = (dm * gp).astype(BF16)
        dys_ref[...] = (dm * gs).astype(BF16)
        dgp = dm * yp_ref[...] * gp * (1.0 - gp)
        dgs = dm * ys_ref[...] * gs * (1.0 - gs)
        dg_ref[:, :D] = dgp.astype(BF16)
        dg_ref[:, D:] = dgs.astype(BF16)
        db = jnp.concatenate([jnp.sum(dgp, axis=0, keepdims=True), jnp.sum(dgs, axis=0, keepdims=True)], axis=1)

        @pl.when(pl.program_id(0) == 0)
        def _():
            db_ref[...] = db

        @pl.when(pl.program_id(0) > 0)
        def _():
            db_ref[...] += db

    row = pl.BlockSpec((tr, D), lambda i: (i, 0))
    row1 = pl.BlockSpec((tr, D), lambda i: (i, 1))
    wide = pl.BlockSpec((tr, 2 * D), lambda i: (i, 0))
    vec = pl.BlockSpec((1, D), lambda i: (0, 0))
    vec1 = pl.BlockSpec((1, D), lambda i: (0, 1))
    vec2 = pl.BlockSpec((1, 2 * D), lambda i: (0, 0))
    b2 = bg.reshape(1, 2 * D)
    return pl.pallas_call(
        body, name=name, grid=(T // tr,),
        in_specs=[row, row1, vec, vec1, row, row, row], out_specs=(wide, row, row, vec2),
        out_shape=(jax.ShapeDtypeStruct((T, 2 * D), BF16), jax.ShapeDtypeStruct((T, D), BF16),
                   jax.ShapeDtypeStruct((T, D), BF16), jax.ShapeDtypeStruct((1, 2 * D), F32)),
        compiler_params=_cp("arbitrary"),
    )(pg, pg, b2, b2, y_pool, y_ssd, dmix)


def _pool_count(c, pad, window):
    pos = c * CHUNK + _iota((CHUNK, 1), 0) - pad
    return jnp.clip(pos + 1, 1, window).astype(F32)


def _by_group(g, vals):
    out = vals[-1]
    for k in range(len(vals) - 2, -1, -1):
        out = jnp.where(g == k, vals[k], out)
    return out


def _by_row_block(rows, vals, block):
    out = vals[0]
    for r in range(1, len(vals)):
        out = jnp.where(rows >= r * block, vals[r], out)
    return out


def _pool_fwd(u, wg, scale, pad, name):
    T, C = u.shape
    G, Cg, _ = wg.shape
    nc = T // CHUNK

    def body(u_ref, wg_ref, s_ref, p_ref, y_ref):
        g = pl.program_id(0)
        window = _by_group(g, POOL_WINDOWS)

        def chunk(c, carry):
            r0 = pl.multiple_of(c * CHUNK, CHUNK)
            h0 = pl.multiple_of(jnp.maximum(r0 - POOL_HALO, 0), 8)
            halo = jnp.where(c > 0, u_ref[pl.ds(h0, POOL_HALO), :], 0.0)
            xc = u_ref[pl.ds(r0, CHUNK), :]
            s = jnp.concatenate([halo, xc], axis=0)
            sums = []
            k = 1
            while k < POOL_WINDOWS[-1]:
                s = s + pltpu.roll(s, k, 0)
                k *= 2
                if k in POOL_WINDOWS:
                    sums.append(s[POOL_HALO:])
            wsum = _by_group(g, sums)
            pooled = wsum / _pool_count(c, pad, window) - xc
            pb = pooled.astype(BF16)
            p_ref[pl.ds(r0, CHUNK), :] = pb
            y_ref[pl.ds(r0, CHUNK), :] = (_dot(pb, wg_ref[0]) * s_ref[...]).astype(BF16)
            return carry

        lax.fori_loop(0, nc, chunk, 0)

    col = pl.BlockSpec((T, Cg), lambda g: (0, g))
    return pl.pallas_call(
        body, name=name, grid=(G,),
        in_specs=[col, pl.BlockSpec((1, Cg, Cg), lambda g: (g, 0, 0)), pl.BlockSpec((1, Cg), lambda g: (0, g))],
        out_specs=(col, col),
        out_shape=(jax.ShapeDtypeStruct((T, C), BF16), jax.ShapeDtypeStruct((T, C), BF16)),
        compiler_params=_cp("parallel"),
    )(u, wg, scale)


def _pool_bwd(pooled, wg, scale, dy, pad, name):
    T, C = dy.shape
    G, Cg, _ = wg.shape
    nc = T // CHUNK

    def body(p_ref, wg_ref, s_ref, dy_ref, du_ref, dwg_ref, ds_ref, halo_ref):
        g = pl.program_id(0)
        window = _by_group(g, POOL_WINDOWS)
        halo_ref[...] = jnp.zeros_like(halo_ref)
        dwg_ref[...] = jnp.zeros_like(dwg_ref)
        ds_ref[...] = jnp.zeros_like(ds_ref)

        def chunk(i, carry):
            c = nc - 1 - i
            r0 = pl.multiple_of(c * CHUNK, CHUNK)
            pb = p_ref[pl.ds(r0, CHUNK), :]
            dyc = dy_ref[pl.ds(r0, CHUNK), :]
            w = wg_ref[0]
            ypre = _dot(pb, w)
            ds_ref[...] += jnp.sum(dyc * ypre, axis=0, keepdims=True)
            dyp = (dyc * s_ref[...]).astype(BF16)
            dwg_ref[0] += _dot(pb, dyp, "tn")
            dpool = _dot(dyp, w, "nt")
            q = dpool / _pool_count(c, pad, window)
            s = jnp.concatenate([q, halo_ref[...]], axis=0)
            n = CHUNK + POOL_HALO
            sums = []
            k = 1
            while k < POOL_WINDOWS[-1]:
                s = s + pltpu.roll(s, n - k, 0)
                k *= 2
                if k in POOL_WINDOWS:
                    sums.append(s[:CHUNK])
            du = _by_group(g, sums) - dpool
            rows = r0 + _iota((CHUNK, 1), 0)
            du_ref[pl.ds(r0, CHUNK), :] = jnp.where(rows >= pad, du, 0.0).astype(BF16)
            halo_ref[...] = q[:POOL_HALO]
            return carry

        lax.fori_loop(0, nc, chunk, 0)

    col = pl.BlockSpec((T, Cg), lambda g: (0, g))
    return pl.pallas_call(
        body, name=name, grid=(G,),
        in_specs=[col, pl.BlockSpec((1, Cg, Cg), lambda g: (g, 0, 0)), pl.BlockSpec((1, Cg), lambda g: (0, g)), col],
        out_specs=(col, pl.BlockSpec((1, Cg, Cg), lambda g: (g, 0, 0)), pl.BlockSpec((1, Cg), lambda g: (0, g))),
        out_shape=(jax.ShapeDtypeStruct((T, C), BF16), jax.ShapeDtypeStruct((G, Cg, Cg), F32), jax.ShapeDtypeStruct((1, C), F32)),
        scratch_shapes=[pltpu.VMEM((POOL_HALO, Cg), F32)],
        compiler_params=_cp("parallel"),
    )(pooled, wg, scale, dy)


def _conv_pre(x_ref, w_ref, b_ref, c, r0):
    h0 = pl.multiple_of(jnp.maximum(r0 - CONV_HALO, 0), 8)
    halo = jnp.where(c > 0, x_ref[pl.ds(h0, CONV_HALO), :], 0.0)
    xe = jnp.concatenate([halo, x_ref[pl.ds(r0, CHUNK), :]], axis=0)
    y = jnp.broadcast_to(b_ref[...], (CHUNK, xe.shape[1]))
    for k in range(CONV_WIDTH):
        shift = CONV_WIDTH - 1 - k
        xs = xe if shift == 0 else pltpu.roll(xe, shift, 0)
        y = y + xs[CONV_HALO:] * w_ref[k:k + 1, :]
    return y, xe


def _conv_fwd(x, w, b, name):
    T = x.shape[0]
    C = w.shape[1]
    tc = _tile(C, 256, 128)
    nc = T // CHUNK

    def body(x_ref, w_ref, b_ref, o_ref):
        def chunk(c, carry):
            r0 = pl.multiple_of(c * CHUNK, CHUNK)
            y, _ = _conv_pre(x_ref, w_ref, b_ref, c, r0)
            o_ref[pl.ds(r0, CHUNK), :] = y * _sigmoid(y)
            return carry

        lax.fori_loop(0, nc, chunk, 0)

    col = pl.BlockSpec((T, tc), lambda j: (0, j))
    return pl.pallas_call(
        body, name=name, grid=(C // tc,),
        in_specs=[col, pl.BlockSpec((CONV_WIDTH, tc), lambda j: (0, j)), pl.BlockSpec((1, tc), lambda j: (0, j))],
        out_specs=col, out_shape=jax.ShapeDtypeStruct((T, C), F32), compiler_params=_cp("parallel"),
    )(x, w, b)


def _conv_bwd(x, w, b, dact, pad, name):
    T = x.shape[0]
    C = w.shape[1]
    tc = _tile(C, 256, 128)
    nc = T // CHUNK

    def body(x_ref, w_ref, b_ref, da_ref, dx_ref, dw_ref, db_ref, halo_ref):
        halo_ref[...] = jnp.zeros_like(halo_ref)
        dw_ref[...] = jnp.zeros_like(dw_ref)
        db_ref[...] = jnp.zeros_like(db_ref)

        def chunk(i, carry):
            c = nc - 1 - i
            r0 = pl.multiple_of(c * CHUNK, CHUNK)
            y, xe = _conv_pre(x_ref, w_ref, b_ref, c, r0)
            sg = _sigmoid(y)
            dpre = da_ref[pl.ds(r0, CHUNK), :] * (sg * (1.0 + y * (1.0 - sg)))
            db_ref[...] += jnp.sum(dpre, axis=0, keepdims=True)
            de = jnp.concatenate([dpre, halo_ref[...]], axis=0)
            n = CHUNK + CONV_HALO
            dx = jnp.zeros_like(dpre)
            for k in range(CONV_WIDTH):
                shift = CONV_WIDTH - 1 - k
                xs = xe if shift == 0 else pltpu.roll(xe, shift, 0)
                dw_ref[k:k + 1, :] += jnp.sum(dpre * xs[CONV_HALO:], axis=0, keepdims=True)
                ds = de if shift == 0 else pltpu.roll(de, n - shift, 0)
                dx = dx + ds[:CHUNK] * w_ref[k:k + 1, :]
            rows = r0 + _iota((CHUNK, 1), 0)
            dx_ref[pl.ds(r0, CHUNK), :] = jnp.where(rows >= pad, dx, 0.0).astype(BF16)
            halo_ref[...] = dpre[:CONV_HALO]
            return carry

        lax.fori_loop(0, nc, chunk, 0)

    col = pl.BlockSpec((T, tc), lambda j: (0, j))
    wspec = pl.BlockSpec((CONV_WIDTH, tc), lambda j: (0, j))
    bspec = pl.BlockSpec((1, tc), lambda j: (0, j))
    return pl.pallas_call(
        body, name=name, grid=(C // tc,),
        in_specs=[col, wspec, bspec, col], out_specs=(col, wspec, bspec),
        out_shape=(jax.ShapeDtypeStruct((T, C), BF16), jax.ShapeDtypeStruct((CONV_WIDTH, C), F32), jax.ShapeDtypeStruct((1, C), F32)),
        scratch_shapes=[pltpu.VMEM((CONV_HALO, tc), F32)],
        compiler_params=_cp("parallel"),
    )(x, w, b, dact)


def _cumsum_rows(x, reverse=False):
    n = x.shape[0]
    idx = _iota(x.shape, 0)
    k = 1
    while k < n:
        if reverse:
            x = x + jnp.where(idx < n - k, pltpu.roll(x, n - k, 0), 0.0)
        else:
            x = x + jnp.where(idx >= k, pltpu.roll(x, k, 0), 0.0)
        k *= 2
    return x


def _softplus(x):
    return jnp.maximum(x, 0.0) + jnp.log1p(jnp.exp(-jnp.abs(x)))


def _head_selector(n_heads, width):
    lane = jnp.arange(n_heads * width)[None, :] // width
    return (lane == jnp.arange(LANES)[:, None]).astype(BF16)


def _dot_exact(x, sel, parts, mode="nn"):
    acc = None
    for _ in range(parts):
        piece = x.astype(BF16)
        x = x - piece.astype(F32)
        t = lax.dot_general(piece, sel, _DIMS[mode], preferred_element_type=F32)
        acc = t if acc is None else acc + t
    return acc


def _ssd_decays(dt, cs, sel_p_ref, sel_q_ref):
    cs_b = _dot_exact(cs, sel_q_ref[...], 3)
    dt_x = _dot_exact(dt, sel_p_ref[...], 3)
    cs_x = _dot_exact(cs, sel_p_ref[...], 3)
    return cs_b, dt_x, jnp.exp(cs_x), jnp.exp(cs_x[CHUNK - 1:CHUNK, :] - cs_x)


def _ssd_common(c, pad, n_heads, dtr_ref, dtb_ref, al_ref):
    rows = c * CHUNK + _iota((CHUNK, 1), 0)
    valid = rows >= pad
    live = jnp.logical_and(valid, _iota((1, LANES), 1) < n_heads)
    pre = dtr_ref[...] + dtb_ref[...]
    dt = jnp.where(live, _softplus(pre), 0.0)
    a = -jnp.exp(al_ref[...])
    cs = _cumsum_rows(dt * a)
    return valid, live, dt, a, cs, cs.T, _sigmoid(pre)


def _ssd_specs(T, DI, GN, cfirst):
    xcol = DI // GN

    def at(col):
        return lambda c: (cfirst(c), col)

    x = pl.BlockSpec((CHUNK, DI), at(0))
    b = pl.BlockSpec((CHUNK, GN), at(xcol))
    cm = pl.BlockSpec((CHUNK, GN), at(xcol + 1))
    dt = pl.BlockSpec((CHUNK, LANES), at(0))
    vec = pl.BlockSpec((1, LANES), lambda c: (0, 0))
    nw = pl.BlockSpec((1, DI), lambda c: (0, 0))
    return x, b, cm, dt, vec, nw


def _ssd_fwd(xbc, pdt, pz, dt_bias, a_log, d_skip, norm_w, pad, n_heads, name):
    T = xbc.shape[0]
    DI = pz.shape[1]
    P = DI // n_heads
    GN = (xbc.shape[1] - DI) // 2
    G = GN // D_STATE
    R = n_heads // G
    GW = R * P
    nc = T // CHUNK
    Q, N = CHUNK, D_STATE

    def body(x_ref, b_ref, c_ref, dtr_ref, z_ref, dtb_ref, al_ref, dsk_ref, nw_ref, sel_p_ref, sel_q_ref,
             y_ref, yn_ref, prev_ref, s_ref):
        c = pl.program_id(0)

        @pl.when(c == 0)
        def _():
            s_ref[...] = jnp.zeros_like(s_ref)

        valid, _, dt, _, cs, cst, _ = _ssd_common(c, pad, n_heads, dtr_ref, dtb_ref, al_ref)
        cs_b, dt_x, e_x, dec_x = _ssd_decays(dt, cs, sel_p_ref, sel_q_ref)
        e_last = jnp.exp(cs[Q - 1:Q, :])
        tri = _iota((Q, Q), 0) >= _iota((Q, Q), 1)
        state_rows = _iota((GW, 1), 0)
        head_lane = _iota((1, GW), 1)
        for g in range(G):
            gs = slice(g * GW, (g + 1) * GW)
            bg = jnp.where(valid, b_ref[:, g * N:(g + 1) * N], 0.0).astype(BF16)
            cg = jnp.where(valid, c_ref[:, g * N:(g + 1) * N], 0.0).astype(BF16)
            xg = jnp.where(valid, x_ref[:, gs], 0.0)
            sg = s_ref[gs, :]
            prev_ref[0, gs, :] = sg
            cb = _dot(cg, bg, "nt")
            xdt = xg * dt_x[:, gs]
            yg = _dot(cg, sg, "nt") * e_x[:, gs]
            for r in range(R):
                h = g * R + r
                lmat = jnp.exp(jnp.where(tri, cs_b[:, h * Q:(h + 1) * Q] - cst[h:h + 1, :], -jnp.inf))
                in_head = jnp.logical_and(head_lane >= r * P, head_lane < (r + 1) * P)
                yg = yg + _dot(cb * lmat, jnp.where(in_head, xdt, 0.0))
            y_ref[:, gs] = yg
            decay = _by_row_block(state_rows, [e_last[:, g * R + r:g * R + r + 1] for r in range(R)], P)
            s_ref[gs, :] = sg * decay + _dot(xdt * dec_x[:, gs], bg, "tn")
            z = z_ref[:, gs]
            gz = (yg + xg * dsk_ref[:, gs]) * (z * _sigmoid(z))
            rstd = lax.rsqrt(jnp.mean(gz * gz, axis=-1, keepdims=True) + EPS)
            yn_ref[:, gs] = ((gz * rstd) * nw_ref[:, gs]).astype(BF16)

    x_s, b_s, c_s, dt_s, vec, nw = _ssd_specs(T, DI, GN, lambda c: c)
    wide = pl.BlockSpec((Q, DI), lambda c: (c, 0))
    sel_p = pl.BlockSpec((LANES, DI), lambda c: (0, 0))
    sel_q = pl.BlockSpec((LANES, n_heads * Q), lambda c: (0, 0))
    return pl.pallas_call(
        body, name=name, grid=(nc,),
        in_specs=[x_s, b_s, c_s, dt_s, wide, vec, vec, nw, nw, sel_p, sel_q],
        out_specs=(wide, wide, pl.BlockSpec((1, DI, N), lambda c: (c, 0, 0))),
        out_shape=(jax.ShapeDtypeStruct((T, DI), F32), jax.ShapeDtypeStruct((T, DI), BF16), jax.ShapeDtypeStruct((nc, DI, N), F32)),
        scratch_shapes=[pltpu.VMEM((DI, N), F32)],
        compiler_params=_cp("arbitrary"),
    )(xbc, xbc, xbc, pdt, pz, dt_bias, a_log, jnp.repeat(d_skip[:, :n_heads], P, axis=1), norm_w,
      _head_selector(n_heads, P), _head_selector(n_heads, Q))


def _ssd_bwd(xbc, pdt, pz, dt_bias, a_log, d_skip, norm_w, y, prev, dyn, pad, n_heads, name):
    T, W = xbc.shape
    DI = pz.shape[1]
    P = DI // n_heads
    GN = (W - DI) // 2
    G = GN // D_STATE
    R = n_heads // G
    GW = R * P
    nc = T // CHUNK
    Q, N = CHUNK, D_STATE

    def body(x_ref, b_ref, c_ref, dtr_ref, z_ref, dtb_ref, al_ref, dsk_ref, nw_ref, y_ref, prev_ref, next_ref, dyn_ref,
             sel_p_ref, sel_q_ref, sel_pt_ref,
             dxbc_ref, dz_ref, ddt_ref, ddtb_ref, dal_ref, ddsk_ref, dnw_ref, ds_ref):
        i = pl.program_id(0)
        c = nc - 1 - i

        @pl.when(i == 0)
        def _():
            ds_ref[...] = jnp.zeros_like(ds_ref)
            ddtb_ref[...] = jnp.zeros_like(ddtb_ref)
            dal_ref[...] = jnp.zeros_like(dal_ref)
            ddsk_ref[...] = jnp.zeros_like(ddsk_ref)
            dnw_ref[...] = jnp.zeros_like(dnw_ref)

        valid, live, dt, a, cs, cst, sig_pre = _ssd_common(c, pad, n_heads, dtr_ref, dtb_ref, al_ref)
        cs_b, dt_x, e_x, dec_x = _ssd_decays(dt, cs, sel_p_ref, sel_q_ref)
        e_last = jnp.exp(cs[Q - 1:Q, :])
        tri = _iota((Q, Q), 0) >= _iota((Q, Q), 1)
        tri_t = _iota((Q, Q), 0) <= _iota((Q, Q), 1)
        state_rows = _iota((GW, 1), 0)
        head_lane = _iota((1, GW), 1)
        lane = _iota((1, LANES), 1)
        head_rows = _iota((LANES, 1), 0)
        s_dy_cs = jnp.zeros((Q, LANES), F32)
        s_x_bds = jnp.zeros((Q, LANES), F32)
        s_x_dxdt = jnp.zeros((Q, LANES), F32)
        dcs_rows = jnp.zeros((Q, LANES), F32)
        dcs_cols = jnp.zeros((LANES, Q), F32)
        c_end = jnp.zeros((1, LANES), F32)
        dsk_rows = []
        for g in range(G):
            gs = slice(g * GW, (g + 1) * GW)
            bg = jnp.where(valid, b_ref[:, g * N:(g + 1) * N], 0.0).astype(BF16)
            cg = jnp.where(valid, c_ref[:, g * N:(g + 1) * N], 0.0).astype(BF16)
            xg = jnp.where(valid, x_ref[:, gs], 0.0)
            s_prev = prev_ref[0, gs, :]
            dsg = ds_ref[gs, :]
            end = dsg * next_ref[0, gs, :]
            yg = y_ref[:, gs]
            dsk = dsk_ref[:, gs]
            ytot = yg + xg * dsk
            z = z_ref[:, gs]
            sz = _sigmoid(z)
            silu = z * sz
            gz = ytot * silu
            rstd = lax.rsqrt(jnp.mean(gz * gz, axis=-1, keepdims=True) + EPS)
            gn = gz * rstd
            dyn_g = dyn_ref[:, gs]
            dnw_ref[:, gs] += jnp.sum(dyn_g * gn, axis=0, keepdims=True)
            dgn = dyn_g * nw_ref[:, gs]
            dgz = rstd * (dgn - gn * jnp.mean(dgn * gn, axis=-1, keepdims=True))
            dz_ref[:, gs] = (dgz * ytot * (sz * (1.0 + z * (1.0 - sz)))).astype(BF16)
            dy = dgz * silu
            dsk_rows.append(jnp.sum(dy * xg, axis=0, keepdims=True))
            cb = _dot(cg, bg, "nt")
            cb_t = _dot(bg, cg, "nt")
            bds = _dot(bg, dsg, "nt") * dec_x[:, gs]
            csg = _dot(cg, s_prev, "nt")
            xdt = xg * dt_x[:, gs]
            dxdt = bds
            dcb = jnp.zeros((Q, Q), F32)
            dcb_t = jnp.zeros((Q, Q), F32)
            for r in range(R):
                h = g * R + r
                cs_col = cs_b[:, h * Q:(h + 1) * Q]
                cs_row = cst[h:h + 1, :]
                lmat = jnp.exp(jnp.where(tri, cs_col - cs_row, -jnp.inf))
                lmat_t = jnp.exp(jnp.where(tri_t, cs_row - cs_col, -jnp.inf))
                in_head = jnp.logical_and(head_lane >= r * P, head_lane < (r + 1) * P)
                dyr = jnp.where(in_head, dy, 0.0)
                dm = _dot(dyr, xdt, "nt")
                dcb = dcb + dm * lmat
                dcb_t = dcb_t + _dot(xdt, dyr, "nt") * lmat_t
                w_rc = dm * (cb * lmat)
                dcs_rows = jnp.where(lane == h, jnp.sum(w_rc, axis=1, keepdims=True), dcs_rows)
                dcs_cols = jnp.where(head_rows == h, jnp.sum(w_rc, axis=0, keepdims=True), dcs_cols)
                dxdt = dxdt + _dot(cb_t * lmat_t, dyr)
            sel_t = sel_pt_ref[gs, :]
            s_dy_cs = s_dy_cs + _dot_exact(dy * csg, sel_t, 2)
            s_x_bds = s_x_bds + _dot_exact(xg * bds, sel_t, 2)
            s_x_dxdt = s_x_dxdt + _dot_exact(xg * dxdt, sel_t, 2)
            c_end = c_end + jnp.sum(jnp.sum(end, axis=1, keepdims=True) * sel_t.astype(F32), axis=0, keepdims=True)
            dye = dy * e_x[:, gs]
            dc = _dot(dcb, bg) + _dot(dye, s_prev)
            db = _dot(dcb_t, cg) + _dot(xdt * dec_x[:, gs], dsg)
            decay = _by_row_block(state_rows, [e_last[:, g * R + r:g * R + r + 1] for r in range(R)], P)
            ds_ref[gs, :] = dsg * decay + _dot(dye, cg, "tn")
            dxbc_ref[:, gs] = jnp.where(valid, dxdt * dt_x[:, gs] + dy * dsk, 0.0)
            dxbc_ref[:, DI + g * N:DI + (g + 1) * N] = jnp.where(valid, db, 0.0)
            dxbc_ref[:, DI + GN + g * N:DI + GN + (g + 1) * N] = jnp.where(valid, dc, 0.0)
        dcs = s_dy_cs * jnp.exp(cs) - dt * s_x_bds
        da_cs = _cumsum_rows(dcs + (dcs_rows - dcs_cols.T), reverse=True) + c_end
        ddt_all = jnp.where(live, da_cs * a + s_x_dxdt, 0.0)
        ddt_raw = ddt_all * sig_pre
        ddt_ref[...] = ddt_raw.astype(BF16)
        ddtb_ref[...] += jnp.sum(ddt_raw, axis=0, keepdims=True)
        dal_ref[...] += jnp.sum(da_cs * dt, axis=0, keepdims=True) * a
        dsk_all = jnp.broadcast_to(jnp.concatenate(dsk_rows, axis=1), (8, DI))
        ddsk_ref[...] += _dot_exact(dsk_all, sel_pt_ref[...], 3)[0:1]

    rev = lambda i: nc - 1 - i
    x_s, b_s, c_s, dt_s, vec, nw = _ssd_specs(T, DI, GN, rev)
    wide = pl.BlockSpec((Q, DI), lambda i: (rev(i), 0))
    st = pl.BlockSpec((1, DI, N), lambda i: (rev(i), 0, 0))
    st_next = pl.BlockSpec((1, DI, N), lambda i: (jnp.minimum(rev(i) + 1, nc - 1), 0, 0))
    sel_p = pl.BlockSpec((LANES, DI), lambda i: (0, 0))
    sel_q = pl.BlockSpec((LANES, n_heads * Q), lambda i: (0, 0))
    sel_pt = pl.BlockSpec((DI, LANES), lambda i: (0, 0))
    sel = _head_selector(n_heads, P)
    return pl.pallas_call(
        body, name=name, grid=(nc,),
        in_specs=[x_s, b_s, c_s, dt_s, wide, vec, vec, nw, nw, wide, st, st_next, wide, sel_p, sel_q, sel_pt],
        out_specs=(pl.BlockSpec((Q, W), lambda i: (rev(i), 0)), wide, dt_s, vec, vec, vec, nw),
        out_shape=(jax.ShapeDtypeStruct((T, W), F32), jax.ShapeDtypeStruct((T, DI), BF16), jax.ShapeDtypeStruct((T, LANES), BF16),
                   jax.ShapeDtypeStruct((1, LANES), F32), jax.ShapeDtypeStruct((1, LANES), F32),
                   jax.ShapeDtypeStruct((1, LANES), F32), jax.ShapeDtypeStruct((1, DI), F32)),
        scratch_shapes=[pltpu.VMEM((DI, N), F32)],
        compiler_params=_cp("arbitrary"),
    )(xbc, xbc, xbc, pdt, pz, dt_bias, a_log, jnp.repeat(d_skip[:, :n_heads], P, axis=1), norm_w, y, prev, prev, dyn,
      sel, _head_selector(n_heads, Q), sel.T)


def _adamw(w, g, m, v, name):
    layers, rows, cols = w.shape
    tr = _tile(rows, 256, 8)

    def body(w_ref, g_ref, m_ref, v_ref, d_ref, nm_ref, nv_ref):
        g = g_ref[...]
        m = ADAM_B1 * m_ref[...] + (1.0 - ADAM_B1) * g
        v = ADAM_B2 * v_ref[...] + (1.0 - ADAM_B2) * (g * g)
        m_hat = m / (1.0 - ADAM_B1 ** ADAM_STEP)
        v_hat = v / (1.0 - ADAM_B2 ** ADAM_STEP)
        d_ref[...] = -ADAM_LR * (m_hat / (jnp.sqrt(v_hat) + ADAM_EPS) + ADAM_WD * w_ref[...])
        nm_ref[...] = m
        nv_ref[...] = v

    blk = pl.BlockSpec((1, tr, cols), lambda l, i: (l, i, 0))
    out = jax.ShapeDtypeStruct((layers, rows, cols), F32)
    return pl.pallas_call(
        body, name=name, grid=(layers, rows // tr), in_specs=[blk] * 4, out_specs=(blk,) * 3, out_shape=(out,) * 3,
        compiler_params=_cp("parallel", "parallel"),
    )(w, g, m, v)


def _sum_stack(parts, name, out_dtype=F32):
    rows, cols = parts[0].shape
    tr = _tile(rows, 512, 16)

    def body(*refs):
        acc = refs[0][...].astype(F32)
        for r in refs[1:-1]:
            acc = acc + r[...].astype(F32)
        refs[-1][...] = acc.astype(out_dtype)

    blk = pl.BlockSpec((tr, cols), lambda i: (i, 0))
    return pl.pallas_call(
        body, name=name, grid=(rows // tr,), in_specs=[blk] * len(parts), out_specs=blk,
        out_shape=jax.ShapeDtypeStruct((rows, cols), out_dtype), compiler_params=_cp("parallel"),
    )(*parts)


def _place():
    return lax.axis_index("x"), lax.axis_index("y"), lax.axis_index("c")


def _other_chips(x, y):
    return [(1 - x, y), (x, 1 - y), (1 - x, 1 - y)]


def _all_gather(shards, name):
    nb = len(shards)

    def body(*refs):
        ins, outs = refs[:nb], refs[nb:2 * nb]
        send_sems, recv_sems, local_sems = refs[2 * nb:]
        x, y, c = _place()
        me, sibling = (x, y, c), (x, y, 1 - c)
        chips = _other_chips(x, y)

        def copy(q, k, block, to, src=None):
            dst = outs[q].at[4 * block[0] + 2 * block[1] + block[2]]
            return pltpu.make_async_remote_copy(
                src_ref=dst if src is None else src, dst_ref=dst,
                send_sem=send_sems.at[7 * q + k], recv_sem=recv_sems.at[7 * q + k], device_id=to, device_id_type=MESH)

        started = []
        for q in range(nb):
            mine = pltpu.make_async_copy(ins[q], outs[q].at[4 * x + 2 * y + c], local_sems.at[q])
            mine.start()
            started.append(mine)
        first = []
        for q in range(nb):
            first.append(copy(q, 0, me, sibling, src=ins[q]))
            first += [copy(q, 1 + j, me, (*chip, c), src=ins[q]) for j, chip in enumerate(chips)]
        for cp in first:
            cp.start()
        passed = []
        for j, chip in enumerate(chips):
            for q in range(nb):
                copy(q, 1 + j, (*chip, c), me).wait_recv()
                fwd = copy(q, 4 + j, (*chip, c), sibling)
                fwd.start()
                passed.append(fwd)
        for q in range(nb):
            copy(q, 0, sibling, me).wait_recv()
            for j, chip in enumerate(chips):
                copy(q, 4 + j, (*chip, 1 - c), me).wait_recv()
        for cp in first + passed:
            cp.wait_send()
        for mine in started:
            mine.wait()

    return pl.pallas_call(
        body, name=name, in_specs=[ANY] * nb, out_specs=tuple([ANY] * nb),
        out_shape=tuple(jax.ShapeDtypeStruct((N_DEV,) + s.shape, s.dtype) for s in shards),
        scratch_shapes=[pltpu.SemaphoreType.DMA((7 * nb,)), pltpu.SemaphoreType.DMA((7 * nb,)), pltpu.SemaphoreType.DMA((nb,))],
    )(*shards)


HBM = pl.BlockSpec(memory_space=pltpu.HBM)
SEM = pl.BlockSpec(memory_space=pltpu.SEMAPHORE)
N_PEERS = N_DEV - 1


def _peer(k, x, y, c):
    return (1 - x if k & 4 else x, 1 - y if k & 2 else y, 1 - c if k & 1 else c)


def _direct_copies(gather, srcs, lands, send_sems, recv_sems):
    x, y, c = _place()
    copies = []
    for q in range(len(srcs)):
        for k in range(1, N_DEV):
            px, py, pc = _peer(k, x, y, c)
            if gather:
                src, dst = srcs[q], lands[q].at[4 * x + 2 * y + c]
            else:
                src, dst = srcs[q].at[4 * px + 2 * py + pc], lands[q].at[k - 1]
            copies.append(pltpu.make_async_remote_copy(
                src_ref=src, dst_ref=dst, send_sem=send_sems.at[N_PEERS * q + k - 1], recv_sem=recv_sems.at[N_PEERS * q + k - 1],
                device_id=(px, py, pc), device_id_type=MESH))
    return copies


def _exchange_start(gather, srcs, lands, name):
    n = len(srcs)

    def body(*refs):
        send_sems, recv_sems = refs[2 * n], refs[2 * n + 1]
        for cp in _direct_copies(gather, refs[:n], refs[n:2 * n], send_sems, recv_sems):
            cp.start()
        refs[-1][...] = jnp.zeros_like(refs[-1])

    held = [pltpu.with_memory_space_constraint(t, pltpu.HBM) for t in list(srcs) + list(lands)]
    out = pl.pallas_call(
        body, name=name,
        out_shape=(pltpu.SemaphoreType.DMA((N_PEERS * n,)), pltpu.SemaphoreType.DMA((N_PEERS * n,)),
                   *[pltpu.HBM(t.shape, t.dtype) for t in held], jax.ShapeDtypeStruct((8, LANES), F32)),
        in_specs=[HBM] * (2 * n), out_specs=(SEM, SEM, *[HBM] * (2 * n), pl.BlockSpec(memory_space=pltpu.VMEM)),
        input_output_aliases={i: 2 + i for i in range(2 * n)},
        compiler_params=pltpu.CompilerParams(has_side_effects=pltpu.SideEffectType.DATAFLOW_SIDE_EFFECTING),
    )(*held)
    return out[0], out[1], list(out[2:2 + 2 * n]), out[-1]


def _exchange_wait(gather, started, after, name):
    send_sems, recv_sems, held, _ = started
    n = len(held) // 2

    def body(*refs):
        for cp in _direct_copies(gather, refs[:n], refs[n:2 * n], refs[2 * n], refs[2 * n + 1]):
            cp.wait_send()
            cp.wait_recv()

    out = pl.pallas_call(
        body, name=name, out_shape=tuple(pltpu.HBM(t.shape, t.dtype) for t in held),
        in_specs=[HBM] * (2 * n) + [SEM, SEM, pl.BlockSpec(memory_space=pl.ANY)], out_specs=tuple([HBM] * (2 * n)),
        input_output_aliases={i: i for i in range(2 * n)},
        compiler_params=pltpu.CompilerParams(has_side_effects=pltpu.SideEffectType.DATAFLOW_SIDE_EFFECTING),
    )(*held, send_sems, recv_sems, after)
    return list(out[n:])


def _own_plus_received(stack, got, name):
    n_blocks, rows, cols = stack.shape
    tr = _tile(rows, 512, 16)

    def body(*refs):
        acc = refs[0][0]
        for r in refs[1:-1]:
            acc = acc + r[0].astype(F32)
        refs[-1][...] = acc

    def mine(i):
        if n_blocks == 1:
            return (0, i, 0)
        x, y, c = _place()
        return (4 * x + 2 * y + c, i, 0)

    return pl.pallas_call(
        body, name=name, grid=(rows // tr,),
        in_specs=[pl.BlockSpec((1, tr, cols), mine)]
        + [pl.BlockSpec((1, tr, cols), functools.partial(lambda k, i: (k, i, 0), k)) for k in range(N_PEERS)],
        out_specs=pl.BlockSpec((tr, cols), lambda i: (i, 0)),
        out_shape=jax.ShapeDtypeStruct((rows, cols), F32), compiler_params=_cp("parallel"),
    )(stack, *[got] * N_PEERS)


class _Shard:
    def __init__(self, name, axis):
        self.name, self.axis = name, axis


BIG = [_Shard("w_in", 2), _Shard("pool_w_group", 2), _Shard("w_pool_up", 1), _Shard("w_ssd_out", 1),
       _Shard("w_o", 1), _Shard("w_ff1", 2), _Shard("w_ff2", 1)]
SMALL_SHARDED = [_Shard("meta_tokens", 1), _Shard("conv_w", 2)]
REPLICATED = ["mix_norm_w", "b_gate", "pool_scale", "conv_b", "dt_bias", "a_log", "d_skip", "ssd_norm_w",
              "mlp_norm_w", "final_norm_w"]
WEIGHTS = ["meta_tokens", "mix_norm_w", "w_in", "b_gate", "pool_w_group", "pool_scale", "w_pool_up", "conv_w", "conv_b",
           "dt_bias", "a_log", "d_skip", "ssd_norm_w", "w_ssd_out", "w_o", "mlp_norm_w", "w_ff1", "w_ff2", "final_norm_w"]


def _columns(blocks, a, b):
    pieces, o = [], 0
    for t in blocks:
        n = t.shape[-1]
        lo, hi = max(a, o), min(b, o + n)
        if lo < hi:
            pieces.append(t[..., lo - o:hi - o])
        o += n
    return pieces[0] if len(pieces) == 1 else jnp.concatenate(pieces, axis=-1)


def _rows2(a):
    return a.reshape(-1, a.shape[-1])


def _unshard(stack, shard_shape, axis):
    t = stack.reshape((N_DEV,) + tuple(shard_shape))
    return jnp.concatenate([t[d] for d in range(N_DEV)], axis=axis)


def _reshard(layers, axis):
    cut = [jnp.split(t, N_DEV, axis=axis - 1) for t in layers]
    blocks = [jnp.concatenate([_rows2(pieces[d]) for pieces in cut], axis=0) for d in range(N_DEV)]
    return jnp.stack(blocks).reshape((4, 2) + blocks[0].shape)


def _lane_rows(a):
    n = a.size
    tile = 8 * LANES
    if n % tile:
        return jnp.pad(a.reshape(-1), (0, (-n) % tile)).reshape(-1, LANES)
    return a.reshape(-1, LANES)


def _unpack_small(buf, spans, shapes):
    out = []
    for (o, r), shp in zip(spans, shapes):
        n = 1
        for d in shp:
            n *= d
        t = buf[o:o + r]
        out.append(t.reshape(shp) if n == r * LANES else t.reshape(-1)[:n].reshape(shp))
    return out


def _pack_small(parts, mult):
    mats = [_lane_rows(p) for p in parts]
    spans, o = [], 0
    for t in mats:
        spans.append((o, t.shape[0]))
        o += t.shape[0]
    fill = (-o) % mult
    if fill:
        mats.append(jnp.zeros((fill, LANES), mats[0].dtype))
    return jnp.concatenate(mats, axis=0), spans


def _layer_fwd(h, lw, cfg, tag, dep=None):
    pad, n_heads = cfg["pad"], cfg["n_heads"]
    u = _rms_fwd(h, lw["mix_norm_w"], f"rms_mix_{tag}", dep=dep)
    p_xbc = _mm(u, lw["w_xbc"], "nn", f"proj_xbc_{tag}")
    p_z = _mm(u, lw["w_z"], "nn", f"proj_z_{tag}")
    p_gate = _mm(u, lw["w_gate"], "nn", f"proj_gate_{tag}")
    p_pool = _mm(u, lw["w_pool"], "nn", f"proj_pool_{tag}")
    p_dt = _mm(u, lw["w_dt"], "nn", f"proj_dt_{tag}")
    pooled, y1 = _pool_fwd(p_pool, lw["pool_w_group"], lw["pool_scale"], pad, f"pool_fwd_{tag}")
    y_pool = _mm(y1, lw["w_pool_up"], "nn", f"pool_up_{tag}")
    xbc = _conv_fwd(p_xbc, lw["conv_w"], lw["conv_b"], f"conv_fwd_{tag}")
    y, yn, prev = _ssd_fwd(xbc, p_dt, p_z, lw["dt_bias"], lw["a_log"], lw["d_skip"], lw["ssd_norm_w"], pad, n_heads, f"ssd_fwd_{tag}")
    y_ssd = _mm(yn, lw["w_ssd_out"], "nn", f"ssd_out_{tag}")
    mix = _gate_fwd(p_gate, lw["b_gate"], y_pool, y_ssd, f"gate_fwd_{tag}")
    h_mid = _mm(mix, lw["w_o"], "nn", f"mix_out_{tag}", res=h)
    v = _rms_fwd(h_mid, lw["mlp_norm_w"], f"rms_mlp_{tag}")
    hid, act = _mm(v, lw["w_ff1"], "nn", f"ff1_{tag}", epi="relu2")
    h_out = _mm(act, lw["w_ff2"], "nn", f"ff2_{tag}", res=h_mid)
    saved = dict(h=h, u=u, p_xbc=p_xbc, p_z=p_z, p_gate=p_gate, p_dt=p_dt, pooled=pooled, y1=y1, y_pool=y_pool, xbc=xbc,
                 y=y, yn=yn, prev=prev, y_ssd=y_ssd, mix=mix, h_mid=h_mid, v=v, hid=hid, act=act)
    return h_out, saved


def _layer_bwd(dh, dh_b, lw, s, cfg, tag, traffic, i):
    pad, n_heads = cfg["pad"], cfg["n_heads"]
    g = {}
    dhid = _mm(dh_b, lw["w_ff2"], "nt", f"d_act_{tag}", epi="drelu2", aux=s["hid"], dep=traffic.bwd_begin(i))
    gb = {}
    g["w_ff2"], gb["w_ff2"] = _mm(s["act"], dh_b, "tn", f"dw_ff2_{tag}", epi="twin")
    dv = _mm(dhid, lw["w_ff1"], "nt", f"d_v_{tag}")
    g["w_ff1"], gb["w_ff1"] = _mm(s["v"], dhid, "tn", f"dw_ff1_{tag}", epi="twin")
    dep = traffic.grads_ready(i, {k: (g[k], gb[k]) for k in ("w_ff1", "w_ff2")}, "mlp")
    dh_mid, dh_mid_b, g["mlp_norm_w"] = _rms_bwd(s["h_mid"], lw["mlp_norm_w"], dv, dh, f"rms_mlp_bwd_{tag}")
    dmix = _mm(dh_mid_b, lw["w_o"], "nt", f"d_mix_{tag}", dep=dep)
    g["w_o"], gb["w_o"] = _mm(s["mix"], dh_mid_b, "tn", f"dw_o_{tag}", epi="twin")
    dgate, dy_pool, dy_ssd, g["b_gate"] = _gate_bwd(s["p_gate"], lw["b_gate"], s["y_pool"], s["y_ssd"], dmix, f"gate_bwd_{tag}")
    dy1 = _mm(dy_pool, lw["w_pool_up"], "nt", f"d_y1_{tag}")
    g["w_pool_up"], gb["w_pool_up"] = _mm(s["y1"], dy_pool, "tn", f"dw_pool_up_{tag}", epi="twin")
    dpool, g["pool_w_group"], g["pool_scale"] = _pool_bwd(s["pooled"], lw["pool_w_group"], lw["pool_scale"], dy1, pad, f"pool_bwd_{tag}")
    g["w_ssd_out"], gb["w_ssd_out"] = _mm(s["yn"], dy_ssd, "tn", f"dw_ssd_out_{tag}", epi="twin")
    gb["pool_w_group"] = g["pool_w_group"].astype(BF16)
    dep = traffic.grads_ready(i, {k: (g[k], gb[k]) for k in ("pool_w_group", "w_pool_up", "w_ssd_out", "w_o")}, "mix")
    dyn = _mm(dy_ssd, lw["w_ssd_out"], "nt", f"d_yn_{tag}", dep=dep)
    dact, dz, ddt, g["dt_bias"], g["a_log"], g["d_skip"], g["ssd_norm_w"] = _ssd_bwd(
        s["xbc"], s["p_dt"], s["p_z"], lw["dt_bias"], lw["a_log"], lw["d_skip"], lw["ssd_norm_w"], s["y"], s["prev"], dyn,
        pad, n_heads, f"ssd_bwd_{tag}")
    dxbc, g["conv_w"], g["conv_b"] = _conv_bwd(s["p_xbc"], lw["conv_w"], lw["conv_b"], dact, pad, f"conv_bwd_{tag}")
    u = s["u"]
    for k, d in (("w_xbc", dxbc), ("w_z", dz), ("w_gate", dgate), ("w_pool", dpool), ("w_dt", ddt)):
        g[k], gb[k] = _mm(u, d, "tn", f"d{k}_{tag}", epi="twin")
    c_dt = cfg["cols"][3]
    parts, parts_b = ([t["w_pool"], t["w_z"], t["w_xbc"], t["w_dt"][:, :c_dt], t["w_gate"]] for t in (g, gb))
    g["w_in"] = jnp.concatenate(parts, axis=1)
    dep = traffic.grads_ready(i, {"w_in": (parts, parts_b)}, "in")
    du = _mm(dxbc, lw["w_xbc"], "nt", f"du_xbc_{tag}", dep=dep)
    du = _mm(dz, lw["w_z"], "nt", f"du_z_{tag}", res=du)
    du = _mm(dgate, lw["w_gate"], "nt", f"du_gate_{tag}", res=du)
    du = _mm(dpool, lw["w_pool"], "nt", f"du_pool_{tag}", res=du)
    du = _mm(ddt, lw["w_dt"], "nt", f"du_dt_{tag}", res=du)
    dh_in, dh_in_b, g["mix_norm_w"] = _rms_bwd(s["h"], lw["mix_norm_w"], du, dh_mid, f"rms_mix_bwd_{tag}")
    traffic.bwd_end(i, dh_in)
    return dh_in, dh_in_b, g


def _pad_lanes(v):
    return jnp.pad(v, (0, LANES - v.shape[0])).reshape(1, LANES)


class _WholeWeights:
    def __init__(self, full):
        self.full = full

    def fwd_begin(self, i):
        full = {k: t[i] for k, t in self.full.items()}
        full["w_in"] = [full["w_in"]]
        return full, None

    def fwd_end(self, i, h_out):
        pass

    def bwd_begin(self, i):
        return None

    def grads_ready(self, i, grads, tag):
        return None

    def bwd_end(self, i, dh_in):
        pass


def _local_step(x2, target, meta_full, traffic, rep, cfg):
    depth, pad, n_meta, H = cfg["depth"], cfg["pad"], cfg["n_meta"], cfg["n_heads"]
    D = x2.shape[1]
    di = rep["ssd_norm_w"].shape[1]
    c_pool, c_z, c_xbc, c_dt = cfg["cols"]
    h = jnp.concatenate([jnp.zeros((pad, D), F32), meta_full, x2], axis=0)
    lws, saves = [], []
    for i in range(depth):
        full, dep = traffic.fwd_begin(i)
        blocks = full.pop("w_in")
        o_z, o_xbc, o_dt, o_gate = c_pool, c_pool + c_z, c_pool + c_z + c_xbc, c_pool + c_z + c_xbc + c_dt
        w_dt = jnp.pad(_columns(blocks, o_dt, o_gate), ((0, 0), (0, LANES - c_dt)))
        lw = dict(
            full, w_pool=_columns(blocks, 0, o_z), w_z=_columns(blocks, o_z, o_xbc), w_xbc=_columns(blocks, o_xbc, o_dt),
            w_dt=w_dt, w_gate=_columns(blocks, o_gate, o_gate + 2 * D),
            mix_norm_w=rep["mix_norm_w"][i], b_gate=rep["b_gate"][i], pool_scale=rep["pool_scale"][i].reshape(1, -1),
            conv_b=rep["conv_b"][i].reshape(1, -1), dt_bias=_pad_lanes(rep["dt_bias"][i]), a_log=_pad_lanes(rep["a_log"][i]),
            d_skip=_pad_lanes(rep["d_skip"][i]), ssd_norm_w=rep["ssd_norm_w"][i].reshape(1, di), mlp_norm_w=rep["mlp_norm_w"][i])
        lws.append(lw)
        h, s = _layer_fwd(h, lw, cfg, f"l{i}", dep)
        saves.append(s)
        traffic.fwd_end(i, h)
    loss, dh, dh_b, g_final = _final_loss(h, rep["final_norm_w"], target, pad + n_meta, "final_loss")
    per_layer = []
    for i in range(depth - 1, -1, -1):
        dh, dh_b, g = _layer_bwd(dh, dh_b, lws[i], saves[i], cfg, f"l{i}", traffic, i)
        per_layer.append(g)
    per_layer.reverse()

    def stack(key, fn=lambda t: t):
        return jnp.stack([fn(g[key]) for g in per_layer])

    def layers(key):
        return [g[key] for g in per_layer]

    grads = dict(
        w_in=layers("w_in"), pool_w_group=layers("pool_w_group"), w_pool_up=layers("w_pool_up"), w_ssd_out=layers("w_ssd_out"), w_o=layers("w_o"),
        w_ff1=layers("w_ff1"), w_ff2=layers("w_ff2"), conv_w=stack("conv_w"),
        mix_norm_w=stack("mix_norm_w", lambda t: t[0]), b_gate=stack("b_gate", lambda t: t[0]),
        pool_scale=stack("pool_scale", lambda t: t[0]), conv_b=stack("conv_b", lambda t: t[0]),
        dt_bias=stack("dt_bias", lambda t: t[0, :H]), a_log=stack("a_log", lambda t: t[0, :H]), d_skip=stack("d_skip", lambda t: t[0, :H]),
        ssd_norm_w=stack("ssd_norm_w", lambda t: t[0]), mlp_norm_w=stack("mlp_norm_w", lambda t: t[0]),
        final_norm_w=g_final[0], meta_tokens=dh[pad:pad + n_meta])
    return loss, dh[pad + n_meta:], grads


class _ShardedWeights:
    def __init__(self, w, me):
        self.w, self.me = w, me
        self.depth = w[BIG[0].name].shape[0]
        small = [_rows2(w[s.name]) for s in SMALL_SHARDED]
        landed = _all_gather([_rows2(w[s.name][0]).astype(BF16) for s in BIG] + small, "gather_l0")
        self.small = {s.name: _unshard(t, w[s.name].shape, s.axis) for s, t in zip(SMALL_SHARDED, landed[len(BIG):])}
        self.landed = dict(zip([s.name for s in BIG], landed))
        self.fetching = None
        self.sending, self.token = [], None
        self.local_grads = [{} for _ in range(self.depth)]

    def _fetch(self, shards, i, name):
        mine = [_rows2(self.w[s.name][i]).astype(BF16) for s in shards]
        lands = [lax.dynamic_update_slice(lax.empty((N_DEV,) + t.shape, t.dtype), t[None], (self.me, 0, 0)) for t in mine]
        return _exchange_start(True, mine, lands, name)

    def _whole(self, shards):
        return {s.name: _unshard(self.landed[s.name], self.w[s.name].shape[1:], s.axis - 1) for s in shards}

    def fwd_begin(self, i):
        rows, cols = self.w["w_in"].shape[1:]
        full = {"w_in": list(self.landed["w_in"].reshape(N_DEV, rows, cols)), "conv_w": self.small["conv_w"][i]}
        full.update(self._whole(BIG[1:]))
        dep = None
        if i + 1 < self.depth:
            self.fetching = self._fetch(BIG, i + 1, f"gather_start_l{i + 1}")
            dep = self.fetching[3]
        return full, dep

    def fwd_end(self, i, h_out):
        if self.fetching is not None:
            self.landed = dict(zip([s.name for s in BIG], _exchange_wait(True, self.fetching, h_out, f"gather_wait_l{i + 1}")))
            self.fetching = None

    def bwd_begin(self, i):
        return self.token

    def grads_ready(self, i, grads, tag):
        shards = [s for s in BIG if s.name in grads]

        def blocks(s, val):
            shape = (N_DEV,) + _rows2(self.w[s.name][i]).shape
            if isinstance(val, list):
                return jnp.stack([_columns(val, d * shape[2], (d + 1) * shape[2]) for d in range(N_DEV)])
            return _reshard([val], s.axis).reshape(shape)

        def own(s, val):
            if isinstance(val, list):
                cols = self.w[s.name].shape[2]
                return lax.dynamic_slice_in_dim(jnp.concatenate(val, axis=1), self.me * cols, cols, axis=1)[None]
            return blocks(s, val)

        stacks = [own(s, grads[s.name][0]) for s in shards]
        sends = [blocks(s, grads[s.name][1]) for s in shards]
        lands = [lax.empty((N_PEERS,) + t.shape[1:], BF16) for t in sends]
        started = _exchange_start(False, sends, lands, f"rs_start_{tag}_l{i}")
        self.sending.append((i, tag, shards, stacks, started))
        self.token = started[3]
        return self.token

    def _collect(self, entry, after):
        i, tag, shards, stacks, started = entry
        got = _exchange_wait(False, started, after, f"rs_wait_{tag}_l{i}")
        for s, st, g in zip(shards, stacks, got):
            self.local_grads[i][s.name] = _own_plus_received(st, g, f"rs_sum_{s.name}_l{i}")

    def bwd_end(self, i, dh_in):
        for entry in [e for e in self.sending if e[0] > i]:
            self._collect(entry, dh_in)
        self.sending = [e for e in self.sending if e[0] <= i]

    def finish(self, after, names):
        for entry in [e for e in self.sending if e[2][0].name in names]:
            self._collect(entry, after)
        self.sending = [e for e in self.sending if e[2][0].name not in names]
        return {k: jnp.concatenate([g[k] for g in self.local_grads], axis=0) for k in names}


def kernel(x, meta_tokens, mix_norm_w, w_in, b_gate, pool_w_group, pool_scale, w_pool_up, conv_w, conv_b, dt_bias, a_log, d_skip, ssd_norm_w, w_ssd_out, w_o, mlp_norm_w, w_ff1, w_ff2, final_norm_w, loss_target, m_meta_tokens, m_mix_norm_w, m_w_in, m_b_gate, m_pool_w_group, m_pool_scale, m_w_pool_up, m_conv_w, m_conv_b, m_dt_bias, m_a_log, m_d_skip, m_ssd_norm_w, m_w_ssd_out, m_w_o, m_mlp_norm_w, m_w_ff1, m_w_ff2, m_final_norm_w, v_meta_tokens, v_mix_norm_w, v_w_in, v_b_gate, v_pool_w_group, v_pool_scale, v_w_pool_up, v_conv_w, v_conv_b, v_dt_bias, v_a_log, v_d_skip, v_ssd_norm_w, v_w_ssd_out, v_w_o, v_mlp_norm_w, v_w_ff1, v_w_ff2, v_final_norm_w):
    w = dict(meta_tokens=meta_tokens, mix_norm_w=mix_norm_w, w_in=w_in, b_gate=b_gate, pool_w_group=pool_w_group,
             pool_scale=pool_scale, w_pool_up=w_pool_up, conv_w=conv_w, conv_b=conv_b, dt_bias=dt_bias, a_log=a_log,
             d_skip=d_skip, ssd_norm_w=ssd_norm_w, w_ssd_out=w_ssd_out, w_o=w_o, mlp_norm_w=mlp_norm_w, w_ff1=w_ff1,
             w_ff2=w_ff2, final_norm_w=final_norm_w)
    m = dict(meta_tokens=m_meta_tokens, mix_norm_w=m_mix_norm_w, w_in=m_w_in, b_gate=m_b_gate, pool_w_group=m_pool_w_group,
             pool_scale=m_pool_scale, w_pool_up=m_w_pool_up, conv_w=m_conv_w, conv_b=m_conv_b, dt_bias=m_dt_bias, a_log=m_a_log,
             d_skip=m_d_skip, ssd_norm_w=m_ssd_norm_w, w_ssd_out=m_w_ssd_out, w_o=m_w_o, mlp_norm_w=m_mlp_norm_w, w_ff1=m_w_ff1,
             w_ff2=m_w_ff2, final_norm_w=m_final_norm_w)
    v = dict(meta_tokens=v_meta_tokens, mix_norm_w=v_mix_norm_w, w_in=v_w_in, b_gate=v_b_gate, pool_w_group=v_pool_w_group,
             pool_scale=v_pool_scale, w_pool_up=v_w_pool_up, conv_w=v_conv_w, conv_b=v_conv_b, dt_bias=v_dt_bias, a_log=v_a_log,
             d_skip=v_d_skip, ssd_norm_w=v_ssd_norm_w, w_ssd_out=v_w_ssd_out, w_o=v_w_o, mlp_norm_w=v_mlp_norm_w, w_ff1=v_w_ff1,
             w_ff2=v_w_ff2, final_norm_w=v_final_norm_w)

    _, seq, D = x.shape
    n_meta = meta_tokens.shape[0]
    depth = w_in.shape[0]
    n_heads = dt_bias.shape[1]
    d_inner = ssd_norm_w.shape[1]
    d_xbc = conv_b.shape[1]
    pool_width = pool_scale.shape[1]
    pad = (-n_meta) % CHUNK
    cfg = dict(depth=depth, pad=pad, n_meta=n_meta, n_heads=n_heads, cols=(pool_width, d_inner, d_xbc, n_heads))
    assert (pad + n_meta + seq) % CHUNK == 0 and pad + n_meta == CHUNK

    xi, yi, ci = _place()
    me = 4 * xi + 2 * yi + ci

    traffic = _ShardedWeights(w, me)
    rep = {k: w[k] for k in REPLICATED}
    loss_part, dx, grads = _local_step(x[0], loss_target[0], traffic.small["meta_tokens"], traffic, rep, cfg)
    loss = lax.psum(loss_part[0, 0], ("x", "y", "c"))

    small_names = REPLICATED + [s.name for s in SMALL_SHARDED]
    sm_buf, sm_spans = _pack_small([grads[k] for k in small_names], 16)
    (sm_all,) = _all_gather([sm_buf], "gather_small_grads")
    sm_sum = _sum_stack([sm_all[d] for d in range(N_DEV)], "small_grads_sum")
    g_small = dict(zip(small_names, _unpack_small(sm_sum, sm_spans, [grads[k].shape for k in small_names])))
    g_loc = {k: g_small[k] for k in REPLICATED}
    for s in SMALL_SHARDED:
        blk = w[s.name].shape[s.axis]
        g_loc[s.name] = lax.dynamic_slice_in_dim(g_small[s.name], me * blk, blk, axis=s.axis)

    delta, new_m, new_v = {}, {}, {}
    loc_shapes = [w[k].shape for k in small_names]
    packed = [_pack_small([t[k] for k in small_names], 8) for t in (w, g_loc, m, v)]
    loc_spans = packed[0][1]
    outs = _adamw(*[p[0][None] for p in packed], "adamw_small")
    for res, buf in zip((delta, new_m, new_v), outs):
        res.update(zip(small_names, _unpack_small(buf[0], loc_spans, loc_shapes)))
    after = outs[0]
    for names in ([s.name for s in BIG if s.name != "w_in"], ["w_in"]):
        for k, t in traffic.finish(after, names).items():
            shp = w[k].shape
            by_layer = (shp[0], -1, shp[-1])
            d3, m3, v3 = _adamw(w[k].reshape(by_layer), t.reshape(by_layer), m[k].reshape(by_layer), v[k].reshape(by_layer), f"adamw_{k}")
            g_loc[k], delta[k], new_m[k], new_v[k] = t.reshape(shp), d3.reshape(shp), m3.reshape(shp), v3.reshape(shp)
            after = d3

    return (loss, dx[None], *[g_loc[k] for k in WEIGHTS], *[delta[k] for k in WEIGHTS],
            *[new_m[k] for k in WEIGHTS], *[new_v[k] for k in WEIGHTS])
```

```python
import functools

import jax
import jax.numpy as jnp
from jax import lax
from jax.experimental import pallas as pl
from jax.experimental.pallas import tpu as pltpu

F32 = jnp.float32
BF16 = jnp.bfloat16

EPS = 1e-5
D_STATE = 128
CHUNK = 128
LANES = 128
POOL_WINDOWS = (2, 4, 8, 16)
POOL_HALO = 16
CONV_WIDTH = 4
CONV_HALO = 8
N_DEV = 8
ADAM_LR = 0.001
ADAM_B1 = 0.9
ADAM_B2 = 0.999
ADAM_EPS = 1e-08
ADAM_WD = 0.01
ADAM_STEP = 10
VMEM_LIMIT = 52 * 1024 * 1024
MM_VMEM_BUDGET = 40 * 1024 * 1024
MESH = pl.DeviceIdType.MESH
ANY = pl.BlockSpec(memory_space=pl.ANY)


def _cp(*sem):
    return pltpu.CompilerParams(dimension_semantics=sem, vmem_limit_bytes=VMEM_LIMIT)


def _tile(n, target, mult):
    best = None
    for t in range(mult, min(n, target) + 1, mult):
        if n % t == 0:
            best = t
    return best if best is not None else n


def _sigmoid(x):
    return jax.nn.sigmoid(x)


def _iota(shape, dim):
    return lax.broadcasted_iota(jnp.int32, shape, dim)


_DIMS = {"nn": (((1,), (0,)), ((), ())), "nt": (((1,), (1,)), ((), ())), "tn": (((0,), (0,)), ((), ()))}


def _dot(a, b, mode="nn"):
    return lax.dot_general(a.astype(BF16), b.astype(BF16), _DIMS[mode], preferred_element_type=F32)


DEP = pl.BlockSpec((8, LANES), lambda *_: (0, 0))


def _mm(a, b, mode, name, *, out_dtype=F32, res=None, epi=None, aux=None, dep=None):
    if mode == "nn":
        (M, K), (_, N) = a.shape, b.shape
    elif mode == "nt":
        (M, K), (N, _) = a.shape, b.shape
    else:
        (K, M), (_, N) = a.shape, b.shape
    if mode == "tn":
        tm, tn, tk = _tile(M, 512, 128), _tile(N, 512, 128), K
    else:
        tk = _tile(K, 4096, 128)
        out_bytes = {None: jnp.dtype(out_dtype).itemsize, "relu2": 6, "drelu2": 2, "twin": 6}[epi]
        extra_bytes = 4 * sum(t is not None for t in (res, aux))
        for tm_target, tn_target in ((2112, 1024), (1056, 1024), (1056, 512)):
            tm, tn = _tile(M, tm_target, 16), _tile(N, tn_target, 128)
            need = 2 * (tm * tk * a.dtype.itemsize + tk * tn * b.dtype.itemsize + tm * tn * (out_bytes + extra_bytes)) + 4 * tm * tn
            if need <= MM_VMEM_BUDGET:
                break
    nk = K // tk
    a_spec = pl.BlockSpec((tk, tm), lambda i, j, k: (k, i)) if mode == "tn" else pl.BlockSpec((tm, tk), lambda i, j, k: (i, k))
    b_spec = pl.BlockSpec((tn, tk), lambda i, j, k: (j, k)) if mode == "nt" else pl.BlockSpec((tk, tn), lambda i, j, k: (k, j))
    o_spec = pl.BlockSpec((tm, tn), lambda i, j, k: (i, j))
    extra = [t for t in (res, aux) if t is not None]
    deps = [] if dep is None else [dep]

    def body(*refs):
        a_ref, b_ref = refs[0], refs[1]
        x_ref = refs[2] if extra else None
        outs = refs[2 + len(extra) + len(deps):]
        p = _dot(a_ref[...], b_ref[...], mode)

        def finish(r):
            if res is not None:
                outs[0][...] = (x_ref[...] + r).astype(out_dtype)
            elif epi == "relu2":
                outs[0][...] = r
                hid = jnp.maximum(r, 0.0)
                outs[1][...] = (hid * hid).astype(BF16)
            elif epi == "twin":
                outs[0][...] = r
                outs[1][...] = r.astype(BF16)
            elif epi == "drelu2":
                outs[0][...] = (r * (2.0 * jnp.maximum(x_ref[...], 0.0))).astype(BF16)
            else:
                outs[0][...] = r.astype(out_dtype)

        if nk == 1:
            finish(p)
        else:
            acc = outs[-1]
            k = pl.program_id(2)

            @pl.when(k == 0)
            def _():
                acc[...] = p

            @pl.when(k > 0)
            def _():
                acc[...] += p

            @pl.when(k == nk - 1)
            def _():
                finish(acc[...])

    if epi in ("relu2", "twin"):
        out_shape = (jax.ShapeDtypeStruct((M, N), F32), jax.ShapeDtypeStruct((M, N), BF16))
        out_specs = (o_spec, o_spec)
    elif epi == "drelu2":
        out_shape, out_specs = jax.ShapeDtypeStruct((M, N), BF16), o_spec
    else:
        out_shape, out_specs = jax.ShapeDtypeStruct((M, N), out_dtype), o_spec
    return pl.pallas_call(
        body, name=name, grid=(M // tm, N // tn, nk),
        in_specs=[a_spec, b_spec] + [o_spec] * len(extra) + [DEP] * len(deps),
        out_specs=out_specs, out_shape=out_shape,
        scratch_shapes=[pltpu.VMEM((tm, tn), F32)] if nk > 1 else [],
        compiler_params=_cp("parallel", "parallel", "arbitrary"),
    )(a, b, *extra, *deps)


def _rms_fwd(h, w, name, dep=None):
    T, D = h.shape
    tr = _tile(T, 1056, 16)
    deps = [] if dep is None else [dep]

    def body(h_ref, w_ref, *rest):
        x = h_ref[...]
        xn = x * lax.rsqrt(jnp.mean(x * x, axis=-1, keepdims=True) + EPS)
        rest[-1][...] = (xn * w_ref[...]).astype(BF16)

    return pl.pallas_call(
        body, name=name, grid=(T // tr,),
        in_specs=[pl.BlockSpec((tr, D), lambda i: (i, 0)), pl.BlockSpec((1, D), lambda i: (0, 0))] + [DEP] * len(deps),
        out_specs=pl.BlockSpec((tr, D), lambda i: (i, 0)), out_shape=jax.ShapeDtypeStruct((T, D), BF16),
        compiler_params=_cp("parallel"),
    )(h, w.reshape(1, D), *deps)


def _rms_bwd(h, w, dy, dres, name):
    T, D = h.shape
    tr = _tile(T, 528, 16)

    def body(h_ref, w_ref, dy_ref, dres_ref, dh_ref, dhb_ref, dw_ref):
        x = h_ref[...]
        rstd = lax.rsqrt(jnp.mean(x * x, axis=-1, keepdims=True) + EPS)
        xn = x * rstd
        dy = dy_ref[...]
        dxn = dy * w_ref[...]
        dh = dres_ref[...] + rstd * (dxn - xn * jnp.mean(dxn * xn, axis=-1, keepdims=True))
        dh_ref[...] = dh
        dhb_ref[...] = dh.astype(BF16)
        dw = jnp.sum(dy * xn, axis=0, keepdims=True)

        @pl.when(pl.program_id(0) == 0)
        def _():
            dw_ref[...] = dw

        @pl.when(pl.program_id(0) > 0)
        def _():
            dw_ref[...] += dw

    row = pl.BlockSpec((tr, D), lambda i: (i, 0))
    vec = pl.BlockSpec((1, D), lambda i: (0, 0))
    return pl.pallas_call(
        body, name=name, grid=(T // tr,),
        in_specs=[row, vec, row, row], out_specs=(row, row, vec),
        out_shape=(jax.ShapeDtypeStruct((T, D), F32), jax.ShapeDtypeStruct((T, D), BF16), jax.ShapeDtypeStruct((1, D), F32)),
        compiler_params=_cp("arbitrary"),
    )(h, w.reshape(1, D), dy, dres)


def _final_loss(h, w, target, first_row, name):
    T, D = h.shape
    tr = CHUNK
    assert first_row == tr

    def body(h_ref, w_ref, t_ref, loss_ref, dh_ref, dhb_ref, dw_ref):
        i = pl.program_id(0)
        x = h_ref[...]
        rstd = lax.rsqrt(jnp.mean(x * x, axis=-1, keepdims=True) + EPS)
        xn = x * rstd
        w = w_ref[...]
        live = i > 0
        err = jnp.where(live, xn * w - t_ref[...], 0.0)
        part = 0.5 * jnp.sum(jnp.mean(err * err, axis=-1, keepdims=True), axis=0, keepdims=True)
        dout = err * (1.0 / D)
        dxn = dout * w
        dh = rstd * (dxn - xn * jnp.mean(dxn * xn, axis=-1, keepdims=True))
        dh_ref[...] = dh
        dhb_ref[...] = dh.astype(BF16)
        dw = jnp.sum(dout * xn, axis=0, keepdims=True)

        @pl.when(i == 0)
        def _():
            loss_ref[...] = part
            dw_ref[...] = dw

        @pl.when(i > 0)
        def _():
            loss_ref[...] += part
            dw_ref[...] += dw

    row = pl.BlockSpec((tr, D), lambda i: (i, 0))
    vec = pl.BlockSpec((1, D), lambda i: (0, 0))
    return pl.pallas_call(
        body, name=name, grid=(T // tr,),
        in_specs=[row, vec, pl.BlockSpec((tr, D), lambda i: (jnp.maximum(i - 1, 0), 0))],
        out_specs=(pl.BlockSpec((1, 1), lambda i: (0, 0)), row, row, vec),
        out_shape=(jax.ShapeDtypeStruct((1, 1), F32), jax.ShapeDtypeStruct((T, D), F32), jax.ShapeDtypeStruct((T, D), BF16),
                   jax.ShapeDtypeStruct((1, D), F32)),
        compiler_params=_cp("arbitrary"),
    )(h, w.reshape(1, D), target)


def _gate_fwd(pg, bg, y_pool, y_ssd, name):
    T, D = y_pool.shape
    tr = _tile(T, 528, 16)

    def body(gp_ref, gs_ref, bp_ref, bs_ref, yp_ref, ys_ref, o_ref):
        gp = _sigmoid(gp_ref[...] + bp_ref[...])
        gs = _sigmoid(gs_ref[...] + bs_ref[...])
        o_ref[...] = (gp * yp_ref[...] + gs * ys_ref[...]).astype(BF16)

    row = pl.BlockSpec((tr, D), lambda i: (i, 0))
    row1 = pl.BlockSpec((tr, D), lambda i: (i, 1))
    vec = pl.BlockSpec((1, D), lambda i: (0, 0))
    vec1 = pl.BlockSpec((1, D), lambda i: (0, 1))
    b2 = bg.reshape(1, 2 * D)
    return pl.pallas_call(
        body, name=name, grid=(T // tr,),
        in_specs=[row, row1, vec, vec1, row, row], out_specs=row,
        out_shape=jax.ShapeDtypeStruct((T, D), BF16), compiler_params=_cp("parallel"),
    )(pg, pg, b2, b2, y_pool, y_ssd)


def _gate_bwd(pg, bg, y_pool, y_ssd, dmix, name):
    T, D = y_pool.shape
    tr = _tile(T, 528, 16)

    def body(gp_ref, gs_ref, bp_ref, bs_ref, yp_ref, ys_ref, dm_ref, dg_ref, dyp_ref, dys_ref, db_ref):
        gp = _sigmoid(gp_ref[...] + bp_ref[...])
        gs = _sigmoid(gs_ref[...] + bs_ref[...])
        dm = dm_ref[...]
        dyp_ref[...] = (dm * gp).astype(BF16)
        dys_ref[...] = (dm * gs).astype(BF16)
        dgp = dm * yp_ref[...] * gp * (1.0 - gp)
        dgs = dm * ys_ref[...] * gs * (1.0 - gs)
        dg_ref[:, :D] = dgp.astype(BF16)
        dg_ref[:, D:] = dgs.astype(BF16)
        db = jnp.concatenate([jnp.sum(dgp, axis=0, keepdims=True), jnp.sum(dgs, axis=0, keepdims=True)], axis=1)

        @pl.when(pl.program_id(0) == 0)
        def _():
            db_ref[...] = db

        @pl.when(pl.program_id(0) > 0)
        def _():
            db_ref[...] += db

    row = pl.BlockSpec((tr, D), lambda i: (i, 0))
    row1 = pl.BlockSpec((tr, D), lambda i: (i, 1))
    wide = pl.BlockSpec((tr, 2 * D), lambda i: (i, 0))
    vec = pl.BlockSpec((1, D), lambda i: (0, 0))
    vec1 = pl.BlockSpec((1, D), lambda i: (0, 1))
    vec2 = pl.BlockSpec((1, 2 * D), lambda i: (0, 0))
    b2 = bg.reshape(1, 2 * D)
    return pl.pallas_call(
        body, name=name, grid=(T // tr,),
        in_specs=[row, row1, vec, vec1, row, row, row], out_specs=(wide, row, row, vec2),
        out_shape=(jax.ShapeDtypeStruct((T, 2 * D), BF16), jax.ShapeDtypeStruct((T, D), BF16),
                   jax.ShapeDtypeStruct((T, D), BF16), jax.ShapeDtypeStruct((1, 2 * D), F32)),
        compiler_params=_cp("arbitrary"),
    )(pg, pg, b2, b2, y_pool, y_ssd, dmix)


def _pool_count(c, pad, window):
    pos = c * CHUNK + _iota((CHUNK, 1), 0) - pad
    return jnp.clip(pos + 1, 1, window).astype(F32)


def _by_group(g, vals):
    out = vals[-1]
    for k in range(len(vals) - 2, -1, -1):
        out = jnp.where(g == k, vals[k], out)
    return out


def _by_row_block(rows, vals, block):
    out = vals[0]
    for r in range(1, len(vals)):
        out = jnp.where(rows >= r * block, vals[r], out)
    return out


def _pool_fwd(u, wg, scale, pad, name):
    T, C = u.shape
    G, Cg, _ = wg.shape
    nc = T // CHUNK

    def body(u_ref, wg_ref, s_ref, p_ref, y_ref):
        g = pl.program_id(0)
        window = _by_group(g, POOL_WINDOWS)

        def chunk(c, carry):
            r0 = pl.multiple_of(c * CHUNK, CHUNK)
            h0 = pl.multiple_of(jnp.maximum(r0 - POOL_HALO, 0), 8)
            halo = jnp.where(c > 0, u_ref[pl.ds(h0, POOL_HALO), :], 0.0)
            xc = u_ref[pl.ds(r0, CHUNK), :]
            s = jnp.concatenate([halo, xc], axis=0)
            sums = []
            k = 1
            while k < POOL_WINDOWS[-1]:
                s = s + pltpu.roll(s, k, 0)
                k *= 2
                if k in POOL_WINDOWS:
                    sums.append(s[POOL_HALO:])
            wsum = _by_group(g, sums)
            pooled = wsum / _pool_count(c, pad, window) - xc
            pb = pooled.astype(BF16)
            p_ref[pl.ds(r0, CHUNK), :] = pb
            y_ref[pl.ds(r0, CHUNK), :] = (_dot(pb, wg_ref[0]) * s_ref[...]).astype(BF16)
            return carry

        lax.fori_loop(0, nc, chunk, 0)

    col = pl.BlockSpec((T, Cg), lambda g: (0, g))
    return pl.pallas_call(
        body, name=name, grid=(G,),
        in_specs=[col, pl.BlockSpec((1, Cg, Cg), lambda g: (g, 0, 0)), pl.BlockSpec((1, Cg), lambda g: (0, g))],
        out_specs=(col, col),
        out_shape=(jax.ShapeDtypeStruct((T, C), BF16), jax.ShapeDtypeStruct((T, C), BF16)),
        compiler_params=_cp("parallel"),
    )(u, wg, scale)


def _pool_bwd(pooled, wg, scale, dy, pad, name):
    T, C = dy.shape
    G, Cg, _ = wg.shape
    nc = T // CHUNK

    def body(p_ref, wg_ref, s_ref, dy_ref, du_ref, dwg_ref, ds_ref, halo_ref):
        g = pl.program_id(0)
        window = _by_group(g, POOL_WINDOWS)
        halo_ref[...] = jnp.zeros_like(halo_ref)
        dwg_ref[...] = jnp.zeros_like(dwg_ref)
        ds_ref[...] = jnp.zeros_like(ds_ref)

        def chunk(i, carry):
            c = nc - 1 - i
            r0 = pl.multiple_of(c * CHUNK, CHUNK)
            pb = p_ref[pl.ds(r0, CHUNK), :]
            dyc = dy_ref[pl.ds(r0, CHUNK), :]
            w = wg_ref[0]
            ypre = _dot(pb, w)
            ds_ref[...] += jnp.sum(dyc * ypre, axis=0, keepdims=True)
            dyp = (dyc * s_ref[...]).astype(BF16)
            dwg_ref[0] += _dot(pb, dyp, "tn")
            dpool = _dot(dyp, w, "nt")
            q = dpool / _pool_count(c, pad, window)
            s = jnp.concatenate([q, halo_ref[...]], axis=0)
            n = CHUNK + POOL_HALO
            sums = []
            k = 1
            while k < POOL_WINDOWS[-1]:
                s = s + pltpu.roll(s, n - k, 0)
                k *= 2
                if k in POOL_WINDOWS:
                    sums.append(s[:CHUNK])
            du = _by_group(g, sums) - dpool
            rows = r0 + _iota((CHUNK, 1), 0)
            du_ref[pl.ds(r0, CHUNK), :] = jnp.where(rows >= pad, du, 0.0).astype(BF16)
            halo_ref[...] = q[:POOL_HALO]
            return carry

        lax.fori_loop(0, nc, chunk, 0)

    col = pl.BlockSpec((T, Cg), lambda g: (0, g))
    return pl.pallas_call(
        body, name=name, grid=(G,),
        in_specs=[col, pl.BlockSpec((1, Cg, Cg), lambda g: (g, 0, 0)), pl.BlockSpec((1, Cg), lambda g: (0, g)), col],
        out_specs=(col, pl.BlockSpec((1, Cg, Cg), lambda g: (g, 0, 0)), pl.BlockSpec((1, Cg), lambda g: (0, g))),
        out_shape=(jax.ShapeDtypeStruct((T, C), BF16), jax.ShapeDtypeStruct((G, Cg, Cg), F32), jax.ShapeDtypeStruct((1, C), F32)),
        scratch_shapes=[pltpu.VMEM((POOL_HALO, Cg), F32)],
        compiler_params=_cp("parallel"),
    )(pooled, wg, scale, dy)


def _conv_pre(x_ref, w_ref, b_ref, c, r0):
    h0 = pl.multiple_of(jnp.maximum(r0 - CONV_HALO, 0), 8)
    halo = jnp.where(c > 0, x_ref[pl.ds(h0, CONV_HALO), :], 0.0)
    xe = jnp.concatenate([halo, x_ref[pl.ds(r0, CHUNK), :]], axis=0)
    y = jnp.broadcast_to(b_ref[...], (CHUNK, xe.shape[1]))
    for k in range(CONV_WIDTH):
        shift = CONV_WIDTH - 1 - k
        xs = xe if shift == 0 else pltpu.roll(xe, shift, 0)
        y = y + xs[CONV_HALO:] * w_ref[k:k + 1, :]
    return y, xe


def _conv_fwd(x, w, b, name):
    T = x.shape[0]
    C = w.shape[1]
    tc = _tile(C, 256, 128)
    nc = T // CHUNK

    def body(x_ref, w_ref, b_ref, o_ref):
        def chunk(c, carry):
            r0 = pl.multiple_of(c * CHUNK, CHUNK)
            y, _ = _conv_pre(x_ref, w_ref, b_ref, c, r0)
            o_ref[pl.ds(r0, CHUNK), :] = y * _sigmoid(y)
            return carry

        lax.fori_loop(0, nc, chunk, 0)

    col = pl.BlockSpec((T, tc), lambda j: (0, j))
    return pl.pallas_call(
        body, name=name, grid=(C // tc,),
        in_specs=[col, pl.BlockSpec((CONV_WIDTH, tc), lambda j: (0, j)), pl.BlockSpec((1, tc), lambda j: (0, j))],
        out_specs=col, out_shape=jax.ShapeDtypeStruct((T, C), F32), compiler_params=_cp("parallel"),
    )(x, w, b)


def _conv_bwd(x, w, b, dact, pad, name):
    T = x.shape[0]
    C = w.shape[1]
    tc = _tile(C, 256, 128)
    nc = T // CHUNK

    def body(x_ref, w_ref, b_ref, da_ref, dx_ref, dw_ref, db_ref, halo_ref):
        halo_ref[...] = jnp.zeros_like(halo_ref)
        dw_ref[...] = jnp.zeros_like(dw_ref)
        db_ref[...] = jnp.zeros_like(db_ref)

        def chunk(i, carry):
            c = nc - 1 - i
            r0 = pl.multiple_of(c * CHUNK, CHUNK)
            y, xe = _conv_pre(x_ref, w_ref, b_ref, c, r0)
            sg = _sigmoid(y)
            dpre = da_ref[pl.ds(r0, CHUNK), :] * (sg * (1.0 + y * (1.0 - sg)))
            db_ref[...] += jnp.sum(dpre, axis=0, keepdims=True)
            de = jnp.concatenate([dpre, halo_ref[...]], axis=0)
            n = CHUNK + CONV_HALO
            dx = jnp.zeros_like(dpre)
            for k in range(CONV_WIDTH):
                shift = CONV_WIDTH - 1 - k
                xs = xe if shift == 0 else pltpu.roll(xe, shift, 0)
                dw_ref[k:k + 1, :] += jnp.sum(dpre * xs[CONV_HALO:], axis=0, keepdims=True)
                ds = de if shift == 0 else pltpu.roll(de, n - shift, 0)
                dx = dx + ds[:CHUNK] * w_ref[k:k + 1, :]
            rows = r0 + _iota((CHUNK, 1), 0)
            dx_ref[pl.ds(r0, CHUNK), :] = jnp.where(rows >= pad, dx, 0.0).astype(BF16)
            halo_ref[...] = dpre[:CONV_HALO]
            return carry

        lax.fori_loop(0, nc, chunk, 0)

    col = pl.BlockSpec((T, tc), lambda j: (0, j))
    wspec = pl.BlockSpec((CONV_WIDTH, tc), lambda j: (0, j))
    bspec = pl.BlockSpec((1, tc), lambda j: (0, j))
    return pl.pallas_call(
        body, name=name, grid=(C // tc,),
        in_specs=[col, wspec, bspec, col], out_specs=(col, wspec, bspec),
        out_shape=(jax.ShapeDtypeStruct((T, C), BF16), jax.ShapeDtypeStruct((CONV_WIDTH, C), F32), jax.ShapeDtypeStruct((1, C), F32)),
        scratch_shapes=[pltpu.VMEM((CONV_HALO, tc), F32)],
        compiler_params=_cp("parallel"),
    )(x, w, b, dact)


def _cumsum_rows(x, reverse=False):
    n = x.shape[0]
    idx = _iota(x.shape, 0)
    k = 1
    while k < n:
        if reverse:
            x = x + jnp.where(idx < n - k, pltpu.roll(x, n - k, 0), 0.0)
        else:
            x = x + jnp.where(idx >= k, pltpu.roll(x, k, 0), 0.0)
        k *= 2
    return x


def _softplus(x):
    return jnp.maximum(x, 0.0) + jnp.log1p(jnp.exp(-jnp.abs(x)))


def _head_selector(n_heads, width):
    lane = jnp.arange(n_heads * width)[None, :] // width
    return (lane == jnp.arange(LANES)[:, None]).astype(BF16)


def _dot_exact(x, sel, parts, mode="nn"):
    acc = None
    for _ in range(parts):
        piece = x.astype(BF16)
        x = x - piece.astype(F32)
        t = lax.dot_general(piece, sel, _DIMS[mode], preferred_element_type=F32)
        acc = t if acc is None else acc + t
    return acc


def _ssd_decays(dt, cs, sel_p_ref, sel_q_ref):
    cs_b = _dot_exact(cs, sel_q_ref[...], 3)
    dt_x = _dot_exact(dt, sel_p_ref[...], 3)
    cs_x = _dot_exact(cs, sel_p_ref[...], 3)
    return cs_b, dt_x, jnp.exp(cs_x), jnp.exp(cs_x[CHUNK - 1:CHUNK, :] - cs_x)


def _ssd_common(c, pad, n_heads, dtr_ref, dtb_ref, al_ref):
    rows = c * CHUNK + _iota((CHUNK, 1), 0)
    valid = rows >= pad
    live = jnp.logical_and(valid, _iota((1, LANES), 1) < n_heads)
    pre = dtr_ref[...] + dtb_ref[...]
    dt = jnp.where(live, _softplus(pre), 0.0)
    a = -jnp.exp(al_ref[...])
    cs = _cumsum_rows(dt * a)
    return valid, live, dt, a, cs, cs.T, _sigmoid(pre)


def _ssd_specs(T, DI, GN, cfirst):
    xcol = DI // GN

    def at(col):
        return lambda c: (cfirst(c), col)

    x = pl.BlockSpec((CHUNK, DI), at(0))
    b = pl.BlockSpec((CHUNK, GN), at(xcol))
    cm = pl.BlockSpec((CHUNK, GN), at(xcol + 1))
    dt = pl.BlockSpec((CHUNK, LANES), at(0))
    vec = pl.BlockSpec((1, LANES), lambda c: (0, 0))
    nw = pl.BlockSpec((1, DI), lambda c: (0, 0))
    return x, b, cm, dt, vec, nw


def _ssd_fwd(xbc, pdt, pz, dt_bias, a_log, d_skip, norm_w, pad, n_heads, name):
    T = xbc.shape[0]
    DI = pz.shape[1]
    P = DI // n_heads
    GN = (xbc.shape[1] - DI) // 2
    G = GN // D_STATE
    R = n_heads // G
    GW = R * P
    nc = T // CHUNK
    Q, N = CHUNK, D_STATE

    def body(x_ref, b_ref, c_ref, dtr_ref, z_ref, dtb_ref, al_ref, dsk_ref, nw_ref, sel_p_ref, sel_q_ref,
             y_ref, yn_ref, prev_ref, s_ref):
        c = pl.program_id(0)

        @pl.when(c == 0)
        def _():
            s_ref[...] = jnp.zeros_like(s_ref)

        valid, _, dt, _, cs, cst, _ = _ssd_common(c, pad, n_heads, dtr_ref, dtb_ref, al_ref)
        cs_b, dt_x, e_x, dec_x = _ssd_decays(dt, cs, sel_p_ref, sel_q_ref)
        e_last = jnp.exp(cs[Q - 1:Q, :])
        tri = _iota((Q, Q), 0) >= _iota((Q, Q), 1)
        state_rows = _iota((GW, 1), 0)
        head_lane = _iota((1, GW), 1)
        for g in range(G):
            gs = slice(g * GW, (g + 1) * GW)
            bg = jnp.where(valid, b_ref[:, g * N:(g + 1) * N], 0.0).astype(BF16)
            cg = jnp.where(valid, c_ref[:, g * N:(g + 1) * N], 0.0).astype(BF16)
            xg = jnp.where(valid, x_ref[:, gs], 0.0)
            sg = s_ref[gs, :]
            prev_ref[0, gs, :] = sg
            cb = _dot(cg, bg, "nt")
            xdt = xg * dt_x[:, gs]
            yg = _dot(cg, sg, "nt") * e_x[:, gs]
            for r in range(R):
                h = g * R + r
                lmat = jnp.exp(jnp.where(tri, cs_b[:, h * Q:(h + 1) * Q] - cst[h:h + 1, :], -jnp.inf))
                in_head = jnp.logical_and(head_lane >= r * P, head_lane < (r + 1) * P)
                yg = yg + _dot(cb * lmat, jnp.where(in_head, xdt, 0.0))
            y_ref[:, gs] = yg
            decay = _by_row_block(state_rows, [e_last[:, g * R + r:g * R + r + 1] for r in range(R)], P)
            s_ref[gs, :] = sg * decay + _dot(xdt * dec_x[:, gs], bg, "tn")
            z = z_ref[:, gs]
            gz = (yg + xg * dsk_ref[:, gs]) * (z * _sigmoid(z))
            rstd = lax.rsqrt(jnp.mean(gz * gz, axis=-1, keepdims=True) + EPS)
            yn_ref[:, gs] = ((gz * rstd) * nw_ref[:, gs]).astype(BF16)

    x_s, b_s, c_s, dt_s, vec, nw = _ssd_specs(T, DI, GN, lambda c: c)
    wide = pl.BlockSpec((Q, DI), lambda c: (c, 0))
    sel_p = pl.BlockSpec((LANES, DI), lambda c: (0, 0))
    sel_q = pl.BlockSpec((LANES, n_heads * Q), lambda c: (0, 0))
    return pl.pallas_call(
        body, name=name, grid=(nc,),
        in_specs=[x_s, b_s, c_s, dt_s, wide, vec, vec, nw, nw, sel_p, sel_q],
        out_specs=(wide, wide, pl.BlockSpec((1, DI, N), lambda c: (c, 0, 0))),
        out_shape=(jax.ShapeDtypeStruct((T, DI), F32), jax.ShapeDtypeStruct((T, DI), BF16), jax.ShapeDtypeStruct((nc, DI, N), F32)),
        scratch_shapes=[pltpu.VMEM((DI, N), F32)],
        compiler_params=_cp("arbitrary"),
    )(xbc, xbc, xbc, pdt, pz, dt_bias, a_log, jnp.repeat(d_skip[:, :n_heads], P, axis=1), norm_w,
      _head_selector(n_heads, P), _head_selector(n_heads, Q))


def _ssd_bwd(xbc, pdt, pz, dt_bias, a_log, d_skip, norm_w, y, prev, dyn, pad, n_heads, name):
    T, W = xbc.shape
    DI = pz.shape[1]
    P = DI // n_heads
    GN = (W - DI) // 2
    G = GN // D_STATE
    R = n_heads // G
    GW = R * P
    nc = T // CHUNK
    Q, N = CHUNK, D_STATE

    def body(x_ref, b_ref, c_ref, dtr_ref, z_ref, dtb_ref, al_ref, dsk_ref, nw_ref, y_ref, prev_ref, next_ref, dyn_ref,
             sel_p_ref, sel_q_ref, sel_pt_ref,
             dxbc_ref, dz_ref, ddt_ref, ddtb_ref, dal_ref, ddsk_ref, dnw_ref, ds_ref):
        i = pl.program_id(0)
        c = nc - 1 - i

        @pl.when(i == 0)
        def _():
            ds_ref[...] = jnp.zeros_like(ds_ref)
            ddtb_ref[...] = jnp.zeros_like(ddtb_ref)
            dal_ref[...] = jnp.zeros_like(dal_ref)
            ddsk_ref[...] = jnp.zeros_like(ddsk_ref)
            dnw_ref[...] = jnp.zeros_like(dnw_ref)

        valid, live, dt, a, cs, cst, sig_pre = _ssd_common(c, pad, n_heads, dtr_ref, dtb_ref, al_ref)
        cs_b, dt_x, e_x, dec_x = _ssd_decays(dt, cs, sel_p_ref, sel_q_ref)
        e_last = jnp.exp(cs[Q - 1:Q, :])
        tri = _iota((Q, Q), 0) >= _iota((Q, Q), 1)
        tri_t = _iota((Q, Q), 0) <= _iota((Q, Q), 1)
        state_rows = _iota((GW, 1), 0)
        head_lane = _iota((1, GW), 1)
        lane = _iota((1, LANES), 1)
        head_rows = _iota((LANES, 1), 0)
        s_dy_cs = jnp.zeros((Q, LANES), F32)
        s_x_bds = jnp.zeros((Q, LANES), F32)
        s_x_dxdt = jnp.zeros((Q, LANES), F32)
        dcs_rows = jnp.zeros((Q, LANES), F32)
        dcs_cols = jnp.zeros((LANES, Q), F32)
        c_end = jnp.zeros((1, LANES), F32)
        dsk_rows = []
        for g in range(G):
            gs = slice(g * GW, (g + 1) * GW)
            bg = jnp.where(valid, b_ref[:, g * N:(g + 1) * N], 0.0).astype(BF16)
            cg = jnp.where(valid, c_ref[:, g * N:(g + 1) * N], 0.0).astype(BF16)
            xg = jnp.where(valid, x_ref[:, gs], 0.0)
            s_prev = prev_ref[0, gs, :]
            dsg = ds_ref[gs, :]
            end = dsg * next_ref[0, gs, :]
            yg = y_ref[:, gs]
            dsk = dsk_ref[:, gs]
            ytot = yg + xg * dsk
            z = z_ref[:, gs]
            sz = _sigmoid(z)
            silu = z * sz
            gz = ytot * silu
            rstd = lax.rsqrt(jnp.mean(gz * gz, axis=-1, keepdims=True) + EPS)
            gn = gz * rstd
            dyn_g = dyn_ref[:, gs]
            dnw_ref[:, gs] += jnp.sum(dyn_g * gn, axis=0, keepdims=True)
            dgn = dyn_g * nw_ref[:, gs]
            dgz = rstd * (dgn - gn * jnp.mean(dgn * gn, axis=-1, keepdims=True))
            dz_ref[:, gs] = (dgz * ytot * (sz * (1.0 + z * (1.0 - sz)))).astype(BF16)
            dy = dgz * silu
            dsk_rows.append(jnp.sum(dy * xg, axis=0, keepdims=True))
            cb = _dot(cg, bg, "nt")
            cb_t = _dot(bg, cg, "nt")
            bds = _dot(bg, dsg, "nt") * dec_x[:, gs]
            csg = _dot(cg, s_prev, "nt")
            xdt = xg * dt_x[:, gs]
            dxdt = bds
            dcb = jnp.zeros((Q, Q), F32)
            dcb_t = jnp.zeros((Q, Q), F32)
            for r in range(R):
                h = g * R + r
                cs_col = cs_b[:, h * Q:(h + 1) * Q]
                cs_row = cst[h:h + 1, :]
                lmat = jnp.exp(jnp.where(tri, cs_col - cs_row, -jnp.inf))
                lmat_t = jnp.exp(jnp.where(tri_t, cs_row - cs_col, -jnp.inf))
                in_head = jnp.logical_and(head_lane >= r * P, head_lane < (r + 1) * P)
                dyr = jnp.where(in_head, dy, 0.0)
                dm = _dot(dyr, xdt, "nt")
                dcb = dcb + dm * lmat
                dcb_t = dcb_t + _dot(xdt, dyr, "nt") * lmat_t
                w_rc = dm * (cb * lmat)
                dcs_rows = jnp.where(lane == h, jnp.sum(w_rc, axis=1, keepdims=True), dcs_rows)
                dcs_cols = jnp.where(head_rows == h, jnp.sum(w_rc, axis=0, keepdims=True), dcs_cols)
                dxdt = dxdt + _dot(cb_t * lmat_t, dyr)
            sel_t = sel_pt_ref[gs, :]
            s_dy_cs = s_dy_cs + _dot_exact(dy * csg, sel_t, 2)
            s_x_bds = s_x_bds + _dot_exact(xg * bds, sel_t, 2)
            s_x_dxdt = s_x_dxdt + _dot_exact(xg * dxdt, sel_t, 2)
            c_end = c_end + jnp.sum(jnp.sum(end, axis=1, keepdims=True) * sel_t.astype(F32), axis=0, keepdims=True)
            dye = dy * e_x[:, gs]
            dc = _dot(dcb, bg) + _dot(dye, s_prev)
            db = _dot(dcb_t, cg) + _dot(xdt * dec_x[:, gs], dsg)
            decay = _by_row_block(state_rows, [e_last[:, g * R + r:g * R + r + 1] for r in range(R)], P)
            ds_ref[gs, :] = dsg * decay + _dot(dye, cg, "tn")
            dxbc_ref[:, gs] = jnp.where(valid, dxdt * dt_x[:, gs] + dy * dsk, 0.0)
            dxbc_ref[:, DI + g * N:DI + (g + 1) * N] = jnp.where(valid, db, 0.0)
            dxbc_ref[:, DI + GN + g * N:DI + GN + (g + 1) * N] = jnp.where(valid, dc, 0.0)
        dcs = s_dy_cs * jnp.exp(cs) - dt * s_x_bds
        da_cs = _cumsum_rows(dcs + (dcs_rows - dcs_cols.T), reverse=True) + c_end
        ddt_all = jnp.where(live, da_cs * a + s_x_dxdt, 0.0)
        ddt_raw = ddt_all * sig_pre
        ddt_ref[...] = ddt_raw.astype(BF16)
        ddtb_ref[...] += jnp.sum(ddt_raw, axis=0, keepdims=True)
        dal_ref[...] += jnp.sum(da_cs * dt, axis=0, keepdims=True) * a
        dsk_all = jnp.broadcast_to(jnp.concatenate(dsk_rows, axis=1), (8, DI))
        ddsk_ref[...] += _dot_exact(dsk_all, sel_pt_ref[...], 3)[0:1]

    rev = lambda i: nc - 1 - i
    x_s, b_s, c_s, dt_s, vec, nw = _ssd_specs(T, DI, GN, rev)
    wide = pl.BlockSpec((Q, DI), lambda i: (rev(i), 0))
    st = pl.BlockSpec((1, DI, N), lambda i: (rev(i), 0, 0))
    st_next = pl.BlockSpec((1, DI, N), lambda i: (jnp.minimum(rev(i) + 1, nc - 1), 0, 0))
    sel_p = pl.BlockSpec((LANES, DI), lambda i: (0, 0))
    sel_q = pl.BlockSpec((LANES, n_heads * Q), lambda i: (0, 0))
    sel_pt = pl.BlockSpec((DI, LANES), lambda i: (0, 0))
    sel = _head_selector(n_heads, P)
    return pl.pallas_call(
        body, name=name, grid=(nc,),
        in_specs=[x_s, b_s, c_s, dt_s, wide, vec, vec, nw, nw, wide, st, st_next, wide, sel_p, sel_q, sel_pt],
        out_specs=(pl.BlockSpec((Q, W), lambda i: (rev(i), 0)), wide, dt_s, vec, vec, vec, nw),
        out_shape=(jax.ShapeDtypeStruct((T, W), F32), jax.ShapeDtypeStruct((T, DI), BF16), jax.ShapeDtypeStruct((T, LANES), BF16),
                   jax.ShapeDtypeStruct((1, LANES), F32), jax.ShapeDtypeStruct((1, LANES), F32),
                   jax.ShapeDtypeStruct((1, LANES), F32), jax.ShapeDtypeStruct((1, DI), F32)),
        scratch_shapes=[pltpu.VMEM((DI, N), F32)],
        compiler_params=_cp("arbitrary"),
    )(xbc, xbc, xbc, pdt, pz, dt_bias, a_log, jnp.repeat(d_skip[:, :n_heads], P, axis=1), norm_w, y, prev, prev, dyn,
      sel, _head_selector(n_heads, Q), sel.T)


def _adamw(w, g, m, v, name):
    layers, rows, cols = w.shape
    tr = _tile(rows, 256, 8)

    def body(w_ref, g_ref, m_ref, v_ref, d_ref, nm_ref, nv_ref):
        g = g_ref[...]
        m = ADAM_B1 * m_ref[...] + (1.0 - ADAM_B1) * g
        v = ADAM_B2 * v_ref[...] + (1.0 - ADAM_B2) * (g * g)
        m_hat = m / (1.0 - ADAM_B1 ** ADAM_STEP)
        v_hat = v / (1.0 - ADAM_B2 ** ADAM_STEP)
        d_ref[...] = -ADAM_LR * (m_hat / (jnp.sqrt(v_hat) + ADAM_EPS) + ADAM_WD * w_ref[...])
        nm_ref[...] = m
        nv_ref[...] = v

    blk = pl.BlockSpec((1, tr, cols), lambda l, i: (l, i, 0))
    out = jax.ShapeDtypeStruct((layers, rows, cols), F32)
    return pl.pallas_call(
        body, name=name, grid=(layers, rows // tr), in_specs=[blk] * 4, out_specs=(blk,) * 3, out_shape=(out,) * 3,
        compiler_params=_cp("parallel", "parallel"),
    )(w, g, m, v)


def _sum_stack(parts, name, out_dtype=F32):
    rows, cols = parts[0].shape
    tr = _tile(rows, 512, 16)

    def body(*refs):
        acc = refs[0][...].astype(F32)
        for r in refs[1:-1]:
            acc = acc + r[...].astype(F32)
        refs[-1][...] = acc.astype(out_dtype)

    blk = pl.BlockSpec((tr, cols), lambda i: (i, 0))
    return pl.pallas_call(
        body, name=name, grid=(rows // tr,), in_specs=[blk] * len(parts), out_specs=blk,
        out_shape=jax.ShapeDtypeStruct((rows, cols), out_dtype), compiler_params=_cp("parallel"),
    )(*parts)


def _place():
    return lax.axis_index("x"), lax.axis_index("y"), lax.axis_index("c")


def _other_chips(x, y):
    return [(1 - x, y), (x, 1 - y), (1 - x, 1 - y)]


def _all_gather(shards, name):
    nb = len(shards)

    def body(*refs):
        ins, outs = refs[:nb], refs[nb:2 * nb]
        send_sems, recv_sems, local_sems = refs[2 * nb:]
        x, y, c = _place()
        me, sibling = (x, y, c), (x, y, 1 - c)
        chips = _other_chips(x, y)

        def copy(q, k, block, to, src=None):
            dst = outs[q].at[4 * block[0] + 2 * block[1] + block[2]]
            return pltpu.make_async_remote_copy(
                src_ref=dst if src is None else src, dst_ref=dst,
                send_sem=send_sems.at[7 * q + k], recv_sem=recv_sems.at[7 * q + k], device_id=to, device_id_type=MESH)

        started = []
        for q in range(nb):
            mine = pltpu.make_async_copy(ins[q], outs[q].at[4 * x + 2 * y + c], local_sems.at[q])
            mine.start()
            started.append(mine)
        first = []
        for q in range(nb):
            first.append(copy(q, 0, me, sibling, src=ins[q]))
            first += [copy(q, 1 + j, me, (*chip, c), src=ins[q]) for j, chip in enumerate(chips)]
        for cp in first:
            cp.start()
        passed = []
        for j, chip in enumerate(chips):
            for q in range(nb):
                copy(q, 1 + j, (*chip, c), me).wait_recv()
                fwd = copy(q, 4 + j, (*chip, c), sibling)
                fwd.start()
                passed.append(fwd)
        for q in range(nb):
            copy(q, 0, sibling, me).wait_recv()
            for j, chip in enumerate(chips):
                copy(q, 4 + j, (*chip, 1 - c), me).wait_recv()
        for cp in first + passed:
            cp.wait_send()
        for mine in started:
            mine.wait()

    return pl.pallas_call(
        body, name=name, in_specs=[ANY] * nb, out_specs=tuple([ANY] * nb),
        out_shape=tuple(jax.ShapeDtypeStruct((N_DEV,) + s.shape, s.dtype) for s in shards),
        scratch_shapes=[pltpu.SemaphoreType.DMA((7 * nb,)), pltpu.SemaphoreType.DMA((7 * nb,)), pltpu.SemaphoreType.DMA((nb,))],
    )(*shards)


HBM = pl.BlockSpec(memory_space=pltpu.HBM)
SEM = pl.BlockSpec(memory_space=pltpu.SEMAPHORE)
N_PEERS = N_DEV - 1


def _peer(k, x, y, c):
    return (1 - x if k & 4 else x, 1 - y if k & 2 else y, 1 - c if k & 1 else c)


def _direct_copies(gather, srcs, lands, send_sems, recv_sems):
    x, y, c = _place()
    copies = []
    for q in range(len(srcs)):
        for k in range(1, N_DEV):
            px, py, pc = _peer(k, x, y, c)
            if gather:
                src, dst = srcs[q], lands[q].at[4 * x + 2 * y + c]
            else:
                src, dst = srcs[q].at[4 * px + 2 * py + pc], lands[q].at[k - 1]
            copies.append(pltpu.make_async_remote_copy(
                src_ref=src, dst_ref=dst, send_sem=send_sems.at[N_PEERS * q + k - 1], recv_sem=recv_sems.at[N_PEERS * q + k - 1],
                device_id=(px, py, pc), device_id_type=MESH))
    return copies


def _exchange_start(gather, srcs, lands, name):
    n = len(srcs)

    def body(*refs):
        send_sems, recv_sems = refs[2 * n], refs[2 * n + 1]
        for cp in _direct_copies(gather, refs[:n], refs[n:2 * n], send_sems, recv_sems):
            cp.start()
        refs[-1][...] = jnp.zeros_like(refs[-1])

    held = [pltpu.with_memory_space_constraint(t, pltpu.HBM) for t in list(srcs) + list(lands)]
    out = pl.pallas_call(
        body, name=name,
        out_shape=(pltpu.SemaphoreType.DMA((N_PEERS * n,)), pltpu.SemaphoreType.DMA((N_PEERS * n,)),
                   *[pltpu.HBM(t.shape, t.dtype) for t in held], jax.ShapeDtypeStruct((8, LANES), F32)),
        in_specs=[HBM] * (2 * n), out_specs=(SEM, SEM, *[HBM] * (2 * n), pl.BlockSpec(memory_space=pltpu.VMEM)),
        input_output_aliases={i: 2 + i for i in range(2 * n)},
        compiler_params=pltpu.CompilerParams(has_side_effects=pltpu.SideEffectType.DATAFLOW_SIDE_EFFECTING),
    )(*held)
    return out[0], out[1], list(out[2:2 + 2 * n]), out[-1]


def _exchange_wait(gather, started, after, name):
    send_sems, recv_sems, held, _ = started
    n = len(held) // 2

    def body(*refs):
        for cp in _direct_copies(gather, refs[:n], refs[n:2 * n], refs[2 * n], refs[2 * n + 1]):
            cp.wait_send()
            cp.wait_recv()

    out = pl.pallas_call(
        body, name=name, out_shape=tuple(pltpu.HBM(t.shape, t.dtype) for t in held),
        in_specs=[HBM] * (2 * n) + [SEM, SEM, pl.BlockSpec(memory_space=pl.ANY)], out_specs=tuple([HBM] * (2 * n)),
        input_output_aliases={i: i for i in range(2 * n)},
        compiler_params=pltpu.CompilerParams(has_side_effects=pltpu.SideEffectType.DATAFLOW_SIDE_EFFECTING),
    )(*held, send_sems, recv_sems, after)
    return list(out[n:])


def _own_plus_received(stack, got, name):
    n_blocks, rows, cols = stack.shape
    tr = _tile(rows, 512, 16)

    def body(*refs):
        acc = refs[0][0]
        for r in refs[1:-1]:
            acc = acc + r[0].astype(F32)
        refs[-1][...] = acc

    def mine(i):
        if n_blocks == 1:
            return (0, i, 0)
        x, y, c = _place()
        return (4 * x + 2 * y + c, i, 0)

    return pl.pallas_call(
        body, name=name, grid=(rows // tr,),
        in_specs=[pl.BlockSpec((1, tr, cols), mine)]
        + [pl.BlockSpec((1, tr, cols), functools.partial(lambda k, i: (k, i, 0), k)) for k in range(N_PEERS)],
        out_specs=pl.BlockSpec((tr, cols), lambda i: (i, 0)),
        out_shape=jax.ShapeDtypeStruct((rows, cols), F32), compiler_params=_cp("parallel"),
    )(stack, *[got] * N_PEERS)


class _Shard:
    def __init__(self, name, axis):
        self.name, self.axis = name, axis


BIG = [_Shard("w_in", 2), _Shard("pool_w_group", 2), _Shard("w_pool_up", 1), _Shard("w_ssd_out", 1),
       _Shard("w_o", 1), _Shard("w_ff1", 2), _Shard("w_ff2", 1)]
SMALL_SHARDED = [_Shard("meta_tokens", 1), _Shard("conv_w", 2)]
REPLICATED = ["mix_norm_w", "b_gate", "pool_scale", "conv_b", "dt_bias", "a_log", "d_skip", "ssd_norm_w",
              "mlp_norm_w", "final_norm_w"]
WEIGHTS = ["meta_tokens", "mix_norm_w", "w_in", "b_gate", "pool_w_group", "pool_scale", "w_pool_up", "conv_w", "conv_b",
           "dt_bias", "a_log", "d_skip", "ssd_norm_w", "w_ssd_out", "w_o", "mlp_norm_w", "w_ff1", "w_ff2", "final_norm_w"]


def _columns(blocks, a, b):
    pieces, o = [], 0
    for t in blocks:
        n = t.shape[-1]
        lo, hi = max(a, o), min(b, o + n)
        if lo < hi:
            pieces.append(t[..., lo - o:hi - o])
        o += n
    return pieces[0] if len(pieces) == 1 else jnp.concatenate(pieces, axis=-1)


def _rows2(a):
    return a.reshape(-1, a.shape[-1])


def _unshard(stack, shard_shape, axis):
    t = stack.reshape((N_DEV,) + tuple(shard_shape))
    return jnp.concatenate([t[d] for d in range(N_DEV)], axis=axis)


def _reshard(layers, axis):
    cut = [jnp.split(t, N_DEV, axis=axis - 1) for t in layers]
    blocks = [jnp.concatenate([_rows2(pieces[d]) for pieces in cut], axis=0) for d in range(N_DEV)]
    return jnp.stack(blocks).reshape((4, 2) + blocks[0].shape)


def _lane_rows(a):
    n = a.size
    tile = 8 * LANES
    if n % tile:
        return jnp.pad(a.reshape(-1), (0, (-n) % tile)).reshape(-1, LANES)
    return a.reshape(-1, LANES)


def _unpack_small(buf, spans, shapes):
    out = []
    for (o, r), shp in zip(spans, shapes):
        n = 1
        for d in shp:
            n *= d
        t = buf[o:o + r]
        out.append(t.reshape(shp) if n == r * LANES else t.reshape(-1)[:n].reshape(shp))
    return out


def _pack_small(parts, mult):
    mats = [_lane_rows(p) for p in parts]
    spans, o = [], 0
    for t in mats:
        spans.append((o, t.shape[0]))
        o += t.shape[0]
    fill = (-o) % mult
    if fill:
        mats.append(jnp.zeros((fill, LANES), mats[0].dtype))
    return jnp.concatenate(mats, axis=0), spans


def _layer_fwd(h, lw, cfg, tag, dep=None):
    pad, n_heads = cfg["pad"], cfg["n_heads"]
    u = _rms_fwd(h, lw["mix_norm_w"], f"rms_mix_{tag}", dep=dep)
    p_xbc = _mm(u, lw["w_xbc"], "nn", f"proj_xbc_{tag}")
    p_z = _mm(u, lw["w_z"], "nn", f"proj_z_{tag}")
    p_gate = _mm(u, lw["w_gate"], "nn", f"proj_gate_{tag}")
    p_pool = _mm(u, lw["w_pool"], "nn", f"proj_pool_{tag}")
    p_dt = _mm(u, lw["w_dt"], "nn", f"proj_dt_{tag}")
    pooled, y1 = _pool_fwd(p_pool, lw["pool_w_group"], lw["pool_scale"], pad, f"pool_fwd_{tag}")
    y_pool = _mm(y1, lw["w_pool_up"], "nn", f"pool_up_{tag}")
    xbc = _conv_fwd(p_xbc, lw["conv_w"], lw["conv_b"], f"conv_fwd_{tag}")
    y, yn, prev = _ssd_fwd(xbc, p_dt, p_z, lw["dt_bias"], lw["a_log"], lw["d_skip"], lw["ssd_norm_w"], pad, n_heads, f"ssd_fwd_{tag}")
    y_ssd = _mm(yn, lw["w_ssd_out"], "nn", f"ssd_out_{tag}")
    mix = _gate_fwd(p_gate, lw["b_gate"], y_pool, y_ssd, f"gate_fwd_{tag}")
    h_mid = _mm(mix, lw["w_o"], "nn", f"mix_out_{tag}", res=h)
    v = _rms_fwd(h_mid, lw["mlp_norm_w"], f"rms_mlp_{tag}")
    hid, act = _mm(v, lw["w_ff1"], "nn", f"ff1_{tag}", epi="relu2")
    h_out = _mm(act, lw["w_ff2"], "nn", f"ff2_{tag}", res=h_mid)
    saved = dict(h=h, u=u, p_xbc=p_xbc, p_z=p_z, p_gate=p_gate, p_dt=p_dt, pooled=pooled, y1=y1, y_pool=y_pool, xbc=xbc,
                 y=y, yn=yn, prev=prev, y_ssd=y_ssd, mix=mix, h_mid=h_mid, v=v, hid=hid, act=act)
    return h_out, saved


def _layer_bwd(dh, dh_b, lw, s, cfg, tag, traffic, i):
    pad, n_heads = cfg["pad"], cfg["n_heads"]
    g = {}
    dhid = _mm(dh_b, lw["w_ff2"], "nt", f"d_act_{tag}", epi="drelu2", aux=s["hid"], dep=traffic.bwd_begin(i))
    gb = {}
    g["w_ff2"], gb["w_ff2"] = _mm(s["act"], dh_b, "tn", f"dw_ff2_{tag}", epi="twin")
    dv = _mm(dhid, lw["w_ff1"], "nt", f"d_v_{tag}")
    g["w_ff1"], gb["w_ff1"] = _mm(s["v"], dhid, "tn", f"dw_ff1_{tag}", epi="twin")
    dep = traffic.grads_ready(i, {k: (g[k], gb[k]) for k in ("w_ff1", "w_ff2")}, "mlp")
    dh_mid, dh_mid_b, g["mlp_norm_w"] = _rms_bwd(s["h_mid"], lw["mlp_norm_w"], dv, dh, f"rms_mlp_bwd_{tag}")
    dmix = _mm(dh_mid_b, lw["w_o"], "nt", f"d_mix_{tag}", dep=dep)
    g["w_o"], gb["w_o"] = _mm(s["mix"], dh_mid_b, "tn", f"dw_o_{tag}", epi="twin")
    dgate, dy_pool, dy_ssd, g["b_gate"] = _gate_bwd(s["p_gate"], lw["b_gate"], s["y_pool"], s["y_ssd"], dmix, f"gate_bwd_{tag}")
    dy1 = _mm(dy_pool, lw["w_pool_up"], "nt", f"d_y1_{tag}")
    g["w_pool_up"], gb["w_pool_up"] = _mm(s["y1"], dy_pool, "tn", f"dw_pool_up_{tag}", epi="twin")
    dpool, g["pool_w_group"], g["pool_scale"] = _pool_bwd(s["pooled"], lw["pool_w_group"], lw["pool_scale"], dy1, pad, f"pool_bwd_{tag}")
    g["w_ssd_out"], gb["w_ssd_out"] = _mm(s["yn"], dy_ssd, "tn", f"dw_ssd_out_{tag}", epi="twin")
    gb["pool_w_group"] = g["pool_w_group"].astype(BF16)
    dep = traffic.grads_ready(i, {k: (g[k], gb[k]) for k in ("pool_w_group", "w_pool_up", "w_ssd_out", "w_o")}, "mix")
    dyn = _mm(dy_ssd, lw["w_ssd_out"], "nt", f"d_yn_{tag}", dep=dep)
    dact, dz, ddt, g["dt_bias"], g["a_log"], g["d_skip"], g["ssd_norm_w"] = _ssd_bwd(
        s["xbc"], s["p_dt"], s["p_z"], lw["dt_bias"], lw["a_log"], lw["d_skip"], lw["ssd_norm_w"], s["y"], s["prev"], dyn,
        pad, n_heads, f"ssd_bwd_{tag}")
    dxbc, g["conv_w"], g["conv_b"] = _conv_bwd(s["p_xbc"], lw["conv_w"], lw["conv_b"], dact, pad, f"conv_bwd_{tag}")
    u = s["u"]
    for k, d in (("w_xbc", dxbc), ("w_z", dz), ("w_gate", dgate), ("w_pool", dpool), ("w_dt", ddt)):
        g[k], gb[k] = _mm(u, d, "tn", f"d{k}_{tag}", epi="twin")
    c_dt = cfg["cols"][3]
    parts, parts_b = ([t["w_pool"], t["w_z"], t["w_xbc"], t["w_dt"][:, :c_dt], t["w_gate"]] for t in (g, gb))
    g["w_in"] = jnp.concatenate(parts, axis=1)
    dep = traffic.grads_ready(i, {"w_in": (parts, parts_b)}, "in")
    du = _mm(dxbc, lw["w_xbc"], "nt", f"du_xbc_{tag}", dep=dep)
    du = _mm(dz, lw["w_z"], "nt", f"du_z_{tag}", res=du)
    du = _mm(dgate, lw["w_gate"], "nt", f"du_gate_{tag}", res=du)
    du = _mm(dpool, lw["w_pool"], "nt", f"du_pool_{tag}", res=du)
    du = _mm(ddt, lw["w_dt"], "nt", f"du_dt_{tag}", res=du)
    dh_in, dh_in_b, g["mix_norm_w"] = _rms_bwd(s["h"], lw["mix_norm_w"], du, dh_mid, f"rms_mix_bwd_{tag}")
    traffic.bwd_end(i, dh_in)
    return dh_in, dh_in_b, g


def _pad_lanes(v):
    return jnp.pad(v, (0, LANES - v.shape[0])).reshape(1, LANES)


class _WholeWeights:
    def __init__(self, full):
        self.full = full

    def fwd_begin(self, i):
        full = {k: t[i] for k, t in self.full.items()}
        full["w_in"] = [full["w_in"]]
        return full, None

    def fwd_end(self, i, h_out):
        pass

    def bwd_begin(self, i):
        return None

    def grads_ready(self, i, grads, tag):
        return None

    def bwd_end(self, i, dh_in):
        pass


def _local_step(x2, target, meta_full, traffic, rep, cfg):
    depth, pad, n_meta, H = cfg["depth"], cfg["pad"], cfg["n_meta"], cfg["n_heads"]
    D = x2.shape[1]
    di = rep["ssd_norm_w"].shape[1]
    c_pool, c_z, c_xbc, c_dt = cfg["cols"]
    h = jnp.concatenate([jnp.zeros((pad, D), F32), meta_full, x2], axis=0)
    lws, saves = [], []
    for i in range(depth):
        full, dep = traffic.fwd_begin(i)
        blocks = full.pop("w_in")
        o_z, o_xbc, o_dt, o_gate = c_pool, c_pool + c_z, c_pool + c_z + c_xbc, c_pool + c_z + c_xbc + c_dt
        w_dt = jnp.pad(_columns(blocks, o_dt, o_gate), ((0, 0), (0, LANES - c_dt)))
        lw = dict(
            full, w_pool=_columns(blocks, 0, o_z), w_z=_columns(blocks, o_z, o_xbc), w_xbc=_columns(blocks, o_xbc, o_dt),
            w_dt=w_dt, w_gate=_columns(blocks, o_gate, o_gate + 2 * D),
            mix_norm_w=rep["mix_norm_w"][i], b_gate=rep["b_gate"][i], pool_scale=rep["pool_scale"][i].reshape(1, -1),
            conv_b=rep["conv_b"][i].reshape(1, -1), dt_bias=_pad_lanes(rep["dt_bias"][i]), a_log=_pad_lanes(rep["a_log"][i]),
            d_skip=_pad_lanes(rep["d_skip"][i]), ssd_norm_w=rep["ssd_norm_w"][i].reshape(1, di), mlp_norm_w=rep["mlp_norm_w"][i])
        lws.append(lw)
        h, s = _layer_fwd(h, lw, cfg, f"l{i}", dep)
        saves.append(s)
        traffic.fwd_end(i, h)
    loss, dh, dh_b, g_final = _final_loss(h, rep["final_norm_w"], target, pad + n_meta, "final_loss")
    per_layer = []
    for i in range(depth - 1, -1, -1):
        dh, dh_b, g = _layer_bwd(dh, dh_b, lws[i], saves[i], cfg, f"l{i}", traffic, i)
        per_layer.append(g)
    per_layer.reverse()

    def stack(key, fn=lambda t: t):
        return jnp.stack([fn(g[key]) for g in per_layer])

    def layers(key):
        return [g[key] for g in per_layer]

    grads = dict(
        w_in=layers("w_in"), pool_w_group=layers("pool_w_group"), w_pool_up=layers("w_pool_up"), w_ssd_out=layers("w_ssd_out"), w_o=layers("w_o"),
        w_ff1=layers("w_ff1"), w_ff2=layers("w_ff2"), conv_w=stack("conv_w"),
        mix_norm_w=stack("mix_norm_w", lambda t: t[0]), b_gate=stack("b_gate", lambda t: t[0]),
        pool_scale=stack("pool_scale", lambda t: t[0]), conv_b=stack("conv_b", lambda t: t[0]),
        dt_bias=stack("dt_bias", lambda t: t[0, :H]), a_log=stack("a_log", lambda t: t[0, :H]), d_skip=stack("d_skip", lambda t: t[0, :H]),
        ssd_norm_w=stack("ssd_norm_w", lambda t: t[0]), mlp_norm_w=stack("mlp_norm_w", lambda t: t[0]),
        final_norm_w=g_final[0], meta_tokens=dh[pad:pad + n_meta])
    return loss, dh[pad + n_meta:], grads


class _ShardedWeights:
    def __init__(self, w, me):
        self.w, self.me = w, me
        self.depth = w[BIG[0].name].shape[0]
        small = [_rows2(w[s.name]) for s in SMALL_SHARDED]
        landed = _all_gather([_rows2(w[s.name][0]).astype(BF16) for s in BIG] + small, "gather_l0")
        self.small = {s.name: _unshard(t, w[s.name].shape, s.axis) for s, t in zip(SMALL_SHARDED, landed[len(BIG):])}
        self.landed = dict(zip([s.name for s in BIG], landed))
        self.fetching = None
        self.sending, self.token = [], None
        self.local_grads = [{} for _ in range(self.depth)]

    def _fetch(self, shards, i, name):
        mine = [_rows2(self.w[s.name][i]).astype(BF16) for s in shards]
        lands = [lax.dynamic_update_slice(lax.empty((N_DEV,) + t.shape, t.dtype), t[None], (self.me, 0, 0)) for t in mine]
        return _exchange_start(True, mine, lands, name)

    def _whole(self, shards):
        return {s.name: _unshard(self.landed[s.name], self.w[s.name].shape[1:], s.axis - 1) for s in shards}

    def fwd_begin(self, i):
        rows, cols = self.w["w_in"].shape[1:]
        full = {"w_in": list(self.landed["w_in"].reshape(N_DEV, rows, cols)), "conv_w": self.small["conv_w"][i]}
        full.update(self._whole(BIG[1:]))
        dep = None
        if i + 1 < self.depth:
            self.fetching = self._fetch(BIG, i + 1, f"gather_start_l{i + 1}")
            dep = self.fetching[3]
        return full, dep

    def fwd_end(self, i, h_out):
        if self.fetching is not None:
            self.landed = dict(zip([s.name for s in BIG], _exchange_wait(True, self.fetching, h_out, f"gather_wait_l{i + 1}")))
            self.fetching = None

    def bwd_begin(self, i):
        return self.token

    def grads_ready(self, i, grads, tag):
        shards = [s for s in BIG if s.name in grads]

        def blocks(s, val):
            shape = (N_DEV,) + _rows2(self.w[s.name][i]).shape
            if isinstance(val, list):
                return jnp.stack([_columns(val, d * shape[2], (d + 1) * shape[2]) for d in range(N_DEV)])
            return _reshard([val], s.axis).reshape(shape)

        def own(s, val):
            if isinstance(val, list):
                cols = self.w[s.name].shape[2]
                return lax.dynamic_slice_in_dim(jnp.concatenate(val, axis=1), self.me * cols, cols, axis=1)[None]
            return blocks(s, val)

        stacks = [own(s, grads[s.name][0]) for s in shards]
        sends = [blocks(s, grads[s.name][1]) for s in shards]
        lands = [lax.empty((N_PEERS,) + t.shape[1:], BF16) for t in sends]
        started = _exchange_start(False, sends, lands, f"rs_start_{tag}_l{i}")
        self.sending.append((i, tag, shards, stacks, started))
        self.token = started[3]
        return self.token

    def _collect(self, entry, after):
        i, tag, shards, stacks, started = entry
        got = _exchange_wait(False, started, after, f"rs_wait_{tag}_l{i}")
        for s, st, g in zip(shards, stacks, got):
            self.local_grads[i][s.name] = _own_plus_received(st, g, f"rs_sum_{s.name}_l{i}")

    def bwd_end(self, i, dh_in):
        for entry in [e for e in self.sending if e[0] > i]:
            self._collect(entry, dh_in)
        self.sending = [e for e in self.sending if e[0] <= i]

    def finish(self, after, names):
        for entry in [e for e in self.sending if e[2][0].name in names]:
            self._collect(entry, after)
        self.sending = [e for e in self.sending if e[2][0].name not in names]
        return {k: jnp.concatenate([g[k] for g in self.local_grads], axis=0) for k in names}


def kernel(x, meta_tokens, mix_norm_w, w_in, b_gate, pool_w_group, pool_scale, w_pool_up, conv_w, conv_b, dt_bias, a_log, d_skip, ssd_norm_w, w_ssd_out, w_o, mlp_norm_w, w_ff1, w_ff2, final_norm_w, loss_target, m_meta_tokens, m_mix_norm_w, m_w_in, m_b_gate, m_pool_w_group, m_pool_scale, m_w_pool_up, m_conv_w, m_conv_b, m_dt_bias, m_a_log, m_d_skip, m_ssd_norm_w, m_w_ssd_out, m_w_o, m_mlp_norm_w, m_w_ff1, m_w_ff2, m_final_norm_w, v_meta_tokens, v_mix_norm_w, v_w_in, v_b_gate, v_pool_w_group, v_pool_scale, v_w_pool_up, v_conv_w, v_conv_b, v_dt_bias, v_a_log, v_d_skip, v_ssd_norm_w, v_w_ssd_out, v_w_o, v_mlp_norm_w, v_w_ff1, v_w_ff2, v_final_norm_w):
    w = dict(meta_tokens=meta_tokens, mix_norm_w=mix_norm_w, w_in=w_in, b_gate=b_gate, pool_w_group=pool_w_group,
             pool_scale=pool_scale, w_pool_up=w_pool_up, conv_w=conv_w, conv_b=conv_b, dt_bias=dt_bias, a_log=a_log,
             d_skip=d_skip, ssd_norm_w=ssd_norm_w, w_ssd_out=w_ssd_out, w_o=w_o, mlp_norm_w=mlp_norm_w, w_ff1=w_ff1,
             w_ff2=w_ff2, final_norm_w=final_norm_w)
    m = dict(meta_tokens=m_meta_tokens, mix_norm_w=m_mix_norm_w, w_in=m_w_in, b_gate=m_b_gate, pool_w_group=m_pool_w_group,
             pool_scale=m_pool_scale, w_pool_up=m_w_pool_up, conv_w=m_conv_w, conv_b=m_conv_b, dt_bias=m_dt_bias, a_log=m_a_log,
             d_skip=m_d_skip, ssd_norm_w=m_ssd_norm_w, w_ssd_out=m_w_ssd_out, w_o=m_w_o, mlp_norm_w=m_mlp_norm_w, w_ff1=m_w_ff1,
             w_ff2=m_w_ff2, final_norm_w=m_final_norm_w)
    v = dict(meta_tokens=v_meta_tokens, mix_norm_w=v_mix_norm_w, w_in=v_w_in, b_gate=v_b_gate, pool_w_group=v_pool_w_group,
             pool_scale=v_pool_scale, w_pool_up=v_w_pool_up, conv_w=v_conv_w, conv_b=v_conv_b, dt_bias=v_dt_bias, a_log=v_a_log,
             d_skip=v_d_skip, ssd_norm_w=v_ssd_norm_w, w_ssd_out=v_w_ssd_out, w_o=v_w_o, mlp_norm_w=v_mlp_norm_w, w_ff1=v_w_ff1,
             w_ff2=v_w_ff2, final_norm_w=v_final_norm_w)

    _, seq, D = x.shape
    n_meta = meta_tokens.shape[0]
    depth = w_in.shape[0]
    n_heads = dt_bias.shape[1]
    d_inner = ssd_norm_w.shape[1]
    d_xbc = conv_b.shape[1]
    pool_width = pool_scale.shape[1]
    pad = (-n_meta) % CHUNK
    cfg = dict(depth=depth, pad=pad, n_meta=n_meta, n_heads=n_heads, cols=(pool_width, d_inner, d_xbc, n_heads))
    assert (pad + n_meta + seq) % CHUNK == 0 and pad + n_meta == CHUNK

    xi, yi, ci = _place()
    me = 4 * xi + 2 * yi + ci

    traffic = _ShardedWeights(w, me)
    rep = {k: w[k] for k in REPLICATED}
    loss_part, dx, grads = _local_step(x[0], loss_target[0], traffic.small["meta_tokens"], traffic, rep, cfg)
    loss = lax.psum(loss_part[0, 0], ("x", "y", "c"))

    small_names = REPLICATED + [s.name for s in SMALL_SHARDED]
    sm_buf, sm_spans = _pack_small([grads[k] for k in small_names], 16)
    (sm_all,) = _all_gather([sm_buf], "gather_small_grads")
    sm_sum = _sum_stack([sm_all[d] for d in range(N_DEV)], "small_grads_sum")
    g_small = dict(zip(small_names, _unpack_small(sm_sum, sm_spans, [grads[k].shape for k in small_names])))
    g_loc = {k: g_small[k] for k in REPLICATED}
    for s in SMALL_SHARDED:
        blk = w[s.name].shape[s.axis]
        g_loc[s.name] = lax.dynamic_slice_in_dim(g_small[s.name], me * blk, blk, axis=s.axis)

    delta, new_m, new_v = {}, {}, {}
    loc_shapes = [w[k].shape for k in small_names]
    packed = [_pack_small([t[k] for k in small_names], 8) for t in (w, g_loc, m, v)]
    loc_spans = packed[0][1]
    outs = _adamw(*[p[0][None] for p in packed], "adamw_small")
    for res, buf in zip((delta, new_m, new_v), outs):
        res.update(zip(small_names, _unpack_small(buf[0], loc_spans, loc_shapes)))
    after = outs[0]
    for names in ([s.name for s in BIG if s.name != "w_in"], ["w_in"]):
        for k, t in traffic.finish(after, names).items():
            shp = w[k].shape
            by_layer = (shp[0], -1, shp[-1])
            d3, m3, v3 = _adamw(w[k].reshape(by_layer), t.reshape(by_layer), m[k].reshape(by_layer), v[k].reshape(by_layer), f"adamw_{k}")
            g_loc[k], delta[k], new_m[k], new_v[k] = t.reshape(shp), d3.reshape(shp), m3.reshape(shp), v3.reshape(shp)
            after = d3

    return (loss, dx[None], *[g_loc[k] for k in WEIGHTS], *[delta[k] for k in WEIGHTS],
            *[new_m[k] for k in WEIGHTS], *[new_v[k] for k in WEIGHTS])
```

```python
import functools

import jax
import jax.numpy as jnp
from jax import lax
from jax.experimental import pallas as pl
from jax.experimental.pallas import tpu as pltpu

F32 = jnp.float32
BF16 = jnp.bfloat16

EPS = 1e-5
D_STATE = 128
CHUNK = 128
LANES = 128
POOL_WINDOWS = (2, 4, 8, 16)
POOL_HALO = 16
CONV_WIDTH = 4
CONV_HALO = 8
N_DEV = 8
ADAM_LR = 0.001
ADAM_B1 = 0.9
ADAM_B2 = 0.999
ADAM_EPS = 1e-08
ADAM_WD = 0.01
ADAM_STEP = 10
VMEM_LIMIT = 52 * 1024 * 1024
MM_VMEM_BUDGET = 36 * 1024 * 1024
MESH = pl.DeviceIdType.MESH
ANY = pl.BlockSpec(memory_space=pl.ANY)


def _cp(*sem):
    return pltpu.CompilerParams(dimension_semantics=sem, vmem_limit_bytes=VMEM_LIMIT)


def _tile(n, target, mult):
    best = None
    for t in range(mult, min(n, target) + 1, mult):
        if n % t == 0:
            best = t
    return best if best is not None else n


def _sigmoid(x):
    return jax.nn.sigmoid(x)


def _iota(shape, dim):
    return lax.broadcasted_iota(jnp.int32, shape, dim)


_DIMS = {"nn": (((1,), (0,)), ((), ())), "nt": (((1,), (1,)), ((), ())), "tn": (((0,), (0,)), ((), ()))}


def _dot(a, b, mode="nn"):
    return lax.dot_general(a.astype(BF16), b.astype(BF16), _DIMS[mode], preferred_element_type=F32)


DEP = pl.BlockSpec((8, LANES), lambda *_: (0, 0))


def _mm(a, b, mode, name, *, out_dtype=F32, res=None, epi=None, aux=None, dep=None):
    if mode == "nn":
        (M, K), (_, N) = a.shape, b.shape
    elif mode == "nt":
        (M, K), (N, _) = a.shape, b.shape
    else:
        (K, M), (_, N) = a.shape, b.shape
    if mode == "tn":
        tm, tk = _tile(M, 512, 128), K
        for tn in (_tile(N, 1024, 128), _tile(N, 512, 128)):
            if 2 * (tk * (tm * a.dtype.itemsize + tn * b.dtype.itemsize) + tm * tn * 6) <= MM_VMEM_BUDGET:
                break
    else:
        tm, tk = _tile(M, 1056, 16), _tile(K, 4096, 128)
        out_bytes = {None: jnp.dtype(out_dtype).itemsize, "relu2": 6, "drelu2": 2, "twin": 6}[epi]
        extra_bytes = 4 * sum(t is not None for t in (res, aux))
        for tn in (_tile(N, 1024, 128), _tile(N, 512, 128)):
            need = 2 * (tm * tk * a.dtype.itemsize + tk * tn * b.dtype.itemsize + tm * tn * (out_bytes + extra_bytes))
            if need <= MM_VMEM_BUDGET:
                break
    nk = K // tk
    a_spec = pl.BlockSpec((tk, tm), lambda i, j, k: (k, i)) if mode == "tn" else pl.BlockSpec((tm, tk), lambda i, j, k: (i, k))
    b_spec = pl.BlockSpec((tn, tk), lambda i, j, k: (j, k)) if mode == "nt" else pl.BlockSpec((tk, tn), lambda i, j, k: (k, j))
    o_spec = pl.BlockSpec((tm, tn), lambda i, j, k: (i, j))
    extra = [t for t in (res, aux) if t is not None]
    deps = [] if dep is None else [dep]

    def body(*refs):
        a_ref, b_ref = refs[0], refs[1]
        x_ref = refs[2] if extra else None
        outs = refs[2 + len(extra) + len(deps):]
        p = _dot(a_ref[...], b_ref[...], mode)

        def finish(r):
            if res is not None:
                outs[0][...] = (x_ref[...] + r).astype(out_dtype)
            elif epi == "relu2":
                outs[0][...] = r
                hid = jnp.maximum(r, 0.0)
                outs[1][...] = (hid * hid).astype(BF16)
            elif epi == "twin":
                outs[0][...] = r
                outs[1][...] = r.astype(BF16)
            elif epi == "drelu2":
                outs[0][...] = (r * (2.0 * jnp.maximum(x_ref[...], 0.0))).astype(BF16)
            else:
                outs[0][...] = r.astype(out_dtype)

        if nk == 1:
            finish(p)
        else:
            acc = outs[-1]
            k = pl.program_id(2)

            @pl.when(k == 0)
            def _():
                acc[...] = p

            @pl.when(k > 0)
            def _():
                acc[...] += p

            @pl.when(k == nk - 1)
            def _():
                finish(acc[...])

    if epi in ("relu2", "twin"):
        out_shape = (jax.ShapeDtypeStruct((M, N), F32), jax.ShapeDtypeStruct((M, N), BF16))
        out_specs = (o_spec, o_spec)
    elif epi == "drelu2":
        out_shape, out_specs = jax.ShapeDtypeStruct((M, N), BF16), o_spec
    else:
        out_shape, out_specs = jax.ShapeDtypeStruct((M, N), out_dtype), o_spec
    return pl.pallas_call(
        body, name=name, grid=(M // tm, N // tn, nk),
        in_specs=[a_spec, b_spec] + [o_spec] * len(extra) + [DEP] * len(deps),
        out_specs=out_specs, out_shape=out_shape,
        scratch_shapes=[pltpu.VMEM((tm, tn), F32)] if nk > 1 else [],
        compiler_params=_cp("parallel", "parallel", "arbitrary"),
    )(a, b, *extra, *deps)


def _rms_fwd(h, w, name, dep=None):
    T, D = h.shape
    tr = _tile(T, 1056, 16)
    deps = [] if dep is None else [dep]

    def body(h_ref, w_ref, *rest):
        x = h_ref[...]
        xn = x * lax.rsqrt(jnp.mean(x * x, axis=-1, keepdims=True) + EPS)
        rest[-1][...] = (xn * w_ref[...]).astype(BF16)

    return pl.pallas_call(
        body, name=name, grid=(T // tr,),
        in_specs=[pl.BlockSpec((tr, D), lambda i: (i, 0)), pl.BlockSpec((1, D), lambda i: (0, 0))] + [DEP] * len(deps),
        out_specs=pl.BlockSpec((tr, D), lambda i: (i, 0)), out_shape=jax.ShapeDtypeStruct((T, D), BF16),
        compiler_params=_cp("parallel"),
    )(h, w.reshape(1, D), *deps)


def _rms_bwd(h, w, dy, dres, name):
    T, D = h.shape
    tr = _tile(T, 528, 16)

    def body(h_ref, w_ref, dy_ref, dres_ref, dh_ref, dhb_ref, dw_ref):
        x = h_ref[...]
        rstd = lax.rsqrt(jnp.mean(x * x, axis=-1, keepdims=True) + EPS)
        xn = x * rstd
        dy = dy_ref[...]
        dxn = dy * w_ref[...]
        dh = dres_ref[...] + rstd * (dxn - xn * jnp.mean(dxn * xn, axis=-1, keepdims=True))
        dh_ref[...] = dh
        dhb_ref[...] = dh.astype(BF16)
        dw = jnp.sum(dy * xn, axis=0, keepdims=True)

        @pl.when(pl.program_id(0) == 0)
        def _():
            dw_ref[...] = dw

        @pl.when(pl.program_id(0) > 0)
        def _():
            dw_ref[...] += dw

    row = pl.BlockSpec((tr, D), lambda i: (i, 0))
    vec = pl.BlockSpec((1, D), lambda i: (0, 0))
    return pl.pallas_call(
        body, name=name, grid=(T // tr,),
        in_specs=[row, vec, row, row], out_specs=(row, row, vec),
        out_shape=(jax.ShapeDtypeStruct((T, D), F32), jax.ShapeDtypeStruct((T, D), BF16), jax.ShapeDtypeStruct((1, D), F32)),
        compiler_params=_cp("arbitrary"),
    )(h, w.reshape(1, D), dy, dres)


def _final_loss(h, w, target, first_row, name):
    T, D = h.shape
    tr = CHUNK
    assert first_row == tr

    def body(h_ref, w_ref, t_ref, loss_ref, dh_ref, dhb_ref, dw_ref):
        i = pl.program_id(0)
        x = h_ref[...]
        rstd = lax.rsqrt(jnp.mean(x * x, axis=-1, keepdims=True) + EPS)
        xn = x * rstd
        w = w_ref[...]
        live = i > 0
        err = jnp.where(live, xn * w - t_ref[...], 0.0)
        part = 0.5 * jnp.sum(jnp.mean(err * err, axis=-1, keepdims=True), axis=0, keepdims=True)
        dout = err * (1.0 / D)
        dxn = dout * w
        dh = rstd * (dxn - xn * jnp.mean(dxn * xn, axis=-1, keepdims=True))
        dh_ref[...] = dh
        dhb_ref[...] = dh.astype(BF16)
        dw = jnp.sum(dout * xn, axis=0, keepdims=True)

        @pl.when(i == 0)
        def _():
            loss_ref[...] = part
            dw_ref[...] = dw

        @pl.when(i > 0)
        def _():
            loss_ref[...] += part
            dw_ref[...] += dw

    row = pl.BlockSpec((tr, D), lambda i: (i, 0))
    vec = pl.BlockSpec((1, D), lambda i: (0, 0))
    return pl.pallas_call(
        body, name=name, grid=(T // tr,),
        in_specs=[row, vec, pl.BlockSpec((tr, D), lambda i: (jnp.maximum(i - 1, 0), 0))],
        out_specs=(pl.BlockSpec((1, 1), lambda i: (0, 0)), row, row, vec),
        out_shape=(jax.ShapeDtypeStruct((1, 1), F32), jax.ShapeDtypeStruct((T, D), F32), jax.ShapeDtypeStruct((T, D), BF16),
                   jax.ShapeDtypeStruct((1, D), F32)),
        compiler_params=_cp("arbitrary"),
    )(h, w.reshape(1, D), target)


def _gate_fwd(pg, bg, y_pool, y_ssd, name):
    T, D = y_pool.shape
    tr = _tile(T, 528, 16)

    def body(gp_ref, gs_ref, bp_ref, bs_ref, yp_ref, ys_ref, o_ref):
        gp = _sigmoid(gp_ref[...] + bp_ref[...])
        gs = _sigmoid(gs_ref[...] + bs_ref[...])
        o_ref[...] = (gp * yp_ref[...] + gs * ys_ref[...]).astype(BF16)

    row = pl.BlockSpec((tr, D), lambda i: (i, 0))
    row1 = pl.BlockSpec((tr, D), lambda i: (i, 1))
    vec = pl.BlockSpec((1, D), lambda i: (0, 0))
    vec1 = pl.BlockSpec((1, D), lambda i: (0, 1))
    b2 = bg.reshape(1, 2 * D)
    return pl.pallas_call(
        body, name=name, grid=(T // tr,),
        in_specs=[row, row1, vec, vec1, row, row], out_specs=row,
        out_shape=jax.ShapeDtypeStruct((T, D), BF16), compiler_params=_cp("parallel"),
    )(pg, pg, b2, b2, y_pool, y_ssd)


def _gate_bwd(pg, bg, y_pool, y_ssd, dmix, name):
    T, D = y_pool.shape
    tr = _tile(T, 528, 16)

    def body(gp_ref, gs_ref, bp_ref, bs_ref, yp_ref, ys_ref, dm_ref, dg_ref, dyp_ref, dys_ref, db_ref):
        gp = _sigmoid(gp_ref[...] + bp_ref[...])
        gs = _sigmoid(gs_ref[...] + bs_ref[...])
        dm = dm_ref[...]
        dyp_ref[...] = (dm * gp).astype(BF16)
        dys_ref[...] = (dm * gs).astype(BF16)
        dgp = dm * yp_ref[...] * gp * (1.0 - gp)
        dgs = dm * ys_ref[...] * gs * (1.0 - gs)
        dg_ref[:, :D] = dgp.astype(BF16)
        dg_ref[:, D:] = dgs.astype(BF16)
        db = jnp.concatenate([jnp.sum(dgp, axis=0, keepdims=True), jnp.sum(dgs, axis=0, keepdims=True)], axis=1)

        @pl.when(pl.program_id(0) == 0)
        def _():
            db_ref[...] = db

        @pl.when(pl.program_id(0) > 0)
        def _():
            db_ref[...] += db

    row = pl.BlockSpec((tr, D), lambda i: (i, 0))
    row1 = pl.BlockSpec((tr, D), lambda i: (i, 1))
    wide = pl.BlockSpec((tr, 2 * D), lambda i: (i, 0))
    vec = pl.BlockSpec((1, D), lambda i: (0, 0))
    vec1 = pl.BlockSpec((1, D), lambda i: (0, 1))
    vec2 = pl.BlockSpec((1, 2 * D), lambda i: (0, 0))
    b2 = bg.reshape(1, 2 * D)
    return pl.pallas_call(
        body, name=name, grid=(T // tr,),
        in_specs=[row, row1, vec, vec1, row, row, row], out_specs=(wide, row, row, vec2),
        out_shape=(jax.ShapeDtypeStruct((T, 2 * D), BF16), jax.ShapeDtypeStruct((T, D), BF16),
                   jax.ShapeDtypeStruct((T, D), BF16), jax.ShapeDtypeStruct((1, 2 * D), F32)),
        compiler_params=_cp("arbitrary"),
    )(pg, pg, b2, b2, y_pool, y_ssd, dmix)


def _pool_count(c, pad, window):
    pos = c * CHUNK + _iota((CHUNK, 1), 0) - pad
    return jnp.clip(pos + 1, 1, window).astype(F32)


def _by_group(g, vals):
    out = vals[-1]
    for k in range(len(vals) - 2, -1, -1):
        out = jnp.where(g == k, vals[k], out)
    return out


def _by_row_block(rows, vals, block):
    out = vals[0]
    for r in range(1, len(vals)):
        out = jnp.where(rows >= r * block, vals[r], out)
    return out


def _pool_fwd(u, wg, scale, pad, name):
    T, C = u.shape
    G, Cg, _ = wg.shape
    nc = T // CHUNK

    def body(u_ref, wg_ref, s_ref, p_ref, y_ref):
        g = pl.program_id(0)
        window = _by_group(g, POOL_WINDOWS)

        def chunk(c, carry):
            r0 = pl.multiple_of(c * CHUNK, CHUNK)
            h0 = pl.multiple_of(jnp.maximum(r0 - POOL_HALO, 0), 8)
            halo = jnp.where(c > 0, u_ref[pl.ds(h0, POOL_HALO), :], 0.0)
            xc = u_ref[pl.ds(r0, CHUNK), :]
            s = jnp.concatenate([halo, xc], axis=0)
            sums = []
            k = 1
            while k < POOL_WINDOWS[-1]:
                s = s + pltpu.roll(s, k, 0)
                k *= 2
                if k in POOL_WINDOWS:
                    sums.append(s[POOL_HALO:])
            wsum = _by_group(g, sums)
            pooled = wsum / _pool_count(c, pad, window) - xc
            pb = pooled.astype(BF16)
            p_ref[pl.ds(r0, CHUNK), :] = pb
            y_ref[pl.ds(r0, CHUNK), :] = (_dot(pb, wg_ref[0]) * s_ref[...]).astype(BF16)
            return carry

        lax.fori_loop(0, nc, chunk, 0)

    col = pl.BlockSpec((T, Cg), lambda g: (0, g))
    return pl.pallas_call(
        body, name=name, grid=(G,),
        in_specs=[col, pl.BlockSpec((1, Cg, Cg), lambda g: (g, 0, 0)), pl.BlockSpec((1, Cg), lambda g: (0, g))],
        out_specs=(col, col),
        out_shape=(jax.ShapeDtypeStruct((T, C), BF16), jax.ShapeDtypeStruct((T, C), BF16)),
        compiler_params=_cp("parallel"),
    )(u, wg, scale)


def _pool_bwd(pooled, wg, scale, dy, pad, name):
    T, C = dy.shape
    G, Cg, _ = wg.shape
    nc = T // CHUNK

    def body(p_ref, wg_ref, s_ref, dy_ref, du_ref, dwg_ref, ds_ref, halo_ref):
        g = pl.program_id(0)
        window = _by_group(g, POOL_WINDOWS)
        halo_ref[...] = jnp.zeros_like(halo_ref)
        dwg_ref[...] = jnp.zeros_like(dwg_ref)
        ds_ref[...] = jnp.zeros_like(ds_ref)

        def chunk(i, carry):
            c = nc - 1 - i
            r0 = pl.multiple_of(c * CHUNK, CHUNK)
            pb = p_ref[pl.ds(r0, CHUNK), :]
            dyc = dy_ref[pl.ds(r0, CHUNK), :]
            w = wg_ref[0]
            ypre = _dot(pb, w)
            ds_ref[...] += jnp.sum(dyc * ypre, axis=0, keepdims=True)
            dyp = (dyc * s_ref[...]).astype(BF16)
            dwg_ref[0] += _dot(pb, dyp, "tn")
            dpool = _dot(dyp, w, "nt")
            q = dpool / _pool_count(c, pad, window)
            s = jnp.concatenate([q, halo_ref[...]], axis=0)
            n = CHUNK + POOL_HALO
            sums = []
            k = 1
            while k < POOL_WINDOWS[-1]:
                s = s + pltpu.roll(s, n - k, 0)
                k *= 2
                if k in POOL_WINDOWS:
                    sums.append(s[:CHUNK])
            du = _by_group(g, sums) - dpool
            rows = r0 + _iota((CHUNK, 1), 0)
            du_ref[pl.ds(r0, CHUNK), :] = jnp.where(rows >= pad, du, 0.0).astype(BF16)
            halo_ref[...] = q[:POOL_HALO]
            return carry

        lax.fori_loop(0, nc, chunk, 0)

    col = pl.BlockSpec((T, Cg), lambda g: (0, g))
    return pl.pallas_call(
        body, name=name, grid=(G,),
        in_specs=[col, pl.BlockSpec((1, Cg, Cg), lambda g: (g, 0, 0)), pl.BlockSpec((1, Cg), lambda g: (0, g)), col],
        out_specs=(col, pl.BlockSpec((1, Cg, Cg), lambda g: (g, 0, 0)), pl.BlockSpec((1, Cg), lambda g: (0, g))),
        out_shape=(jax.ShapeDtypeStruct((T, C), BF16), jax.ShapeDtypeStruct((G, Cg, Cg), F32), jax.ShapeDtypeStruct((1, C), F32)),
        scratch_shapes=[pltpu.VMEM((POOL_HALO, Cg), F32)],
        compiler_params=_cp("parallel"),
    )(pooled, wg, scale, dy)


def _conv_pre(x_ref, w_ref, b_ref, c, r0):
    h0 = pl.multiple_of(jnp.maximum(r0 - CONV_HALO, 0), 8)
    halo = jnp.where(c > 0, x_ref[pl.ds(h0, CONV_HALO), :], 0.0)
    xe = jnp.concatenate([halo, x_ref[pl.ds(r0, CHUNK), :]], axis=0)
    y = jnp.broadcast_to(b_ref[...], (CHUNK, xe.shape[1]))
    for k in range(CONV_WIDTH):
        shift = CONV_WIDTH - 1 - k
        xs = xe if shift == 0 else pltpu.roll(xe, shift, 0)
        y = y + xs[CONV_HALO:] * w_ref[k:k + 1, :]
    return y, xe


def _conv_fwd(x, w, b, name):
    T = x.shape[0]
    C = w.shape[1]
    tc = _tile(C, 256, 128)
    nc = T // CHUNK

    def body(x_ref, w_ref, b_ref, o_ref):
        def chunk(c, carry):
            r0 = pl.multiple_of(c * CHUNK, CHUNK)
            y, _ = _conv_pre(x_ref, w_ref, b_ref, c, r0)
            o_ref[pl.ds(r0, CHUNK), :] = y * _sigmoid(y)
            return carry

        lax.fori_loop(0, nc, chunk, 0)

    col = pl.BlockSpec((T, tc), lambda j: (0, j))
    return pl.pallas_call(
        body, name=name, grid=(C // tc,),
        in_specs=[col, pl.BlockSpec((CONV_WIDTH, tc), lambda j: (0, j)), pl.BlockSpec((1, tc), lambda j: (0, j))],
        out_specs=col, out_shape=jax.ShapeDtypeStruct((T, C), F32), compiler_params=_cp("parallel"),
    )(x, w, b)


def _conv_bwd(x, w, b, dact, pad, name):
    T = x.shape[0]
    C = w.shape[1]
    tc = _tile(C, 256, 128)
    nc = T // CHUNK

    def body(x_ref, w_ref, b_ref, da_ref, dx_ref, dw_ref, db_ref, halo_ref):
        halo_ref[...] = jnp.zeros_like(halo_ref)
        dw_ref[...] = jnp.zeros_like(dw_ref)
        db_ref[...] = jnp.zeros_like(db_ref)

        def chunk(i, carry):
            c = nc - 1 - i
            r0 = pl.multiple_of(c * CHUNK, CHUNK)
            y, xe = _conv_pre(x_ref, w_ref, b_ref, c, r0)
            sg = _sigmoid(y)
            dpre = da_ref[pl.ds(r0, CHUNK), :] * (sg * (1.0 + y * (1.0 - sg)))
            db_ref[...] += jnp.sum(dpre, axis=0, keepdims=True)
            de = jnp.concatenate([dpre, halo_ref[...]], axis=0)
            n = CHUNK + CONV_HALO
            dx = jnp.zeros_like(dpre)
            for k in range(CONV_WIDTH):
                shift = CONV_WIDTH - 1 - k
                xs = xe if shift == 0 else pltpu.roll(xe, shift, 0)
                dw_ref[k:k + 1, :] += jnp.sum(dpre * xs[CONV_HALO:], axis=0, keepdims=True)
                ds = de if shift == 0 else pltpu.roll(de, n - shift, 0)
                dx = dx + ds[:CHUNK] * w_ref[k:k + 1, :]
            rows = r0 + _iota((CHUNK, 1), 0)
            dx_ref[pl.ds(r0, CHUNK), :] = jnp.where(rows >= pad, dx, 0.0).astype(BF16)
            halo_ref[...] = dpre[:CONV_HALO]
            return carry

        lax.fori_loop(0, nc, chunk, 0)

    col = pl.BlockSpec((T, tc), lambda j: (0, j))
    wspec = pl.BlockSpec((CONV_WIDTH, tc), lambda j: (0, j))
    bspec = pl.BlockSpec((1, tc), lambda j: (0, j))
    return pl.pallas_call(
        body, name=name, grid=(C // tc,),
        in_specs=[col, wspec, bspec, col], out_specs=(col, wspec, bspec),
        out_shape=(jax.ShapeDtypeStruct((T, C), BF16), jax.ShapeDtypeStruct((CONV_WIDTH, C), F32), jax.ShapeDtypeStruct((1, C), F32)),
        scratch_shapes=[pltpu.VMEM((CONV_HALO, tc), F32)],
        compiler_params=_cp("parallel"),
    )(x, w, b, dact)


def _cumsum_rows(x, reverse=False):
    n = x.shape[0]
    idx = _iota(x.shape, 0)
    k = 1
    while k < n:
        if reverse:
            x = x + jnp.where(idx < n - k, pltpu.roll(x, n - k, 0), 0.0)
        else:
            x = x + jnp.where(idx >= k, pltpu.roll(x, k, 0), 0.0)
        k *= 2
    return x


def _softplus(x):
    return jnp.maximum(x, 0.0) + jnp.log1p(jnp.exp(-jnp.abs(x)))


def _head_selector(n_heads, width):
    lane = jnp.arange(n_heads * width)[None, :] // width
    return (lane == jnp.arange(LANES)[:, None]).astype(BF16)


def _dot_exact(x, sel, parts, mode="nn"):
    acc = None
    for _ in range(parts):
        piece = x.astype(BF16)
        x = x - piece.astype(F32)
        t = lax.dot_general(piece, sel, _DIMS[mode], preferred_element_type=F32)
        acc = t if acc is None else acc + t
    return acc


def _ssd_decays(dt, cs, sel_p_ref, sel_q_ref):
    cs_b = _dot_exact(cs, sel_q_ref[...], 3)
    dt_x = _dot_exact(dt, sel_p_ref[...], 3)
    cs_x = _dot_exact(cs, sel_p_ref[...], 3)
    return cs_b, dt_x, jnp.exp(cs_x), jnp.exp(cs_x[CHUNK - 1:CHUNK, :] - cs_x)


def _ssd_common(c, pad, n_heads, dtr_ref, dtb_ref, al_ref):
    rows = c * CHUNK + _iota((CHUNK, 1), 0)
    valid = rows >= pad
    live = jnp.logical_and(valid, _iota((1, LANES), 1) < n_heads)
    pre = dtr_ref[...] + dtb_ref[...]
    dt = jnp.where(live, _softplus(pre), 0.0)
    a = -jnp.exp(al_ref[...])
    cs = _cumsum_rows(dt * a)
    return valid, live, dt, a, cs, cs.T, _sigmoid(pre)


def _ssd_specs(T, DI, GN, cfirst):
    xcol = DI // GN

    def at(col):
        return lambda c: (cfirst(c), col)

    x = pl.BlockSpec((CHUNK, DI), at(0))
    b = pl.BlockSpec((CHUNK, GN), at(xcol))
    cm = pl.BlockSpec((CHUNK, GN), at(xcol + 1))
    dt = pl.BlockSpec((CHUNK, LANES), at(0))
    vec = pl.BlockSpec((1, LANES), lambda c: (0, 0))
    nw = pl.BlockSpec((1, DI), lambda c: (0, 0))
    return x, b, cm, dt, vec, nw


def _ssd_fwd(xbc, pdt, pz, dt_bias, a_log, d_skip, norm_w, pad, n_heads, name):
    T = xbc.shape[0]
    DI = pz.shape[1]
    P = DI // n_heads
    GN = (xbc.shape[1] - DI) // 2
    G = GN // D_STATE
    R = n_heads // G
    GW = R * P
    nc = T // CHUNK
    Q, N = CHUNK, D_STATE

    def body(x_ref, b_ref, c_ref, dtr_ref, z_ref, dtb_ref, al_ref, dsk_ref, nw_ref, sel_p_ref, sel_q_ref,
             y_ref, yn_ref, prev_ref, s_ref):
        c = pl.program_id(0)

        @pl.when(c == 0)
        def _():
            s_ref[...] = jnp.zeros_like(s_ref)

        valid, _, dt, _, cs, cst, _ = _ssd_common(c, pad, n_heads, dtr_ref, dtb_ref, al_ref)
        cs_b, dt_x, e_x, dec_x = _ssd_decays(dt, cs, sel_p_ref, sel_q_ref)
        e_last = jnp.exp(cs[Q - 1:Q, :])
        tri = _iota((Q, Q), 0) >= _iota((Q, Q), 1)
        state_rows = _iota((GW, 1), 0)
        head_lane = _iota((1, GW), 1)
        for g in range(G):
            gs = slice(g * GW, (g + 1) * GW)
            bg = jnp.where(valid, b_ref[:, g * N:(g + 1) * N], 0.0).astype(BF16)
            cg = jnp.where(valid, c_ref[:, g * N:(g + 1) * N], 0.0).astype(BF16)
            xg = jnp.where(valid, x_ref[:, gs], 0.0)
            sg = s_ref[gs, :]
            prev_ref[0, gs, :] = sg
            cb = _dot(cg, bg, "nt")
            xdt = xg * dt_x[:, gs]
            yg = _dot(cg, sg, "nt") * e_x[:, gs]
            for r in range(R):
                h = g * R + r
                lmat = jnp.exp(jnp.where(tri, cs_b[:, h * Q:(h + 1) * Q] - cst[h:h + 1, :], -jnp.inf))
                in_head = jnp.logical_and(head_lane >= r * P, head_lane < (r + 1) * P)
                yg = yg + _dot(cb * lmat, jnp.where(in_head, xdt, 0.0))
            y_ref[:, gs] = yg
            decay = _by_row_block(state_rows, [e_last[:, g * R + r:g * R + r + 1] for r in range(R)], P)
            s_ref[gs, :] = sg * decay + _dot(xdt * dec_x[:, gs], bg, "tn")
            z = z_ref[:, gs]
            gz = (yg + xg * dsk_ref[:, gs]) * (z * _sigmoid(z))
            rstd = lax.rsqrt(jnp.mean(gz * gz, axis=-1, keepdims=True) + EPS)
            yn_ref[:, gs] = ((gz * rstd) * nw_ref[:, gs]).astype(BF16)

    x_s, b_s, c_s, dt_s, vec, nw = _ssd_specs(T, DI, GN, lambda c: c)
    wide = pl.BlockSpec((Q, DI), lambda c: (c, 0))
    sel_p = pl.BlockSpec((LANES, DI), lambda c: (0, 0))
    sel_q = pl.BlockSpec((LANES, n_heads * Q), lambda c: (0, 0))
    return pl.pallas_call(
        body, name=name, grid=(nc,),
        in_specs=[x_s, b_s, c_s, dt_s, wide, vec, vec, nw, nw, sel_p, sel_q],
        out_specs=(wide, wide, pl.BlockSpec((1, DI, N), lambda c: (c, 0, 0))),
        out_shape=(jax.ShapeDtypeStruct((T, DI), F32), jax.ShapeDtypeStruct((T, DI), BF16), jax.ShapeDtypeStruct((nc, DI, N), F32)),
        scratch_shapes=[pltpu.VMEM((DI, N), F32)],
        compiler_params=_cp("arbitrary"),
    )(xbc, xbc, xbc, pdt, pz, dt_bias, a_log, jnp.repeat(d_skip[:, :n_heads], P, axis=1), norm_w,
      _head_selector(n_heads, P), _head_selector(n_heads, Q))


def _ssd_bwd(xbc, pdt, pz, dt_bias, a_log, d_skip, norm_w, y, prev, dyn, pad, n_heads, name):
    T, W = xbc.shape
    DI = pz.shape[1]
    P = DI // n_heads
    GN = (W - DI) // 2
    G = GN // D_STATE
    R = n_heads // G
    GW = R * P
    nc = T // CHUNK
    Q, N = CHUNK, D_STATE

    def body(x_ref, b_ref, c_ref, dtr_ref, z_ref, dtb_ref, al_ref, dsk_ref, nw_ref, y_ref, prev_ref, next_ref, dyn_ref,
             sel_p_ref, sel_q_ref, sel_pt_ref,
             dxbc_ref, dz_ref, ddt_ref, ddtb_ref, dal_ref, ddsk_ref, dnw_ref, ds_ref):
        i = pl.program_id(0)
        c = nc - 1 - i

        @pl.when(i == 0)
        def _():
            ds_ref[...] = jnp.zeros_like(ds_ref)
            ddtb_ref[...] = jnp.zeros_like(ddtb_ref)
            dal_ref[...] = jnp.zeros_like(dal_ref)
            ddsk_ref[...] = jnp.zeros_like(ddsk_ref)
            dnw_ref[...] = jnp.zeros_like(dnw_ref)

        valid, live, dt, a, cs, cst, sig_pre = _ssd_common(c, pad, n_heads, dtr_ref, dtb_ref, al_ref)
        cs_b, dt_x, e_x, dec_x = _ssd_decays(dt, cs, sel_p_ref, sel_q_ref)
        e_last = jnp.exp(cs[Q - 1:Q, :])
        tri = _iota((Q, Q), 0) >= _iota((Q, Q), 1)
        tri_t = _iota((Q, Q), 0) <= _iota((Q, Q), 1)
        state_rows = _iota((GW, 1), 0)
        head_lane = _iota((1, GW), 1)
        lane = _iota((1, LANES), 1)
        head_rows = _iota((LANES, 1), 0)
        s_dy_cs = jnp.zeros((Q, LANES), F32)
        s_x_bds = jnp.zeros((Q, LANES), F32)
        s_x_dxdt = jnp.zeros((Q, LANES), F32)
        dcs_rows = jnp.zeros((Q, LANES), F32)
        dcs_cols = jnp.zeros((LANES, Q), F32)
        c_end = jnp.zeros((1, LANES), F32)
        dsk_rows = []
        for g in range(G):
            gs = slice(g * GW, (g + 1) * GW)
            bg = jnp.where(valid, b_ref[:, g * N:(g + 1) * N], 0.0).astype(BF16)
            cg = jnp.where(valid, c_ref[:, g * N:(g + 1) * N], 0.0).astype(BF16)
            xg = jnp.where(valid, x_ref[:, gs], 0.0)
            s_prev = prev_ref[0, gs, :]
            dsg = ds_ref[gs, :]
            end = dsg * next_ref[0, gs, :]
            yg = y_ref[:, gs]
            dsk = dsk_ref[:, gs]
            ytot = yg + xg * dsk
            z = z_ref[:, gs]
            sz = _sigmoid(z)
            silu = z * sz
            gz = ytot * silu
            rstd = lax.rsqrt(jnp.mean(gz * gz, axis=-1, keepdims=True) + EPS)
            gn = gz * rstd
            dyn_g = dyn_ref[:, gs]
            dnw_ref[:, gs] += jnp.sum(dyn_g * gn, axis=0, keepdims=True)
            dgn = dyn_g * nw_ref[:, gs]
            dgz = rstd * (dgn - gn * jnp.mean(dgn * gn, axis=-1, keepdims=True))
            dz_ref[:, gs] = (dgz * ytot * (sz * (1.0 + z * (1.0 - sz)))).astype(BF16)
            dy = dgz * silu
            dsk_rows.append(jnp.sum(dy * xg, axis=0, keepdims=True))
            cb = _dot(cg, bg, "nt")
            cb_t = _dot(bg, cg, "nt")
            bds = _dot(bg, dsg, "nt") * dec_x[:, gs]
            csg = _dot(cg, s_prev, "nt")
            xdt = xg * dt_x[:, gs]
            dxdt = bds
            dcb = jnp.zeros((Q, Q), F32)
            dcb_t = jnp.zeros((Q, Q), F32)
            for r in range(R):
                h = g * R + r
                cs_col = cs_b[:, h * Q:(h + 1) * Q]
                cs_row = cst[h:h + 1, :]
                lmat = jnp.exp(jnp.where(tri, cs_col - cs_row, -jnp.inf))
                lmat_t = jnp.exp(jnp.where(tri_t, cs_row - cs_col, -jnp.inf))
                in_head = jnp.logical_and(head_lane >= r * P, head_lane < (r + 1) * P)
                dyr = jnp.where(in_head, dy, 0.0)
                dm = _dot(dyr, xdt, "nt")
                dcb = dcb + dm * lmat
                dcb_t = dcb_t + _dot(xdt, dyr, "nt") * lmat_t
                w_rc = dm * (cb * lmat)
                dcs_rows = jnp.where(lane == h, jnp.sum(w_rc, axis=1, keepdims=True), dcs_rows)
                dcs_cols = jnp.where(head_rows == h, jnp.sum(w_rc, axis=0, keepdims=True), dcs_cols)
                dxdt = dxdt + _dot(cb_t * lmat_t, dyr)
            sel_t = sel_pt_ref[gs, :]
            s_dy_cs = s_dy_cs + _dot_exact(dy * csg, sel_t, 2)
            s_x_bds = s_x_bds + _dot_exact(xg * bds, sel_t, 2)
            s_x_dxdt = s_x_dxdt + _dot_exact(xg * dxdt, sel_t, 2)
            c_end = c_end + jnp.sum(jnp.sum(end, axis=1, keepdims=True) * sel_t.astype(F32), axis=0, keepdims=True)
            dye = dy * e_x[:, gs]
            dc = _dot(dcb, bg) + _dot(dye, s_prev)
            db = _dot(dcb_t, cg) + _dot(xdt * dec_x[:, gs], dsg)
            decay = _by_row_block(state_rows, [e_last[:, g * R + r:g * R + r + 1] for r in range(R)], P)
            ds_ref[gs, :] = dsg * decay + _dot(dye, cg, "tn")
            dxbc_ref[:, gs] = jnp.where(valid, dxdt * dt_x[:, gs] + dy * dsk, 0.0)
            dxbc_ref[:, DI + g * N:DI + (g + 1) * N] = jnp.where(valid, db, 0.0)
            dxbc_ref[:, DI + GN + g * N:DI + GN + (g + 1) * N] = jnp.where(valid, dc, 0.0)
        dcs = s_dy_cs * jnp.exp(cs) - dt * s_x_bds
        da_cs = _cumsum_rows(dcs + (dcs_rows - dcs_cols.T), reverse=True) + c_end
        ddt_all = jnp.where(live, da_cs * a + s_x_dxdt, 0.0)
        ddt_raw = ddt_all * sig_pre
        ddt_ref[...] = ddt_raw.astype(BF16)
        ddtb_ref[...] += jnp.sum(ddt_raw, axis=0, keepdims=True)
        dal_ref[...] += jnp.sum(da_cs * dt, axis=0, keepdims=True) * a
        dsk_all = jnp.broadcast_to(jnp.concatenate(dsk_rows, axis=1), (8, DI))
        ddsk_ref[...] += _dot_exact(dsk_all, sel_pt_ref[...], 3)[0:1]

    rev = lambda i: nc - 1 - i
    x_s, b_s, c_s, dt_s, vec, nw = _ssd_specs(T, DI, GN, rev)
    wide = pl.BlockSpec((Q, DI), lambda i: (rev(i), 0))
    st = pl.BlockSpec((1, DI, N), lambda i: (rev(i), 0, 0))
    st_next = pl.BlockSpec((1, DI, N), lambda i: (jnp.minimum(rev(i) + 1, nc - 1), 0, 0))
    sel_p = pl.BlockSpec((LANES, DI), lambda i: (0, 0))
    sel_q = pl.BlockSpec((LANES, n_heads * Q), lambda i: (0, 0))
    sel_pt = pl.BlockSpec((DI, LANES), lambda i: (0, 0))
    sel = _head_selector(n_heads, P)
    return pl.pallas_call(
        body, name=name, grid=(nc,),
        in_specs=[x_s, b_s, c_s, dt_s, wide, vec, vec, nw, nw, wide, st, st_next, wide, sel_p, sel_q, sel_pt],
        out_specs=(pl.BlockSpec((Q, W), lambda i: (rev(i), 0)), wide, dt_s, vec, vec, vec, nw),
        out_shape=(jax.ShapeDtypeStruct((T, W), F32), jax.ShapeDtypeStruct((T, DI), BF16), jax.ShapeDtypeStruct((T, LANES), BF16),
                   jax.ShapeDtypeStruct((1, LANES), F32), jax.ShapeDtypeStruct((1, LANES), F32),
                   jax.ShapeDtypeStruct((1, LANES), F32), jax.ShapeDtypeStruct((1, DI), F32)),
        scratch_shapes=[pltpu.VMEM((DI, N), F32)],
        compiler_params=_cp("arbitrary"),
    )(xbc, xbc, xbc, pdt, pz, dt_bias, a_log, jnp.repeat(d_skip[:, :n_heads], P, axis=1), norm_w, y, prev, prev, dyn,
      sel, _head_selector(n_heads, Q), sel.T)


def _adamw(w, g, m, v, name):
    layers, rows, cols = w.shape
    tr = _tile(rows, 256, 8)

    def body(w_ref, g_ref, m_ref, v_ref, d_ref, nm_ref, nv_ref):
        g = g_ref[...]
        m = ADAM_B1 * m_ref[...] + (1.0 - ADAM_B1) * g
        v = ADAM_B2 * v_ref[...] + (1.0 - ADAM_B2) * (g * g)
        m_hat = m / (1.0 - ADAM_B1 ** ADAM_STEP)
        v_hat = v / (1.0 - ADAM_B2 ** ADAM_STEP)
        d_ref[...] = -ADAM_LR * (m_hat / (jnp.sqrt(v_hat) + ADAM_EPS) + ADAM_WD * w_ref[...])
        nm_ref[...] = m
        nv_ref[...] = v

    blk = pl.BlockSpec((1, tr, cols), lambda l, i: (l, i, 0))
    out = jax.ShapeDtypeStruct((layers, rows, cols), F32)
    return pl.pallas_call(
        body, name=name, grid=(layers, rows // tr), in_specs=[blk] * 4, out_specs=(blk,) * 3, out_shape=(out,) * 3,
        compiler_params=_cp("parallel", "parallel"),
    )(w, g, m, v)


def _sum_stack(parts, name, out_dtype=F32):
    rows, cols = parts[0].shape
    tr = _tile(rows, 512, 16)

    def body(*refs):
        acc = refs[0][...].astype(F32)
        for r in refs[1:-1]:
            acc = acc + r[...].astype(F32)
        refs[-1][...] = acc.astype(out_dtype)

    blk = pl.BlockSpec((tr, cols), lambda i: (i, 0))
    return pl.pallas_call(
        body, name=name, grid=(rows // tr,), in_specs=[blk] * len(parts), out_specs=blk,
        out_shape=jax.ShapeDtypeStruct((rows, cols), out_dtype), compiler_params=_cp("parallel"),
    )(*parts)


def _place():
    return lax.axis_index("x"), lax.axis_index("y"), lax.axis_index("c")


def _other_chips(x, y):
    return [(1 - x, y), (x, 1 - y), (1 - x, 1 - y)]


def _all_gather(shards, name):
    nb = len(shards)

    def body(*refs):
        ins, outs = refs[:nb], refs[nb:2 * nb]
        send_sems, recv_sems, local_sems = refs[2 * nb:]
        x, y, c = _place()
        me, sibling = (x, y, c), (x, y, 1 - c)
        chips = _other_chips(x, y)

        def copy(q, k, block, to, src=None):
            dst = outs[q].at[4 * block[0] + 2 * block[1] + block[2]]
            return pltpu.make_async_remote_copy(
                src_ref=dst if src is None else src, dst_ref=dst,
                send_sem=send_sems.at[7 * q + k], recv_sem=recv_sems.at[7 * q + k], device_id=to, device_id_type=MESH)

        started = []
        for q in range(nb):
            mine = pltpu.make_async_copy(ins[q], outs[q].at[4 * x + 2 * y + c], local_sems.at[q])
            mine.start()
            started.append(mine)
        first = []
        for q in range(nb):
            first.append(copy(q, 0, me, sibling, src=ins[q]))
            first += [copy(q, 1 + j, me, (*chip, c), src=ins[q]) for j, chip in enumerate(chips)]
        for cp in first:
            cp.start()
        passed = []
        for j, chip in enumerate(chips):
            for q in range(nb):
                copy(q, 1 + j, (*chip, c), me).wait_recv()
                fwd = copy(q, 4 + j, (*chip, c), sibling)
                fwd.start()
                passed.append(fwd)
        for q in range(nb):
            copy(q, 0, sibling, me).wait_recv()
            for j, chip in enumerate(chips):
                copy(q, 4 + j, (*chip, 1 - c), me).wait_recv()
        for cp in first + passed:
            cp.wait_send()
        for mine in started:
            mine.wait()

    return pl.pallas_call(
        body, name=name, in_specs=[ANY] * nb, out_specs=tuple([ANY] * nb),
        out_shape=tuple(jax.ShapeDtypeStruct((N_DEV,) + s.shape, s.dtype) for s in shards),
        scratch_shapes=[pltpu.SemaphoreType.DMA((7 * nb,)), pltpu.SemaphoreType.DMA((7 * nb,)), pltpu.SemaphoreType.DMA((nb,))],
    )(*shards)


HBM = pl.BlockSpec(memory_space=pltpu.HBM)
SEM = pl.BlockSpec(memory_space=pltpu.SEMAPHORE)
N_PEERS = N_DEV - 1


def _peer(k, x, y, c):
    return (1 - x if k & 4 else x, 1 - y if k & 2 else y, 1 - c if k & 1 else c)


def _direct_copies(gather, srcs, lands, send_sems, recv_sems):
    x, y, c = _place()
    copies = []
    for q in range(len(srcs)):
        for k in range(1, N_DEV):
            px, py, pc = _peer(k, x, y, c)
            if gather:
                src, dst = srcs[q], lands[q].at[4 * x + 2 * y + c]
            else:
                src, dst = srcs[q].at[4 * px + 2 * py + pc], lands[q].at[k - 1]
            copies.append(pltpu.make_async_remote_copy(
                src_ref=src, dst_ref=dst, send_sem=send_sems.at[N_PEERS * q + k - 1], recv_sem=recv_sems.at[N_PEERS * q + k - 1],
                device_id=(px, py, pc), device_id_type=MESH))
    return copies


def _exchange_start(gather, srcs, lands, name):
    n = len(srcs)

    def body(*refs):
        send_sems, recv_sems = refs[2 * n], refs[2 * n + 1]
        for cp in _direct_copies(gather, refs[:n], refs[n:2 * n], send_sems, recv_sems):
            cp.start()
        refs[-1][...] = jnp.zeros_like(refs[-1])

    held = [pltpu.with_memory_space_constraint(t, pltpu.HBM) for t in list(srcs) + list(lands)]
    out = pl.pallas_call(
        body, name=name,
        out_shape=(pltpu.SemaphoreType.DMA((N_PEERS * n,)), pltpu.SemaphoreType.DMA((N_PEERS * n,)),
                   *[pltpu.HBM(t.shape, t.dtype) for t in held], jax.ShapeDtypeStruct((8, LANES), F32)),
        in_specs=[HBM] * (2 * n), out_specs=(SEM, SEM, *[HBM] * (2 * n), pl.BlockSpec(memory_space=pltpu.VMEM)),
        input_output_aliases={i: 2 + i for i in range(2 * n)},
        compiler_params=pltpu.CompilerParams(has_side_effects=pltpu.SideEffectType.DATAFLOW_SIDE_EFFECTING),
    )(*held)
    return out[0], out[1], list(out[2:2 + 2 * n]), out[-1]


def _exchange_wait(gather, started, after, name):
    send_sems, recv_sems, held, _ = started
    n = len(held) // 2

    def body(*refs):
        for cp in _direct_copies(gather, refs[:n], refs[n:2 * n], refs[2 * n], refs[2 * n + 1]):
            cp.wait_send()
            cp.wait_recv()

    out = pl.pallas_call(
        body, name=name, out_shape=tuple(pltpu.HBM(t.shape, t.dtype) for t in held),
        in_specs=[HBM] * (2 * n) + [SEM, SEM, pl.BlockSpec(memory_space=pl.ANY)], out_specs=tuple([HBM] * (2 * n)),
        input_output_aliases={i: i for i in range(2 * n)},
        compiler_params=pltpu.CompilerParams(has_side_effects=pltpu.SideEffectType.DATAFLOW_SIDE_EFFECTING),
    )(*held, send_sems, recv_sems, after)
    return list(out[n:])


def _own_plus_received(stack, got, name):
    n_blocks, rows, cols = stack.shape
    tr = _tile(rows, 512, 16)

    def body(*refs):
        acc = refs[0][0]
        for r in refs[1:-1]:
            acc = acc + r[0].astype(F32)
        refs[-1][...] = acc

    def mine(i):
        if n_blocks == 1:
            return (0, i, 0)
        x, y, c = _place()
        return (4 * x + 2 * y + c, i, 0)

    return pl.pallas_call(
        body, name=name, grid=(rows // tr,),
        in_specs=[pl.BlockSpec((1, tr, cols), mine)]
        + [pl.BlockSpec((1, tr, cols), functools.partial(lambda k, i: (k, i, 0), k)) for k in range(N_PEERS)],
        out_specs=pl.BlockSpec((tr, cols), lambda i: (i, 0)),
        out_shape=jax.ShapeDtypeStruct((rows, cols), F32), compiler_params=_cp("parallel"),
    )(stack, *[got] * N_PEERS)


class _Shard:
    def __init__(self, name, axis):
        self.name, self.axis = name, axis


BIG = [_Shard("w_in", 2), _Shard("pool_w_group", 2), _Shard("w_pool_up", 1), _Shard("w_ssd_out", 1),
       _Shard("w_o", 1), _Shard("w_ff1", 2), _Shard("w_ff2", 1)]
SMALL_SHARDED = [_Shard("meta_tokens", 1), _Shard("conv_w", 2)]
REPLICATED = ["mix_norm_w", "b_gate", "pool_scale", "conv_b", "dt_bias", "a_log", "d_skip", "ssd_norm_w",
              "mlp_norm_w", "final_norm_w"]
WEIGHTS = ["meta_tokens", "mix_norm_w", "w_in", "b_gate", "pool_w_group", "pool_scale", "w_pool_up", "conv_w", "conv_b",
           "dt_bias", "a_log", "d_skip", "ssd_norm_w", "w_ssd_out", "w_o", "mlp_norm_w", "w_ff1", "w_ff2", "final_norm_w"]


def _columns(blocks, a, b):
    pieces, o = [], 0
    for t in blocks:
        n = t.shape[-1]
        lo, hi = max(a, o), min(b, o + n)
        if lo < hi:
            pieces.append(t[..., lo - o:hi - o])
        o += n
    return pieces[0] if len(pieces) == 1 else jnp.concatenate(pieces, axis=-1)


def _rows2(a):
    return a.reshape(-1, a.shape[-1])


def _unshard(stack, shard_shape, axis):
    t = stack.reshape((N_DEV,) + tuple(shard_shape))
    return jnp.concatenate([t[d] for d in range(N_DEV)], axis=axis)


def _reshard(layers, axis):
    cut = [jnp.split(t, N_DEV, axis=axis - 1) for t in layers]
    blocks = [jnp.concatenate([_rows2(pieces[d]) for pieces in cut], axis=0) for d in range(N_DEV)]
    return jnp.stack(blocks).reshape((4, 2) + blocks[0].shape)


def _lane_rows(a):
    n = a.size
    tile = 8 * LANES
    if n % tile:
        return jnp.pad(a.reshape(-1), (0, (-n) % tile)).reshape(-1, LANES)
    return a.reshape(-1, LANES)


def _unpack_small(buf, spans, shapes):
    out = []
    for (o, r), shp in zip(spans, shapes):
        n = 1
        for d in shp:
            n *= d
        t = buf[o:o + r]
        out.append(t.reshape(shp) if n == r * LANES else t.reshape(-1)[:n].reshape(shp))
    return out


def _pack_small(parts, mult):
    mats = [_lane_rows(p) for p in parts]
    spans, o = [], 0
    for t in mats:
        spans.append((o, t.shape[0]))
        o += t.shape[0]
    fill = (-o) % mult
    if fill:
        mats.append(jnp.zeros((fill, LANES), mats[0].dtype))
    return jnp.concatenate(mats, axis=0), spans


def _layer_fwd(h, lw, cfg, tag, dep=None):
    pad, n_heads = cfg["pad"], cfg["n_heads"]
    u = _rms_fwd(h, lw["mix_norm_w"], f"rms_mix_{tag}", dep=dep)
    p_xbc = _mm(u, lw["w_xbc"], "nn", f"proj_xbc_{tag}")
    p_z = _mm(u, lw["w_z"], "nn", f"proj_z_{tag}")
    p_gate = _mm(u, lw["w_gate"], "nn", f"proj_gate_{tag}")
    p_pool = _mm(u, lw["w_pool"], "nn", f"proj_pool_{tag}")
    p_dt = _mm(u, lw["w_dt"], "nn", f"proj_dt_{tag}")
    pooled, y1 = _pool_fwd(p_pool, lw["pool_w_group"], lw["pool_scale"], pad, f"pool_fwd_{tag}")
    y_pool = _mm(y1, lw["w_pool_up"], "nn", f"pool_up_{tag}")
    xbc = _conv_fwd(p_xbc, lw["conv_w"], lw["conv_b"], f"conv_fwd_{tag}")
    y, yn, prev = _ssd_fwd(xbc, p_dt, p_z, lw["dt_bias"], lw["a_log"], lw["d_skip"], lw["ssd_norm_w"], pad, n_heads, f"ssd_fwd_{tag}")
    y_ssd = _mm(yn, lw["w_ssd_out"], "nn", f"ssd_out_{tag}")
    mix = _gate_fwd(p_gate, lw["b_gate"], y_pool, y_ssd, f"gate_fwd_{tag}")
    h_mid = _mm(mix, lw["w_o"], "nn", f"mix_out_{tag}", res=h)
    v = _rms_fwd(h_mid, lw["mlp_norm_w"], f"rms_mlp_{tag}")
    hid, act = _mm(v, lw["w_ff1"], "nn", f"ff1_{tag}", epi="relu2")
    h_out = _mm(act, lw["w_ff2"], "nn", f"ff2_{tag}", res=h_mid)
    saved = dict(h=h, u=u, p_xbc=p_xbc, p_z=p_z, p_gate=p_gate, p_dt=p_dt, pooled=pooled, y1=y1, y_pool=y_pool, xbc=xbc,
                 y=y, yn=yn, prev=prev, y_ssd=y_ssd, mix=mix, h_mid=h_mid, v=v, hid=hid, act=act)
    return h_out, saved


def _layer_bwd(dh, dh_b, lw, s, cfg, tag, traffic, i):
    pad, n_heads = cfg["pad"], cfg["n_heads"]
    g = {}
    dhid = _mm(dh_b, lw["w_ff2"], "nt", f"d_act_{tag}", epi="drelu2", aux=s["hid"], dep=traffic.bwd_begin(i))
    gb = {}
    g["w_ff2"], gb["w_ff2"] = _mm(s["act"], dh_b, "tn", f"dw_ff2_{tag}", epi="twin")
    dv = _mm(dhid, lw["w_ff1"], "nt", f"d_v_{tag}")
    g["w_ff1"], gb["w_ff1"] = _mm(s["v"], dhid, "tn", f"dw_ff1_{tag}", epi="twin")
    dep = traffic.grads_ready(i, {k: (g[k], gb[k]) for k in ("w_ff1", "w_ff2")}, "mlp")
    dh_mid, dh_mid_b, g["mlp_norm_w"] = _rms_bwd(s["h_mid"], lw["mlp_norm_w"], dv, dh, f"rms_mlp_bwd_{tag}")
    dmix = _mm(dh_mid_b, lw["w_o"], "nt", f"d_mix_{tag}", dep=dep)
    g["w_o"], gb["w_o"] = _mm(s["mix"], dh_mid_b, "tn", f"dw_o_{tag}", epi="twin")
    dgate, dy_pool, dy_ssd, g["b_gate"] = _gate_bwd(s["p_gate"], lw["b_gate"], s["y_pool"], s["y_ssd"], dmix, f"gate_bwd_{tag}")
    dy1 = _mm(dy_pool, lw["w_pool_up"], "nt", f"d_y1_{tag}")
    g["w_pool_up"], gb["w_pool_up"] = _mm(s["y1"], dy_pool, "tn", f"dw_pool_up_{tag}", epi="twin")
    dpool, g["pool_w_group"], g["pool_scale"] = _pool_bwd(s["pooled"], lw["pool_w_group"], lw["pool_scale"], dy1, pad, f"pool_bwd_{tag}")
    g["w_ssd_out"], gb["w_ssd_out"] = _mm(s["yn"], dy_ssd, "tn", f"dw_ssd_out_{tag}", epi="twin")
    gb["pool_w_group"] = g["pool_w_group"].astype(BF16)
    dep = traffic.grads_ready(i, {k: (g[k], gb[k]) for k in ("pool_w_group", "w_pool_up", "w_ssd_out", "w_o")}, "mix")
    dyn = _mm(dy_ssd, lw["w_ssd_out"], "nt", f"d_yn_{tag}", dep=dep)
    dact, dz, ddt, g["dt_bias"], g["a_log"], g["d_skip"], g["ssd_norm_w"] = _ssd_bwd(
        s["xbc"], s["p_dt"], s["p_z"], lw["dt_bias"], lw["a_log"], lw["d_skip"], lw["ssd_norm_w"], s["y"], s["prev"], dyn,
        pad, n_heads, f"ssd_bwd_{tag}")
    dxbc, g["conv_w"], g["conv_b"] = _conv_bwd(s["p_xbc"], lw["conv_w"], lw["conv_b"], dact, pad, f"conv_bwd_{tag}")
    u = s["u"]
    for k, d in (("w_xbc", dxbc), ("w_z", dz), ("w_gate", dgate), ("w_pool", dpool), ("w_dt", ddt)):
        g[k], gb[k] = _mm(u, d, "tn", f"d{k}_{tag}", epi="twin")
    c_dt = cfg["cols"][3]
    parts, parts_b = ([t["w_pool"], t["w_z"], t["w_xbc"], t["w_dt"][:, :c_dt], t["w_gate"]] for t in (g, gb))
    g["w_in"] = jnp.concatenate(parts, axis=1)
    dep = traffic.grads_ready(i, {"w_in": (parts, parts_b)}, "in")
    du = _mm(dxbc, lw["w_xbc"], "nt", f"du_xbc_{tag}", dep=dep)
    du = _mm(dz, lw["w_z"], "nt", f"du_z_{tag}", res=du)
    du = _mm(dgate, lw["w_gate"], "nt", f"du_gate_{tag}", res=du)
    du = _mm(dpool, lw["w_pool"], "nt", f"du_pool_{tag}", res=du)
    du = _mm(ddt, lw["w_dt"], "nt", f"du_dt_{tag}", res=du)
    dh_in, dh_in_b, g["mix_norm_w"] = _rms_bwd(s["h"], lw["mix_norm_w"], du, dh_mid, f"rms_mix_bwd_{tag}")
    traffic.bwd_end(i, dh_in)
    return dh_in, dh_in_b, g


def _pad_lanes(v):
    return jnp.pad(v, (0, LANES - v.shape[0])).reshape(1, LANES)


class _WholeWeights:
    def __init__(self, full):
        self.full = full

    def fwd_begin(self, i):
        full = {k: t[i] for k, t in self.full.items()}
        full["w_in"] = [full["w_in"]]
        return full, None

    def fwd_end(self, i, h_out):
        pass

    def bwd_begin(self, i):
        return None

    def grads_ready(self, i, grads, tag):
        return None

    def bwd_end(self, i, dh_in):
        pass


def _local_step(x2, target, meta_full, traffic, rep, cfg):
    depth, pad, n_meta, H = cfg["depth"], cfg["pad"], cfg["n_meta"], cfg["n_heads"]
    D = x2.shape[1]
    di = rep["ssd_norm_w"].shape[1]
    c_pool, c_z, c_xbc, c_dt = cfg["cols"]
    h = jnp.concatenate([jnp.zeros((pad, D), F32), meta_full, x2], axis=0)
    lws, saves = [], []
    for i in range(depth):
        full, dep = traffic.fwd_begin(i)
        blocks = full.pop("w_in")
        o_z, o_xbc, o_dt, o_gate = c_pool, c_pool + c_z, c_pool + c_z + c_xbc, c_pool + c_z + c_xbc + c_dt
        w_dt = jnp.pad(_columns(blocks, o_dt, o_gate), ((0, 0), (0, LANES - c_dt)))
        lw = dict(
            full, w_pool=_columns(blocks, 0, o_z), w_z=_columns(blocks, o_z, o_xbc), w_xbc=_columns(blocks, o_xbc, o_dt),
            w_dt=w_dt, w_gate=_columns(blocks, o_gate, o_gate + 2 * D),
            mix_norm_w=rep["mix_norm_w"][i], b_gate=rep["b_gate"][i], pool_scale=rep["pool_scale"][i].reshape(1, -1),
            conv_b=rep["conv_b"][i].reshape(1, -1), dt_bias=_pad_lanes(rep["dt_bias"][i]), a_log=_pad_lanes(rep["a_log"][i]),
            d_skip=_pad_lanes(rep["d_skip"][i]), ssd_norm_w=rep["ssd_norm_w"][i].reshape(1, di), mlp_norm_w=rep["mlp_norm_w"][i])
        lws.append(lw)
        h, s = _layer_fwd(h, lw, cfg, f"l{i}", dep)
        saves.append(s)
        traffic.fwd_end(i, h)
    loss, dh, dh_b, g_final = _final_loss(h, rep["final_norm_w"], target, pad + n_meta, "final_loss")
    per_layer = []
    for i in range(depth - 1, -1, -1):
        dh, dh_b, g = _layer_bwd(dh, dh_b, lws[i], saves[i], cfg, f"l{i}", traffic, i)
        per_layer.append(g)
    per_layer.reverse()

    def stack(key, fn=lambda t: t):
        return jnp.stack([fn(g[key]) for g in per_layer])

    def layers(key):
        return [g[key] for g in per_layer]

    grads = dict(
        w_in=layers("w_in"), pool_w_group=layers("pool_w_group"), w_pool_up=layers("w_pool_up"), w_ssd_out=layers("w_ssd_out"), w_o=layers("w_o"),
        w_ff1=layers("w_ff1"), w_ff2=layers("w_ff2"), conv_w=stack("conv_w"),
        mix_norm_w=stack("mix_norm_w", lambda t: t[0]), b_gate=stack("b_gate", lambda t: t[0]),
        pool_scale=stack("pool_scale", lambda t: t[0]), conv_b=stack("conv_b", lambda t: t[0]),
        dt_bias=stack("dt_bias", lambda t: t[0, :H]), a_log=stack("a_log", lambda t: t[0, :H]), d_skip=stack("d_skip", lambda t: t[0, :H]),
        ssd_norm_w=stack("ssd_norm_w", lambda t: t[0]), mlp_norm_w=stack("mlp_norm_w", lambda t: t[0]),
        final_norm_w=g_final[0], meta_tokens=dh[pad:pad + n_meta])
    return loss, dh[pad + n_meta:], grads


class _ShardedWeights:
    def __init__(self, w, me):
        self.w, self.me = w, me
        self.depth = w[BIG[0].name].shape[0]
        small = [_rows2(w[s.name]) for s in SMALL_SHARDED]
        landed = _all_gather([_rows2(w[s.name][0]).astype(BF16) for s in BIG] + small, "gather_l0")
        self.small = {s.name: _unshard(t, w[s.name].shape, s.axis) for s, t in zip(SMALL_SHARDED, landed[len(BIG):])}
        self.landed = dict(zip([s.name for s in BIG], landed))
        self.fetching = None
        self.sending, self.token = [], None
        self.local_grads = [{} for _ in range(self.depth)]

    def _fetch(self, shards, i, name):
        mine = [_rows2(self.w[s.name][i]).astype(BF16) for s in shards]
        lands = [lax.dynamic_update_slice(lax.empty((N_DEV,) + t.shape, t.dtype), t[None], (self.me, 0, 0)) for t in mine]
        return _exchange_start(True, mine, lands, name)

    def _whole(self, shards):
        return {s.name: _unshard(self.landed[s.name], self.w[s.name].shape[1:], s.axis - 1) for s in shards}

    def fwd_begin(self, i):
        rows, cols = self.w["w_in"].shape[1:]
        full = {"w_in": list(self.landed["w_in"].reshape(N_DEV, rows, cols)), "conv_w": self.small["conv_w"][i]}
        full.update(self._whole(BIG[1:]))
        dep = None
        if i + 1 < self.depth:
            self.fetching = self._fetch(BIG, i + 1, f"gather_start_l{i + 1}")
            dep = self.fetching[3]
        return full, dep

    def fwd_end(self, i, h_out):
        if self.fetching is not None:
            self.landed = dict(zip([s.name for s in BIG], _exchange_wait(True, self.fetching, h_out, f"gather_wait_l{i + 1}")))
            self.fetching = None

    def bwd_begin(self, i):
        return self.token

    def grads_ready(self, i, grads, tag):
        shards = [s for s in BIG if s.name in grads]

        def blocks(s, val):
            shape = (N_DEV,) + _rows2(self.w[s.name][i]).shape
            if isinstance(val, list):
                return jnp.stack([_columns(val, d * shape[2], (d + 1) * shape[2]) for d in range(N_DEV)])
            return _reshard([val], s.axis).reshape(shape)

        def own(s, val):
            if isinstance(val, list):
                cols = self.w[s.name].shape[2]
                return lax.dynamic_slice_in_dim(jnp.concatenate(val, axis=1), self.me * cols, cols, axis=1)[None]
            return blocks(s, val)

        stacks = [own(s, grads[s.name][0]) for s in shards]
        sends = [blocks(s, grads[s.name][1]) for s in shards]
        lands = [lax.empty((N_PEERS,) + t.shape[1:], BF16) for t in sends]
        started = _exchange_start(False, sends, lands, f"rs_start_{tag}_l{i}")
        self.sending.append((i, tag, shards, stacks, started))
        self.token = started[3]
        return self.token

    def _collect(self, entry, after):
        i, tag, shards, stacks, started = entry
        got = _exchange_wait(False, started, after, f"rs_wait_{tag}_l{i}")
        for s, st, g in zip(shards, stacks, got):
            self.local_grads[i][s.name] = _own_plus_received(st, g, f"rs_sum_{s.name}_l{i}")

    def bwd_end(self, i, dh_in):
        for entry in [e for e in self.sending if e[0] > i]:
            self._collect(entry, dh_in)
        self.sending = [e for e in self.sending if e[0] <= i]

    def finish(self, after, names):
        for entry in [e for e in self.sending if e[2][0].name in names]:
            self._collect(entry, after)
        self.sending = [e for e in self.sending if e[2][0].name not in names]
        return {k: jnp.concatenate([g[k] for g in self.local_grads], axis=0) for k in names}


def kernel(x, meta_tokens, mix_norm_w, w_in, b_gate, pool_w_group, pool_scale, w_pool_up, conv_w, conv_b, dt_bias, a_log, d_skip, ssd_norm_w, w_ssd_out, w_o, mlp_norm_w, w_ff1, w_ff2, final_norm_w, loss_target, m_meta_tokens, m_mix_norm_w, m_w_in, m_b_gate, m_pool_w_group, m_pool_scale, m_w_pool_up, m_conv_w, m_conv_b, m_dt_bias, m_a_log, m_d_skip, m_ssd_norm_w, m_w_ssd_out, m_w_o, m_mlp_norm_w, m_w_ff1, m_w_ff2, m_final_norm_w, v_meta_tokens, v_mix_norm_w, v_w_in, v_b_gate, v_pool_w_group, v_pool_scale, v_w_pool_up, v_conv_w, v_conv_b, v_dt_bias, v_a_log, v_d_skip, v_ssd_norm_w, v_w_ssd_out, v_w_o, v_mlp_norm_w, v_w_ff1, v_w_ff2, v_final_norm_w):
    w = dict(meta_tokens=meta_tokens, mix_norm_w=mix_norm_w, w_in=w_in, b_gate=b_gate, pool_w_group=pool_w_group,
             pool_scale=pool_scale, w_pool_up=w_pool_up, conv_w=conv_w, conv_b=conv_b, dt_bias=dt_bias, a_log=a_log,
             d_skip=d_skip, ssd_norm_w=ssd_norm_w, w_ssd_out=w_ssd_out, w_o=w_o, mlp_norm_w=mlp_norm_w, w_ff1=w_ff1,
             w_ff2=w_ff2, final_norm_w=final_norm_w)
    m = dict(meta_tokens=m_meta_tokens, mix_norm_w=m_mix_norm_w, w_in=m_w_in, b_gate=m_b_gate, pool_w_group=m_pool_w_group,
             pool_scale=m_pool_scale, w_pool_up=m_w_pool_up, conv_w=m_conv_w, conv_b=m_conv_b, dt_bias=m_dt_bias, a_log=m_a_log,
             d_skip=m_d_skip, ssd_norm_w=m_ssd_norm_w, w_ssd_out=m_w_ssd_out, w_o=m_w_o, mlp_norm_w=m_mlp_norm_w, w_ff1=m_w_ff1,
             w_ff2=m_w_ff2, final_norm_w=m_final_norm_w)
    v = dict(meta_tokens=v_meta_tokens, mix_norm_w=v_mix_norm_w, w_in=v_w_in, b_gate=v_b_gate, pool_w_group=v_pool_w_group,
             pool_scale=v_pool_scale, w_pool_up=v_w_pool_up, conv_w=v_conv_w, conv_b=v_conv_b, dt_bias=v_dt_bias, a_log=v_a_log,
             d_skip=v_d_skip, ssd_norm_w=v_ssd_norm_w, w_ssd_out=v_w_ssd_out, w_o=v_w_o, mlp_norm_w=v_mlp_norm_w, w_ff1=v_w_ff1,
             w_ff2=v_w_ff2, final_norm_w=v_final_norm_w)

    _, seq, D = x.shape
    n_meta = meta_tokens.shape[0]
    depth = w_in.shape[0]
    n_heads = dt_bias.shape[1]
    d_inner = ssd_norm_w.shape[1]
    d_xbc = conv_b.shape[1]
    pool_width = pool_scale.shape[1]
    pad = (-n_meta) % CHUNK
    cfg = dict(depth=depth, pad=pad, n_meta=n_meta, n_heads=n_heads, cols=(pool_width, d_inner, d_xbc, n_heads))
    assert (pad + n_meta + seq) % CHUNK == 0 and pad + n_meta == CHUNK

    xi, yi, ci = _place()
    me = 4 * xi + 2 * yi + ci

    traffic = _ShardedWeights(w, me)
    rep = {k: w[k] for k in REPLICATED}
    loss_part, dx, grads = _local_step(x[0], loss_target[0], traffic.small["meta_tokens"], traffic, rep, cfg)
    loss = lax.psum(loss_part[0, 0], ("x", "y", "c"))

    small_names = REPLICATED + [s.name for s in SMALL_SHARDED]
    sm_buf, sm_spans = _pack_small([grads[k] for k in small_names], 16)
    (sm_all,) = _all_gather([sm_buf], "gather_small_grads")
    sm_sum = _sum_stack([sm_all[d] for d in range(N_DEV)], "small_grads_sum")
    g_small = dict(zip(small_names, _unpack_small(sm_sum, sm_spans, [grads[k].shape for k in small_names])))
    g_loc = {k: g_small[k] for k in REPLICATED}
    for s in SMALL_SHARDED:
        blk = w[s.name].shape[s.axis]
        g_loc[s.name] = lax.dynamic_slice_in_dim(g_small[s.name], me * blk, blk, axis=s.axis)

    delta, new_m, new_v = {}, {}, {}
    loc_shapes = [w[k].shape for k in small_names]
    packed = [_pack_small([t[k] for k in small_names], 8) for t in (w, g_loc, m, v)]
    loc_spans = packed[0][1]
    outs = _adamw(*[p[0][None] for p in packed], "adamw_small")
    for res, buf in zip((delta, new_m, new_v), outs):
        res.update(zip(small_names, _unpack_small(buf[0], loc_spans, loc_shapes)))
    after = outs[0]
    for names in ([s.name for s in BIG if s.name != "w_in"], ["w_in"]):
        for k, t in traffic.finish(after, names).items():
            shp = w[k].shape
            by_layer = (shp[0], -1, shp[-1])
            d3, m3, v3 = _adamw(w[k].reshape(by_layer), t.reshape(by_layer), m[k].reshape(by_layer), v[k].reshape(by_layer), f"adamw_{k}")
            g_loc[k], delta[k], new_m[k], new_v[k] = t.reshape(shp), d3.reshape(shp), m3.reshape(shp), v3.reshape(shp)
            after = d3

    return (loss, dx[None], *[g_loc[k] for k in WEIGHTS], *[delta[k] for k in WEIGHTS],
            *[new_m[k] for k in WEIGHTS], *[new_v[k] for k in WEIGHTS])
```
